```python
import jax, jax.numpy as jnp
from jax import lax
import numpy as np

D_MODEL = 1024
BATCH = 16
SEQ = 2048
DEPTH = 1

N_MEM = 256
XATTN_HEADS = 4
XATTN_HEAD_DIM = D_MODEL // XATTN_HEADS
CONV_WIDTH = D_MODEL // 2
CONV_K = 3
ATTN_HEADS = 8
HEAD_DIM = 64
ATTN_WIDTH = ATTN_HEADS * HEAD_DIM
MOBA_BLOCK = 256
MOBA_TOPK = 3
MOBA_Q_CHUNK = 32
ROPE_THETA = 10000.0
N_BRANCH = 2
IN_PROJ_WIDTH = 3 * CONV_WIDTH + 3 * ATTN_WIDTH + N_BRANCH * D_MODEL
PEER_HEADS = 8
PEER_N_KEYS = 128
PEER_N_EXPERTS = PEER_N_KEYS * PEER_N_KEYS
PEER_QUERY_DIM = 256
PEER_HALF = PEER_QUERY_DIM // 2
PEER_TOPK = 16
PEER_TOKEN_CHUNK = 128

EPS = 1e-6
MASK_VALUE = -1e30

kernel_name = "hybrid_conv_moba_xattn_peer_block"


def rmsnorm(x, g):
    xf = x.astype(jnp.float32)
    y = xf * lax.rsqrt(jnp.mean(xf * xf, axis=-1, keepdims=True) + EPS)
    return (y * g.astype(jnp.float32)).astype(x.dtype)


def rope(x):
    s, hd = x.shape[2], x.shape[3]
    half = hd // 2
    inv = ROPE_THETA ** (-jnp.arange(half, dtype=jnp.float32) / half)
    ang = jnp.arange(s, dtype=jnp.float32)[:, None] * inv[None, :]
    cos, sin = jnp.cos(ang), jnp.sin(ang)
    xf = x.astype(jnp.float32)
    x1, x2 = xf[..., :half], xf[..., half:]
    out = jnp.concatenate([x1 * cos - x2 * sin, x2 * cos + x1 * sin], axis=-1)
    return out.astype(x.dtype)


def short_gated_conv(xin, b_gate, c_gate, w_conv):
    u = c_gate * xin
    ch = u.shape[-1]
    y = lax.conv_general_dilated(
        u, w_conv.reshape(CONV_K, 1, ch), window_strides=(1,),
        padding=[(CONV_K - 1, 0)], dimension_numbers=("NWC", "WIO", "NWC"),
        feature_group_count=ch)
    return b_gate * y


def moba_attention(q, k, v):
    b, h, s, hd = q.shape
    n_blk = -(-s // MOBA_BLOCK)
    pad = n_blk * MOBA_BLOCK - s
    kb = jnp.pad(k, ((0, 0), (0, 0), (0, pad), (0, 0))).reshape(b, h, n_blk, MOBA_BLOCK, hd)
    vb = jnp.pad(v, ((0, 0), (0, 0), (0, pad), (0, 0))).reshape(b, h, n_blk, MOBA_BLOCK, hd)
    kbar = jnp.mean(kb.astype(jnp.float32), axis=3)
    q_blk = jnp.arange(s) // MOBA_BLOCK
    gate = jnp.einsum("bhsd,bhnd->bhsn", q.astype(jnp.float32), kbar)
    past = jnp.arange(n_blk)[None, :] < q_blk[:, None]
    gate = jnp.where(past, gate, MASK_VALUE)
    if n_blk > 1:
        _, sel = lax.top_k(gate, min(MOBA_TOPK, n_blk - 1))
    else:
        sel = jnp.zeros((b, h, s, 1), dtype=jnp.int32)
    n_sel = sel.shape[-1]
    scale = hd ** -0.5
    n_chunk = s // MOBA_Q_CHUNK
    q_c = q.reshape(b, h, n_chunk, MOBA_Q_CHUNK, hd).transpose(2, 0, 1, 3, 4)
    sel_c = sel.reshape(b, h, n_chunk, MOBA_Q_CHUNK, n_sel).transpose(2, 0, 1, 3, 4)
    gather_blocks = jax.vmap(jax.vmap(lambda t, i: t[i]))

    def step(args):
        c, qc, selc = args
        start = c * MOBA_Q_CHUNK
        b_own = start // MOBA_BLOCK
        k_own = lax.dynamic_index_in_dim(kb, b_own, axis=2, keepdims=False)
        v_own = lax.dynamic_index_in_dim(vb, b_own, axis=2, keepdims=False)
        q_pos = start + jnp.arange(MOBA_Q_CHUNK)
        k_pos = b_own * MOBA_BLOCK + jnp.arange(MOBA_BLOCK)
        s_own = jnp.einsum("bhqd,bhkd->bhqk", qc, k_own).astype(jnp.float32) * scale
        s_own = jnp.where(k_pos[None, :] <= q_pos[:, None], s_own, MASK_VALUE)
        k_sel = gather_blocks(kb, selc)
        v_sel = gather_blocks(vb, selc)
        s_sel = jnp.einsum("bhqd,bhqnkd->bhqnk", qc, k_sel).astype(jnp.float32) * scale
        ok = selc < b_own
        s_sel = jnp.where(ok[..., None], s_sel, MASK_VALUE)
        scores = jnp.concatenate(
            [s_sel.reshape(b, h, MOBA_Q_CHUNK, n_sel * MOBA_BLOCK), s_own], axis=-1)
        p = jax.nn.softmax(scores, axis=-1).astype(v.dtype)
        p_sel = p[..., :n_sel * MOBA_BLOCK].reshape(b, h, MOBA_Q_CHUNK, n_sel, MOBA_BLOCK)
        p_own = p[..., n_sel * MOBA_BLOCK:]
        return (jnp.einsum("bhqnk,bhqnkd->bhqd", p_sel, v_sel)
                + jnp.einsum("bhqk,bhkd->bhqd", p_own, v_own))

    out = lax.map(step, (jnp.arange(n_chunk), q_c, sel_c))
    return out.transpose(1, 2, 0, 3, 4).reshape(b, h, s, hd)


def memory_cross_attention(a, mem_n, w_q, w_kv, w_o):
    b, s, d = a.shape
    q = (a @ w_q).reshape(b, s, XATTN_HEADS, XATTN_HEAD_DIM)
    k, v = jnp.split(mem_n @ w_kv, 2, axis=-1)
    k = k.reshape(b, -1, XATTN_HEADS, XATTN_HEAD_DIM)
    v = v.reshape(b, -1, XATTN_HEADS, XATTN_HEAD_DIM)
    scores = jnp.einsum("bshd,bmhd->bhsm", q, k).astype(jnp.float32) * (XATTN_HEAD_DIM ** -0.5)
    p = jax.nn.softmax(scores, axis=-1).astype(v.dtype)
    o = jnp.einsum("bhsm,bmhd->bshd", p, v).reshape(b, s, d)
    return o @ w_o


def peer_ffn(a, w_pq, sub_keys, u, v):
    b, s, d = a.shape
    q = (a @ w_pq).reshape(b, s, PEER_HEADS, 2, PEER_HALF)
    s_half = jnp.einsum("bshpd,hpnd->bshpn", q, sub_keys).astype(jnp.float32)
    v_half, i_half = lax.top_k(s_half, PEER_TOPK)
    cand = v_half[..., 0, :, None] + v_half[..., 1, None, :]
    cand_idx = i_half[..., 0, :, None] * PEER_N_KEYS + i_half[..., 1, None, :]
    cand = cand.reshape(b, s, PEER_HEADS, PEER_TOPK * PEER_TOPK)
    cand_idx = cand_idx.reshape(b, s, PEER_HEADS, PEER_TOPK * PEER_TOPK)
    top_s, pos = lax.top_k(cand, PEER_TOPK)
    expert = jnp.take_along_axis(cand_idx, pos, axis=-1)
    g = jax.nn.softmax(top_s, axis=-1).astype(a.dtype)
    n_tok = b * s
    n_chunk = n_tok // PEER_TOKEN_CHUNK
    n_sel = PEER_HEADS * PEER_TOPK
    x_c = a.reshape(n_chunk, PEER_TOKEN_CHUNK, d)
    e_c = expert.reshape(n_chunk, PEER_TOKEN_CHUNK, n_sel)
    g_c = g.reshape(n_chunk, PEER_TOKEN_CHUNK, n_sel)

    def step(args):
        xc, ec, gc = args
        act = jax.nn.gelu(jnp.einsum("cd,ced->ce", xc, u[ec]))
        return jnp.einsum("ce,ced->cd", gc * act, v[ec])

    return lax.map(step, (x_c, e_c, g_c)).reshape(b, s, d)


def setup_inputs(seed: int = 0) -> dict:
    key = jax.random.key(seed)
    ks = jax.random.split(key, 20)
    f32 = jnp.float32
    nrm = lambda k, shape, sc: jax.random.normal(k, shape, f32) * sc
    gain = lambda k, shape: 1.0 + 0.02 * jax.random.normal(k, shape, f32)
    L, D = DEPTH, D_MODEL
    return {
        "x": nrm(ks[0], (BATCH, SEQ, D), 1.0),
        "mem": nrm(ks[1], (BATCH, N_MEM, D), 1.0),
        "g_mix": gain(ks[2], (L, D)),
        "w_in": nrm(ks[3], (L, D, IN_PROJ_WIDTH), D ** -0.5),
        "w_conv": nrm(ks[4], (L, CONV_K, CONV_WIDTH), CONV_K ** -0.5),
        "w_conv_out": nrm(ks[5], (L, CONV_WIDTH, D), CONV_WIDTH ** -0.5),
        "w_attn_out": nrm(ks[6], (L, ATTN_WIDTH, D), ATTN_WIDTH ** -0.5),
        "w_merge": nrm(ks[7], (L, D, D), D ** -0.5),
        "g_xattn": gain(ks[8], (L, D)),
        "g_mem": gain(ks[9], (L, D)),
        "w_xq": nrm(ks[10], (L, D, D), D ** -0.5),
        "w_xkv": nrm(ks[11], (L, D, 2 * D), D ** -0.5),
        "w_xo": nrm(ks[12], (L, D, D), D ** -0.5),
        "g_ffn": gain(ks[13], (L, D)),
        "w_pq": nrm(ks[14], (L, D, PEER_HEADS * PEER_QUERY_DIM), D ** -0.5),
        "peer_sub_keys": nrm(ks[15], (L, PEER_HEADS, 2, PEER_N_KEYS, PEER_HALF), PEER_HALF ** -0.5),
        "peer_u": nrm(ks[16], (L, PEER_N_EXPERTS, D), D ** -0.5),
        "peer_v": nrm(ks[17], (L, PEER_N_EXPERTS, D), PEER_HEADS ** -0.5),
        "g_final": gain(ks[18], (D,)),
    }


def reference(x, mem, g_mix, w_in, w_conv, w_conv_out, w_attn_out, w_merge, g_xattn, g_mem,
              w_xq, w_xkv, w_xo, g_ffn, w_pq, peer_sub_keys, peer_u, peer_v, g_final):
    b, s, d = x.shape
    cw, aw = CONV_WIDTH, ATTN_WIDTH
    splits = [cw, 2 * cw, 3 * cw, 3 * cw + aw, 3 * cw + 2 * aw, 3 * cw + 3 * aw,
              3 * cw + 3 * aw + d]
    h = x
    for l in range(DEPTH):
        a = rmsnorm(h, g_mix[l])
        xin, b_gate, c_gate, q, k, v, gate_conv, gate_attn = jnp.split(a @ w_in[l], splits, axis=-1)
        y_conv = short_gated_conv(xin, b_gate, c_gate, w_conv[l]) @ w_conv_out[l]
        to_heads = lambda t: t.reshape(b, s, ATTN_HEADS, HEAD_DIM).transpose(0, 2, 1, 3)
        o = moba_attention(rope(to_heads(q)), rope(to_heads(k)), to_heads(v))
        y_attn = o.transpose(0, 2, 1, 3).reshape(b, s, aw) @ w_attn_out[l]
        merged = jax.nn.sigmoid(gate_conv) * y_conv + jax.nn.sigmoid(gate_attn) * y_attn
        h = h + merged @ w_merge[l]
        h = h + memory_cross_attention(rmsnorm(h, g_xattn[l]), rmsnorm(mem, g_mem[l]),
                                       w_xq[l], w_xkv[l], w_xo[l])
        h = h + peer_ffn(rmsnorm(h, g_ffn[l]), w_pq[l], peer_sub_keys[l], peer_u[l], peer_v[l])
    return rmsnorm(h, g_final)
```

```python
import functools

import jax
import jax.numpy as jnp
from jax import lax
from jax.experimental import pallas as pl
from jax.experimental.pallas import tpu as pltpu

f32 = jnp.float32
bf16 = jnp.bfloat16
i32 = jnp.int32

EPS = 1e-6
MASK_VALUE = -1e30
ROPE_THETA = 10000.0

XATTN_HEADS = 4
ATTN_HEADS = 8
HEAD_DIM = 64
MOBA_BLOCK = 256
MOBA_TOPK = 3
PEER_HEADS = 8
PEER_N_KEYS = 128
PEER_HALF = 128
PEER_TOPK = 16
N_SEL = PEER_HEADS * PEER_TOPK

LANES = 128
SUBLANES = 8
VMEM_LIMIT = 56 * 1024 * 1024

_NT = (((1,), (1,)), ((), ()))


def _params(n_axes=1, vmem=VMEM_LIMIT):
    return pltpu.CompilerParams(dimension_semantics=("arbitrary",) * n_axes, vmem_limit_bytes=vmem)


def _rms(x, g):
    return x * lax.rsqrt(jnp.mean(x * x, axis=-1, keepdims=True) + EPS) * g


def _inproj_kernel(x_ref, g_ref, w_ref, o_ref, *, chunk):
    a = _rms(x_ref[...], g_ref[...]).astype(bf16)
    for c in range(o_ref.shape[1] // chunk):
        o_ref[:, c * chunk:(c + 1) * chunk] = jnp.dot(
            a, w_ref[:, c * chunk:(c + 1) * chunk], preferred_element_type=f32)


def _inproj(x2, g, w, tm):
    n, d = x2.shape
    width = w.shape[1]
    return pl.pallas_call(
        functools.partial(_inproj_kernel, chunk=width // 4),
        grid=(n // tm,),
        in_specs=[pl.BlockSpec((tm, d), lambda i: (i, 0)),
                  pl.BlockSpec((1, d), lambda i: (0, 0)),
                  pl.BlockSpec((d, width), lambda i: (0, 0), pipeline_mode=pl.Buffered(1))],
        out_specs=pl.BlockSpec((tm, width), lambda i: (i, 0)),
        out_shape=jax.ShapeDtypeStruct((n, width), f32),
        compiler_params=_params(),
        name="inproj",
    )(x2, g, w)


def _moba_kernel(q_ref, k_ref, v_ref, cos_ref, sin_ref, o_ref,
                 qb_sc, kb_sc, vt_sc, bias_sc, ot_sc, *, nb, topk):
    blk = MOBA_BLOCK
    s = q_ref.shape[0]
    lane = lax.broadcasted_iota(i32, (s, LANES), 1)
    first_half = (lane % HEAD_DIM) < (HEAD_DIM // 2)
    cos = cos_ref[...]
    sin = sin_ref[...]

    def rope(t):
        partner = jnp.where(first_half, pltpu.roll(t, LANES - HEAD_DIM // 2, 1),
                            pltpu.roll(t, HEAD_DIM // 2, 1))
        return t * cos + partner * sin

    q2 = rope(q_ref[...])
    k2 = rope(k_ref[...])
    v2 = v_ref[...]
    scale = HEAD_DIM ** -0.5
    blk_of_q = lax.broadcasted_iota(i32, (nb, s), 1) // blk
    m_iota = lax.broadcasted_iota(i32, (nb, s), 0)
    past = m_iota < blk_of_q
    kpos = lax.broadcasted_iota(i32, (blk, blk), 0)
    qpos = lax.broadcasted_iota(i32, (blk, blk), 1)
    causal = kpos <= qpos

    for hh in range(LANES // HEAD_DIM):
        qh = q2[:, hh * HEAD_DIM:(hh + 1) * HEAD_DIM]
        kh = k2[:, hh * HEAD_DIM:(hh + 1) * HEAD_DIM]
        vh = v2[:, hh * HEAD_DIM:(hh + 1) * HEAD_DIM]
        kbar = jnp.mean(kh.reshape(nb, blk, HEAD_DIM), axis=1)
        gt = lax.dot_general(kbar, qh, _NT, precision=lax.Precision.HIGHEST,
                             preferred_element_type=f32)
        rows = []
        for n in range(nb):
            gn = gt[n:n + 1, :]
            beats = past & ((gt > gn) | ((gt == gn) & (m_iota < n)))
            cnt = jnp.sum(beats.astype(i32), axis=0, keepdims=True)
            sel = (blk_of_q[0:1, :] > n) & (cnt < topk)
            rows.append(jnp.where(sel, 0.0, MASK_VALUE).astype(f32))
        bias = jnp.concatenate(rows, axis=0)
        vt = jnp.transpose(vh).astype(bf16)
        for i in range(nb):
            bias_sc[i] = bias[:, i * blk:(i + 1) * blk]
            vt_sc[i] = vt[:, i * blk:(i + 1) * blk]
        qb_sc[...] = qh.astype(bf16)
        kb_sc[...] = kh.astype(bf16)

        def q_tile(i, carry):
            q0 = pl.multiple_of(i * blk, blk)
            qi = qb_sc[pl.ds(q0, blk), :]
            sd = lax.dot_general(kb_sc[pl.ds(q0, blk), :], qi, _NT, preferred_element_type=f32) * scale
            sd = jnp.where(causal, sd, MASK_VALUE)
            m0 = jnp.max(sd, axis=0, keepdims=True)
            p0 = jnp.exp(sd - m0)
            l0 = jnp.sum(p0, axis=0, keepdims=True)
            a0 = jnp.dot(vt_sc[i], p0.astype(bf16), preferred_element_type=f32)

            def kv_step(j, st):
                m, l, acc = st
                k0 = pl.multiple_of(j * blk, blk)
                sj = lax.dot_general(kb_sc[pl.ds(k0, blk), :], qi, _NT, preferred_element_type=f32) * scale
                sj = sj + bias_sc[i, pl.ds(j, 1), :]
                m_new = jnp.maximum(m, jnp.max(sj, axis=0, keepdims=True))
                alpha = jnp.exp(m - m_new)
                p = jnp.exp(sj - m_new)
                l = alpha * l + jnp.sum(p, axis=0, keepdims=True)
                acc = alpha * acc + jnp.dot(vt_sc[j], p.astype(bf16), preferred_element_type=f32)
                return m_new, l, acc

            _, l, acc = lax.fori_loop(0, i, kv_step, (m0, l0, a0))
            ot_sc[i] = acc * (1.0 / l)
            return carry

        lax.fori_loop(0, nb, q_tile, 0)
        for i in range(nb):
            o_ref[i * blk:(i + 1) * blk, hh * HEAD_DIM:(hh + 1) * HEAD_DIM] = jnp.transpose(ot_sc[i])


def _moba(proj, cos_t, sin_t, b, s):
    n = b * s
    nb = s // MOBA_BLOCK
    topk = min(MOBA_TOPK, nb - 1)
    aw = ATTN_HEADS * HEAD_DIM
    hp = aw // LANES
    qc, kc, vc = 3 * aw // LANES, 4 * aw // LANES, 5 * aw // LANES
    return pl.pallas_call(
        functools.partial(_moba_kernel, nb=nb, topk=topk),
        grid=(b, hp),
        in_specs=[pl.BlockSpec((s, LANES), lambda bi, h: (bi, qc + h)),
                  pl.BlockSpec((s, LANES), lambda bi, h: (bi, kc + h)),
                  pl.BlockSpec((s, LANES), lambda bi, h: (bi, vc + h)),
                  pl.BlockSpec((s, LANES), lambda bi, h: (0, 0)),
                  pl.BlockSpec((s, LANES), lambda bi, h: (0, 0))],
        out_specs=pl.BlockSpec((s, LANES), lambda bi, h: (bi, h)),
        out_shape=jax.ShapeDtypeStruct((n, aw), f32),
        scratch_shapes=[pltpu.VMEM((s, HEAD_DIM), bf16),
                        pltpu.VMEM((s, HEAD_DIM), bf16),
                        pltpu.VMEM((nb, HEAD_DIM, MOBA_BLOCK), bf16),
                        pltpu.VMEM((nb, nb, MOBA_BLOCK), f32),
                        pltpu.VMEM((nb, HEAD_DIM, MOBA_BLOCK), f32)],
        compiler_params=_params(2),
        name="moba",
    )(proj, proj, proj, cos_t, sin_t)


def _mix_kernel(xin_ref, bg_ref, cg_ref, pxin_ref, pcg_ref, gc_ref, ga_ref, o_ref, x_ref,
                wc_ref, wco_ref, wao_ref, wm_ref, h_ref, *, tiles_per_seq):
    tm = xin_ref.shape[0]
    i = pl.program_id(0)
    keep_prev = jnp.where(i % tiles_per_seq == 0, 0.0, 1.0).astype(f32)
    u = cg_ref[...] * xin_ref[...]
    pu = pcg_ref[...] * pxin_ref[...] * keep_prev
    rows = lax.broadcasted_iota(i32, u.shape, 0)
    u1 = jnp.where(rows == 0, pu[SUBLANES - 1:SUBLANES, :], pltpu.roll(u, 1, 0))
    u2 = jnp.where(rows == 0, pu[SUBLANES - 2:SUBLANES - 1, :],
                   jnp.where(rows == 1, pu[SUBLANES - 1:SUBLANES, :], pltpu.roll(u, 2, 0)))
    y = wc_ref[0:1, :] * u2 + wc_ref[1:2, :] * u1 + wc_ref[2:3, :] * u
    y_conv = jnp.dot((bg_ref[...] * y).astype(bf16), wco_ref[...], preferred_element_type=f32)
    y_attn = jnp.dot(o_ref[...].astype(bf16), wao_ref[...], preferred_element_type=f32)
    merged = jax.nn.sigmoid(gc_ref[...]) * y_conv + jax.nn.sigmoid(ga_ref[...]) * y_attn
    h_ref[...] = x_ref[...] + jnp.dot(merged.astype(bf16), wm_ref[...], preferred_element_type=f32)


def _mix(proj, o, x2, w_conv, wco, wao, wm, s, tm):
    n, d = x2.shape
    cw = wco.shape[0]
    aw = wao.shape[0]
    rb = tm // SUBLANES
    prev = lambda col: (lambda i: (jnp.maximum(i * rb - 1, 0), col))
    gcol = 3 * cw + 3 * aw
    return pl.pallas_call(
        functools.partial(_mix_kernel, tiles_per_seq=s // tm),
        grid=(n // tm,),
        in_specs=[pl.BlockSpec((tm, cw), lambda i: (i, 0)),
                  pl.BlockSpec((tm, cw), lambda i: (i, 1)),
                  pl.BlockSpec((tm, cw), lambda i: (i, 2)),
                  pl.BlockSpec((SUBLANES, cw), prev(0)),
                  pl.BlockSpec((SUBLANES, cw), prev(2)),
                  pl.BlockSpec((tm, d), lambda i: (i, gcol // d)),
                  pl.BlockSpec((tm, d), lambda i: (i, gcol // d + 1)),
                  pl.BlockSpec((tm, aw), lambda i: (i, 0)),
                  pl.BlockSpec((tm, d), lambda i: (i, 0)),
                  pl.BlockSpec(w_conv.shape, lambda i: (0, 0)),
                  pl.BlockSpec(wco.shape, lambda i: (0, 0)),
                  pl.BlockSpec(wao.shape, lambda i: (0, 0)),
                  pl.BlockSpec(wm.shape, lambda i: (0, 0))],
        out_specs=pl.BlockSpec((tm, d), lambda i: (i, 0)),
        out_shape=jax.ShapeDtypeStruct((n, d), f32),
        compiler_params=_params(),
        name="mix",
    )(proj, proj, proj, proj, proj, proj, proj, o, x2, w_conv, wco, wao, wm)


def _memkv_kernel(m_ref, g_ref, w_ref, o_ref):
    a = _rms(m_ref[...], g_ref[...]).astype(bf16)
    o_ref[...] = jnp.dot(a, w_ref[...], preferred_element_type=f32).astype(bf16)


def _memkv(mem2, g, w, n_mem):
    nm, d = mem2.shape
    return pl.pallas_call(
        _memkv_kernel,
        grid=(nm // n_mem,),
        in_specs=[pl.BlockSpec((n_mem, d), lambda i: (i, 0)),
                  pl.BlockSpec((1, d), lambda i: (0, 0)),
                  pl.BlockSpec(w.shape, lambda i: (0, 0))],
        out_specs=pl.BlockSpec((n_mem, w.shape[1]), lambda i: (i, 0)),
        out_shape=jax.ShapeDtypeStruct((nm, w.shape[1]), bf16),
        compiler_params=_params(),
        name="memkv",
    )(mem2, g, w)


def _xattn_kernel(h_ref, g_ref, kv_ref, wq_ref, wo_ref, o_ref):
    h = h_ref[...]
    d = h.shape[1]
    hd = d // XATTN_HEADS
    q = jnp.dot(_rms(h, g_ref[...]).astype(bf16), wq_ref[...], preferred_element_type=f32)
    outs = []
    for hh in range(XATTN_HEADS):
        qh = q[:, hh * hd:(hh + 1) * hd].astype(bf16)
        kh = kv_ref[:, hh * hd:(hh + 1) * hd]
        vh = kv_ref[:, d + hh * hd:d + (hh + 1) * hd]
        sc = lax.dot_general(qh, kh, _NT, preferred_element_type=f32) * (hd ** -0.5)
        sc = sc - jnp.max(sc, axis=-1, keepdims=True)
        p = jnp.exp(sc)
        p = p / jnp.sum(p, axis=-1, keepdims=True)
        outs.append(jnp.dot(p.astype(bf16), vh, preferred_element_type=f32).astype(bf16))
    o = jnp.concatenate(outs, axis=-1)
    o_ref[...] = h + jnp.dot(o, wo_ref[...], preferred_element_type=f32)


def _xattn(h1, g, kv, wq, wo, s, n_mem, tm):
    n, d = h1.shape
    tps = s // tm
    return pl.pallas_call(
        _xattn_kernel,
        grid=(n // tm,),
        in_specs=[pl.BlockSpec((tm, d), lambda i: (i, 0)),
                  pl.BlockSpec((1, d), lambda i: (0, 0)),
                  pl.BlockSpec((n_mem, 2 * d), lambda i: (i // tps, 0)),
                  pl.BlockSpec(wq.shape, lambda i: (0, 0)),
                  pl.BlockSpec(wo.shape, lambda i: (0, 0))],
        out_specs=pl.BlockSpec((tm, d), lambda i: (i, 0)),
        out_shape=jax.ShapeDtypeStruct((n, d), f32),
        compiler_params=_params(),
        name="xattn",
    )(h1, g, kv, wq, wo)


def _topk_cols(sc, k):
    r = sc.shape[0]
    ridx = lax.broadcasted_iota(i32, sc.shape, 0)
    vals, idxs = [], []
    for _ in range(k):
        m = jnp.max(sc, axis=0, keepdims=True)
        im = jnp.min(jnp.where(sc == m, ridx, r), axis=0, keepdims=True)
        vals.append(m)
        idxs.append(im)
        sc = jnp.where(ridx == im, -jnp.inf, sc)
    return jnp.concatenate(vals, axis=0), jnp.concatenate(idxs, axis=0)


def _select_kernel(h_ref, g_ref, wq_ref, keys_ref, a_ref, ids_ref, gate_ref):
    k = PEER_TOPK
    a = _rms(h_ref[...], g_ref[...])
    a_ref[...] = a
    q = jnp.dot(a.astype(bf16), wq_ref[...], preferred_element_type=f32)
    id_rows, gate_rows = [], []
    for h in range(PEER_HEADS):
        tops = []
        for p in range(2):
            c0 = (h * 2 + p) * PEER_HALF
            qhp = q[:, c0:c0 + PEER_HALF].astype(bf16)
            st = lax.dot_general(keys_ref[h, p], qhp, _NT, preferred_element_type=f32)
            tops.append(_topk_cols(st, k))
        (v0, i0), (v1, i1) = tops
        cand = jnp.concatenate([v0[a_:a_ + 1, :] + v1 for a_ in range(k)], axis=0)
        top_s, pos = _topk_cols(cand, k)
        pa = pos // k
        pb = pos % k
        e0 = jnp.zeros_like(pos)
        e1 = jnp.zeros_like(pos)
        for a_ in range(k):
            e0 = jnp.where(pa == a_, i0[a_:a_ + 1, :], e0)
            e1 = jnp.where(pb == a_, i1[a_:a_ + 1, :], e1)
        id_rows.append(e0 * PEER_N_KEYS + e1)
        ex = jnp.exp(top_s - top_s[0:1, :])
        gate_rows.append(ex / jnp.sum(ex, axis=0, keepdims=True))
    ids_t = jnp.concatenate(id_rows, axis=0).astype(f32)
    ids_ref[...] = jnp.transpose(ids_t).astype(i32)
    gate_ref[...] = jnp.transpose(jnp.concatenate(gate_rows, axis=0))


def _select(h2, g, wq, keys, tm):
    n, d = h2.shape
    return pl.pallas_call(
        _select_kernel,
        grid=(n // tm,),
        in_specs=[pl.BlockSpec((tm, d), lambda i: (i, 0)),
                  pl.BlockSpec((1, d), lambda i: (0, 0)),
                  pl.BlockSpec(wq.shape, lambda i: (0, 0)),
                  pl.BlockSpec(keys.shape, lambda i: (0, 0, 0, 0))],
        out_specs=[pl.BlockSpec((tm, d), lambda i: (i, 0)),
                   pl.BlockSpec((tm, N_SEL), lambda i: (i, 0)),
                   pl.BlockSpec((tm, N_SEL), lambda i: (i, 0))],
        out_shape=[jax.ShapeDtypeStruct((n, d), f32),
                   jax.ShapeDtypeStruct((n, N_SEL), i32),
                   jax.ShapeDtypeStruct((n, N_SEL), f32)],
        compiler_params=_params(),
        name="peer_select",
    )(h2, g, wq, keys)


def _gelu_tanh(x):
    cdf = 0.5 * (1.0 + jnp.tanh(0.7978845608028654 * (x + 0.044715 * (x * x * x))))
    return x * cdf


def _peer_u_kernel(ids_ref, x_ref, gate_ref, u_ref, w_ref):
    tt = x_ref.shape[0]
    lane = lax.broadcasted_iota(i32, (SUBLANES, LANES), 1)

    def token(t, carry):
        xt = x_ref[t]
        m = jnp.zeros((SUBLANES, LANES), f32)
        for e in range(N_SEL):
            z = u_ref[ids_ref[t, e]].astype(f32) * xt
            m = jnp.where(lane == e, jnp.sum(z, axis=1, keepdims=True), m)
        act = jnp.sum(m, axis=0, keepdims=True)
        w_ref[pl.ds(t, 1), :] = gate_ref[pl.ds(t, 1), :] * _gelu_tanh(act)
        return carry

    lax.fori_loop(0, tt, token, 0)


def _peer_u(ids, a3, gate, u_tab, tt):
    n = ids.shape[0]
    return pl.pallas_call(
        _peer_u_kernel,
        grid=(n // tt,),
        in_specs=[pl.BlockSpec((tt, N_SEL), lambda i: (i, 0), memory_space=pltpu.SMEM),
                  pl.BlockSpec((tt, SUBLANES, LANES), lambda i: (i, 0, 0)),
                  pl.BlockSpec((tt, N_SEL), lambda i: (i, 0)),
                  pl.BlockSpec(u_tab.shape, lambda i: (0, 0, 0), pipeline_mode=pl.Buffered(1))],
        out_specs=pl.BlockSpec((tt, N_SEL), lambda i: (i, 0)),
        out_shape=jax.ShapeDtypeStruct((n, N_SEL), f32),
        compiler_params=_params(),
        name="peer_u",
    )(ids, a3, gate, u_tab)


def _peer_v_kernel(ids_ref, w_ref, v_ref, o_ref):
    tt = o_ref.shape[0]
    n_acc = 4

    def token(t, carry):
        accs = [jnp.zeros((SUBLANES, LANES), f32) for _ in range(n_acc)]
        for e in range(N_SEL):
            accs[e % n_acc] = accs[e % n_acc] + v_ref[ids_ref[t, e]].astype(f32) * w_ref[t, e]
        o_ref[t] = (accs[0] + accs[1]) + (accs[2] + accs[3])
        return carry

    lax.fori_loop(0, tt, token, 0)


def _peer_v(ids, w, v_tab, tt):
    n = ids.shape[0]
    return pl.pallas_call(
        _peer_v_kernel,
        grid=(n // tt,),
        in_specs=[pl.BlockSpec((tt, N_SEL), lambda i: (i, 0), memory_space=pltpu.SMEM),
                  pl.BlockSpec((tt, N_SEL), lambda i: (i, 0), memory_space=pltpu.SMEM),
                  pl.BlockSpec(v_tab.shape, lambda i: (0, 0, 0), pipeline_mode=pl.Buffered(1))],
        out_specs=pl.BlockSpec((tt, SUBLANES, LANES), lambda i: (i, 0, 0)),
        out_shape=jax.ShapeDtypeStruct((n, SUBLANES, LANES), f32),
        compiler_params=_params(),
        name="peer_v",
    )(ids, w, v_tab)


def _final_kernel(h_ref, p_ref, g_ref, o_ref):
    o_ref[...] = _rms(h_ref[...] + p_ref[...], g_ref[...])


def _final(h2, peer, g, tm):
    n, d = h2.shape
    return pl.pallas_call(
        _final_kernel,
        grid=(n // tm,),
        in_specs=[pl.BlockSpec((tm, d), lambda i: (i, 0)),
                  pl.BlockSpec((tm, d), lambda i: (i, 0)),
                  pl.BlockSpec((1, d), lambda i: (0, 0))],
        out_specs=pl.BlockSpec((tm, d), lambda i: (i, 0)),
        out_shape=jax.ShapeDtypeStruct((n, d), f32),
        compiler_params=_params(),
        name="final_norm",
    )(h2, peer, g)


def _rope_tables(s):
    half = HEAD_DIM // 2
    inv = ROPE_THETA ** (-jnp.arange(half, dtype=f32) / half)
    ang = jnp.arange(s, dtype=f32)[:, None] * inv[None, :]
    cos, sin = jnp.cos(ang), jnp.sin(ang)
    reps = LANES // HEAD_DIM
    cos_t = jnp.tile(jnp.concatenate([cos, cos], axis=1), (1, reps))
    sin_t = jnp.tile(jnp.concatenate([-sin, sin], axis=1), (1, reps))
    return cos_t, sin_t


def _row_tile(s, want):
    t = min(want, s)
    while s % t:
        t //= 2
    assert t % SUBLANES == 0
    return t


def kernel(x, mem, g_mix, w_in, w_conv, w_conv_out, w_attn_out, w_merge, g_xattn, g_mem,
           w_xq, w_xkv, w_xo, g_ffn, w_pq, peer_sub_keys, peer_u, peer_v, g_final):
    b, s, d = x.shape
    n = b * s
    n_mem = mem.shape[1]
    assert w_in.shape[0] == 1, "single-layer trunk only"
    assert d == SUBLANES * LANES and s % MOBA_BLOCK == 0
    n_exp = peer_u.shape[1]
    cos_t, sin_t = _rope_tables(s)
    tm = _row_tile(s, 512)
    h = x.reshape(n, d)
    mem2 = mem.reshape(b * n_mem, d)
    l = 0
    proj = _inproj(h, g_mix[l][None, :], w_in[l].astype(bf16), tm)
    o = _moba(proj, cos_t, sin_t, b, s)
    h1 = _mix(proj, o, h, w_conv[l], w_conv_out[l].astype(bf16), w_attn_out[l].astype(bf16),
              w_merge[l].astype(bf16), s, tm)
    kv = _memkv(mem2, g_mem[l][None, :], w_xkv[l].astype(bf16), n_mem)
    h2 = _xattn(h1, g_xattn[l][None, :], kv, w_xq[l].astype(bf16), w_xo[l].astype(bf16), s, n_mem, tm)
    a3, ids, gate = _select(h2, g_ffn[l][None, :], w_pq[l].astype(bf16),
                            peer_sub_keys[l].astype(bf16), _row_tile(s, 256))
    u_tab = peer_u[l].astype(bf16).reshape(n_exp, SUBLANES, LANES)
    v_tab = peer_v[l].astype(bf16).reshape(n_exp, SUBLANES, LANES)
    tt = _row_tile(s, 64)
    w = _peer_u(ids, a3.reshape(n, SUBLANES, LANES), gate, u_tab, tt)
    peer = _peer_v(ids, w, v_tab, tt).reshape(n, d)
    return _final(h2, peer, g_final[None, :], tm).reshape(b, s, d)
```

```python
import functools

import jax
import jax.numpy as jnp
from jax import lax
from jax.experimental import pallas as pl
from jax.experimental.pallas import tpu as pltpu

f32 = jnp.float32
bf16 = jnp.bfloat16
i32 = jnp.int32

EPS = 1e-6
MASK_VALUE = -1e30
ROPE_THETA = 10000.0

XATTN_HEADS = 4
ATTN_HEADS = 8
HEAD_DIM = 64
MOBA_BLOCK = 256
MOBA_TOPK = 3
PEER_HEADS = 8
PEER_N_KEYS = 128
PEER_HALF = 128
PEER_TOPK = 16
N_SEL = PEER_HEADS * PEER_TOPK
ROWS_PER_EXPERT = 4

LANES = 128
SUBLANES = 8
VMEM_LIMIT = 56 * 1024 * 1024

_NT = (((1,), (1,)), ((), ()))


def _params(n_axes=1, vmem=VMEM_LIMIT):
    return pltpu.CompilerParams(dimension_semantics=("arbitrary",) * n_axes, vmem_limit_bytes=vmem)


def _rms(x, g):
    return x * lax.rsqrt(jnp.mean(x * x, axis=-1, keepdims=True) + EPS) * g


def _inproj_kernel(x_ref, g_ref, w_ref, o_ref, *, chunk):
    a = _rms(x_ref[...], g_ref[...]).astype(bf16)
    for c in range(o_ref.shape[1] // chunk):
        o_ref[:, c * chunk:(c + 1) * chunk] = jnp.dot(
            a, w_ref[:, c * chunk:(c + 1) * chunk], preferred_element_type=f32)


def _inproj(x2, g, w, tm):
    n, d = x2.shape
    width = w.shape[1]
    return pl.pallas_call(
        functools.partial(_inproj_kernel, chunk=width // 4),
        grid=(n // tm,),
        in_specs=[pl.BlockSpec((tm, d), lambda i: (i, 0)),
                  pl.BlockSpec((1, d), lambda i: (0, 0)),
                  pl.BlockSpec((d, width), lambda i: (0, 0), pipeline_mode=pl.Buffered(1))],
        out_specs=pl.BlockSpec((tm, width), lambda i: (i, 0)),
        out_shape=jax.ShapeDtypeStruct((n, width), f32),
        compiler_params=_params(),
        name="inproj",
    )(x2, g, w)


def _moba_kernel(q_ref, k_ref, v_ref, cos_ref, sin_ref, o_ref,
                 qb_sc, kb_sc, vt_sc, bias_sc, ot_sc, *, nb, topk):
    blk = MOBA_BLOCK
    s = q_ref.shape[0]
    lane = lax.broadcasted_iota(i32, (s, LANES), 1)
    first_half = (lane % HEAD_DIM) < (HEAD_DIM // 2)
    cos = cos_ref[...]
    sin = sin_ref[...]

    def rope(t):
        partner = jnp.where(first_half, pltpu.roll(t, LANES - HEAD_DIM // 2, 1),
                            pltpu.roll(t, HEAD_DIM // 2, 1))
        return t * cos + partner * sin

    q2 = rope(q_ref[...])
    k2 = rope(k_ref[...])
    v2 = v_ref[...]
    scale = HEAD_DIM ** -0.5
    blk_of_q = lax.broadcasted_iota(i32, (nb, s), 1) // blk
    m_iota = lax.broadcasted_iota(i32, (nb, s), 0)
    past = m_iota < blk_of_q
    kpos = lax.broadcasted_iota(i32, (blk, blk), 0)
    qpos = lax.broadcasted_iota(i32, (blk, blk), 1)
    causal = kpos <= qpos

    for hh in range(LANES // HEAD_DIM):
        qh = q2[:, hh * HEAD_DIM:(hh + 1) * HEAD_DIM]
        kh = k2[:, hh * HEAD_DIM:(hh + 1) * HEAD_DIM]
        vh = v2[:, hh * HEAD_DIM:(hh + 1) * HEAD_DIM]
        kbar = jnp.mean(kh.reshape(nb, blk, HEAD_DIM), axis=1)
        gt = lax.dot_general(kbar, qh, _NT, precision=lax.Precision.HIGHEST,
                             preferred_element_type=f32)
        rows = []
        for n in range(nb):
            gn = gt[n:n + 1, :]
            beats = past & ((gt > gn) | ((gt == gn) & (m_iota < n)))
            cnt = jnp.sum(beats.astype(i32), axis=0, keepdims=True)
            sel = (blk_of_q[0:1, :] > n) & (cnt < topk)
            rows.append(jnp.where(sel, 0.0, MASK_VALUE).astype(f32))
        bias = jnp.concatenate(rows, axis=0)
        vt = jnp.transpose(vh).astype(bf16)
        for i in range(nb):
            bias_sc[i] = bias[:, i * blk:(i + 1) * blk]
            vt_sc[i] = vt[:, i * blk:(i + 1) * blk]
        qb_sc[...] = qh.astype(bf16)
        kb_sc[...] = kh.astype(bf16)

        def q_tile(i, carry):
            q0 = pl.multiple_of(i * blk, blk)
            qi = qb_sc[pl.ds(q0, blk), :]
            sd = lax.dot_general(kb_sc[pl.ds(q0, blk), :], qi, _NT, preferred_element_type=f32) * scale
            sd = jnp.where(causal, sd, MASK_VALUE)
            m0 = jnp.max(sd, axis=0, keepdims=True)
            p0 = jnp.exp(sd - m0)
            l0 = jnp.sum(p0, axis=0, keepdims=True)
            a0 = jnp.dot(vt_sc[i], p0.astype(bf16), preferred_element_type=f32)

            def kv_step(j, st):
                m, l, acc = st
                k0 = pl.multiple_of(j * blk, blk)
                sj = lax.dot_general(kb_sc[pl.ds(k0, blk), :], qi, _NT, preferred_element_type=f32) * scale
                sj = sj + bias_sc[i, pl.ds(j, 1), :]
                m_new = jnp.maximum(m, jnp.max(sj, axis=0, keepdims=True))
                alpha = jnp.exp(m - m_new)
                p = jnp.exp(sj - m_new)
                l = alpha * l + jnp.sum(p, axis=0, keepdims=True)
                acc = alpha * acc + jnp.dot(vt_sc[j], p.astype(bf16), preferred_element_type=f32)
                return m_new, l, acc

            _, l, acc = lax.fori_loop(0, i, kv_step, (m0, l0, a0))
            ot_sc[i] = acc * (1.0 / l)
            return carry

        lax.fori_loop(0, nb, q_tile, 0)
        for i in range(nb):
            o_ref[i * blk:(i + 1) * blk, hh * HEAD_DIM:(hh + 1) * HEAD_DIM] = jnp.transpose(ot_sc[i])


def _moba(proj, cos_t, sin_t, b, s):
    n = b * s
    nb = s // MOBA_BLOCK
    topk = min(MOBA_TOPK, nb - 1)
    aw = ATTN_HEADS * HEAD_DIM
    hp = aw // LANES
    qc, kc, vc = 3 * aw // LANES, 4 * aw // LANES, 5 * aw // LANES
    return pl.pallas_call(
        functools.partial(_moba_kernel, nb=nb, topk=topk),
        grid=(b, hp),
        in_specs=[pl.BlockSpec((s, LANES), lambda bi, h: (bi, qc + h)),
                  pl.BlockSpec((s, LANES), lambda bi, h: (bi, kc + h)),
                  pl.BlockSpec((s, LANES), lambda bi, h: (bi, vc + h)),
                  pl.BlockSpec((s, LANES), lambda bi, h: (0, 0)),
                  pl.BlockSpec((s, LANES), lambda bi, h: (0, 0))],
        out_specs=pl.BlockSpec((s, LANES), lambda bi, h: (bi, h)),
        out_shape=jax.ShapeDtypeStruct((n, aw), f32),
        scratch_shapes=[pltpu.VMEM((s, HEAD_DIM), bf16),
                        pltpu.VMEM((s, HEAD_DIM), bf16),
                        pltpu.VMEM((nb, HEAD_DIM, MOBA_BLOCK), bf16),
                        pltpu.VMEM((nb, nb, MOBA_BLOCK), f32),
                        pltpu.VMEM((nb, HEAD_DIM, MOBA_BLOCK), f32)],
        compiler_params=_params(2),
        name="moba",
    )(proj, proj, proj, cos_t, sin_t)


def _mix_kernel(xin_ref, bg_ref, cg_ref, pxin_ref, pcg_ref, gc_ref, ga_ref, o_ref, x_ref,
                wc_ref, wco_ref, wao_ref, wm_ref, h_ref, *, tiles_per_seq):
    tm = xin_ref.shape[0]
    i = pl.program_id(0)
    keep_prev = jnp.where(i % tiles_per_seq == 0, 0.0, 1.0).astype(f32)
    u = cg_ref[...] * xin_ref[...]
    pu = pcg_ref[...] * pxin_ref[...] * keep_prev
    rows = lax.broadcasted_iota(i32, u.shape, 0)
    u1 = jnp.where(rows == 0, pu[SUBLANES - 1:SUBLANES, :], pltpu.roll(u, 1, 0))
    u2 = jnp.where(rows == 0, pu[SUBLANES - 2:SUBLANES - 1, :],
                   jnp.where(rows == 1, pu[SUBLANES - 1:SUBLANES, :], pltpu.roll(u, 2, 0)))
    y = wc_ref[0:1, :] * u2 + wc_ref[1:2, :] * u1 + wc_ref[2:3, :] * u
    y_conv = jnp.dot((bg_ref[...] * y).astype(bf16), wco_ref[...], preferred_element_type=f32)
    y_attn = jnp.dot(o_ref[...].astype(bf16), wao_ref[...], preferred_element_type=f32)
    merged = jax.nn.sigmoid(gc_ref[...]) * y_conv + jax.nn.sigmoid(ga_ref[...]) * y_attn
    h_ref[...] = x_ref[...] + jnp.dot(merged.astype(bf16), wm_ref[...], preferred_element_type=f32)


def _mix(proj, o, x2, w_conv, wco, wao, wm, s, tm):
    n, d = x2.shape
    cw = wco.shape[0]
    aw = wao.shape[0]
    rb = tm // SUBLANES
    prev = lambda col: (lambda i: (jnp.maximum(i * rb - 1, 0), col))
    gcol = 3 * cw + 3 * aw
    return pl.pallas_call(
        functools.partial(_mix_kernel, tiles_per_seq=s // tm),
        grid=(n // tm,),
        in_specs=[pl.BlockSpec((tm, cw), lambda i: (i, 0)),
                  pl.BlockSpec((tm, cw), lambda i: (i, 1)),
                  pl.BlockSpec((tm, cw), lambda i: (i, 2)),
                  pl.BlockSpec((SUBLANES, cw), prev(0)),
                  pl.BlockSpec((SUBLANES, cw), prev(2)),
                  pl.BlockSpec((tm, d), lambda i: (i, gcol // d)),
                  pl.BlockSpec((tm, d), lambda i: (i, gcol // d + 1)),
                  pl.BlockSpec((tm, aw), lambda i: (i, 0)),
                  pl.BlockSpec((tm, d), lambda i: (i, 0)),
                  pl.BlockSpec(w_conv.shape, lambda i: (0, 0)),
                  pl.BlockSpec(wco.shape, lambda i: (0, 0)),
                  pl.BlockSpec(wao.shape, lambda i: (0, 0)),
                  pl.BlockSpec(wm.shape, lambda i: (0, 0))],
        out_specs=pl.BlockSpec((tm, d), lambda i: (i, 0)),
        out_shape=jax.ShapeDtypeStruct((n, d), f32),
        compiler_params=_params(),
        name="mix",
    )(proj, proj, proj, proj, proj, proj, proj, o, x2, w_conv, wco, wao, wm)


def _memkv_kernel(m_ref, g_ref, w_ref, o_ref):
    a = _rms(m_ref[...], g_ref[...]).astype(bf16)
    o_ref[...] = jnp.dot(a, w_ref[...], preferred_element_type=f32).astype(bf16)


def _memkv(mem2, g, w, n_mem):
    nm, d = mem2.shape
    return pl.pallas_call(
        _memkv_kernel,
        grid=(nm // n_mem,),
        in_specs=[pl.BlockSpec((n_mem, d), lambda i: (i, 0)),
                  pl.BlockSpec((1, d), lambda i: (0, 0)),
                  pl.BlockSpec(w.shape, lambda i: (0, 0))],
        out_specs=pl.BlockSpec((n_mem, w.shape[1]), lambda i: (i, 0)),
        out_shape=jax.ShapeDtypeStruct((nm, w.shape[1]), bf16),
        compiler_params=_params(),
        name="memkv",
    )(mem2, g, w)


def _xattn_kernel(h_ref, g_ref, kv_ref, wq_ref, wo_ref, o_ref):
    h = h_ref[...]
    d = h.shape[1]
    hd = d // XATTN_HEADS
    q = jnp.dot(_rms(h, g_ref[...]).astype(bf16), wq_ref[...], preferred_element_type=f32)
    outs = []
    for hh in range(XATTN_HEADS):
        qh = q[:, hh * hd:(hh + 1) * hd].astype(bf16)
        kh = kv_ref[:, hh * hd:(hh + 1) * hd]
        vh = kv_ref[:, d + hh * hd:d + (hh + 1) * hd]
        sc = lax.dot_general(qh, kh, _NT, preferred_element_type=f32) * (hd ** -0.5)
        sc = sc - jnp.max(sc, axis=-1, keepdims=True)
        p = jnp.exp(sc)
        p = p / jnp.sum(p, axis=-1, keepdims=True)
        outs.append(jnp.dot(p.astype(bf16), vh, preferred_element_type=f32).astype(bf16))
    o = jnp.concatenate(outs, axis=-1)
    o_ref[...] = h + jnp.dot(o, wo_ref[...], preferred_element_type=f32)


def _xattn(h1, g, kv, wq, wo, s, n_mem, tm):
    n, d = h1.shape
    tps = s // tm
    return pl.pallas_call(
        _xattn_kernel,
        grid=(n // tm,),
        in_specs=[pl.BlockSpec((tm, d), lambda i: (i, 0)),
                  pl.BlockSpec((1, d), lambda i: (0, 0)),
                  pl.BlockSpec((n_mem, 2 * d), lambda i: (i // tps, 0)),
                  pl.BlockSpec(wq.shape, lambda i: (0, 0)),
                  pl.BlockSpec(wo.shape, lambda i: (0, 0))],
        out_specs=pl.BlockSpec((tm, d), lambda i: (i, 0)),
        out_shape=jax.ShapeDtypeStruct((n, d), f32),
        compiler_params=_params(),
        name="xattn",
    )(h1, g, kv, wq, wo)


def _topk_cols(sc, k):
    r = sc.shape[0]
    ridx = lax.broadcasted_iota(i32, sc.shape, 0)
    vals, idxs = [], []
    for _ in range(k):
        m = jnp.max(sc, axis=0, keepdims=True)
        im = jnp.min(jnp.where(sc == m, ridx, r), axis=0, keepdims=True)
        vals.append(m)
        idxs.append(im)
        sc = jnp.where(ridx == im, -jnp.inf, sc)
    return jnp.concatenate(vals, axis=0), jnp.concatenate(idxs, axis=0)


def _select_kernel(h_ref, g_ref, wq_ref, keys_ref, a_ref, ids_ref, gate_ref):
    k = PEER_TOPK
    a = _rms(h_ref[...], g_ref[...])
    a_ref[...] = a
    q = jnp.dot(a.astype(bf16), wq_ref[...], preferred_element_type=f32)
    id_rows, gate_rows = [], []
    for h in range(PEER_HEADS):
        tops = []
        for p in range(2):
            c0 = (h * 2 + p) * PEER_HALF
            qhp = q[:, c0:c0 + PEER_HALF].astype(bf16)
            st = lax.dot_general(keys_ref[h, p], qhp, _NT, preferred_element_type=f32)
            tops.append(_topk_cols(st, k))
        (v0, i0), (v1, i1) = tops
        cand = jnp.concatenate([v0[a_:a_ + 1, :] + v1 for a_ in range(k)], axis=0)
        top_s, pos = _topk_cols(cand, k)
        pa = pos // k
        pb = pos % k
        e0 = jnp.zeros_like(pos)
        e1 = jnp.zeros_like(pos)
        for a_ in range(k):
            e0 = jnp.where(pa == a_, i0[a_:a_ + 1, :], e0)
            e1 = jnp.where(pb == a_, i1[a_:a_ + 1, :], e1)
        id_rows.append(e0 * PEER_N_KEYS + e1)
        ex = jnp.exp(top_s - top_s[0:1, :])
        gate_rows.append(ex / jnp.sum(ex, axis=0, keepdims=True))
    ids_t = jnp.concatenate(id_rows, axis=0) * ROWS_PER_EXPERT
    for j in range(ids_ref.shape[0]):
        ids_ref[j] = ids_t[:, j * LANES:(j + 1) * LANES]
    gate_ref[...] = jnp.transpose(jnp.concatenate(gate_rows, axis=0))


def _select(h2, g, wq, keys, tm):
    n, d = h2.shape
    return pl.pallas_call(
        _select_kernel,
        grid=(n // tm,),
        in_specs=[pl.BlockSpec((tm, d), lambda i: (i, 0)),
                  pl.BlockSpec((1, d), lambda i: (0, 0)),
                  pl.BlockSpec(wq.shape, lambda i: (0, 0)),
                  pl.BlockSpec(keys.shape, lambda i: (0, 0, 0, 0))],
        out_specs=[pl.BlockSpec((tm, d), lambda i: (i, 0)),
                   pl.BlockSpec((tm // LANES, N_SEL, LANES), lambda i: (i, 0, 0)),
                   pl.BlockSpec((tm, N_SEL), lambda i: (i, 0))],
        out_shape=[jax.ShapeDtypeStruct((n, d), f32),
                   jax.ShapeDtypeStruct((n // LANES, N_SEL, LANES), i32),
                   jax.ShapeDtypeStruct((n, N_SEL), f32)],
        compiler_params=_params(),
        name="peer_select",
    )(h2, g, wq, keys)


def _gelu_tanh(x):
    cdf = 0.5 * (1.0 + jnp.tanh(0.7978845608028654 * (x + 0.044715 * (x * x * x))))
    return x * cdf


def _pack_table(t):
    e, d = t.shape
    tb = t.astype(bf16).reshape(e * ROWS_PER_EXPERT, 2, LANES).transpose(0, 2, 1)
    return lax.bitcast_convert_type(tb, jnp.uint32)


def _expert_row(tab_ref, row0):
    words = tab_ref[pl.ds(pl.multiple_of(row0, ROWS_PER_EXPERT), ROWS_PER_EXPERT), :]
    return pltpu.bitcast(words, bf16).astype(f32)


def _token_id(ids_ref, e, t):
    return ids_ref.at[pl.ds(e * LANES, LANES)][t]


_FOLD_ORDER = (0, 4, 2, 6, 1, 5, 3, 7)


def _fold8(z):
    sub = lax.broadcasted_iota(i32, z[0].shape, 0)
    lo4 = sub < 4
    lo2 = (sub % 4) < 2
    even = (sub % 2) == 0
    c = [jnp.where(lo4, z[2 * k], z[2 * k + 1]) + pltpu.roll(jnp.where(lo4, z[2 * k + 1], z[2 * k]), 4, 0)
         for k in range(4)]
    e = [jnp.where(lo2, c[2 * k] + pltpu.roll(c[2 * k], 6, 0), c[2 * k + 1] + pltpu.roll(c[2 * k + 1], 2, 0))
         for k in range(2)]
    return jnp.where(even, e[0] + pltpu.roll(e[0], 7, 0), e[1] + pltpu.roll(e[1], 1, 0))


def _peer_u_kernel(ids_ref, x_ref, gate_ref, u_ref, w_ref, a_sc):
    tt = x_ref.shape[0]
    lane = lax.broadcasted_iota(i32, (N_SEL, LANES), 1)

    def finish(part, t):
        col = jnp.sum(part, axis=1, keepdims=True)
        a_sc[...] = jnp.where(lane == t, col, a_sc[...])

    def token(t, part_prev):
        finish(part_prev, t - 1)
        xt = x_ref[t]
        groups = []
        for g in range(N_SEL // SUBLANES):
            z = [_expert_row(u_ref, _token_id(ids_ref, g * SUBLANES + _FOLD_ORDER[k], t)) * xt
                 for k in range(SUBLANES)]
            groups.append(_fold8(z))
        return jnp.concatenate(groups, axis=0)

    a_sc[...] = jnp.zeros_like(a_sc)
    part = lax.fori_loop(0, tt, token, jnp.zeros((N_SEL, LANES), f32))
    finish(part, tt - 1)
    act = jnp.transpose(a_sc[...])[0:tt, :]
    w_ref[...] = gate_ref[...] * _gelu_tanh(act)


def _peer_u(ids, a3, gate, u_tab, tt):
    n = ids.shape[0] // N_SEL
    assert tt == LANES
    return pl.pallas_call(
        _peer_u_kernel,
        grid=(n // tt,),
        in_specs=[pl.BlockSpec((N_SEL * tt,), lambda i: (i,), memory_space=pltpu.SMEM,
                               pipeline_mode=pl.Buffered(1)),
                  pl.BlockSpec((tt, SUBLANES, LANES), lambda i: (i, 0, 0)),
                  pl.BlockSpec((tt, N_SEL), lambda i: (i, 0)),
                  pl.BlockSpec(u_tab.shape, lambda i: (0, 0), pipeline_mode=pl.Buffered(1))],
        out_specs=pl.BlockSpec((tt, N_SEL), lambda i: (i, 0)),
        out_shape=jax.ShapeDtypeStruct((n, N_SEL), f32),
        scratch_shapes=[pltpu.VMEM((N_SEL, LANES), f32)],
        compiler_params=_params(),
        name="peer_u",
    )(ids, a3, gate, u_tab)


def _peer_v_kernel(ids_ref, w_ref, v_ref, o_ref):
    tt = o_ref.shape[0]
    n_acc = 4

    def lane_bcast_weights(t):
        return jnp.transpose(jnp.broadcast_to(w_ref[pl.ds(t, 1), :], (LANES, N_SEL)))

    def token(t, wb):
        wb_next = lane_bcast_weights(jnp.minimum(t + 1, tt - 1))
        accs = [jnp.zeros((SUBLANES, LANES), f32) for _ in range(n_acc)]
        for e in range(N_SEL):
            wv = jnp.broadcast_to(wb[e:e + 1, :], (SUBLANES, LANES))
            accs[e % n_acc] = accs[e % n_acc] + _expert_row(v_ref, _token_id(ids_ref, e, t)) * wv
        o_ref[t] = (accs[0] + accs[1]) + (accs[2] + accs[3])
        return wb_next

    lax.fori_loop(0, tt, token, lane_bcast_weights(0))


def _peer_v(ids, w, v_tab, tt):
    n = ids.shape[0] // N_SEL
    assert tt == LANES
    return pl.pallas_call(
        _peer_v_kernel,
        grid=(n // tt,),
        in_specs=[pl.BlockSpec((N_SEL * tt,), lambda i: (i,), memory_space=pltpu.SMEM,
                               pipeline_mode=pl.Buffered(1)),
                  pl.BlockSpec((tt, N_SEL), lambda i: (i, 0)),
                  pl.BlockSpec(v_tab.shape, lambda i: (0, 0), pipeline_mode=pl.Buffered(1))],
        out_specs=pl.BlockSpec((tt, SUBLANES, LANES), lambda i: (i, 0, 0)),
        out_shape=jax.ShapeDtypeStruct((n, SUBLANES, LANES), f32),
        compiler_params=_params(),
        name="peer_v",
    )(ids, w, v_tab)


def _final_kernel(h_ref, p_ref, g_ref, o_ref):
    o_ref[...] = _rms(h_ref[...] + p_ref[...], g_ref[...])


def _final(h2, peer, g, tm):
    n, d = h2.shape
    return pl.pallas_call(
        _final_kernel,
        grid=(n // tm,),
        in_specs=[pl.BlockSpec((tm, d), lambda i: (i, 0)),
                  pl.BlockSpec((tm, d), lambda i: (i, 0)),
                  pl.BlockSpec((1, d), lambda i: (0, 0))],
        out_specs=pl.BlockSpec((tm, d), lambda i: (i, 0)),
        out_shape=jax.ShapeDtypeStruct((n, d), f32),
        compiler_params=_params(),
        name="final_norm",
    )(h2, peer, g)


def _rope_tables(s):
    half = HEAD_DIM // 2
    inv = ROPE_THETA ** (-jnp.arange(half, dtype=f32) / half)
    ang = jnp.arange(s, dtype=f32)[:, None] * inv[None, :]
    cos, sin = jnp.cos(ang), jnp.sin(ang)
    reps = LANES // HEAD_DIM
    cos_t = jnp.tile(jnp.concatenate([cos, cos], axis=1), (1, reps))
    sin_t = jnp.tile(jnp.concatenate([-sin, sin], axis=1), (1, reps))
    return cos_t, sin_t


def _row_tile(s, want):
    t = min(want, s)
    while s % t:
        t //= 2
    assert t % SUBLANES == 0
    return t


def kernel(x, mem, g_mix, w_in, w_conv, w_conv_out, w_attn_out, w_merge, g_xattn, g_mem,
           w_xq, w_xkv, w_xo, g_ffn, w_pq, peer_sub_keys, peer_u, peer_v, g_final):
    b, s, d = x.shape
    n = b * s
    n_mem = mem.shape[1]
    assert w_in.shape[0] == 1, "single-layer trunk only"
    assert d == SUBLANES * LANES and s % MOBA_BLOCK == 0
    n_exp = peer_u.shape[1]
    cos_t, sin_t = _rope_tables(s)
    tm = _row_tile(s, 512)
    h = x.reshape(n, d)
    mem2 = mem.reshape(b * n_mem, d)
    l = 0
    proj = _inproj(h, g_mix[l][None, :], w_in[l].astype(bf16), tm)
    o = _moba(proj, cos_t, sin_t, b, s)
    h1 = _mix(proj, o, h, w_conv[l], w_conv_out[l].astype(bf16), w_attn_out[l].astype(bf16),
              w_merge[l].astype(bf16), s, tm)
    kv = _memkv(mem2, g_mem[l][None, :], w_xkv[l].astype(bf16), n_mem)
    h2 = _xattn(h1, g_xattn[l][None, :], kv, w_xq[l].astype(bf16), w_xo[l].astype(bf16), s, n_mem, tm)
    a3, ids, gate = _select(h2, g_ffn[l][None, :], w_pq[l].astype(bf16),
                            peer_sub_keys[l].astype(bf16), _row_tile(s, 256))
    u_tab = _pack_table(peer_u[l])
    v_tab = _pack_table(peer_v[l])
    tt = _row_tile(s, LANES)
    ids = ids.reshape(-1)
    w = _peer_u(ids, a3.reshape(n, SUBLANES, LANES), gate, u_tab, tt)
    peer = _peer_v(ids, w, v_tab, tt).reshape(n, d)
    return _final(h2, peer, g_final[None, :], tm).reshape(b, s, d)
```

```python
import functools

import jax
import jax.numpy as jnp
from jax import lax
from jax.experimental import pallas as pl
from jax.experimental.pallas import tpu as pltpu

f32 = jnp.float32
bf16 = jnp.bfloat16
i32 = jnp.int32

EPS = 1e-6
MASK_VALUE = -1e30
ROPE_THETA = 10000.0

XATTN_HEADS = 4
ATTN_HEADS = 8
HEAD_DIM = 64
MOBA_BLOCK = 256
MOBA_TOPK = 3
MOBA_GROUP_LANES = 256
PEER_HEADS = 8
PEER_N_KEYS = 128
PEER_HALF = 128
PEER_TOPK = 16
N_SEL = PEER_HEADS * PEER_TOPK
ROWS_PER_EXPERT = 4

LANES = 128
SUBLANES = 8
VMEM_LIMIT = 56 * 1024 * 1024

_NT = (((1,), (1,)), ((), ()))


def _params(n_axes=1, vmem=VMEM_LIMIT):
    return pltpu.CompilerParams(dimension_semantics=("arbitrary",) * n_axes, vmem_limit_bytes=vmem)


def _rms(x, g):
    return x * lax.rsqrt(jnp.mean(x * x, axis=-1, keepdims=True) + EPS) * g


def _inproj_kernel(x_ref, g_ref, w_ref, o_ref, *, chunk):
    a = _rms(x_ref[...], g_ref[...]).astype(bf16)
    for c in range(o_ref.shape[1] // chunk):
        o_ref[:, c * chunk:(c + 1) * chunk] = jnp.dot(
            a, w_ref[:, c * chunk:(c + 1) * chunk], preferred_element_type=f32)


def _inproj(x2, g, w, tm):
    n, d = x2.shape
    width = w.shape[1]
    return pl.pallas_call(
        functools.partial(_inproj_kernel, chunk=width // 4),
        grid=(n // tm,),
        in_specs=[pl.BlockSpec((tm, d), lambda i: (i, 0)),
                  pl.BlockSpec((1, d), lambda i: (0, 0)),
                  pl.BlockSpec((d, width), lambda i: (0, 0), pipeline_mode=pl.Buffered(1))],
        out_specs=pl.BlockSpec((tm, width), lambda i: (i, 0)),
        out_shape=jax.ShapeDtypeStruct((n, width), f32),
        compiler_params=_params(),
        name="inproj",
    )(x2, g, w)


def _moba_kernel(q_ref, k_ref, v_ref, cos_ref, sin_ref, o_ref,
                 qb_sc, kb_sc, vt_sc, bias_sc, ot_sc, *, nb, topk):
    blk = MOBA_BLOCK
    s, gw = q_ref.shape
    lane = lax.broadcasted_iota(i32, (s, gw), 1)
    first_half = (lane % HEAD_DIM) < (HEAD_DIM // 2)
    cos = cos_ref[...]
    sin = sin_ref[...]

    def rope(t):
        partner = jnp.where(first_half, pltpu.roll(t, gw - HEAD_DIM // 2, 1),
                            pltpu.roll(t, HEAD_DIM // 2, 1))
        return t * cos + partner * sin

    q2 = rope(q_ref[...])
    k2 = rope(k_ref[...])
    v2 = v_ref[...]
    scale = HEAD_DIM ** -0.5
    blk_of_q = lax.broadcasted_iota(i32, (nb, s), 1) // blk
    m_iota = lax.broadcasted_iota(i32, (nb, s), 0)
    past = m_iota < blk_of_q
    kpos = lax.broadcasted_iota(i32, (blk, blk), 0)
    qpos = lax.broadcasted_iota(i32, (blk, blk), 1)
    causal = kpos <= qpos

    heads = range(gw // HEAD_DIM)
    for hh in heads:
        qh = q2[:, hh * HEAD_DIM:(hh + 1) * HEAD_DIM]
        kh = k2[:, hh * HEAD_DIM:(hh + 1) * HEAD_DIM]
        vh = v2[:, hh * HEAD_DIM:(hh + 1) * HEAD_DIM]
        kbar = jnp.mean(kh.reshape(nb, blk, HEAD_DIM), axis=1)
        gt = lax.dot_general(kbar, qh, _NT, precision=lax.Precision.HIGHEST,
                             preferred_element_type=f32)
        rows = []
        for n in range(nb):
            gn = gt[n:n + 1, :]
            beats = past & ((gt > gn) | ((gt == gn) & (m_iota < n)))
            cnt = jnp.sum(beats.astype(i32), axis=0, keepdims=True)
            sel = (blk_of_q[0:1, :] > n) & (cnt < topk)
            rows.append(jnp.where(sel, 0.0, MASK_VALUE).astype(f32))
        bias = jnp.concatenate(rows, axis=0)
        vt = jnp.transpose(vh).astype(bf16)
        for i in range(nb):
            bias_sc[hh, i] = bias[:, i * blk:(i + 1) * blk]
            vt_sc[hh, i] = vt[:, i * blk:(i + 1) * blk]
        qb_sc[hh] = qh.astype(bf16)
        kb_sc[hh] = kh.astype(bf16)

    def q_tile(i, carry):
        q0 = pl.multiple_of(i * blk, blk)
        diag = [lax.dot_general(kb_sc[hh, pl.ds(q0, blk), :], qb_sc[hh, pl.ds(q0, blk), :], _NT,
                                preferred_element_type=f32) for hh in heads]
        probs, stats = [], []
        for hh in heads:
            sd = jnp.where(causal, diag[hh] * scale, MASK_VALUE)
            m0 = jnp.max(sd, axis=0, keepdims=True)
            p0 = jnp.exp(sd - m0)
            probs.append(p0.astype(bf16))
            stats.append((m0, jnp.sum(p0, axis=0, keepdims=True)))
        init = [stats[hh] + (jnp.dot(vt_sc[hh, i], probs[hh], preferred_element_type=f32),)
                for hh in heads]

        def kv_step(j, st):
            k0 = pl.multiple_of(j * blk, blk)
            scores = [lax.dot_general(kb_sc[hh, pl.ds(k0, blk), :], qb_sc[hh, pl.ds(q0, blk), :], _NT,
                                      preferred_element_type=f32) for hh in heads]
            probs, stats = [], []
            for hh in heads:
                m, l, _ = st[hh]
                sj = scores[hh] * scale + bias_sc[hh, i, pl.ds(j, 1), :]
                m_new = jnp.maximum(m, jnp.max(sj, axis=0, keepdims=True))
                alpha = jnp.exp(m - m_new)
                p = jnp.exp(sj - m_new)
                probs.append(p.astype(bf16))
                stats.append((m_new, alpha * l + jnp.sum(p, axis=0, keepdims=True), alpha))
            out = []
            for hh in heads:
                m_new, l, alpha = stats[hh]
                acc = alpha * st[hh][2] + jnp.dot(vt_sc[hh, j], probs[hh], preferred_element_type=f32)
                out.append((m_new, l, acc))
            return tuple(out)

        final = lax.fori_loop(0, i, kv_step, tuple(init))
        for hh in heads:
            _, l, acc = final[hh]
            ot_sc[hh, i] = acc * (1.0 / l)
        return carry

    lax.fori_loop(0, nb, q_tile, 0)
    for hh in heads:
        for i in range(nb):
            o_ref[i * blk:(i + 1) * blk, hh * HEAD_DIM:(hh + 1) * HEAD_DIM] = jnp.transpose(ot_sc[hh, i])


def _moba(proj, cos_t, sin_t, b, s):
    n = b * s
    nb = s // MOBA_BLOCK
    topk = min(MOBA_TOPK, nb - 1)
    aw = ATTN_HEADS * HEAD_DIM
    gw = MOBA_GROUP_LANES
    hp = aw // gw
    nh = gw // HEAD_DIM
    qc, kc, vc = 3 * aw // gw, 4 * aw // gw, 5 * aw // gw
    return pl.pallas_call(
        functools.partial(_moba_kernel, nb=nb, topk=topk),
        grid=(b, hp),
        in_specs=[pl.BlockSpec((s, gw), lambda bi, h: (bi, qc + h)),
                  pl.BlockSpec((s, gw), lambda bi, h: (bi, kc + h)),
                  pl.BlockSpec((s, gw), lambda bi, h: (bi, vc + h)),
                  pl.BlockSpec((s, gw), lambda bi, h: (0, 0)),
                  pl.BlockSpec((s, gw), lambda bi, h: (0, 0))],
        out_specs=pl.BlockSpec((s, gw), lambda bi, h: (bi, h)),
        out_shape=jax.ShapeDtypeStruct((n, aw), f32),
        scratch_shapes=[pltpu.VMEM((nh, s, HEAD_DIM), bf16),
                        pltpu.VMEM((nh, s, HEAD_DIM), bf16),
                        pltpu.VMEM((nh, nb, HEAD_DIM, MOBA_BLOCK), bf16),
                        pltpu.VMEM((nh, nb, nb, MOBA_BLOCK), f32),
                        pltpu.VMEM((nh, nb, HEAD_DIM, MOBA_BLOCK), f32)],
        compiler_params=_params(2),
        name="moba",
    )(proj, proj, proj, cos_t, sin_t)


def _mix_kernel(xin_ref, bg_ref, cg_ref, pxin_ref, pcg_ref, gc_ref, ga_ref, o_ref, x_ref,
                wc_ref, wco_ref, wao_ref, wm_ref, h_ref, *, tiles_per_seq):
    tm = xin_ref.shape[0]
    i = pl.program_id(0)
    keep_prev = jnp.where(i % tiles_per_seq == 0, 0.0, 1.0).astype(f32)
    u = cg_ref[...] * xin_ref[...]
    pu = pcg_ref[...] * pxin_ref[...] * keep_prev
    rows = lax.broadcasted_iota(i32, u.shape, 0)
    u1 = jnp.where(rows == 0, pu[SUBLANES - 1:SUBLANES, :], pltpu.roll(u, 1, 0))
    u2 = jnp.where(rows == 0, pu[SUBLANES - 2:SUBLANES - 1, :],
                   jnp.where(rows == 1, pu[SUBLANES - 1:SUBLANES, :], pltpu.roll(u, 2, 0)))
    y = wc_ref[0:1, :] * u2 + wc_ref[1:2, :] * u1 + wc_ref[2:3, :] * u
    y_conv = jnp.dot((bg_ref[...] * y).astype(bf16), wco_ref[...], preferred_element_type=f32)
    y_attn = jnp.dot(o_ref[...].astype(bf16), wao_ref[...], preferred_element_type=f32)
    merged = jax.nn.sigmoid(gc_ref[...]) * y_conv + jax.nn.sigmoid(ga_ref[...]) * y_attn
    h_ref[...] = x_ref[...] + jnp.dot(merged.astype(bf16), wm_ref[...], preferred_element_type=f32)


def _mix(proj, o, x2, w_conv, wco, wao, wm, s, tm):
    n, d = x2.shape
    cw = wco.shape[0]
    aw = wao.shape[0]
    rb = tm // SUBLANES
    prev = lambda col: (lambda i: (jnp.maximum(i * rb - 1, 0), col))
    gcol = 3 * cw + 3 * aw
    return pl.pallas_call(
        functools.partial(_mix_kernel, tiles_per_seq=s // tm),
        grid=(n // tm,),
        in_specs=[pl.BlockSpec((tm, cw), lambda i: (i, 0)),
                  pl.BlockSpec((tm, cw), lambda i: (i, 1)),
                  pl.BlockSpec((tm, cw), lambda i: (i, 2)),
                  pl.BlockSpec((SUBLANES, cw), prev(0)),
                  pl.BlockSpec((SUBLANES, cw), prev(2)),
                  pl.BlockSpec((tm, d), lambda i: (i, gcol // d)),
                  pl.BlockSpec((tm, d), lambda i: (i, gcol // d + 1)),
                  pl.BlockSpec((tm, aw), lambda i: (i, 0)),
                  pl.BlockSpec((tm, d), lambda i: (i, 0)),
                  pl.BlockSpec(w_conv.shape, lambda i: (0, 0)),
                  pl.BlockSpec(wco.shape, lambda i: (0, 0)),
                  pl.BlockSpec(wao.shape, lambda i: (0, 0)),
                  pl.BlockSpec(wm.shape, lambda i: (0, 0))],
        out_specs=pl.BlockSpec((tm, d), lambda i: (i, 0)),
        out_shape=jax.ShapeDtypeStruct((n, d), f32),
        compiler_params=_params(),
        name="mix",
    )(proj, proj, proj, proj, proj, proj, proj, o, x2, w_conv, wco, wao, wm)


def _memkv_kernel(m_ref, g_ref, w_ref, o_ref):
    a = _rms(m_ref[...], g_ref[...]).astype(bf16)
    o_ref[...] = jnp.dot(a, w_ref[...], preferred_element_type=f32).astype(bf16)


def _memkv(mem2, g, w, n_mem):
    nm, d = mem2.shape
    return pl.pallas_call(
        _memkv_kernel,
        grid=(nm // n_mem,),
        in_specs=[pl.BlockSpec((n_mem, d), lambda i: (i, 0)),
                  pl.BlockSpec((1, d), lambda i: (0, 0)),
                  pl.BlockSpec(w.shape, lambda i: (0, 0))],
        out_specs=pl.BlockSpec((n_mem, w.shape[1]), lambda i: (i, 0)),
        out_shape=jax.ShapeDtypeStruct((nm, w.shape[1]), bf16),
        compiler_params=_params(),
        name="memkv",
    )(mem2, g, w)


def _xattn_kernel(h_ref, g_ref, kv_ref, wq_ref, wo_ref, o_ref):
    h = h_ref[...]
    d = h.shape[1]
    hd = d // XATTN_HEADS
    q = jnp.dot(_rms(h, g_ref[...]).astype(bf16), wq_ref[...], preferred_element_type=f32)
    outs = []
    for hh in range(XATTN_HEADS):
        qh = q[:, hh * hd:(hh + 1) * hd].astype(bf16)
        kh = kv_ref[:, hh * hd:(hh + 1) * hd]
        vh = kv_ref[:, d + hh * hd:d + (hh + 1) * hd]
        sc = lax.dot_general(qh, kh, _NT, preferred_element_type=f32) * (hd ** -0.5)
        sc = sc - jnp.max(sc, axis=-1, keepdims=True)
        p = jnp.exp(sc)
        p = p / jnp.sum(p, axis=-1, keepdims=True)
        outs.append(jnp.dot(p.astype(bf16), vh, preferred_element_type=f32).astype(bf16))
    o = jnp.concatenate(outs, axis=-1)
    o_ref[...] = h + jnp.dot(o, wo_ref[...], preferred_element_type=f32)


def _xattn(h1, g, kv, wq, wo, s, n_mem, tm):
    n, d = h1.shape
    tps = s // tm
    return pl.pallas_call(
        _xattn_kernel,
        grid=(n // tm,),
        in_specs=[pl.BlockSpec((tm, d), lambda i: (i, 0)),
                  pl.BlockSpec((1, d), lambda i: (0, 0)),
                  pl.BlockSpec((n_mem, 2 * d), lambda i: (i // tps, 0)),
                  pl.BlockSpec(wq.shape, lambda i: (0, 0)),
                  pl.BlockSpec(wo.shape, lambda i: (0, 0))],
        out_specs=pl.BlockSpec((tm, d), lambda i: (i, 0)),
        out_shape=jax.ShapeDtypeStruct((n, d), f32),
        compiler_params=_params(),
        name="xattn",
    )(h1, g, kv, wq, wo)


def _topk_cols(sc, k, payload=None):
    r = sc.shape[0]
    ridx = lax.broadcasted_iota(i32, sc.shape, 0).astype(f32)
    vals, picks = [], []
    for _ in range(k):
        m = jnp.max(sc, axis=0, keepdims=True)
        im = jnp.min(jnp.where(sc == m, ridx, float(r)), axis=0, keepdims=True)
        hit = ridx == im
        vals.append(m)
        picks.append(im if payload is None else jnp.max(jnp.where(hit, payload, -1.0), axis=0, keepdims=True))
        sc = jnp.where(hit, -jnp.inf, sc)
    return jnp.concatenate(vals, axis=0), jnp.concatenate(picks, axis=0)


def _pair_candidates(k):
    return [(a, b) for a in range(k) for b in range(k) if (a + 1) * (b + 1) <= k]


def _select_kernel(h_ref, g_ref, wq_ref, keys_ref, a_ref, ids_ref, gate_ref):
    k = PEER_TOPK
    a = _rms(h_ref[...], g_ref[...])
    a_ref[...] = a
    q = jnp.dot(a.astype(bf16), wq_ref[...], preferred_element_type=f32)
    tm = q.shape[0]
    pairs = _pair_candidates(k)
    n_pad = -len(pairs) % SUBLANES
    id_rows, gate_rows = [], []
    for h in range(PEER_HEADS):
        tops = []
        for p in range(2):
            c0 = (h * 2 + p) * PEER_HALF
            qhp = q[:, c0:c0 + PEER_HALF].astype(bf16)
            st = lax.dot_general(keys_ref[h, p], qhp, _NT, preferred_element_type=f32)
            tops.append(_topk_cols(st, k))
        (v0, i0), (v1, i1) = tops
        i0 = i0 * float(PEER_N_KEYS)
        cand, cand_id = [], []
        for a_ in range(k):
            nb = sum(1 for (pa, _) in pairs if pa == a_)
            cand.append(v0[a_:a_ + 1, :] + v1[0:nb, :])
            cand_id.append(i0[a_:a_ + 1, :] + i1[0:nb, :])
        if n_pad:
            cand.append(jnp.full((n_pad, tm), -jnp.inf, f32))
            cand_id.append(jnp.zeros((n_pad, tm), f32))
        top_s, top_id = _topk_cols(jnp.concatenate(cand, axis=0), k, jnp.concatenate(cand_id, axis=0))
        id_rows.append(top_id.astype(i32))
        ex = jnp.exp(top_s - top_s[0:1, :])
        gate_rows.append(ex / jnp.sum(ex, axis=0, keepdims=True))
    ids_t = jnp.concatenate(id_rows, axis=0) * ROWS_PER_EXPERT
    for j in range(ids_ref.shape[0]):
        ids_ref[j] = ids_t[:, j * LANES:(j + 1) * LANES]
    gate_ref[...] = jnp.transpose(jnp.concatenate(gate_rows, axis=0))


def _select(h2, g, wq, keys, tm):
    n, d = h2.shape
    return pl.pallas_call(
        _select_kernel,
        grid=(n // tm,),
        in_specs=[pl.BlockSpec((tm, d), lambda i: (i, 0)),
                  pl.BlockSpec((1, d), lambda i: (0, 0)),
                  pl.BlockSpec(wq.shape, lambda i: (0, 0)),
                  pl.BlockSpec(keys.shape, lambda i: (0, 0, 0, 0))],
        out_specs=[pl.BlockSpec((tm, d), lambda i: (i, 0)),
                   pl.BlockSpec((tm // LANES, N_SEL, LANES), lambda i: (i, 0, 0)),
                   pl.BlockSpec((tm, N_SEL), lambda i: (i, 0))],
        out_shape=[jax.ShapeDtypeStruct((n, d), f32),
                   jax.ShapeDtypeStruct((n // LANES, N_SEL, LANES), i32),
                   jax.ShapeDtypeStruct((n, N_SEL), f32)],
        compiler_params=_params(),
        name="peer_select",
    )(h2, g, wq, keys)


def _gelu_tanh(x):
    cdf = 0.5 * (1.0 + jnp.tanh(0.7978845608028654 * (x + 0.044715 * (x * x * x))))
    return x * cdf


def _pack_kernel(t_ref, o_ref):
    o_ref[...] = pltpu.bitcast(t_ref[...].astype(bf16), jnp.uint32)


def _pack_table(t):
    e, d = t.shape
    rows = e * d // LANES
    blk = min(rows, 8192)
    assert rows % blk == 0 and d == 2 * ROWS_PER_EXPERT * LANES
    return pl.pallas_call(
        _pack_kernel,
        grid=(rows // blk,),
        in_specs=[pl.BlockSpec((blk, LANES), lambda i: (i, 0))],
        out_specs=pl.BlockSpec((blk // 2, LANES), lambda i: (i, 0)),
        out_shape=jax.ShapeDtypeStruct((rows // 2, LANES), jnp.uint32),
        compiler_params=_params(),
        name="pack_table",
    )(t.reshape(rows, LANES))


def _expert_row(tab_ref, row0):
    words = tab_ref[pl.ds(pl.multiple_of(row0, ROWS_PER_EXPERT), ROWS_PER_EXPERT), :]
    return pltpu.bitcast(words, bf16).astype(f32)


def _token_id(ids_ref, e, t):
    return ids_ref.at[pl.ds(e * LANES, LANES)][t]


_FOLD_ORDER = (0, 4, 2, 6, 1, 5, 3, 7)


def _fold8(z):
    sub = lax.broadcasted_iota(i32, z[0].shape, 0)
    lo4 = sub < 4
    lo2 = (sub % 4) < 2
    even = (sub % 2) == 0
    c = [jnp.where(lo4, z[2 * k], z[2 * k + 1]) + pltpu.roll(jnp.where(lo4, z[2 * k + 1], z[2 * k]), 4, 0)
         for k in range(4)]
    e = [jnp.where(lo2, c[2 * k] + pltpu.roll(c[2 * k], 6, 0), c[2 * k + 1] + pltpu.roll(c[2 * k + 1], 2, 0))
         for k in range(2)]
    return jnp.where(even, e[0] + pltpu.roll(e[0], 7, 0), e[1] + pltpu.roll(e[1], 1, 0))


def _with_ids_tile(ids_hbm, bufs, sems, body):
    i = pl.program_id(0)
    n_steps = pl.num_programs(0)

    def tile_copy(tile, slot):
        return pltpu.make_async_copy(ids_hbm.at[tile], bufs[slot], sems.at[slot])

    @pl.when(i == 0)
    def _():
        tile_copy(0, 0).start()

    for slot in range(2):
        @pl.when(i % 2 == slot)
        def _():
            @pl.when(i + 1 < n_steps)
            def _():
                tile_copy(i + 1, 1 - slot).start()
            tile_copy(i, slot).wait()
            body(bufs[slot])


def _ids_scratch():
    return [pltpu.SMEM((N_SEL * LANES,), i32), pltpu.SMEM((N_SEL * LANES,), i32), pltpu.SemaphoreType.DMA((2,))]


def _peer_u_kernel(ids_hbm, x_ref, gate_ref, u_ref, w_ref, a_sc, ids0, ids1, sems):
    tt = x_ref.shape[0]
    lane = lax.broadcasted_iota(i32, (N_SEL, LANES), 1)

    def finish(part, t):
        col = jnp.sum(part, axis=1, keepdims=True)
        a_sc[...] = jnp.where(lane == t, col, a_sc[...])

    def body(ids_ref):
        def token(t, part_prev):
            finish(part_prev, t - 1)
            xt = x_ref[t]
            groups = []
            for g in range(N_SEL // SUBLANES):
                z = [_expert_row(u_ref, _token_id(ids_ref, g * SUBLANES + _FOLD_ORDER[k], t)) * xt
                     for k in range(SUBLANES)]
                groups.append(_fold8(z))
            return jnp.concatenate(groups, axis=0)

        a_sc[...] = jnp.zeros_like(a_sc)
        part = lax.fori_loop(0, tt, token, jnp.zeros((N_SEL, LANES), f32))
        finish(part, tt - 1)
        act = jnp.transpose(a_sc[...])[0:tt, :]
        w_ref[...] = gate_ref[...] * _gelu_tanh(act)

    _with_ids_tile(ids_hbm, (ids0, ids1), sems, body)


def _peer_u(ids, a3, gate, u_tab, tt):
    n_tiles = ids.shape[0]
    n = n_tiles * tt
    assert tt == LANES and ids.shape[1] == N_SEL * LANES
    return pl.pallas_call(
        _peer_u_kernel,
        grid=(n_tiles,),
        in_specs=[pl.BlockSpec(memory_space=pl.ANY),
                  pl.BlockSpec((tt, SUBLANES, LANES), lambda i: (i, 0, 0)),
                  pl.BlockSpec((tt, N_SEL), lambda i: (i, 0)),
                  pl.BlockSpec(u_tab.shape, lambda i: (0, 0), pipeline_mode=pl.Buffered(1))],
        out_specs=pl.BlockSpec((tt, N_SEL), lambda i: (i, 0)),
        out_shape=jax.ShapeDtypeStruct((n, N_SEL), f32),
        scratch_shapes=[pltpu.VMEM((N_SEL, LANES), f32)] + _ids_scratch(),
        compiler_params=_params(),
        name="peer_u",
    )(ids, a3, gate, u_tab)


def _peer_v_kernel(ids_hbm, w_ref, v_ref, o_ref, ids0, ids1, sems):
    tt = o_ref.shape[0]
    n_acc = 4

    def lane_bcast_weights(t):
        return jnp.transpose(jnp.broadcast_to(w_ref[pl.ds(t, 1), :], (LANES, N_SEL)))

    def body(ids_ref):
        def token(t, wb):
            wb_next = lane_bcast_weights(jnp.minimum(t + 1, tt - 1))
            accs = [jnp.zeros((SUBLANES, LANES), f32) for _ in range(n_acc)]
            for e in range(N_SEL):
                wv = jnp.broadcast_to(wb[e:e + 1, :], (SUBLANES, LANES))
                accs[e % n_acc] = accs[e % n_acc] + _expert_row(v_ref, _token_id(ids_ref, e, t)) * wv
            o_ref[t] = (accs[0] + accs[1]) + (accs[2] + accs[3])
            return wb_next

        lax.fori_loop(0, tt, token, lane_bcast_weights(0))

    _with_ids_tile(ids_hbm, (ids0, ids1), sems, body)


def _peer_v(ids, w, v_tab, tt):
    n_tiles = ids.shape[0]
    n = n_tiles * tt
    assert tt == LANES and ids.shape[1] == N_SEL * LANES
    return pl.pallas_call(
        _peer_v_kernel,
        grid=(n_tiles,),
        in_specs=[pl.BlockSpec(memory_space=pl.ANY),
                  pl.BlockSpec((tt, N_SEL), lambda i: (i, 0)),
                  pl.BlockSpec(v_tab.shape, lambda i: (0, 0), pipeline_mode=pl.Buffered(1))],
        out_specs=pl.BlockSpec((tt, SUBLANES, LANES), lambda i: (i, 0, 0)),
        out_shape=jax.ShapeDtypeStruct((n, SUBLANES, LANES), f32),
        scratch_shapes=_ids_scratch(),
        compiler_params=_params(),
        name="peer_v",
    )(ids, w, v_tab)


def _final_kernel(h_ref, p_ref, g_ref, o_ref):
    o_ref[...] = _rms(h_ref[...] + p_ref[...], g_ref[...])


def _final(h2, peer, g, tm):
    n, d = h2.shape
    return pl.pallas_call(
        _final_kernel,
        grid=(n // tm,),
        in_specs=[pl.BlockSpec((tm, d), lambda i: (i, 0)),
                  pl.BlockSpec((tm, d), lambda i: (i, 0)),
                  pl.BlockSpec((1, d), lambda i: (0, 0))],
        out_specs=pl.BlockSpec((tm, d), lambda i: (i, 0)),
        out_shape=jax.ShapeDtypeStruct((n, d), f32),
        compiler_params=_params(),
        name="final_norm",
    )(h2, peer, g)


def _rope_tables(s):
    half = HEAD_DIM // 2
    inv = ROPE_THETA ** (-jnp.arange(half, dtype=f32) / half)
    ang = jnp.arange(s, dtype=f32)[:, None] * inv[None, :]
    cos, sin = jnp.cos(ang), jnp.sin(ang)
    reps = MOBA_GROUP_LANES // HEAD_DIM
    cos_t = jnp.tile(jnp.concatenate([cos, cos], axis=1), (1, reps))
    sin_t = jnp.tile(jnp.concatenate([-sin, sin], axis=1), (1, reps))
    return cos_t, sin_t


def _row_tile(s, want):
    t = min(want, s)
    while s % t:
        t //= 2
    assert t % SUBLANES == 0
    return t


def kernel(x, mem, g_mix, w_in, w_conv, w_conv_out, w_attn_out, w_merge, g_xattn, g_mem,
           w_xq, w_xkv, w_xo, g_ffn, w_pq, peer_sub_keys, peer_u, peer_v, g_final):
    b, s, d = x.shape
    n = b * s
    n_mem = mem.shape[1]
    assert w_in.shape[0] == 1, "single-layer trunk only"
    assert d == SUBLANES * LANES and s % MOBA_BLOCK == 0
    n_exp = peer_u.shape[1]
    cos_t, sin_t = _rope_tables(s)
    tm = _row_tile(s, 512)
    h = x.reshape(n, d)
    mem2 = mem.reshape(b * n_mem, d)
    l = 0
    proj = _inproj(h, g_mix[l][None, :], w_in[l].astype(bf16), tm)
    o = _moba(proj, cos_t, sin_t, b, s)
    h1 = _mix(proj, o, h, w_conv[l], w_conv_out[l].astype(bf16), w_attn_out[l].astype(bf16),
              w_merge[l].astype(bf16), s, tm)
    kv = _memkv(mem2, g_mem[l][None, :], w_xkv[l].astype(bf16), n_mem)
    h2 = _xattn(h1, g_xattn[l][None, :], kv, w_xq[l].astype(bf16), w_xo[l].astype(bf16), s, n_mem, tm)
    a3, ids, gate = _select(h2, g_ffn[l][None, :], w_pq[l].astype(bf16),
                            peer_sub_keys[l].astype(bf16), _row_tile(s, 256))
    u_tab = _pack_table(peer_u[l])
    v_tab = _pack_table(peer_v[l])
    tt = _row_tile(s, LANES)
    ids = ids.reshape(n // LANES, N_SEL * LANES)
    w = _peer_u(ids, a3.reshape(n, SUBLANES, LANES), gate, u_tab, tt)
    peer = _peer_v(ids, w, v_tab, tt).reshape(n, d)
    return _final(h2, peer, g_final[None, :], tm).reshape(b, s, d)
```

```python
import functools

import jax
import jax.numpy as jnp
from jax import lax
from jax.experimental import pallas as pl
from jax.experimental.pallas import tpu as pltpu

f32 = jnp.float32
bf16 = jnp.bfloat16
i32 = jnp.int32

EPS = 1e-6
MASK_VALUE = -1e30
ROPE_THETA = 10000.0

XATTN_HEADS = 4
ATTN_HEADS = 8
HEAD_DIM = 64
MOBA_BLOCK = 256
MOBA_TOPK = 3
MOBA_GROUP_LANES = 256
PEER_HEADS = 8
PEER_N_KEYS = 128
PEER_HALF = 128
PEER_TOPK = 16
N_SEL = PEER_HEADS * PEER_TOPK
ROWS_PER_EXPERT = 4

LANES = 128
SUBLANES = 8
VMEM_LIMIT = 56 * 1024 * 1024

_NT = (((1,), (1,)), ((), ()))


def _params(n_axes=1, vmem=VMEM_LIMIT):
    return pltpu.CompilerParams(dimension_semantics=("arbitrary",) * n_axes, vmem_limit_bytes=vmem)


def _rms(x, g):
    return x * lax.rsqrt(jnp.mean(x * x, axis=-1, keepdims=True) + EPS) * g


def _inproj_kernel(x_ref, g_ref, w_ref, o_ref, *, chunk):
    a = _rms(x_ref[...], g_ref[...]).astype(bf16)
    for c in range(o_ref.shape[1] // chunk):
        o_ref[:, c * chunk:(c + 1) * chunk] = jnp.dot(
            a, w_ref[:, c * chunk:(c + 1) * chunk], preferred_element_type=f32)


def _inproj(x2, g, w, tm):
    n, d = x2.shape
    width = w.shape[1]
    return pl.pallas_call(
        functools.partial(_inproj_kernel, chunk=width // 4),
        grid=(n // tm,),
        in_specs=[pl.BlockSpec((tm, d), lambda i: (i, 0)),
                  pl.BlockSpec((1, d), lambda i: (0, 0)),
                  pl.BlockSpec((d, width), lambda i: (0, 0), pipeline_mode=pl.Buffered(1))],
        out_specs=pl.BlockSpec((tm, width), lambda i: (i, 0)),
        out_shape=jax.ShapeDtypeStruct((n, width), f32),
        compiler_params=_params(),
        name="inproj",
    )(x2, g, w)


def _moba_kernel(q_ref, k_ref, v_ref, cos_ref, sin_ref, o_ref,
                 qb_sc, kb_sc, vt_sc, bias_sc, ot_sc, *, nb, topk):
    blk = MOBA_BLOCK
    s, gw = q_ref.shape
    lane = lax.broadcasted_iota(i32, (s, gw), 1)
    first_half = (lane % HEAD_DIM) < (HEAD_DIM // 2)
    cos = cos_ref[...]
    sin = sin_ref[...]

    def rope(t):
        partner = jnp.where(first_half, pltpu.roll(t, gw - HEAD_DIM // 2, 1),
                            pltpu.roll(t, HEAD_DIM // 2, 1))
        return t * cos + partner * sin

    q2 = rope(q_ref[...])
    k2 = rope(k_ref[...])
    v2 = v_ref[...]
    scale = HEAD_DIM ** -0.5
    blk_of_q = lax.broadcasted_iota(i32, (nb, s), 1) // blk
    m_iota = lax.broadcasted_iota(i32, (nb, s), 0)
    past = m_iota < blk_of_q
    kpos = lax.broadcasted_iota(i32, (blk, blk), 0)
    qpos = lax.broadcasted_iota(i32, (blk, blk), 1)
    causal = kpos <= qpos

    heads = range(gw // HEAD_DIM)
    for hh in heads:
        qh = q2[:, hh * HEAD_DIM:(hh + 1) * HEAD_DIM]
        kh = k2[:, hh * HEAD_DIM:(hh + 1) * HEAD_DIM]
        vh = v2[:, hh * HEAD_DIM:(hh + 1) * HEAD_DIM]
        kbar = jnp.mean(kh.reshape(nb, blk, HEAD_DIM), axis=1)
        gt = lax.dot_general(kbar, qh, _NT, precision=lax.Precision.HIGHEST,
                             preferred_element_type=f32)
        rows = []
        for n in range(nb):
            gn = gt[n:n + 1, :]
            beats = past & ((gt > gn) | ((gt == gn) & (m_iota < n)))
            cnt = jnp.sum(beats.astype(i32), axis=0, keepdims=True)
            sel = (blk_of_q[0:1, :] > n) & (cnt < topk)
            rows.append(jnp.where(sel, 0.0, MASK_VALUE).astype(f32))
        bias = jnp.concatenate(rows, axis=0)
        vt = jnp.transpose(vh).astype(bf16)
        for i in range(nb):
            bias_sc[hh, i] = bias[:, i * blk:(i + 1) * blk]
            vt_sc[hh, i] = vt[:, i * blk:(i + 1) * blk]
        qb_sc[hh] = qh.astype(bf16)
        kb_sc[hh] = kh.astype(bf16)

    def q_tile(i, carry):
        q0 = pl.multiple_of(i * blk, blk)
        diag = [lax.dot_general(kb_sc[hh, pl.ds(q0, blk), :], qb_sc[hh, pl.ds(q0, blk), :], _NT,
                                preferred_element_type=f32) for hh in heads]
        probs, stats = [], []
        for hh in heads:
            sd = jnp.where(causal, diag[hh] * scale, MASK_VALUE)
            m0 = jnp.max(sd, axis=0, keepdims=True)
            p0 = jnp.exp(sd - m0)
            probs.append(p0.astype(bf16))
            stats.append((m0, jnp.sum(p0, axis=0, keepdims=True)))
        init = [stats[hh] + (jnp.dot(vt_sc[hh, i], probs[hh], preferred_element_type=f32),)
                for hh in heads]

        def kv_step(j, st):
            k0 = pl.multiple_of(j * blk, blk)
            scores = [lax.dot_general(kb_sc[hh, pl.ds(k0, blk), :], qb_sc[hh, pl.ds(q0, blk), :], _NT,
                                      preferred_element_type=f32) for hh in heads]
            probs, stats = [], []
            for hh in heads:
                m, l, _ = st[hh]
                sj = scores[hh] * scale + bias_sc[hh, i, pl.ds(j, 1), :]
                m_new = jnp.maximum(m, jnp.max(sj, axis=0, keepdims=True))
                alpha = jnp.exp(m - m_new)
                p = jnp.exp(sj - m_new)
                probs.append(p.astype(bf16))
                stats.append((m_new, alpha * l + jnp.sum(p, axis=0, keepdims=True), alpha))
            out = []
            for hh in heads:
                m_new, l, alpha = stats[hh]
                acc = alpha * st[hh][2] + jnp.dot(vt_sc[hh, j], probs[hh], preferred_element_type=f32)
                out.append((m_new, l, acc))
            return tuple(out)

        final = lax.fori_loop(0, i, kv_step, tuple(init))
        for hh in heads:
            _, l, acc = final[hh]
            ot_sc[hh, i] = acc * (1.0 / l)
        return carry

    lax.fori_loop(0, nb, q_tile, 0)
    for hh in heads:
        for i in range(nb):
            o_ref[i * blk:(i + 1) * blk, hh * HEAD_DIM:(hh + 1) * HEAD_DIM] = jnp.transpose(ot_sc[hh, i])


def _moba(proj, cos_t, sin_t, b, s):
    n = b * s
    nb = s // MOBA_BLOCK
    topk = min(MOBA_TOPK, nb - 1)
    aw = ATTN_HEADS * HEAD_DIM
    gw = MOBA_GROUP_LANES
    hp = aw // gw
    nh = gw // HEAD_DIM
    qc, kc, vc = 3 * aw // gw, 4 * aw // gw, 5 * aw // gw
    return pl.pallas_call(
        functools.partial(_moba_kernel, nb=nb, topk=topk),
        grid=(b, hp),
        in_specs=[pl.BlockSpec((s, gw), lambda bi, h: (bi, qc + h)),
                  pl.BlockSpec((s, gw), lambda bi, h: (bi, kc + h)),
                  pl.BlockSpec((s, gw), lambda bi, h: (bi, vc + h)),
                  pl.BlockSpec((s, gw), lambda bi, h: (0, 0)),
                  pl.BlockSpec((s, gw), lambda bi, h: (0, 0))],
        out_specs=pl.BlockSpec((s, gw), lambda bi, h: (bi, h)),
        out_shape=jax.ShapeDtypeStruct((n, aw), f32),
        scratch_shapes=[pltpu.VMEM((nh, s, HEAD_DIM), bf16),
                        pltpu.VMEM((nh, s, HEAD_DIM), bf16),
                        pltpu.VMEM((nh, nb, HEAD_DIM, MOBA_BLOCK), bf16),
                        pltpu.VMEM((nh, nb, nb, MOBA_BLOCK), f32),
                        pltpu.VMEM((nh, nb, HEAD_DIM, MOBA_BLOCK), f32)],
        compiler_params=_params(2),
        name="moba",
    )(proj, proj, proj, cos_t, sin_t)


def _mix_kernel(xin_ref, bg_ref, cg_ref, pxin_ref, pcg_ref, gc_ref, ga_ref, o_ref, x_ref,
                wc_ref, wco_ref, wao_ref, wm_ref, h_ref, *, tiles_per_seq):
    tm = xin_ref.shape[0]
    i = pl.program_id(0)
    keep_prev = jnp.where(i % tiles_per_seq == 0, 0.0, 1.0).astype(f32)
    u = cg_ref[...] * xin_ref[...]
    pu = pcg_ref[...] * pxin_ref[...] * keep_prev
    rows = lax.broadcasted_iota(i32, u.shape, 0)
    u1 = jnp.where(rows == 0, pu[SUBLANES - 1:SUBLANES, :], pltpu.roll(u, 1, 0))
    u2 = jnp.where(rows == 0, pu[SUBLANES - 2:SUBLANES - 1, :],
                   jnp.where(rows == 1, pu[SUBLANES - 1:SUBLANES, :], pltpu.roll(u, 2, 0)))
    y = wc_ref[0:1, :] * u2 + wc_ref[1:2, :] * u1 + wc_ref[2:3, :] * u
    y_conv = jnp.dot((bg_ref[...] * y).astype(bf16), wco_ref[...], preferred_element_type=f32)
    y_attn = jnp.dot(o_ref[...].astype(bf16), wao_ref[...], preferred_element_type=f32)
    merged = jax.nn.sigmoid(gc_ref[...]) * y_conv + jax.nn.sigmoid(ga_ref[...]) * y_attn
    h_ref[...] = x_ref[...] + jnp.dot(merged.astype(bf16), wm_ref[...], preferred_element_type=f32)


def _mix(proj, o, x2, w_conv, wco, wao, wm, s, tm):
    n, d = x2.shape
    cw = wco.shape[0]
    aw = wao.shape[0]
    rb = tm // SUBLANES
    prev = lambda col: (lambda i: (jnp.maximum(i * rb - 1, 0), col))
    gcol = 3 * cw + 3 * aw
    return pl.pallas_call(
        functools.partial(_mix_kernel, tiles_per_seq=s // tm),
        grid=(n // tm,),
        in_specs=[pl.BlockSpec((tm, cw), lambda i: (i, 0)),
                  pl.BlockSpec((tm, cw), lambda i: (i, 1)),
                  pl.BlockSpec((tm, cw), lambda i: (i, 2)),
                  pl.BlockSpec((SUBLANES, cw), prev(0)),
                  pl.BlockSpec((SUBLANES, cw), prev(2)),
                  pl.BlockSpec((tm, d), lambda i: (i, gcol // d)),
                  pl.BlockSpec((tm, d), lambda i: (i, gcol // d + 1)),
                  pl.BlockSpec((tm, aw), lambda i: (i, 0)),
                  pl.BlockSpec((tm, d), lambda i: (i, 0)),
                  pl.BlockSpec(w_conv.shape, lambda i: (0, 0)),
                  pl.BlockSpec(wco.shape, lambda i: (0, 0)),
                  pl.BlockSpec(wao.shape, lambda i: (0, 0)),
                  pl.BlockSpec(wm.shape, lambda i: (0, 0))],
        out_specs=pl.BlockSpec((tm, d), lambda i: (i, 0)),
        out_shape=jax.ShapeDtypeStruct((n, d), f32),
        compiler_params=_params(),
        name="mix",
    )(proj, proj, proj, proj, proj, proj, proj, o, x2, w_conv, wco, wao, wm)


def _memkv_kernel(m_ref, g_ref, w_ref, o_ref):
    a = _rms(m_ref[...], g_ref[...]).astype(bf16)
    o_ref[...] = jnp.dot(a, w_ref[...], preferred_element_type=f32).astype(bf16)


def _memkv(mem2, g, w, n_mem):
    nm, d = mem2.shape
    return pl.pallas_call(
        _memkv_kernel,
        grid=(nm // n_mem,),
        in_specs=[pl.BlockSpec((n_mem, d), lambda i: (i, 0)),
                  pl.BlockSpec((1, d), lambda i: (0, 0)),
                  pl.BlockSpec(w.shape, lambda i: (0, 0))],
        out_specs=pl.BlockSpec((n_mem, w.shape[1]), lambda i: (i, 0)),
        out_shape=jax.ShapeDtypeStruct((nm, w.shape[1]), bf16),
        compiler_params=_params(),
        name="memkv",
    )(mem2, g, w)


def _xattn_kernel(h_ref, g_ref, kv_ref, wq_ref, wo_ref, o_ref):
    h = h_ref[...]
    d = h.shape[1]
    hd = d // XATTN_HEADS
    q = jnp.dot(_rms(h, g_ref[...]).astype(bf16), wq_ref[...], preferred_element_type=f32)
    outs = []
    for hh in range(XATTN_HEADS):
        qh = q[:, hh * hd:(hh + 1) * hd].astype(bf16)
        kh = kv_ref[:, hh * hd:(hh + 1) * hd]
        vh = kv_ref[:, d + hh * hd:d + (hh + 1) * hd]
        sc = lax.dot_general(qh, kh, _NT, preferred_element_type=f32) * (hd ** -0.5)
        sc = sc - jnp.max(sc, axis=-1, keepdims=True)
        p = jnp.exp(sc)
        p = p / jnp.sum(p, axis=-1, keepdims=True)
        outs.append(jnp.dot(p.astype(bf16), vh, preferred_element_type=f32).astype(bf16))
    o = jnp.concatenate(outs, axis=-1)
    o_ref[...] = h + jnp.dot(o, wo_ref[...], preferred_element_type=f32)


def _xattn(h1, g, kv, wq, wo, s, n_mem, tm):
    n, d = h1.shape
    tps = s // tm
    return pl.pallas_call(
        _xattn_kernel,
        grid=(n // tm,),
        in_specs=[pl.BlockSpec((tm, d), lambda i: (i, 0)),
                  pl.BlockSpec((1, d), lambda i: (0, 0)),
                  pl.BlockSpec((n_mem, 2 * d), lambda i: (i // tps, 0)),
                  pl.BlockSpec(wq.shape, lambda i: (0, 0)),
                  pl.BlockSpec(wo.shape, lambda i: (0, 0))],
        out_specs=pl.BlockSpec((tm, d), lambda i: (i, 0)),
        out_shape=jax.ShapeDtypeStruct((n, d), f32),
        compiler_params=_params(),
        name="xattn",
    )(h1, g, kv, wq, wo)


def _topk_cols(sc, k, payload=None):
    r = sc.shape[0]
    ridx = lax.broadcasted_iota(i32, sc.shape, 0).astype(f32)
    vals, picks = [], []
    for _ in range(k):
        m = jnp.max(sc, axis=0, keepdims=True)
        im = jnp.min(jnp.where(sc == m, ridx, float(r)), axis=0, keepdims=True)
        hit = ridx == im
        vals.append(m)
        picks.append(im if payload is None else jnp.max(jnp.where(hit, payload, -1.0), axis=0, keepdims=True))
        sc = jnp.where(hit, -jnp.inf, sc)
    return jnp.concatenate(vals, axis=0), jnp.concatenate(picks, axis=0)


def _pair_candidates(k):
    return [(a, b) for a in range(k) for b in range(k) if (a + 1) * (b + 1) <= k]


def _select_kernel(h_ref, g_ref, wq_ref, keys_ref, a_ref, ids_ref, gate_ref):
    k = PEER_TOPK
    a = _rms(h_ref[...], g_ref[...])
    for c in range(a_ref.shape[1]):
        a_ref[:, c, :] = a[:, c * LANES:(c + 1) * LANES]
    q = jnp.dot(a.astype(bf16), wq_ref[...], preferred_element_type=f32)
    tm = q.shape[0]
    pairs = _pair_candidates(k)
    n_pad = -len(pairs) % SUBLANES
    id_rows, gate_rows = [], []
    for h in range(PEER_HEADS):
        tops = []
        for p in range(2):
            c0 = (h * 2 + p) * PEER_HALF
            qhp = q[:, c0:c0 + PEER_HALF].astype(bf16)
            st = lax.dot_general(keys_ref[h, p], qhp, _NT, preferred_element_type=f32)
            tops.append(_topk_cols(st, k))
        (v0, i0), (v1, i1) = tops
        i0 = i0 * float(PEER_N_KEYS)
        cand, cand_id = [], []
        for a_ in range(k):
            nb = sum(1 for (pa, _) in pairs if pa == a_)
            cand.append(v0[a_:a_ + 1, :] + v1[0:nb, :])
            cand_id.append(i0[a_:a_ + 1, :] + i1[0:nb, :])
        if n_pad:
            cand.append(jnp.full((n_pad, tm), -jnp.inf, f32))
            cand_id.append(jnp.zeros((n_pad, tm), f32))
        top_s, top_id = _topk_cols(jnp.concatenate(cand, axis=0), k, jnp.concatenate(cand_id, axis=0))
        id_rows.append(top_id.astype(i32))
        ex = jnp.exp(top_s - top_s[0:1, :])
        gate_rows.append(ex / jnp.sum(ex, axis=0, keepdims=True))
    ids_t = jnp.concatenate(id_rows, axis=0) * ROWS_PER_EXPERT
    for j in range(ids_ref.shape[0]):
        ids_ref[j] = ids_t[:, j * LANES:(j + 1) * LANES]
    gate_ref[...] = jnp.transpose(jnp.concatenate(gate_rows, axis=0))


def _select(h2, g, wq, keys, tm):
    n, d = h2.shape
    return pl.pallas_call(
        _select_kernel,
        grid=(n // tm,),
        in_specs=[pl.BlockSpec((tm, d), lambda i: (i, 0)),
                  pl.BlockSpec((1, d), lambda i: (0, 0)),
                  pl.BlockSpec(wq.shape, lambda i: (0, 0)),
                  pl.BlockSpec(keys.shape, lambda i: (0, 0, 0, 0))],
        out_specs=[pl.BlockSpec((tm, d // LANES, LANES), lambda i: (i, 0, 0)),
                   pl.BlockSpec((tm // LANES, N_SEL, LANES), lambda i: (i, 0, 0)),
                   pl.BlockSpec((tm, N_SEL), lambda i: (i, 0))],
        out_shape=[jax.ShapeDtypeStruct((n, d // LANES, LANES), f32),
                   jax.ShapeDtypeStruct((n // LANES, N_SEL, LANES), i32),
                   jax.ShapeDtypeStruct((n, N_SEL), f32)],
        compiler_params=_params(),
        name="peer_select",
    )(h2, g, wq, keys)


def _gelu_tanh(x):
    cdf = 0.5 * (1.0 + jnp.tanh(0.7978845608028654 * (x + 0.044715 * (x * x * x))))
    return x * cdf


def _pack_kernel(t_ref, o_ref):
    o_ref[...] = pltpu.bitcast(t_ref[...].astype(bf16), jnp.uint32)


def _pack_table(t):
    e, d = t.shape
    rows = e * d // LANES
    blk = min(rows, 8192)
    assert rows % blk == 0 and d == 2 * ROWS_PER_EXPERT * LANES
    return pl.pallas_call(
        _pack_kernel,
        grid=(rows // blk,),
        in_specs=[pl.BlockSpec((blk, LANES), lambda i: (i, 0))],
        out_specs=pl.BlockSpec((blk // 2, LANES), lambda i: (i, 0)),
        out_shape=jax.ShapeDtypeStruct((rows // 2, LANES), jnp.uint32),
        compiler_params=_params(),
        name="pack_table",
    )(t.reshape(rows, LANES))


def _expert_row(tab_ref, row0):
    words = tab_ref[pl.ds(pl.multiple_of(row0, ROWS_PER_EXPERT), ROWS_PER_EXPERT), :]
    return pltpu.bitcast(words, bf16).astype(f32)


def _token_id(ids_ref, e, t):
    return ids_ref.at[pl.ds(e * LANES, LANES)][t]


_FOLD_ORDER = (0, 4, 2, 6, 1, 5, 3, 7)


def _fold8(z):
    sub = lax.broadcasted_iota(i32, z[0].shape, 0)
    lo4 = sub < 4
    lo2 = (sub % 4) < 2
    even = (sub % 2) == 0
    c = [jnp.where(lo4, z[2 * k], z[2 * k + 1]) + pltpu.roll(jnp.where(lo4, z[2 * k + 1], z[2 * k]), 4, 0)
         for k in range(4)]
    e = [jnp.where(lo2, c[2 * k] + pltpu.roll(c[2 * k], 6, 0), c[2 * k + 1] + pltpu.roll(c[2 * k + 1], 2, 0))
         for k in range(2)]
    return jnp.where(even, e[0] + pltpu.roll(e[0], 7, 0), e[1] + pltpu.roll(e[1], 1, 0))


def _with_ids_tile(ids_hbm, bufs, sems, body):
    i = pl.program_id(0)
    n_steps = pl.num_programs(0)

    def tile_copy(tile, slot):
        return pltpu.make_async_copy(ids_hbm.at[tile], bufs[slot], sems.at[slot])

    @pl.when(i == 0)
    def _():
        tile_copy(0, 0).start()

    for slot in range(2):
        @pl.when(i % 2 == slot)
        def _():
            @pl.when(i + 1 < n_steps)
            def _():
                tile_copy(i + 1, 1 - slot).start()
            tile_copy(i, slot).wait()
            body(bufs[slot])


def _ids_scratch():
    return [pltpu.SMEM((N_SEL * LANES,), i32), pltpu.SMEM((N_SEL * LANES,), i32), pltpu.SemaphoreType.DMA((2,))]


def _peer_u_kernel(ids_hbm, x_ref, gate_ref, u_ref, w_ref, a_sc, ids0, ids1, sems):
    tt = x_ref.shape[0]
    lane = lax.broadcasted_iota(i32, (N_SEL, LANES), 1)

    def finish(part, t):
        col = jnp.sum(part, axis=1, keepdims=True)
        a_sc[...] = jnp.where(lane == t, col, a_sc[...])

    def body(ids_ref):
        def token(t, part_prev):
            finish(part_prev, t - 1)
            xt = x_ref[t]
            groups = []
            for g in range(N_SEL // SUBLANES):
                z = [_expert_row(u_ref, _token_id(ids_ref, g * SUBLANES + _FOLD_ORDER[k], t)) * xt
                     for k in range(SUBLANES)]
                groups.append(_fold8(z))
            return jnp.concatenate(groups, axis=0)

        a_sc[...] = jnp.zeros_like(a_sc)
        part = lax.fori_loop(0, tt, token, jnp.zeros((N_SEL, LANES), f32))
        finish(part, tt - 1)
        act = jnp.transpose(a_sc[...])[0:tt, :]
        w_ref[...] = gate_ref[...] * _gelu_tanh(act)

    _with_ids_tile(ids_hbm, (ids0, ids1), sems, body)


def _peer_u(ids, a3, gate, u_tab, tt):
    n_tiles = ids.shape[0]
    n = n_tiles * tt
    assert tt == LANES and ids.shape[1] == N_SEL * LANES
    return pl.pallas_call(
        _peer_u_kernel,
        grid=(n_tiles,),
        in_specs=[pl.BlockSpec(memory_space=pl.ANY),
                  pl.BlockSpec((tt, SUBLANES, LANES), lambda i: (i, 0, 0)),
                  pl.BlockSpec((tt, N_SEL), lambda i: (i, 0)),
                  pl.BlockSpec(u_tab.shape, lambda i: (0, 0), pipeline_mode=pl.Buffered(1))],
        out_specs=pl.BlockSpec((tt, N_SEL), lambda i: (i, 0)),
        out_shape=jax.ShapeDtypeStruct((n, N_SEL), f32),
        scratch_shapes=[pltpu.VMEM((N_SEL, LANES), f32)] + _ids_scratch(),
        compiler_params=_params(),
        name="peer_u",
    )(ids, a3, gate, u_tab)


def _peer_v_kernel(ids_hbm, w_ref, v_ref, h_ref, g_ref, o_ref, ids0, ids1, sems, p_sc):
    tt = o_ref.shape[0]
    n_acc = 4

    def lane_bcast_weights(t):
        return jnp.transpose(jnp.broadcast_to(w_ref[pl.ds(t, 1), :], (LANES, N_SEL)))

    def body(ids_ref):
        def token(t, wb):
            wb_next = lane_bcast_weights(jnp.minimum(t + 1, tt - 1))
            accs = [jnp.zeros((SUBLANES, LANES), f32) for _ in range(n_acc)]
            for e in range(N_SEL):
                wv = jnp.broadcast_to(wb[e:e + 1, :], (SUBLANES, LANES))
                accs[e % n_acc] = accs[e % n_acc] + _expert_row(v_ref, _token_id(ids_ref, e, t)) * wv
            p_sc[t] = (accs[0] + accs[1]) + (accs[2] + accs[3])
            return wb_next

        lax.fori_loop(0, tt, token, lane_bcast_weights(0))
        peer = jnp.concatenate([p_sc[:, c, :] for c in range(SUBLANES)], axis=1)
        o_ref[...] = _rms(h_ref[...] + peer, g_ref[...])

    _with_ids_tile(ids_hbm, (ids0, ids1), sems, body)


def _peer_v(ids, w, v_tab, h2, g, tt):
    n_tiles = ids.shape[0]
    n, d = h2.shape
    assert tt == LANES and ids.shape[1] == N_SEL * LANES and n == n_tiles * tt and d == SUBLANES * LANES
    return pl.pallas_call(
        _peer_v_kernel,
        grid=(n_tiles,),
        in_specs=[pl.BlockSpec(memory_space=pl.ANY),
                  pl.BlockSpec((tt, N_SEL), lambda i: (i, 0)),
                  pl.BlockSpec(v_tab.shape, lambda i: (0, 0), pipeline_mode=pl.Buffered(1)),
                  pl.BlockSpec((tt, d), lambda i: (i, 0)),
                  pl.BlockSpec((1, d), lambda i: (0, 0))],
        out_specs=pl.BlockSpec((tt, d), lambda i: (i, 0)),
        out_shape=jax.ShapeDtypeStruct((n, d), f32),
        scratch_shapes=_ids_scratch() + [pltpu.VMEM((tt, SUBLANES, LANES), f32)],
        compiler_params=_params(),
        name="peer_v",
    )(ids, w, v_tab, h2, g)


def _rope_tables(s):
    half = HEAD_DIM // 2
    inv = ROPE_THETA ** (-jnp.arange(half, dtype=f32) / half)
    ang = jnp.arange(s, dtype=f32)[:, None] * inv[None, :]
    cos, sin = jnp.cos(ang), jnp.sin(ang)
    reps = MOBA_GROUP_LANES // HEAD_DIM
    cos_t = jnp.tile(jnp.concatenate([cos, cos], axis=1), (1, reps))
    sin_t = jnp.tile(jnp.concatenate([-sin, sin], axis=1), (1, reps))
    return cos_t, sin_t


def _row_tile(s, want):
    t = min(want, s)
    while s % t:
        t //= 2
    assert t % SUBLANES == 0
    return t


def kernel(x, mem, g_mix, w_in, w_conv, w_conv_out, w_attn_out, w_merge, g_xattn, g_mem,
           w_xq, w_xkv, w_xo, g_ffn, w_pq, peer_sub_keys, peer_u, peer_v, g_final):
    b, s, d = x.shape
    n = b * s
    n_mem = mem.shape[1]
    assert w_in.shape[0] == 1, "single-layer trunk only"
    assert d == SUBLANES * LANES and s % MOBA_BLOCK == 0
    n_exp = peer_u.shape[1]
    cos_t, sin_t = _rope_tables(s)
    tm = _row_tile(s, 512)
    h = x.reshape(n, d)
    mem2 = mem.reshape(b * n_mem, d)
    l = 0
    proj = _inproj(h, g_mix[l][None, :], w_in[l].astype(bf16), tm)
    o = _moba(proj, cos_t, sin_t, b, s)
    h1 = _mix(proj, o, h, w_conv[l], w_conv_out[l].astype(bf16), w_attn_out[l].astype(bf16),
              w_merge[l].astype(bf16), s, tm)
    kv = _memkv(mem2, g_mem[l][None, :], w_xkv[l].astype(bf16), n_mem)
    h2 = _xattn(h1, g_xattn[l][None, :], kv, w_xq[l].astype(bf16), w_xo[l].astype(bf16), s, n_mem, tm)
    a3, ids, gate = _select(h2, g_ffn[l][None, :], w_pq[l].astype(bf16),
                            peer_sub_keys[l].astype(bf16), _row_tile(s, 256))
    u_tab = _pack_table(peer_u[l])
    v_tab = _pack_table(peer_v[l])
    tt = _row_tile(s, LANES)
    ids = ids.reshape(n // LANES, N_SEL * LANES)
    w = _peer_u(ids, a3, gate, u_tab, tt)
    return _peer_v(ids, w, v_tab, h2, g_final[None, :], tt).reshape(b, s, d)
```

```python
import functools

import jax
import jax.numpy as jnp
from jax import lax
from jax.experimental import pallas as pl
from jax.experimental.pallas import tpu as pltpu

f32 = jnp.float32
bf16 = jnp.bfloat16
i32 = jnp.int32

EPS = 1e-6
MASK_VALUE = -1e30
ROPE_THETA = 10000.0

XATTN_HEADS = 4
ATTN_HEADS = 8
HEAD_DIM = 64
MOBA_BLOCK = 256
MOBA_TOPK = 3
MOBA_GROUP_LANES = 256
PEER_HEADS = 8
PEER_N_KEYS = 128
PEER_HALF = 128
PEER_TOPK = 16
N_SEL = PEER_HEADS * PEER_TOPK
ROWS_PER_EXPERT = 4

LANES = 128
SUBLANES = 8
VMEM_LIMIT = 56 * 1024 * 1024

_NT = (((1,), (1,)), ((), ()))


def _params(n_axes=1, vmem=VMEM_LIMIT):
    return pltpu.CompilerParams(dimension_semantics=("arbitrary",) * n_axes, vmem_limit_bytes=vmem)


def _rms(x, g):
    return x * lax.rsqrt(jnp.mean(x * x, axis=-1, keepdims=True) + EPS) * g


def _inproj_kernel(x_ref, g_ref, w_ref, o_ref, *, chunk):
    a = _rms(x_ref[...], g_ref[...]).astype(bf16)
    for c in range(o_ref.shape[1] // chunk):
        o_ref[:, c * chunk:(c + 1) * chunk] = jnp.dot(
            a, w_ref[:, c * chunk:(c + 1) * chunk], preferred_element_type=f32)


def _inproj(x2, g, w, tm):
    n, d = x2.shape
    width = w.shape[1]
    return pl.pallas_call(
        functools.partial(_inproj_kernel, chunk=width // 4),
        grid=(n // tm,),
        in_specs=[pl.BlockSpec((tm, d), lambda i: (i, 0)),
                  pl.BlockSpec((1, d), lambda i: (0, 0)),
                  pl.BlockSpec((d, width), lambda i: (0, 0), pipeline_mode=pl.Buffered(1))],
        out_specs=pl.BlockSpec((tm, width), lambda i: (i, 0)),
        out_shape=jax.ShapeDtypeStruct((n, width), f32),
        compiler_params=_params(),
        name="inproj",
    )(x2, g, w)


def _moba_kernel(q_ref, k_ref, v_ref, cos_ref, sin_ref, o_ref,
                 qb_sc, kb_sc, vt_sc, bias_sc, ot_sc, *, nb, topk):
    blk = MOBA_BLOCK
    s, gw = q_ref.shape
    lane = lax.broadcasted_iota(i32, (s, gw), 1)
    first_half = (lane % HEAD_DIM) < (HEAD_DIM // 2)
    cos = cos_ref[...]
    sin = sin_ref[...]

    def rope(t):
        partner = jnp.where(first_half, pltpu.roll(t, gw - HEAD_DIM // 2, 1),
                            pltpu.roll(t, HEAD_DIM // 2, 1))
        return t * cos + partner * sin

    q2 = rope(q_ref[...])
    k2 = rope(k_ref[...])
    v2 = v_ref[...]
    scale = HEAD_DIM ** -0.5
    blk_of_q = lax.broadcasted_iota(i32, (nb, s), 1) // blk
    m_iota = lax.broadcasted_iota(i32, (nb, s), 0)
    past = m_iota < blk_of_q
    kpos = lax.broadcasted_iota(i32, (blk, blk), 0)
    qpos = lax.broadcasted_iota(i32, (blk, blk), 1)
    causal = kpos <= qpos

    heads = range(gw // HEAD_DIM)
    for hh in heads:
        qh = q2[:, hh * HEAD_DIM:(hh + 1) * HEAD_DIM]
        kh = k2[:, hh * HEAD_DIM:(hh + 1) * HEAD_DIM]
        vh = v2[:, hh * HEAD_DIM:(hh + 1) * HEAD_DIM]
        kbar = jnp.mean(kh.reshape(nb, blk, HEAD_DIM), axis=1)
        gt = lax.dot_general(kbar, qh, _NT, precision=lax.Precision.HIGHEST,
                             preferred_element_type=f32)
        rows = []
        for n in range(nb):
            gn = gt[n:n + 1, :]
            beats = past & ((gt > gn) | ((gt == gn) & (m_iota < n)))
            cnt = jnp.sum(beats.astype(i32), axis=0, keepdims=True)
            sel = (blk_of_q[0:1, :] > n) & (cnt < topk)
            rows.append(jnp.where(sel, 0.0, MASK_VALUE).astype(f32))
        bias = jnp.concatenate(rows, axis=0)
        vt = jnp.transpose(vh).astype(bf16)
        for i in range(nb):
            bias_sc[hh, i] = bias[:, i * blk:(i + 1) * blk]
            vt_sc[hh, i] = vt[:, i * blk:(i + 1) * blk]
        qb_sc[hh] = qh.astype(bf16)
        kb_sc[hh] = kh.astype(bf16)

    def q_tile(i, carry):
        q0 = pl.multiple_of(i * blk, blk)
        diag = [lax.dot_general(kb_sc[hh, pl.ds(q0, blk), :], qb_sc[hh, pl.ds(q0, blk), :], _NT,
                                preferred_element_type=f32) for hh in heads]
        probs, stats = [], []
        for hh in heads:
            sd = jnp.where(causal, diag[hh] * scale, MASK_VALUE)
            m0 = jnp.max(sd, axis=0, keepdims=True)
            p0 = jnp.exp(sd - m0)
            probs.append(p0.astype(bf16))
            stats.append((m0, jnp.sum(p0, axis=0, keepdims=True)))
        init = [stats[hh] + (jnp.dot(vt_sc[hh, i], probs[hh], preferred_element_type=f32),)
                for hh in heads]

        def kv_step(j, st):
            k0 = pl.multiple_of(j * blk, blk)
            scores = [lax.dot_general(kb_sc[hh, pl.ds(k0, blk), :], qb_sc[hh, pl.ds(q0, blk), :], _NT,
                                      preferred_element_type=f32) for hh in heads]
            probs, stats = [], []
            for hh in heads:
                m, l, _ = st[hh]
                sj = scores[hh] * scale + bias_sc[hh, i, pl.ds(j, 1), :]
                m_new = jnp.maximum(m, jnp.max(sj, axis=0, keepdims=True))
                alpha = jnp.exp(m - m_new)
                p = jnp.exp(sj - m_new)
                probs.append(p.astype(bf16))
                stats.append((m_new, alpha * l + jnp.sum(p, axis=0, keepdims=True), alpha))
            out = []
            for hh in heads:
                m_new, l, alpha = stats[hh]
                acc = alpha * st[hh][2] + jnp.dot(vt_sc[hh, j], probs[hh], preferred_element_type=f32)
                out.append((m_new, l, acc))
            return tuple(out)

        final = lax.fori_loop(0, i, kv_step, tuple(init))
        for hh in heads:
            _, l, acc = final[hh]
            ot_sc[hh, i] = acc * (1.0 / l)
        return carry

    lax.fori_loop(0, nb, q_tile, 0)
    for hh in heads:
        for i in range(nb):
            o_ref[i * blk:(i + 1) * blk, hh * HEAD_DIM:(hh + 1) * HEAD_DIM] = jnp.transpose(ot_sc[hh, i])


def _moba(proj, cos_t, sin_t, b, s):
    n = b * s
    nb = s // MOBA_BLOCK
    topk = min(MOBA_TOPK, nb - 1)
    aw = ATTN_HEADS * HEAD_DIM
    gw = MOBA_GROUP_LANES
    hp = aw // gw
    nh = gw // HEAD_DIM
    qc, kc, vc = 3 * aw // gw, 4 * aw // gw, 5 * aw // gw
    return pl.pallas_call(
        functools.partial(_moba_kernel, nb=nb, topk=topk),
        grid=(b, hp),
        in_specs=[pl.BlockSpec((s, gw), lambda bi, h: (bi, qc + h)),
                  pl.BlockSpec((s, gw), lambda bi, h: (bi, kc + h)),
                  pl.BlockSpec((s, gw), lambda bi, h: (bi, vc + h)),
                  pl.BlockSpec((s, gw), lambda bi, h: (0, 0)),
                  pl.BlockSpec((s, gw), lambda bi, h: (0, 0))],
        out_specs=pl.BlockSpec((s, gw), lambda bi, h: (bi, h)),
        out_shape=jax.ShapeDtypeStruct((n, aw), f32),
        scratch_shapes=[pltpu.VMEM((nh, s, HEAD_DIM), bf16),
                        pltpu.VMEM((nh, s, HEAD_DIM), bf16),
                        pltpu.VMEM((nh, nb, HEAD_DIM, MOBA_BLOCK), bf16),
                        pltpu.VMEM((nh, nb, nb, MOBA_BLOCK), f32),
                        pltpu.VMEM((nh, nb, HEAD_DIM, MOBA_BLOCK), f32)],
        compiler_params=_params(2),
        name="moba",
    )(proj, proj, proj, cos_t, sin_t)


def _mix_kernel(xin_ref, bg_ref, cg_ref, pxin_ref, pcg_ref, gc_ref, ga_ref, o_ref, x_ref,
                wc_ref, wco_ref, wao_ref, wm_ref, h_ref, *, tiles_per_seq):
    tm = xin_ref.shape[0]
    i = pl.program_id(0)
    keep_prev = jnp.where(i % tiles_per_seq == 0, 0.0, 1.0).astype(f32)
    u = cg_ref[...] * xin_ref[...]
    pu = pcg_ref[...] * pxin_ref[...] * keep_prev
    rows = lax.broadcasted_iota(i32, u.shape, 0)
    u1 = jnp.where(rows == 0, pu[SUBLANES - 1:SUBLANES, :], pltpu.roll(u, 1, 0))
    u2 = jnp.where(rows == 0, pu[SUBLANES - 2:SUBLANES - 1, :],
                   jnp.where(rows == 1, pu[SUBLANES - 1:SUBLANES, :], pltpu.roll(u, 2, 0)))
    y = wc_ref[0:1, :] * u2 + wc_ref[1:2, :] * u1 + wc_ref[2:3, :] * u
    y_conv = jnp.dot((bg_ref[...] * y).astype(bf16), wco_ref[...], preferred_element_type=f32)
    y_attn = jnp.dot(o_ref[...].astype(bf16), wao_ref[...], preferred_element_type=f32)
    merged = jax.nn.sigmoid(gc_ref[...]) * y_conv + jax.nn.sigmoid(ga_ref[...]) * y_attn
    h_ref[...] = x_ref[...] + jnp.dot(merged.astype(bf16), wm_ref[...], preferred_element_type=f32)


def _mix(proj, o, x2, w_conv, wco, wao, wm, s, tm):
    n, d = x2.shape
    cw = wco.shape[0]
    aw = wao.shape[0]
    rb = tm // SUBLANES
    prev = lambda col: (lambda i: (jnp.maximum(i * rb - 1, 0), col))
    gcol = 3 * cw + 3 * aw
    return pl.pallas_call(
        functools.partial(_mix_kernel, tiles_per_seq=s // tm),
        grid=(n // tm,),
        in_specs=[pl.BlockSpec((tm, cw), lambda i: (i, 0)),
                  pl.BlockSpec((tm, cw), lambda i: (i, 1)),
                  pl.BlockSpec((tm, cw), lambda i: (i, 2)),
                  pl.BlockSpec((SUBLANES, cw), prev(0)),
                  pl.BlockSpec((SUBLANES, cw), prev(2)),
                  pl.BlockSpec((tm, d), lambda i: (i, gcol // d)),
                  pl.BlockSpec((tm, d), lambda i: (i, gcol // d + 1)),
                  pl.BlockSpec((tm, aw), lambda i: (i, 0)),
                  pl.BlockSpec((tm, d), lambda i: (i, 0)),
                  pl.BlockSpec(w_conv.shape, lambda i: (0, 0)),
                  pl.BlockSpec(wco.shape, lambda i: (0, 0)),
                  pl.BlockSpec(wao.shape, lambda i: (0, 0)),
                  pl.BlockSpec(wm.shape, lambda i: (0, 0))],
        out_specs=pl.BlockSpec((tm, d), lambda i: (i, 0)),
        out_shape=jax.ShapeDtypeStruct((n, d), f32),
        compiler_params=_params(),
        name="mix",
    )(proj, proj, proj, proj, proj, proj, proj, o, x2, w_conv, wco, wao, wm)


def _memkv_kernel(m_ref, g_ref, w_ref, o_ref):
    a = _rms(m_ref[...], g_ref[...]).astype(bf16)
    o_ref[...] = jnp.dot(a, w_ref[...], preferred_element_type=f32).astype(bf16)


def _memkv(mem2, g, w, n_mem):
    nm, d = mem2.shape
    return pl.pallas_call(
        _memkv_kernel,
        grid=(nm // n_mem,),
        in_specs=[pl.BlockSpec((n_mem, d), lambda i: (i, 0)),
                  pl.BlockSpec((1, d), lambda i: (0, 0)),
                  pl.BlockSpec(w.shape, lambda i: (0, 0))],
        out_specs=pl.BlockSpec((n_mem, w.shape[1]), lambda i: (i, 0)),
        out_shape=jax.ShapeDtypeStruct((nm, w.shape[1]), bf16),
        compiler_params=_params(),
        name="memkv",
    )(mem2, g, w)


def _xattn_kernel(h_ref, g_ref, kv_ref, wq_ref, wo_ref, o_ref):
    h = h_ref[...]
    d = h.shape[1]
    hd = d // XATTN_HEADS
    q = jnp.dot(_rms(h, g_ref[...]).astype(bf16), wq_ref[...], preferred_element_type=f32)
    outs = []
    for hh in range(XATTN_HEADS):
        qh = q[:, hh * hd:(hh + 1) * hd].astype(bf16)
        kh = kv_ref[:, hh * hd:(hh + 1) * hd]
        vh = kv_ref[:, d + hh * hd:d + (hh + 1) * hd]
        sc = lax.dot_general(qh, kh, _NT, preferred_element_type=f32) * (hd ** -0.5)
        sc = sc - jnp.max(sc, axis=-1, keepdims=True)
        p = jnp.exp(sc)
        p = p / jnp.sum(p, axis=-1, keepdims=True)
        outs.append(jnp.dot(p.astype(bf16), vh, preferred_element_type=f32).astype(bf16))
    o = jnp.concatenate(outs, axis=-1)
    o_ref[...] = h + jnp.dot(o, wo_ref[...], preferred_element_type=f32)


def _xattn(h1, g, kv, wq, wo, s, n_mem, tm):
    n, d = h1.shape
    tps = s // tm
    return pl.pallas_call(
        _xattn_kernel,
        grid=(n // tm,),
        in_specs=[pl.BlockSpec((tm, d), lambda i: (i, 0)),
                  pl.BlockSpec((1, d), lambda i: (0, 0)),
                  pl.BlockSpec((n_mem, 2 * d), lambda i: (i // tps, 0)),
                  pl.BlockSpec(wq.shape, lambda i: (0, 0)),
                  pl.BlockSpec(wo.shape, lambda i: (0, 0))],
        out_specs=pl.BlockSpec((tm, d), lambda i: (i, 0)),
        out_shape=jax.ShapeDtypeStruct((n, d), f32),
        compiler_params=_params(),
        name="xattn",
    )(h1, g, kv, wq, wo)


def _topk_cols(sc, k, payload=None):
    r = sc.shape[0]
    ridx = lax.broadcasted_iota(i32, sc.shape, 0).astype(f32)
    vals, picks = [], []
    for _ in range(k):
        m = jnp.max(sc, axis=0, keepdims=True)
        im = jnp.min(jnp.where(sc == m, ridx, float(r)), axis=0, keepdims=True)
        hit = ridx == im
        vals.append(m)
        picks.append(im if payload is None else jnp.max(jnp.where(hit, payload, -1.0), axis=0, keepdims=True))
        sc = jnp.where(hit, -jnp.inf, sc)
    return jnp.concatenate(vals, axis=0), jnp.concatenate(picks, axis=0)


def _pair_candidates(k):
    return [(a, b) for a in range(k) for b in range(k) if (a + 1) * (b + 1) <= k]


def _select_kernel(h_ref, g_ref, wq_ref, keys_ref, a_ref, ids_ref, gate_ref):
    k = PEER_TOPK
    a = _rms(h_ref[...], g_ref[...])
    for c in range(a_ref.shape[1]):
        a_ref[:, c, :] = a[:, c * LANES:(c + 1) * LANES]
    q = jnp.dot(a.astype(bf16), wq_ref[...], preferred_element_type=f32)
    tm = q.shape[0]
    pairs = _pair_candidates(k)
    n_pad = -len(pairs) % SUBLANES
    id_rows, gate_rows = [], []
    for h in range(PEER_HEADS):
        tops = []
        for p in range(2):
            c0 = (h * 2 + p) * PEER_HALF
            qhp = q[:, c0:c0 + PEER_HALF].astype(bf16)
            st = lax.dot_general(keys_ref[h, p], qhp, _NT, preferred_element_type=f32)
            tops.append(_topk_cols(st, k))
        (v0, i0), (v1, i1) = tops
        i0 = i0 * float(PEER_N_KEYS)
        cand, cand_id = [], []
        for a_ in range(k):
            nb = sum(1 for (pa, _) in pairs if pa == a_)
            cand.append(v0[a_:a_ + 1, :] + v1[0:nb, :])
            cand_id.append(i0[a_:a_ + 1, :] + i1[0:nb, :])
        if n_pad:
            cand.append(jnp.full((n_pad, tm), -jnp.inf, f32))
            cand_id.append(jnp.zeros((n_pad, tm), f32))
        top_s, top_id = _topk_cols(jnp.concatenate(cand, axis=0), k, jnp.concatenate(cand_id, axis=0))
        id_rows.append(top_id.astype(i32))
        ex = jnp.exp(top_s - top_s[0:1, :])
        gate_rows.append(ex / jnp.sum(ex, axis=0, keepdims=True))
    ids_t = jnp.concatenate(id_rows, axis=0) * ROWS_PER_EXPERT
    for j in range(ids_ref.shape[0]):
        ids_ref[j] = ids_t[:, j * LANES:(j + 1) * LANES]
    gate_ref[...] = jnp.transpose(jnp.concatenate(gate_rows, axis=0))


def _select(h2, g, wq, keys, tm):
    n, d = h2.shape
    return pl.pallas_call(
        _select_kernel,
        grid=(n // tm,),
        in_specs=[pl.BlockSpec((tm, d), lambda i: (i, 0)),
                  pl.BlockSpec((1, d), lambda i: (0, 0)),
                  pl.BlockSpec(wq.shape, lambda i: (0, 0)),
                  pl.BlockSpec(keys.shape, lambda i: (0, 0, 0, 0))],
        out_specs=[pl.BlockSpec((tm, d // LANES, LANES), lambda i: (i, 0, 0)),
                   pl.BlockSpec((tm // LANES, N_SEL, LANES), lambda i: (i, 0, 0)),
                   pl.BlockSpec((tm, N_SEL), lambda i: (i, 0))],
        out_shape=[jax.ShapeDtypeStruct((n, d // LANES, LANES), f32),
                   jax.ShapeDtypeStruct((n // LANES, N_SEL, LANES), i32),
                   jax.ShapeDtypeStruct((n, N_SEL), f32)],
        compiler_params=_params(),
        name="peer_select",
    )(h2, g, wq, keys)


def _gelu_tanh(x):
    cdf = 0.5 * (1.0 + jnp.tanh(0.7978845608028654 * (x + 0.044715 * (x * x * x))))
    return x * cdf


def _pack_kernel(t_ref, o_ref):
    o_ref[...] = pltpu.bitcast(t_ref[...].astype(bf16), jnp.uint32)


def _pack_table(t):
    e, d = t.shape
    rows = e * d // LANES
    blk = min(rows, 8192)
    assert rows % blk == 0 and d == 2 * ROWS_PER_EXPERT * LANES
    return pl.pallas_call(
        _pack_kernel,
        grid=(rows // blk,),
        in_specs=[pl.BlockSpec((blk, LANES), lambda i: (i, 0))],
        out_specs=pl.BlockSpec((blk // 2, LANES), lambda i: (i, 0)),
        out_shape=jax.ShapeDtypeStruct((rows // 2, LANES), jnp.uint32),
        compiler_params=_params(),
        name="pack_table",
    )(t.reshape(rows, LANES))


def _expert_row(tab_ref, row0):
    words = tab_ref[pl.ds(pl.multiple_of(row0, ROWS_PER_EXPERT), ROWS_PER_EXPERT), :]
    return pltpu.bitcast(words, bf16).astype(f32)


def _token_id(ids_ref, e, t):
    return ids_ref.at[pl.ds(e * LANES, LANES)][t]


_FOLD_ORDER = (0, 4, 2, 6, 1, 5, 3, 7)


def _fold8(z):
    sub = lax.broadcasted_iota(i32, z[0].shape, 0)
    lo4 = sub < 4
    lo2 = (sub % 4) < 2
    even = (sub % 2) == 0
    c = [jnp.where(lo4, z[2 * k], z[2 * k + 1]) + pltpu.roll(jnp.where(lo4, z[2 * k + 1], z[2 * k]), 4, 0)
         for k in range(4)]
    e = [jnp.where(lo2, c[2 * k] + pltpu.roll(c[2 * k], 6, 0), c[2 * k + 1] + pltpu.roll(c[2 * k + 1], 2, 0))
         for k in range(2)]
    return jnp.where(even, e[0] + pltpu.roll(e[0], 7, 0), e[1] + pltpu.roll(e[1], 1, 0))


def _with_ids_tile(ids_hbm, bufs, sems, body):
    i = pl.program_id(0)
    n_steps = pl.num_programs(0)

    def tile_copy(tile, slot):
        return pltpu.make_async_copy(ids_hbm.at[tile], bufs[slot], sems.at[slot])

    @pl.when(i == 0)
    def _():
        tile_copy(0, 0).start()

    for slot in range(2):
        @pl.when(i % 2 == slot)
        def _():
            @pl.when(i + 1 < n_steps)
            def _():
                tile_copy(i + 1, 1 - slot).start()
            tile_copy(i, slot).wait()
            body(bufs[slot])


def _ids_scratch():
    return [pltpu.SMEM((N_SEL * LANES,), i32), pltpu.SMEM((N_SEL * LANES,), i32), pltpu.SemaphoreType.DMA((2,))]


def _peer_u_kernel(ids_hbm, x_ref, gate_ref, u_ref, w_ref, a_sc, ids0, ids1, sems):
    tt = x_ref.shape[0]
    lane = lax.broadcasted_iota(i32, (N_SEL, LANES), 1)

    def finish(part, t):
        col = jnp.sum(part, axis=1, keepdims=True)
        a_sc[...] = jnp.where(lane == t, col, a_sc[...])

    def body(ids_ref):
        def token(t, part_prev):
            finish(part_prev, t - 1)
            xt = x_ref[t]
            groups = []
            for g in range(N_SEL // SUBLANES):
                z = [_expert_row(u_ref, _token_id(ids_ref, g * SUBLANES + _FOLD_ORDER[k], t)) * xt
                     for k in range(SUBLANES)]
                groups.append(_fold8(z))
            return jnp.concatenate(groups, axis=0)

        a_sc[...] = jnp.zeros_like(a_sc)
        part = lax.fori_loop(0, tt, token, jnp.zeros((N_SEL, LANES), f32))
        finish(part, tt - 1)
        act = jnp.transpose(a_sc[...])[0:tt, :]
        w_ref[...] = gate_ref[...] * _gelu_tanh(act)

    _with_ids_tile(ids_hbm, (ids0, ids1), sems, body)


def _peer_u(ids, a3, gate, u_tab, tt):
    n_tiles = ids.shape[0]
    n = n_tiles * tt
    assert tt == LANES and ids.shape[1] == N_SEL * LANES
    return pl.pallas_call(
        _peer_u_kernel,
        grid=(n_tiles,),
        in_specs=[pl.BlockSpec(memory_space=pl.ANY),
                  pl.BlockSpec((tt, SUBLANES, LANES), lambda i: (i, 0, 0)),
                  pl.BlockSpec((tt, N_SEL), lambda i: (i, 0)),
                  pl.BlockSpec(u_tab.shape, lambda i: (0, 0), pipeline_mode=pl.Buffered(1))],
        out_specs=pl.BlockSpec((tt, N_SEL), lambda i: (i, 0)),
        out_shape=jax.ShapeDtypeStruct((n, N_SEL), f32),
        scratch_shapes=[pltpu.VMEM((N_SEL, LANES), f32)] + _ids_scratch(),
        compiler_params=_params(),
        name="peer_u",
    )(ids, a3, gate, u_tab)


def _peer_v_kernel(ids_hbm, w_ref, v_ref, h_ref, g_ref, o_ref, ids0, ids1, sems, p_sc):
    tt = o_ref.shape[0]
    n_acc = 4

    def lane_bcast_weights(t):
        return jnp.transpose(jnp.broadcast_to(w_ref[pl.ds(t, 1), :], (LANES, N_SEL)))

    def body(ids_ref):
        def token(t, wb):
            wb_next = lane_bcast_weights(jnp.minimum(t + 1, tt - 1))
            accs = [jnp.zeros((SUBLANES, LANES), f32) for _ in range(n_acc)]
            for e in range(N_SEL):
                wv = jnp.broadcast_to(wb[e:e + 1, :], (SUBLANES, LANES))
                accs[e % n_acc] = accs[e % n_acc] + _expert_row(v_ref, _token_id(ids_ref, e, t)) * wv
            p_sc[t] = (accs[0] + accs[1]) + (accs[2] + accs[3])
            return wb_next

        lax.fori_loop(0, tt, token, lane_bcast_weights(0))
        peer = jnp.concatenate([p_sc[:, c, :] for c in range(SUBLANES)], axis=1)
        o_ref[...] = _rms(h_ref[...] + peer, g_ref[...])

    _with_ids_tile(ids_hbm, (ids0, ids1), sems, body)


def _peer_v(ids, w, v_tab, h2, g, tt, n_tiles):
    n, d = h2.shape
    assert tt == LANES and ids.shape[1] == N_SEL * LANES and n == ids.shape[0] * tt and d == SUBLANES * LANES
    return pl.pallas_call(
        _peer_v_kernel,
        grid=(n_tiles,),
        in_specs=[pl.BlockSpec(memory_space=pl.ANY),
                  pl.BlockSpec((tt, N_SEL), lambda i: (i, 0)),
                  pl.BlockSpec(v_tab.shape, lambda i: (0, 0), pipeline_mode=pl.Buffered(1)),
                  pl.BlockSpec((tt, d), lambda i: (i, 0)),
                  pl.BlockSpec((1, d), lambda i: (0, 0))],
        out_specs=pl.BlockSpec((tt, d), lambda i: (i, 0)),
        out_shape=jax.ShapeDtypeStruct((n, d), f32),
        scratch_shapes=_ids_scratch() + [pltpu.VMEM((tt, SUBLANES, LANES), f32)],
        compiler_params=_params(),
        name="peer_v",
    )(ids, w, v_tab, h2, g)


SC_WORKERS = 32
SC_LANES = 16
SC_TILE_SHARE_16THS = 3


def _peer_v_sc(ids_tok, w, v_words, t0):
    from jax.experimental.pallas import tpu_sc as plsc
    n_sc = ids_tok.shape[0]
    words = v_words.shape[1]
    d = 2 * words
    tpw = n_sc // SC_WORKERS
    assert n_sc % SC_WORKERS == 0 and words == ROWS_PER_EXPERT * LANES
    half = words // 2
    nvec = half // SC_LANES
    mesh = plsc.VectorSubcoreMesh(core_axis_name="c", subcore_axis_name="s")

    def body(tab_hbm, ids_hbm, w_hbm, out_hbm, idx_v, w_v, rows_v, out_v, sem):
        wid = lax.axis_index("s") * 2 + lax.axis_index("c")
        lane = lax.broadcasted_iota(i32, (SC_LANES,), 0)

        def token(i, carry):
            t = wid * tpw + i
            pltpu.sync_copy(ids_hbm.at[t], idx_v)
            pltpu.sync_copy(w_hbm.at[t0 + t], w_v)
            pltpu.async_copy(tab_hbm.at[idx_v], rows_v, sem).wait()
            for p in range(2):
                def pair(e, accs):
                    wvec = w_v[pl.ds(pl.multiple_of((e // SC_LANES) * SC_LANES, SC_LANES), SC_LANES)]
                    wr = jnp.sum(jnp.where(lane == e % SC_LANES, wvec, 0.0))
                    out = []
                    for j in range(nvec):
                        wd = rows_v[e, pl.ds(p * half + j * SC_LANES, SC_LANES)]
                        lo = lax.bitcast_convert_type(wd << 16, f32)
                        hi = lax.bitcast_convert_type(wd & jnp.int32(-65536), f32)
                        out.append(accs[2 * j] + lo * wr)
                        out.append(accs[2 * j + 1] + hi * wr)
                    return tuple(out)

                accs = lax.fori_loop(0, N_SEL, pair, tuple(jnp.zeros((SC_LANES,), f32) for _ in range(2 * nvec)))
                for j in range(nvec):
                    r = (p * half + j * SC_LANES) // LANES
                    l0 = (j * SC_LANES) % LANES
                    out_v[pl.ds(2 * LANES * r + l0, SC_LANES)] = accs[2 * j]
                    out_v[pl.ds(2 * LANES * r + LANES + l0, SC_LANES)] = accs[2 * j + 1]
            pltpu.sync_copy(out_v, out_hbm.at[t])
            return carry

        lax.fori_loop(0, tpw, token, 0)

    return pl.kernel(
        body,
        out_type=jax.ShapeDtypeStruct((n_sc, d), f32),
        mesh=mesh,
        scratch_types=[pltpu.VMEM((N_SEL,), i32), pltpu.VMEM((N_SEL,), f32), pltpu.VMEM((N_SEL, words), i32),
                       pltpu.VMEM((d,), f32), pltpu.SemaphoreType.DMA],
        compiler_params=pltpu.CompilerParams(needs_layout_passes=False),
        name="peer_v_sc",
    )(v_words, ids_tok, w)


def _final_tail_kernel(out_in_ref, h_ref, p_ref, g_ref, o_ref):
    del out_in_ref
    o_ref[...] = _rms(h_ref[...] + p_ref[...], g_ref[...])


def _final_tail(out, h2, peer_tail, g, tm, first_tile):
    n, d = h2.shape
    tail_tiles = peer_tail.shape[0] // tm
    return pl.pallas_call(
        _final_tail_kernel,
        grid=(tail_tiles,),
        in_specs=[pl.BlockSpec(memory_space=pl.ANY),
                  pl.BlockSpec((tm, d), lambda i: (i + first_tile, 0)),
                  pl.BlockSpec((tm, d), lambda i: (i, 0)),
                  pl.BlockSpec((1, d), lambda i: (0, 0))],
        out_specs=pl.BlockSpec((tm, d), lambda i: (i + first_tile, 0)),
        out_shape=jax.ShapeDtypeStruct((n, d), f32),
        input_output_aliases={0: 0},
        compiler_params=_params(),
        name="final_tail",
    )(out, h2, peer_tail, g)


def _rope_tables(s):
    half = HEAD_DIM // 2
    inv = ROPE_THETA ** (-jnp.arange(half, dtype=f32) / half)
    ang = jnp.arange(s, dtype=f32)[:, None] * inv[None, :]
    cos, sin = jnp.cos(ang), jnp.sin(ang)
    reps = MOBA_GROUP_LANES // HEAD_DIM
    cos_t = jnp.tile(jnp.concatenate([cos, cos], axis=1), (1, reps))
    sin_t = jnp.tile(jnp.concatenate([-sin, sin], axis=1), (1, reps))
    return cos_t, sin_t


def _row_tile(s, want):
    t = min(want, s)
    while s % t:
        t //= 2
    assert t % SUBLANES == 0
    return t


def kernel(x, mem, g_mix, w_in, w_conv, w_conv_out, w_attn_out, w_merge, g_xattn, g_mem,
           w_xq, w_xkv, w_xo, g_ffn, w_pq, peer_sub_keys, peer_u, peer_v, g_final):
    b, s, d = x.shape
    n = b * s
    n_mem = mem.shape[1]
    assert w_in.shape[0] == 1, "single-layer trunk only"
    assert d == SUBLANES * LANES and s % MOBA_BLOCK == 0
    n_exp = peer_u.shape[1]
    cos_t, sin_t = _rope_tables(s)
    tm = _row_tile(s, 512)
    h = x.reshape(n, d)
    mem2 = mem.reshape(b * n_mem, d)
    l = 0
    proj = _inproj(h, g_mix[l][None, :], w_in[l].astype(bf16), tm)
    o = _moba(proj, cos_t, sin_t, b, s)
    h1 = _mix(proj, o, h, w_conv[l], w_conv_out[l].astype(bf16), w_attn_out[l].astype(bf16),
              w_merge[l].astype(bf16), s, tm)
    kv = _memkv(mem2, g_mem[l][None, :], w_xkv[l].astype(bf16), n_mem)
    h2 = _xattn(h1, g_xattn[l][None, :], kv, w_xq[l].astype(bf16), w_xo[l].astype(bf16), s, n_mem, tm)
    a3, ids, gate = _select(h2, g_ffn[l][None, :], w_pq[l].astype(bf16),
                            peer_sub_keys[l].astype(bf16), _row_tile(s, 256))
    u_tab = _pack_table(peer_u[l])
    v_tab = _pack_table(peer_v[l])
    tt = _row_tile(s, LANES)
    ids = ids.reshape(n // LANES, N_SEL * LANES)
    w = _peer_u(ids, a3, gate, u_tab, tt)
    n_tiles = n // tt
    sc_tiles = (n_tiles * SC_TILE_SHARE_16THS) // 16
    tc_tiles = n_tiles - sc_tiles
    out = _peer_v(ids, w, v_tab, h2, g_final[None, :], tt, tc_tiles)
    if sc_tiles:
        n_tc = tc_tiles * tt
        ids_tok = (ids.reshape(n_tiles, N_SEL, tt)[tc_tiles:] // ROWS_PER_EXPERT).transpose(0, 2, 1)
        v_words = lax.bitcast_convert_type(v_tab, i32).reshape(n_exp, ROWS_PER_EXPERT * LANES)
        peer_sc = _peer_v_sc(ids_tok.reshape(sc_tiles * tt, N_SEL), w, v_words, n_tc)
        out = _final_tail(out, h2, peer_sc, g_final[None, :], tt, tc_tiles)
    return out.reshape(b, s, d)
```

```python
import functools

import jax
import jax.numpy as jnp
from jax import lax
from jax.experimental import pallas as pl
from jax.experimental.pallas import tpu as pltpu

f32 = jnp.float32
bf16 = jnp.bfloat16
i32 = jnp.int32

EPS = 1e-6
MASK_VALUE = -1e30
ROPE_THETA = 10000.0

XATTN_HEADS = 4
ATTN_HEADS = 8
HEAD_DIM = 64
MOBA_BLOCK = 256
MOBA_TOPK = 3
MOBA_GROUP_LANES = 256
PEER_HEADS = 8
PEER_N_KEYS = 128
PEER_HALF = 128
PEER_TOPK = 16
N_SEL = PEER_HEADS * PEER_TOPK
ROWS_PER_EXPERT = 4

LANES = 128
SUBLANES = 8
VMEM_LIMIT = 56 * 1024 * 1024

_NT = (((1,), (1,)), ((), ()))


def _params(n_axes=1, vmem=VMEM_LIMIT):
    return pltpu.CompilerParams(dimension_semantics=("arbitrary",) * n_axes, vmem_limit_bytes=vmem)


def _rms(x, g):
    return x * lax.rsqrt(jnp.mean(x * x, axis=-1, keepdims=True) + EPS) * g


def _inproj_kernel(x_ref, g_ref, w_ref, o_ref, *, chunk):
    a = _rms(x_ref[...], g_ref[...]).astype(bf16)
    for c in range(o_ref.shape[1] // chunk):
        o_ref[:, c * chunk:(c + 1) * chunk] = jnp.dot(
            a, w_ref[:, c * chunk:(c + 1) * chunk], preferred_element_type=f32)


def _inproj(x2, g, w, tm):
    n, d = x2.shape
    width = w.shape[1]
    return pl.pallas_call(
        functools.partial(_inproj_kernel, chunk=width // 4),
        grid=(n // tm,),
        in_specs=[pl.BlockSpec((tm, d), lambda i: (i, 0)),
                  pl.BlockSpec((1, d), lambda i: (0, 0)),
                  pl.BlockSpec((d, width), lambda i: (0, 0), pipeline_mode=pl.Buffered(1))],
        out_specs=pl.BlockSpec((tm, width), lambda i: (i, 0)),
        out_shape=jax.ShapeDtypeStruct((n, width), f32),
        compiler_params=_params(),
        name="inproj",
    )(x2, g, w)


def _moba_kernel(q_ref, k_ref, v_ref, cos_ref, sin_ref, o_ref,
                 qb_sc, kb_sc, vt_sc, bias_sc, ot_sc, *, nb, topk):
    blk = MOBA_BLOCK
    s, gw = q_ref.shape
    lane = lax.broadcasted_iota(i32, (s, gw), 1)
    first_half = (lane % HEAD_DIM) < (HEAD_DIM // 2)
    cos = cos_ref[...]
    sin = sin_ref[...]

    def rope(t):
        partner = jnp.where(first_half, pltpu.roll(t, gw - HEAD_DIM // 2, 1),
                            pltpu.roll(t, HEAD_DIM // 2, 1))
        return t * cos + partner * sin

    q2 = rope(q_ref[...])
    k2 = rope(k_ref[...])
    v2 = v_ref[...]
    scale = HEAD_DIM ** -0.5
    blk_of_q = lax.broadcasted_iota(i32, (nb, s), 1) // blk
    m_iota = lax.broadcasted_iota(i32, (nb, s), 0)
    past = m_iota < blk_of_q
    kpos = lax.broadcasted_iota(i32, (blk, blk), 0)
    qpos = lax.broadcasted_iota(i32, (blk, blk), 1)
    causal = kpos <= qpos

    heads = range(gw // HEAD_DIM)
    for hh in heads:
        qh = q2[:, hh * HEAD_DIM:(hh + 1) * HEAD_DIM]
        kh = k2[:, hh * HEAD_DIM:(hh + 1) * HEAD_DIM]
        vh = v2[:, hh * HEAD_DIM:(hh + 1) * HEAD_DIM]
        kbar = jnp.mean(kh.reshape(nb, blk, HEAD_DIM), axis=1)
        gt = lax.dot_general(kbar, qh, _NT, precision=lax.Precision.HIGHEST,
                             preferred_element_type=f32)
        rows = []
        for n in range(nb):
            gn = gt[n:n + 1, :]
            beats = past & ((gt > gn) | ((gt == gn) & (m_iota < n)))
            cnt = jnp.sum(beats.astype(i32), axis=0, keepdims=True)
            sel = (blk_of_q[0:1, :] > n) & (cnt < topk)
            rows.append(jnp.where(sel, 0.0, MASK_VALUE).astype(f32))
        bias = jnp.concatenate(rows, axis=0)
        vt = jnp.transpose(vh).astype(bf16)
        for i in range(nb):
            bias_sc[hh, i] = bias[:, i * blk:(i + 1) * blk]
            vt_sc[hh, i] = vt[:, i * blk:(i + 1) * blk]
        qb_sc[hh] = qh.astype(bf16)
        kb_sc[hh] = kh.astype(bf16)

    def q_tile(i, carry):
        q0 = pl.multiple_of(i * blk, blk)
        diag = [lax.dot_general(kb_sc[hh, pl.ds(q0, blk), :], qb_sc[hh, pl.ds(q0, blk), :], _NT,
                                preferred_element_type=f32) for hh in heads]
        probs, stats = [], []
        for hh in heads:
            sd = jnp.where(causal, diag[hh] * scale, MASK_VALUE)
            m0 = jnp.max(sd, axis=0, keepdims=True)
            p0 = jnp.exp(sd - m0)
            probs.append(p0.astype(bf16))
            stats.append((m0, jnp.sum(p0, axis=0, keepdims=True)))
        init = [stats[hh] + (jnp.dot(vt_sc[hh, i], probs[hh], preferred_element_type=f32),)
                for hh in heads]

        def kv_step(j, st):
            k0 = pl.multiple_of(j * blk, blk)
            scores = [lax.dot_general(kb_sc[hh, pl.ds(k0, blk), :], qb_sc[hh, pl.ds(q0, blk), :], _NT,
                                      preferred_element_type=f32) for hh in heads]
            probs, stats = [], []
            for hh in heads:
                m, l, _ = st[hh]
                sj = scores[hh] * scale + bias_sc[hh, i, pl.ds(j, 1), :]
                m_new = jnp.maximum(m, jnp.max(sj, axis=0, keepdims=True))
                alpha = jnp.exp(m - m_new)
                p = jnp.exp(sj - m_new)
                probs.append(p.astype(bf16))
                stats.append((m_new, alpha * l + jnp.sum(p, axis=0, keepdims=True), alpha))
            out = []
            for hh in heads:
                m_new, l, alpha = stats[hh]
                acc = alpha * st[hh][2] + jnp.dot(vt_sc[hh, j], probs[hh], preferred_element_type=f32)
                out.append((m_new, l, acc))
            return tuple(out)

        final = lax.fori_loop(0, i, kv_step, tuple(init))
        for hh in heads:
            _, l, acc = final[hh]
            ot_sc[hh, i] = acc * (1.0 / l)
        return carry

    lax.fori_loop(0, nb, q_tile, 0)
    for hh in heads:
        for i in range(nb):
            o_ref[i * blk:(i + 1) * blk, hh * HEAD_DIM:(hh + 1) * HEAD_DIM] = jnp.transpose(ot_sc[hh, i])


def _moba(proj, cos_t, sin_t, b, s):
    n = b * s
    nb = s // MOBA_BLOCK
    topk = min(MOBA_TOPK, nb - 1)
    aw = ATTN_HEADS * HEAD_DIM
    gw = MOBA_GROUP_LANES
    hp = aw // gw
    nh = gw // HEAD_DIM
    qc, kc, vc = 3 * aw // gw, 4 * aw // gw, 5 * aw // gw
    return pl.pallas_call(
        functools.partial(_moba_kernel, nb=nb, topk=topk),
        grid=(b, hp),
        in_specs=[pl.BlockSpec((s, gw), lambda bi, h: (bi, qc + h)),
                  pl.BlockSpec((s, gw), lambda bi, h: (bi, kc + h)),
                  pl.BlockSpec((s, gw), lambda bi, h: (bi, vc + h)),
                  pl.BlockSpec((s, gw), lambda bi, h: (0, 0)),
                  pl.BlockSpec((s, gw), lambda bi, h: (0, 0))],
        out_specs=pl.BlockSpec((s, gw), lambda bi, h: (bi, h)),
        out_shape=jax.ShapeDtypeStruct((n, aw), f32),
        scratch_shapes=[pltpu.VMEM((nh, s, HEAD_DIM), bf16),
                        pltpu.VMEM((nh, s, HEAD_DIM), bf16),
                        pltpu.VMEM((nh, nb, HEAD_DIM, MOBA_BLOCK), bf16),
                        pltpu.VMEM((nh, nb, nb, MOBA_BLOCK), f32),
                        pltpu.VMEM((nh, nb, HEAD_DIM, MOBA_BLOCK), f32)],
        compiler_params=_params(2),
        name="moba",
    )(proj, proj, proj, cos_t, sin_t)


def _mix_kernel(xin_ref, bg_ref, cg_ref, pxin_ref, pcg_ref, gc_ref, ga_ref, o_ref, x_ref,
                wc_ref, wco_ref, wao_ref, wm_ref, h_ref, *, tiles_per_seq):
    tm = xin_ref.shape[0]
    i = pl.program_id(0)
    keep_prev = jnp.where(i % tiles_per_seq == 0, 0.0, 1.0).astype(f32)
    u = cg_ref[...] * xin_ref[...]
    pu = pcg_ref[...] * pxin_ref[...] * keep_prev
    rows = lax.broadcasted_iota(i32, u.shape, 0)
    u1 = jnp.where(rows == 0, pu[SUBLANES - 1:SUBLANES, :], pltpu.roll(u, 1, 0))
    u2 = jnp.where(rows == 0, pu[SUBLANES - 2:SUBLANES - 1, :],
                   jnp.where(rows == 1, pu[SUBLANES - 1:SUBLANES, :], pltpu.roll(u, 2, 0)))
    y = wc_ref[0:1, :] * u2 + wc_ref[1:2, :] * u1 + wc_ref[2:3, :] * u
    y_conv = jnp.dot((bg_ref[...] * y).astype(bf16), wco_ref[...], preferred_element_type=f32)
    y_attn = jnp.dot(o_ref[...].astype(bf16), wao_ref[...], preferred_element_type=f32)
    merged = jax.nn.sigmoid(gc_ref[...]) * y_conv + jax.nn.sigmoid(ga_ref[...]) * y_attn
    h_ref[...] = x_ref[...] + jnp.dot(merged.astype(bf16), wm_ref[...], preferred_element_type=f32)


def _mix(proj, o, x2, w_conv, wco, wao, wm, s, tm):
    n, d = x2.shape
    cw = wco.shape[0]
    aw = wao.shape[0]
    rb = tm // SUBLANES
    prev = lambda col: (lambda i: (jnp.maximum(i * rb - 1, 0), col))
    gcol = 3 * cw + 3 * aw
    return pl.pallas_call(
        functools.partial(_mix_kernel, tiles_per_seq=s // tm),
        grid=(n // tm,),
        in_specs=[pl.BlockSpec((tm, cw), lambda i: (i, 0)),
                  pl.BlockSpec((tm, cw), lambda i: (i, 1)),
                  pl.BlockSpec((tm, cw), lambda i: (i, 2)),
                  pl.BlockSpec((SUBLANES, cw), prev(0)),
                  pl.BlockSpec((SUBLANES, cw), prev(2)),
                  pl.BlockSpec((tm, d), lambda i: (i, gcol // d)),
                  pl.BlockSpec((tm, d), lambda i: (i, gcol // d + 1)),
                  pl.BlockSpec((tm, aw), lambda i: (i, 0)),
                  pl.BlockSpec((tm, d), lambda i: (i, 0)),
                  pl.BlockSpec(w_conv.shape, lambda i: (0, 0)),
                  pl.BlockSpec(wco.shape, lambda i: (0, 0)),
                  pl.BlockSpec(wao.shape, lambda i: (0, 0)),
                  pl.BlockSpec(wm.shape, lambda i: (0, 0))],
        out_specs=pl.BlockSpec((tm, d), lambda i: (i, 0)),
        out_shape=jax.ShapeDtypeStruct((n, d), f32),
        compiler_params=_params(),
        name="mix",
    )(proj, proj, proj, proj, proj, proj, proj, o, x2, w_conv, wco, wao, wm)


def _memkv_kernel(m_ref, g_ref, w_ref, o_ref):
    a = _rms(m_ref[...], g_ref[...]).astype(bf16)
    o_ref[...] = jnp.dot(a, w_ref[...], preferred_element_type=f32).astype(bf16)


def _memkv(mem2, g, w, n_mem):
    nm, d = mem2.shape
    return pl.pallas_call(
        _memkv_kernel,
        grid=(nm // n_mem,),
        in_specs=[pl.BlockSpec((n_mem, d), lambda i: (i, 0)),
                  pl.BlockSpec((1, d), lambda i: (0, 0)),
                  pl.BlockSpec(w.shape, lambda i: (0, 0))],
        out_specs=pl.BlockSpec((n_mem, w.shape[1]), lambda i: (i, 0)),
        out_shape=jax.ShapeDtypeStruct((nm, w.shape[1]), bf16),
        compiler_params=_params(),
        name="memkv",
    )(mem2, g, w)


def _xattn_kernel(h_ref, g_ref, kv_ref, wq_ref, wo_ref, o_ref):
    h = h_ref[...]
    d = h.shape[1]
    hd = d // XATTN_HEADS
    q = jnp.dot(_rms(h, g_ref[...]).astype(bf16), wq_ref[...], preferred_element_type=f32)
    outs = []
    for hh in range(XATTN_HEADS):
        qh = q[:, hh * hd:(hh + 1) * hd].astype(bf16)
        kh = kv_ref[:, hh * hd:(hh + 1) * hd]
        vh = kv_ref[:, d + hh * hd:d + (hh + 1) * hd]
        sc = lax.dot_general(qh, kh, _NT, preferred_element_type=f32) * (hd ** -0.5)
        sc = sc - jnp.max(sc, axis=-1, keepdims=True)
        p = jnp.exp(sc)
        p = p / jnp.sum(p, axis=-1, keepdims=True)
        outs.append(jnp.dot(p.astype(bf16), vh, preferred_element_type=f32).astype(bf16))
    o = jnp.concatenate(outs, axis=-1)
    o_ref[...] = h + jnp.dot(o, wo_ref[...], preferred_element_type=f32)


def _xattn(h1, g, kv, wq, wo, s, n_mem, tm):
    n, d = h1.shape
    tps = s // tm
    return pl.pallas_call(
        _xattn_kernel,
        grid=(n // tm,),
        in_specs=[pl.BlockSpec((tm, d), lambda i: (i, 0)),
                  pl.BlockSpec((1, d), lambda i: (0, 0)),
                  pl.BlockSpec((n_mem, 2 * d), lambda i: (i // tps, 0)),
                  pl.BlockSpec(wq.shape, lambda i: (0, 0)),
                  pl.BlockSpec(wo.shape, lambda i: (0, 0))],
        out_specs=pl.BlockSpec((tm, d), lambda i: (i, 0)),
        out_shape=jax.ShapeDtypeStruct((n, d), f32),
        compiler_params=_params(),
        name="xattn",
    )(h1, g, kv, wq, wo)


def _topk_cols(sc, k, payload=None):
    r = sc.shape[0]
    ridx = lax.broadcasted_iota(i32, sc.shape, 0).astype(f32)
    vals, picks = [], []
    for _ in range(k):
        m = jnp.max(sc, axis=0, keepdims=True)
        im = jnp.min(jnp.where(sc == m, ridx, float(r)), axis=0, keepdims=True)
        hit = ridx == im
        vals.append(m)
        picks.append(im if payload is None else jnp.max(jnp.where(hit, payload, -1.0), axis=0, keepdims=True))
        sc = jnp.where(hit, -jnp.inf, sc)
    return jnp.concatenate(vals, axis=0), jnp.concatenate(picks, axis=0)


def _pair_candidates(k):
    return [(a, b) for a in range(k) for b in range(k) if (a + 1) * (b + 1) <= k]


def _select_kernel(h_ref, g_ref, wq_ref, keys_ref, a_ref, ids_ref, gate_ref):
    k = PEER_TOPK
    a = _rms(h_ref[...], g_ref[...])
    for c in range(a_ref.shape[1]):
        a_ref[:, c, :] = a[:, c * LANES:(c + 1) * LANES]
    q = jnp.dot(a.astype(bf16), wq_ref[...], preferred_element_type=f32)
    tm = q.shape[0]
    pairs = _pair_candidates(k)
    n_pad = -len(pairs) % SUBLANES
    id_rows, gate_rows = [], []
    for h in range(PEER_HEADS):
        tops = []
        for p in range(2):
            c0 = (h * 2 + p) * PEER_HALF
            qhp = q[:, c0:c0 + PEER_HALF].astype(bf16)
            st = lax.dot_general(keys_ref[h, p], qhp, _NT, preferred_element_type=f32)
            tops.append(_topk_cols(st, k))
        (v0, i0), (v1, i1) = tops
        i0 = i0 * float(PEER_N_KEYS)
        cand, cand_id = [], []
        for a_ in range(k):
            nb = sum(1 for (pa, _) in pairs if pa == a_)
            cand.append(v0[a_:a_ + 1, :] + v1[0:nb, :])
            cand_id.append(i0[a_:a_ + 1, :] + i1[0:nb, :])
        if n_pad:
            cand.append(jnp.full((n_pad, tm), -jnp.inf, f32))
            cand_id.append(jnp.zeros((n_pad, tm), f32))
        top_s, top_id = _topk_cols(jnp.concatenate(cand, axis=0), k, jnp.concatenate(cand_id, axis=0))
        id_rows.append(top_id.astype(i32))
        ex = jnp.exp(top_s - top_s[0:1, :])
        gate_rows.append(ex / jnp.sum(ex, axis=0, keepdims=True))
    ids_t = jnp.concatenate(id_rows, axis=0) * ROWS_PER_EXPERT
    for j in range(ids_ref.shape[0]):
        ids_ref[j] = ids_t[:, j * LANES:(j + 1) * LANES]
    gate_ref[...] = jnp.transpose(jnp.concatenate(gate_rows, axis=0))


def _select(h2, g, wq, keys, tm):
    n, d = h2.shape
    return pl.pallas_call(
        _select_kernel,
        grid=(n // tm,),
        in_specs=[pl.BlockSpec((tm, d), lambda i: (i, 0)),
                  pl.BlockSpec((1, d), lambda i: (0, 0)),
                  pl.BlockSpec(wq.shape, lambda i: (0, 0)),
                  pl.BlockSpec(keys.shape, lambda i: (0, 0, 0, 0))],
        out_specs=[pl.BlockSpec((tm, d // LANES, LANES), lambda i: (i, 0, 0)),
                   pl.BlockSpec((tm // LANES, N_SEL, LANES), lambda i: (i, 0, 0)),
                   pl.BlockSpec((tm, N_SEL), lambda i: (i, 0))],
        out_shape=[jax.ShapeDtypeStruct((n, d // LANES, LANES), f32),
                   jax.ShapeDtypeStruct((n // LANES, N_SEL, LANES), i32),
                   jax.ShapeDtypeStruct((n, N_SEL), f32)],
        compiler_params=_params(),
        name="peer_select",
    )(h2, g, wq, keys)


def _gelu_tanh(x):
    cdf = 0.5 * (1.0 + jnp.tanh(0.7978845608028654 * (x + 0.044715 * (x * x * x))))
    return x * cdf


def _pack_kernel(t_ref, o_ref):
    o_ref[...] = pltpu.bitcast(t_ref[...].astype(bf16), jnp.uint32)


def _pack_table(t):
    e, d = t.shape
    rows = e * d // LANES
    blk = min(rows, 8192)
    assert rows % blk == 0 and d == 2 * ROWS_PER_EXPERT * LANES
    return pl.pallas_call(
        _pack_kernel,
        grid=(rows // blk,),
        in_specs=[pl.BlockSpec((blk, LANES), lambda i: (i, 0))],
        out_specs=pl.BlockSpec((blk // 2, LANES), lambda i: (i, 0)),
        out_shape=jax.ShapeDtypeStruct((rows // 2, LANES), jnp.uint32),
        compiler_params=_params(),
        name="pack_table",
    )(t.reshape(rows, LANES))


def _expert_row(tab_ref, row0):
    words = tab_ref[pl.ds(pl.multiple_of(row0, ROWS_PER_EXPERT), ROWS_PER_EXPERT), :]
    return pltpu.bitcast(words, bf16).astype(f32)


def _token_id(ids_ref, e, t):
    return ids_ref.at[pl.ds(e * LANES, LANES)][t]


_FOLD_ORDER = (0, 4, 2, 6, 1, 5, 3, 7)


def _fold8(z):
    sub = lax.broadcasted_iota(i32, z[0].shape, 0)
    lo4 = sub < 4
    lo2 = (sub % 4) < 2
    even = (sub % 2) == 0
    c = [jnp.where(lo4, z[2 * k], z[2 * k + 1]) + pltpu.roll(jnp.where(lo4, z[2 * k + 1], z[2 * k]), 4, 0)
         for k in range(4)]
    e = [jnp.where(lo2, c[2 * k] + pltpu.roll(c[2 * k], 6, 0), c[2 * k + 1] + pltpu.roll(c[2 * k + 1], 2, 0))
         for k in range(2)]
    return jnp.where(even, e[0] + pltpu.roll(e[0], 7, 0), e[1] + pltpu.roll(e[1], 1, 0))


def _with_ids_tile(ids_hbm, bufs, sems, body):
    i = pl.program_id(0)
    n_steps = pl.num_programs(0)

    def tile_copy(tile, slot):
        return pltpu.make_async_copy(ids_hbm.at[tile], bufs[slot], sems.at[slot])

    @pl.when(i == 0)
    def _():
        tile_copy(0, 0).start()

    for slot in range(2):
        @pl.when(i % 2 == slot)
        def _():
            @pl.when(i + 1 < n_steps)
            def _():
                tile_copy(i + 1, 1 - slot).start()
            tile_copy(i, slot).wait()
            body(bufs[slot])


def _ids_scratch():
    return [pltpu.SMEM((N_SEL * LANES,), i32), pltpu.SMEM((N_SEL * LANES,), i32), pltpu.SemaphoreType.DMA((2,))]


def _peer_u_kernel(ids_hbm, x_ref, gate_ref, u_ref, w_ref, a_sc, ids0, ids1, sems):
    tt = x_ref.shape[0]
    lane = lax.broadcasted_iota(i32, (N_SEL, LANES), 1)

    def finish(part, t):
        col = jnp.sum(part, axis=1, keepdims=True)
        a_sc[...] = jnp.where(lane == t, col, a_sc[...])

    def body(ids_ref):
        def token(t, part_prev):
            finish(part_prev, t - 1)
            xt = x_ref[t]
            groups = []
            for g in range(N_SEL // SUBLANES):
                z = [_expert_row(u_ref, _token_id(ids_ref, g * SUBLANES + _FOLD_ORDER[k], t)) * xt
                     for k in range(SUBLANES)]
                groups.append(_fold8(z))
            return jnp.concatenate(groups, axis=0)

        a_sc[...] = jnp.zeros_like(a_sc)
        part = lax.fori_loop(0, tt, token, jnp.zeros((N_SEL, LANES), f32))
        finish(part, tt - 1)
        act = jnp.transpose(a_sc[...])[0:tt, :]
        w_ref[...] = gate_ref[...] * _gelu_tanh(act)

    _with_ids_tile(ids_hbm, (ids0, ids1), sems, body)


def _peer_u(ids, a3, gate, u_tab, tt, n_tiles):
    n = ids.shape[0] * tt
    assert tt == LANES and ids.shape[1] == N_SEL * LANES
    return pl.pallas_call(
        _peer_u_kernel,
        grid=(n_tiles,),
        in_specs=[pl.BlockSpec(memory_space=pl.ANY),
                  pl.BlockSpec((tt, SUBLANES, LANES), lambda i: (i, 0, 0)),
                  pl.BlockSpec((tt, N_SEL), lambda i: (i, 0)),
                  pl.BlockSpec(u_tab.shape, lambda i: (0, 0), pipeline_mode=pl.Buffered(1))],
        out_specs=pl.BlockSpec((tt, N_SEL), lambda i: (i, 0)),
        out_shape=jax.ShapeDtypeStruct((n, N_SEL), f32),
        scratch_shapes=[pltpu.VMEM((N_SEL, LANES), f32)] + _ids_scratch(),
        compiler_params=_params(),
        name="peer_u",
    )(ids, a3, gate, u_tab)


def _peer_v_kernel(ids_hbm, w_ref, v_ref, h_ref, g_ref, o_ref, ids0, ids1, sems, p_sc):
    tt = o_ref.shape[0]
    n_acc = 4

    def lane_bcast_weights(t):
        return jnp.transpose(jnp.broadcast_to(w_ref[pl.ds(t, 1), :], (LANES, N_SEL)))

    def body(ids_ref):
        def token(t, wb):
            wb_next = lane_bcast_weights(jnp.minimum(t + 1, tt - 1))
            accs = [jnp.zeros((SUBLANES, LANES), f32) for _ in range(n_acc)]
            for e in range(N_SEL):
                wv = jnp.broadcast_to(wb[e:e + 1, :], (SUBLANES, LANES))
                accs[e % n_acc] = accs[e % n_acc] + _expert_row(v_ref, _token_id(ids_ref, e, t)) * wv
            p_sc[t] = (accs[0] + accs[1]) + (accs[2] + accs[3])
            return wb_next

        lax.fori_loop(0, tt, token, lane_bcast_weights(0))
        peer = jnp.concatenate([p_sc[:, c, :] for c in range(SUBLANES)], axis=1)
        o_ref[...] = _rms(h_ref[...] + peer, g_ref[...])

    _with_ids_tile(ids_hbm, (ids0, ids1), sems, body)


def _peer_v(ids, w, v_tab, h2, g, tt, n_tiles):
    n, d = h2.shape
    assert tt == LANES and ids.shape[1] == N_SEL * LANES and n == ids.shape[0] * tt and d == SUBLANES * LANES
    return pl.pallas_call(
        _peer_v_kernel,
        grid=(n_tiles,),
        in_specs=[pl.BlockSpec(memory_space=pl.ANY),
                  pl.BlockSpec((tt, N_SEL), lambda i: (i, 0)),
                  pl.BlockSpec(v_tab.shape, lambda i: (0, 0), pipeline_mode=pl.Buffered(1)),
                  pl.BlockSpec((tt, d), lambda i: (i, 0)),
                  pl.BlockSpec((1, d), lambda i: (0, 0))],
        out_specs=pl.BlockSpec((tt, d), lambda i: (i, 0)),
        out_shape=jax.ShapeDtypeStruct((n, d), f32),
        scratch_shapes=_ids_scratch() + [pltpu.VMEM((tt, SUBLANES, LANES), f32)],
        compiler_params=_params(),
        name="peer_v",
    )(ids, w, v_tab, h2, g)


SC_WORKERS = 32
SC_LANES = 16
SC_TILE_SHARE_64THS = 16


def _peer_sc(ids_tok, x3, gate, u_words, v_words, t0):
    from jax.experimental.pallas import tpu_sc as plsc
    n_sc = ids_tok.shape[0]
    words = v_words.shape[1]
    d = 2 * words
    tpw = n_sc // SC_WORKERS
    assert n_sc % SC_WORKERS == 0 and words == ROWS_PER_EXPERT * LANES
    half = words // 2
    nvec = half // SC_LANES
    mesh = plsc.VectorSubcoreMesh(core_axis_name="c", subcore_axis_name="s")

    def halves(wd):
        return (lax.bitcast_convert_type(wd << 16, f32), lax.bitcast_convert_type(wd & jnp.int32(-65536), f32))

    def body(u_hbm, v_hbm, ids_hbm, x_hbm, g_hbm, out_hbm, idx_v, g_v, w_v, x_v, rows_v, part_v, out_v, sem):
        wid = lax.axis_index("s") * 2 + lax.axis_index("c")
        lane = lax.broadcasted_iota(i32, (SC_LANES,), 0)

        def token(i, carry):
            t = wid * tpw + i
            pltpu.sync_copy(ids_hbm.at[t], idx_v)
            pltpu.sync_copy(g_hbm.at[t0 + t], g_v)
            pltpu.sync_copy(x_hbm.at[t0 + t], x_v)
            pltpu.async_copy(u_hbm.at[idx_v], rows_v, sem).wait()
            for p in range(2):
                xs = []
                for j in range(nvec):
                    o = p * half + j * SC_LANES
                    r, l0 = o // LANES, o % LANES
                    xs.append((x_v[2 * r, pl.ds(l0, SC_LANES)], x_v[2 * r + 1, pl.ds(l0, SC_LANES)]))

                def dot_pair(e, c, p=p, xs=xs):
                    acc = None
                    for j in range(nvec):
                        lo, hi = halves(rows_v[e, pl.ds(p * half + j * SC_LANES, SC_LANES)])
                        term = lo * xs[j][0] + hi * xs[j][1]
                        acc = term if acc is None else acc + term
                    part_v[e, :] = acc if p == 0 else part_v[e, :] + acc
                    return c

                lax.fori_loop(0, N_SEL, dot_pair, 0)
            for k in range(N_SEL // SC_LANES):
                act = jnp.zeros((SC_LANES,), f32)
                for q in range(SC_LANES):
                    act = jnp.where(lane == q, jnp.sum(part_v[k * SC_LANES + q, :]), act)
                inner = 0.7978845608028654 * (act + 0.044715 * (act * act * act))
                tanh = 1.0 - 2.0 / (jnp.exp(2.0 * inner) + 1.0)
                w_v[pl.ds(k * SC_LANES, SC_LANES)] = g_v[pl.ds(k * SC_LANES, SC_LANES)] * (act * (0.5 * (1.0 + tanh)))
            pltpu.async_copy(v_hbm.at[idx_v], rows_v, sem).wait()
            for p in range(2):
                def acc_pair(e, accs, p=p):
                    wvec = w_v[pl.ds(pl.multiple_of((e // SC_LANES) * SC_LANES, SC_LANES), SC_LANES)]
                    wr = jnp.sum(jnp.where(lane == e % SC_LANES, wvec, 0.0))
                    out = []
                    for j in range(nvec):
                        lo, hi = halves(rows_v[e, pl.ds(p * half + j * SC_LANES, SC_LANES)])
                        out.append(accs[2 * j] + lo * wr)
                        out.append(accs[2 * j + 1] + hi * wr)
                    return tuple(out)

                accs = lax.fori_loop(0, N_SEL, acc_pair,
                                     tuple(jnp.zeros((SC_LANES,), f32) for _ in range(2 * nvec)))
                for j in range(nvec):
                    o = p * half + j * SC_LANES
                    r, l0 = o // LANES, o % LANES
                    out_v[pl.ds(2 * LANES * r + l0, SC_LANES)] = accs[2 * j]
                    out_v[pl.ds(2 * LANES * r + LANES + l0, SC_LANES)] = accs[2 * j + 1]
            pltpu.sync_copy(out_v, out_hbm.at[t])
            return carry

        lax.fori_loop(0, tpw, token, 0)

    return pl.kernel(
        body,
        out_type=jax.ShapeDtypeStruct((n_sc, d), f32),
        mesh=mesh,
        scratch_types=[pltpu.VMEM((N_SEL,), i32), pltpu.VMEM((N_SEL,), f32), pltpu.VMEM((N_SEL,), f32),
                       pltpu.VMEM((SUBLANES, LANES), f32), pltpu.VMEM((N_SEL, words), i32),
                       pltpu.VMEM((N_SEL, SC_LANES), f32), pltpu.VMEM((d,), f32), pltpu.SemaphoreType.DMA],
        compiler_params=pltpu.CompilerParams(needs_layout_passes=False),
        name="peer_sc",
    )(u_words, v_words, ids_tok, x3, gate)


def _final_tail_kernel(out_in_ref, h_ref, p_ref, g_ref, o_ref):
    del out_in_ref
    o_ref[...] = _rms(h_ref[...] + p_ref[...], g_ref[...])


def _final_tail(out, h2, peer_tail, g, tm, first_tile):
    n, d = h2.shape
    tail_tiles = peer_tail.shape[0] // tm
    return pl.pallas_call(
        _final_tail_kernel,
        grid=(tail_tiles,),
        in_specs=[pl.BlockSpec(memory_space=pl.ANY),
                  pl.BlockSpec((tm, d), lambda i: (i + first_tile, 0)),
                  pl.BlockSpec((tm, d), lambda i: (i, 0)),
                  pl.BlockSpec((1, d), lambda i: (0, 0))],
        out_specs=pl.BlockSpec((tm, d), lambda i: (i + first_tile, 0)),
        out_shape=jax.ShapeDtypeStruct((n, d), f32),
        input_output_aliases={0: 0},
        compiler_params=_params(),
        name="final_tail",
    )(out, h2, peer_tail, g)


def _rope_tables(s):
    half = HEAD_DIM // 2
    inv = ROPE_THETA ** (-jnp.arange(half, dtype=f32) / half)
    ang = jnp.arange(s, dtype=f32)[:, None] * inv[None, :]
    cos, sin = jnp.cos(ang), jnp.sin(ang)
    reps = MOBA_GROUP_LANES // HEAD_DIM
    cos_t = jnp.tile(jnp.concatenate([cos, cos], axis=1), (1, reps))
    sin_t = jnp.tile(jnp.concatenate([-sin, sin], axis=1), (1, reps))
    return cos_t, sin_t


def _row_tile(s, want):
    t = min(want, s)
    while s % t:
        t //= 2
    assert t % SUBLANES == 0
    return t


def kernel(x, mem, g_mix, w_in, w_conv, w_conv_out, w_attn_out, w_merge, g_xattn, g_mem,
           w_xq, w_xkv, w_xo, g_ffn, w_pq, peer_sub_keys, peer_u, peer_v, g_final):
    b, s, d = x.shape
    n = b * s
    n_mem = mem.shape[1]
    assert w_in.shape[0] == 1, "single-layer trunk only"
    assert d == SUBLANES * LANES and s % MOBA_BLOCK == 0
    n_exp = peer_u.shape[1]
    cos_t, sin_t = _rope_tables(s)
    tm = _row_tile(s, 512)
    h = x.reshape(n, d)
    mem2 = mem.reshape(b * n_mem, d)
    l = 0
    proj = _inproj(h, g_mix[l][None, :], w_in[l].astype(bf16), tm)
    o = _moba(proj, cos_t, sin_t, b, s)
    h1 = _mix(proj, o, h, w_conv[l], w_conv_out[l].astype(bf16), w_attn_out[l].astype(bf16),
              w_merge[l].astype(bf16), s, tm)
    kv = _memkv(mem2, g_mem[l][None, :], w_xkv[l].astype(bf16), n_mem)
    h2 = _xattn(h1, g_xattn[l][None, :], kv, w_xq[l].astype(bf16), w_xo[l].astype(bf16), s, n_mem, tm)
    a3, ids, gate = _select(h2, g_ffn[l][None, :], w_pq[l].astype(bf16),
                            peer_sub_keys[l].astype(bf16), _row_tile(s, 256))
    u_tab = _pack_table(peer_u[l])
    v_tab = _pack_table(peer_v[l])
    tt = _row_tile(s, LANES)
    ids = ids.reshape(n // LANES, N_SEL * LANES)
    n_tiles = n // tt
    sc_tiles = (n_tiles * SC_TILE_SHARE_64THS) // 64
    tc_tiles = n_tiles - sc_tiles
    w = _peer_u(ids, a3, gate, u_tab, tt, tc_tiles)
    out = _peer_v(ids, w, v_tab, h2, g_final[None, :], tt, tc_tiles)
    if sc_tiles:
        ids_tok = (ids.reshape(n_tiles, N_SEL, tt)[tc_tiles:] // ROWS_PER_EXPERT).transpose(0, 2, 1)
        as_rows = lambda tab: lax.bitcast_convert_type(tab, i32).reshape(n_exp, ROWS_PER_EXPERT * LANES)
        peer_sc = _peer_sc(ids_tok.reshape(sc_tiles * tt, N_SEL), a3, gate, as_rows(u_tab), as_rows(v_tab),
                           tc_tiles * tt)
        out = _final_tail(out, h2, peer_sc, g_final[None, :], tt, tc_tiles)
    return out.reshape(b, s, d)
```

```python
import functools

import jax
import jax.numpy as jnp
from jax import lax
from jax.experimental import pallas as pl
from jax.experimental.pallas import tpu as pltpu

f32 = jnp.float32
bf16 = jnp.bfloat16
i32 = jnp.int32

EPS = 1e-6
MASK_VALUE = -1e30
ROPE_THETA = 10000.0

XATTN_HEADS = 4
ATTN_HEADS = 8
HEAD_DIM = 64
MOBA_BLOCK = 256
MOBA_TOPK = 3
MOBA_GROUP_LANES = 256
PEER_HEADS = 8
PEER_N_KEYS = 128
PEER_HALF = 128
PEER_TOPK = 16
N_SEL = PEER_HEADS * PEER_TOPK
ROWS_PER_EXPERT = 4

LANES = 128
SUBLANES = 8
VMEM_LIMIT = 56 * 1024 * 1024

_NT = (((1,), (1,)), ((), ()))


def _params(n_axes=1, vmem=VMEM_LIMIT):
    return pltpu.CompilerParams(dimension_semantics=("arbitrary",) * n_axes, vmem_limit_bytes=vmem)


def _rms(x, g):
    return x * lax.rsqrt(jnp.mean(x * x, axis=-1, keepdims=True) + EPS) * g


def _inproj_kernel(x_ref, g_ref, w_ref, o_ref, *, chunk):
    a = _rms(x_ref[...], g_ref[...]).astype(bf16)
    for c in range(o_ref.shape[1] // chunk):
        o_ref[:, c * chunk:(c + 1) * chunk] = jnp.dot(
            a, w_ref[:, c * chunk:(c + 1) * chunk], preferred_element_type=f32)


def _inproj(x2, g, w, tm):
    n, d = x2.shape
    width = w.shape[1]
    return pl.pallas_call(
        functools.partial(_inproj_kernel, chunk=width // 4),
        grid=(n // tm,),
        in_specs=[pl.BlockSpec((tm, d), lambda i: (i, 0)),
                  pl.BlockSpec((1, d), lambda i: (0, 0)),
                  pl.BlockSpec((d, width), lambda i: (0, 0), pipeline_mode=pl.Buffered(1))],
        out_specs=pl.BlockSpec((tm, width), lambda i: (i, 0)),
        out_shape=jax.ShapeDtypeStruct((n, width), f32),
        compiler_params=_params(),
        name="inproj",
    )(x2, g, w)


def _moba_kernel(q_ref, k_ref, v_ref, cos_ref, sin_ref, o_ref,
                 qb_sc, kb_sc, vt_sc, bias_sc, ot_sc, *, nb, topk):
    blk = MOBA_BLOCK
    s, gw = q_ref.shape
    lane = lax.broadcasted_iota(i32, (s, gw), 1)
    first_half = (lane % HEAD_DIM) < (HEAD_DIM // 2)
    cos = cos_ref[...]
    sin = sin_ref[...]

    def rope(t):
        partner = jnp.where(first_half, pltpu.roll(t, gw - HEAD_DIM // 2, 1),
                            pltpu.roll(t, HEAD_DIM // 2, 1))
        return t * cos + partner * sin

    q2 = rope(q_ref[...])
    k2 = rope(k_ref[...])
    v2 = v_ref[...]
    scale = HEAD_DIM ** -0.5
    blk_of_q = lax.broadcasted_iota(i32, (nb, s), 1) // blk
    m_iota = lax.broadcasted_iota(i32, (nb, s), 0)
    past = m_iota < blk_of_q
    kpos = lax.broadcasted_iota(i32, (blk, blk), 0)
    qpos = lax.broadcasted_iota(i32, (blk, blk), 1)
    causal = kpos <= qpos

    heads = range(gw // HEAD_DIM)
    for hh in heads:
        qh = q2[:, hh * HEAD_DIM:(hh + 1) * HEAD_DIM]
        kh = k2[:, hh * HEAD_DIM:(hh + 1) * HEAD_DIM]
        vh = v2[:, hh * HEAD_DIM:(hh + 1) * HEAD_DIM]
        kbar = jnp.mean(kh.reshape(nb, blk, HEAD_DIM), axis=1)
        gt = lax.dot_general(kbar, qh, _NT, precision=lax.Precision.HIGHEST,
                             preferred_element_type=f32)
        rows = []
        for n in range(nb):
            gn = gt[n:n + 1, :]
            beats = past & ((gt > gn) | ((gt == gn) & (m_iota < n)))
            cnt = jnp.sum(beats.astype(i32), axis=0, keepdims=True)
            sel = (blk_of_q[0:1, :] > n) & (cnt < topk)
            rows.append(jnp.where(sel, 0.0, MASK_VALUE).astype(f32))
        bias = jnp.concatenate(rows, axis=0)
        vt = jnp.transpose(vh).astype(bf16)
        for i in range(nb):
            bias_sc[hh, i] = bias[:, i * blk:(i + 1) * blk]
            vt_sc[hh, i] = vt[:, i * blk:(i + 1) * blk]
        qb_sc[hh] = qh.astype(bf16)
        kb_sc[hh] = kh.astype(bf16)

    def q_tile(i, carry):
        q0 = pl.multiple_of(i * blk, blk)
        diag = [lax.dot_general(kb_sc[hh, pl.ds(q0, blk), :], qb_sc[hh, pl.ds(q0, blk), :], _NT,
                                preferred_element_type=f32) for hh in heads]
        probs, stats = [], []
        for hh in heads:
            sd = jnp.where(causal, diag[hh] * scale, MASK_VALUE)
            m0 = jnp.max(sd, axis=0, keepdims=True)
            p0 = jnp.exp(sd - m0)
            probs.append(p0.astype(bf16))
            stats.append((m0, jnp.sum(p0, axis=0, keepdims=True)))
        init = [stats[hh] + (jnp.dot(vt_sc[hh, i], probs[hh], preferred_element_type=f32),)
                for hh in heads]

        def kv_step(j, st):
            k0 = pl.multiple_of(j * blk, blk)
            scores = [lax.dot_general(kb_sc[hh, pl.ds(k0, blk), :], qb_sc[hh, pl.ds(q0, blk), :], _NT,
                                      preferred_element_type=f32) for hh in heads]
            probs, stats = [], []
            for hh in heads:
                m, l, _ = st[hh]
                sj = scores[hh] * scale + bias_sc[hh, i, pl.ds(j, 1), :]
                m_new = jnp.maximum(m, jnp.max(sj, axis=0, keepdims=True))
                alpha = jnp.exp(m - m_new)
                p = jnp.exp(sj - m_new)
                probs.append(p.astype(bf16))
                stats.append((m_new, alpha * l + jnp.sum(p, axis=0, keepdims=True), alpha))
            out = []
            for hh in heads:
                m_new, l, alpha = stats[hh]
                acc = alpha * st[hh][2] + jnp.dot(vt_sc[hh, j], probs[hh], preferred_element_type=f32)
                out.append((m_new, l, acc))
            return tuple(out)

        final = lax.fori_loop(0, i, kv_step, tuple(init))
        for hh in heads:
            _, l, acc = final[hh]
            ot_sc[hh, i] = acc * (1.0 / l)
        return carry

    lax.fori_loop(0, nb, q_tile, 0)
    for hh in heads:
        for i in range(nb):
            o_ref[i * blk:(i + 1) * blk, hh * HEAD_DIM:(hh + 1) * HEAD_DIM] = jnp.transpose(ot_sc[hh, i])


def _moba(proj, cos_t, sin_t, b, s):
    n = b * s
    nb = s // MOBA_BLOCK
    topk = min(MOBA_TOPK, nb - 1)
    aw = ATTN_HEADS * HEAD_DIM
    gw = MOBA_GROUP_LANES
    hp = aw // gw
    nh = gw // HEAD_DIM
    qc, kc, vc = 3 * aw // gw, 4 * aw // gw, 5 * aw // gw
    return pl.pallas_call(
        functools.partial(_moba_kernel, nb=nb, topk=topk),
        grid=(b, hp),
        in_specs=[pl.BlockSpec((s, gw), lambda bi, h: (bi, qc + h)),
                  pl.BlockSpec((s, gw), lambda bi, h: (bi, kc + h)),
                  pl.BlockSpec((s, gw), lambda bi, h: (bi, vc + h)),
                  pl.BlockSpec((s, gw), lambda bi, h: (0, 0)),
                  pl.BlockSpec((s, gw), lambda bi, h: (0, 0))],
        out_specs=pl.BlockSpec((s, gw), lambda bi, h: (bi, h)),
        out_shape=jax.ShapeDtypeStruct((n, aw), f32),
        scratch_shapes=[pltpu.VMEM((nh, s, HEAD_DIM), bf16),
                        pltpu.VMEM((nh, s, HEAD_DIM), bf16),
                        pltpu.VMEM((nh, nb, HEAD_DIM, MOBA_BLOCK), bf16),
                        pltpu.VMEM((nh, nb, nb, MOBA_BLOCK), f32),
                        pltpu.VMEM((nh, nb, HEAD_DIM, MOBA_BLOCK), f32)],
        compiler_params=_params(2),
        name="moba",
    )(proj, proj, proj, cos_t, sin_t)


def _mix_kernel(xin_ref, bg_ref, cg_ref, pxin_ref, pcg_ref, gc_ref, ga_ref, o_ref, x_ref,
                wc_ref, wco_ref, wao_ref, wm_ref, h_ref, *, tiles_per_seq):
    tm = xin_ref.shape[0]
    i = pl.program_id(0)
    keep_prev = jnp.where(i % tiles_per_seq == 0, 0.0, 1.0).astype(f32)
    u = cg_ref[...] * xin_ref[...]
    pu = pcg_ref[...] * pxin_ref[...] * keep_prev
    rows = lax.broadcasted_iota(i32, u.shape, 0)
    u1 = jnp.where(rows == 0, pu[SUBLANES - 1:SUBLANES, :], pltpu.roll(u, 1, 0))
    u2 = jnp.where(rows == 0, pu[SUBLANES - 2:SUBLANES - 1, :],
                   jnp.where(rows == 1, pu[SUBLANES - 1:SUBLANES, :], pltpu.roll(u, 2, 0)))
    y = wc_ref[0:1, :] * u2 + wc_ref[1:2, :] * u1 + wc_ref[2:3, :] * u
    y_conv = jnp.dot((bg_ref[...] * y).astype(bf16), wco_ref[...], preferred_element_type=f32)
    y_attn = jnp.dot(o_ref[...].astype(bf16), wao_ref[...], preferred_element_type=f32)
    merged = jax.nn.sigmoid(gc_ref[...]) * y_conv + jax.nn.sigmoid(ga_ref[...]) * y_attn
    h_ref[...] = x_ref[...] + jnp.dot(merged.astype(bf16), wm_ref[...], preferred_element_type=f32)


def _mix(proj, o, x2, w_conv, wco, wao, wm, s, tm):
    n, d = x2.shape
    cw = wco.shape[0]
    aw = wao.shape[0]
    rb = tm // SUBLANES
    prev = lambda col: (lambda i: (jnp.maximum(i * rb - 1, 0), col))
    gcol = 3 * cw + 3 * aw
    return pl.pallas_call(
        functools.partial(_mix_kernel, tiles_per_seq=s // tm),
        grid=(n // tm,),
        in_specs=[pl.BlockSpec((tm, cw), lambda i: (i, 0)),
                  pl.BlockSpec((tm, cw), lambda i: (i, 1)),
                  pl.BlockSpec((tm, cw), lambda i: (i, 2)),
                  pl.BlockSpec((SUBLANES, cw), prev(0)),
                  pl.BlockSpec((SUBLANES, cw), prev(2)),
                  pl.BlockSpec((tm, d), lambda i: (i, gcol // d)),
                  pl.BlockSpec((tm, d), lambda i: (i, gcol // d + 1)),
                  pl.BlockSpec((tm, aw), lambda i: (i, 0)),
                  pl.BlockSpec((tm, d), lambda i: (i, 0)),
                  pl.BlockSpec(w_conv.shape, lambda i: (0, 0)),
                  pl.BlockSpec(wco.shape, lambda i: (0, 0)),
                  pl.BlockSpec(wao.shape, lambda i: (0, 0)),
                  pl.BlockSpec(wm.shape, lambda i: (0, 0))],
        out_specs=pl.BlockSpec((tm, d), lambda i: (i, 0)),
        out_shape=jax.ShapeDtypeStruct((n, d), f32),
        compiler_params=_params(),
        name="mix",
    )(proj, proj, proj, proj, proj, proj, proj, o, x2, w_conv, wco, wao, wm)


def _memkv_kernel(m_ref, g_ref, w_ref, o_ref):
    a = _rms(m_ref[...], g_ref[...]).astype(bf16)
    o_ref[...] = jnp.dot(a, w_ref[...], preferred_element_type=f32).astype(bf16)


def _memkv(mem2, g, w, n_mem):
    nm, d = mem2.shape
    return pl.pallas_call(
        _memkv_kernel,
        grid=(nm // n_mem,),
        in_specs=[pl.BlockSpec((n_mem, d), lambda i: (i, 0)),
                  pl.BlockSpec((1, d), lambda i: (0, 0)),
                  pl.BlockSpec(w.shape, lambda i: (0, 0))],
        out_specs=pl.BlockSpec((n_mem, w.shape[1]), lambda i: (i, 0)),
        out_shape=jax.ShapeDtypeStruct((nm, w.shape[1]), bf16),
        compiler_params=_params(),
        name="memkv",
    )(mem2, g, w)


def _xattn_kernel(h_ref, g_ref, kv_ref, wq_ref, wo_ref, o_ref):
    h = h_ref[...]
    d = h.shape[1]
    hd = d // XATTN_HEADS
    q = jnp.dot(_rms(h, g_ref[...]).astype(bf16), wq_ref[...], preferred_element_type=f32)
    outs = []
    for hh in range(XATTN_HEADS):
        qh = q[:, hh * hd:(hh + 1) * hd].astype(bf16)
        kh = kv_ref[:, hh * hd:(hh + 1) * hd]
        vh = kv_ref[:, d + hh * hd:d + (hh + 1) * hd]
        sc = lax.dot_general(qh, kh, _NT, preferred_element_type=f32) * (hd ** -0.5)
        sc = sc - jnp.max(sc, axis=-1, keepdims=True)
        p = jnp.exp(sc)
        p = p / jnp.sum(p, axis=-1, keepdims=True)
        outs.append(jnp.dot(p.astype(bf16), vh, preferred_element_type=f32).astype(bf16))
    o = jnp.concatenate(outs, axis=-1)
    o_ref[...] = h + jnp.dot(o, wo_ref[...], preferred_element_type=f32)


def _xattn(h1, g, kv, wq, wo, s, n_mem, tm):
    n, d = h1.shape
    tps = s // tm
    return pl.pallas_call(
        _xattn_kernel,
        grid=(n // tm,),
        in_specs=[pl.BlockSpec((tm, d), lambda i: (i, 0)),
                  pl.BlockSpec((1, d), lambda i: (0, 0)),
                  pl.BlockSpec((n_mem, 2 * d), lambda i: (i // tps, 0)),
                  pl.BlockSpec(wq.shape, lambda i: (0, 0)),
                  pl.BlockSpec(wo.shape, lambda i: (0, 0))],
        out_specs=pl.BlockSpec((tm, d), lambda i: (i, 0)),
        out_shape=jax.ShapeDtypeStruct((n, d), f32),
        compiler_params=_params(),
        name="xattn",
    )(h1, g, kv, wq, wo)


def _topk_cols(sc, k, payload=None):
    r = sc.shape[0]
    ridx = lax.broadcasted_iota(i32, sc.shape, 0).astype(f32)
    vals, picks = [], []
    for _ in range(k):
        m = jnp.max(sc, axis=0, keepdims=True)
        im = jnp.min(jnp.where(sc == m, ridx, float(r)), axis=0, keepdims=True)
        hit = ridx == im
        vals.append(m)
        picks.append(im if payload is None else jnp.max(jnp.where(hit, payload, -1.0), axis=0, keepdims=True))
        sc = jnp.where(hit, -jnp.inf, sc)
    return jnp.concatenate(vals, axis=0), jnp.concatenate(picks, axis=0)


def _pair_candidates(k):
    return [(a, b) for a in range(k) for b in range(k) if (a + 1) * (b + 1) <= k]


def _select_kernel(h_ref, g_ref, wq_ref, keys_ref, a_ref, ids_ref, gate_ref):
    k = PEER_TOPK
    a = _rms(h_ref[...], g_ref[...])
    for c in range(a_ref.shape[1]):
        a_ref[:, c, :] = a[:, c * LANES:(c + 1) * LANES]
    q = jnp.dot(a.astype(bf16), wq_ref[...], preferred_element_type=f32)
    tm = q.shape[0]
    pairs = _pair_candidates(k)
    n_pad = -len(pairs) % SUBLANES
    id_rows, gate_rows = [], []
    for h in range(PEER_HEADS):
        tops = []
        for p in range(2):
            c0 = (h * 2 + p) * PEER_HALF
            qhp = q[:, c0:c0 + PEER_HALF].astype(bf16)
            st = lax.dot_general(keys_ref[h, p], qhp, _NT, preferred_element_type=f32)
            tops.append(_topk_cols(st, k))
        (v0, i0), (v1, i1) = tops
        i0 = i0 * float(PEER_N_KEYS)
        cand, cand_id = [], []
        for a_ in range(k):
            nb = sum(1 for (pa, _) in pairs if pa == a_)
            cand.append(v0[a_:a_ + 1, :] + v1[0:nb, :])
            cand_id.append(i0[a_:a_ + 1, :] + i1[0:nb, :])
        if n_pad:
            cand.append(jnp.full((n_pad, tm), -jnp.inf, f32))
            cand_id.append(jnp.zeros((n_pad, tm), f32))
        top_s, top_id = _topk_cols(jnp.concatenate(cand, axis=0), k, jnp.concatenate(cand_id, axis=0))
        id_rows.append(top_id.astype(i32))
        ex = jnp.exp(top_s - top_s[0:1, :])
        gate_rows.append(ex / jnp.sum(ex, axis=0, keepdims=True))
    ids_t = jnp.concatenate(id_rows, axis=0) * ROWS_PER_EXPERT
    for j in range(ids_ref.shape[0]):
        ids_ref[j] = ids_t[:, j * LANES:(j + 1) * LANES]
    gate_ref[...] = jnp.transpose(jnp.concatenate(gate_rows, axis=0))


def _select(h2, g, wq, keys, tm):
    n, d = h2.shape
    return pl.pallas_call(
        _select_kernel,
        grid=(n // tm,),
        in_specs=[pl.BlockSpec((tm, d), lambda i: (i, 0)),
                  pl.BlockSpec((1, d), lambda i: (0, 0)),
                  pl.BlockSpec(wq.shape, lambda i: (0, 0)),
                  pl.BlockSpec(keys.shape, lambda i: (0, 0, 0, 0))],
        out_specs=[pl.BlockSpec((tm, d // LANES, LANES), lambda i: (i, 0, 0)),
                   pl.BlockSpec((tm // LANES, N_SEL, LANES), lambda i: (i, 0, 0)),
                   pl.BlockSpec((tm, N_SEL), lambda i: (i, 0))],
        out_shape=[jax.ShapeDtypeStruct((n, d // LANES, LANES), f32),
                   jax.ShapeDtypeStruct((n // LANES, N_SEL, LANES), i32),
                   jax.ShapeDtypeStruct((n, N_SEL), f32)],
        compiler_params=_params(),
        name="peer_select",
    )(h2, g, wq, keys)


def _gelu_tanh(x):
    cdf = 0.5 * (1.0 + jnp.tanh(0.7978845608028654 * (x + 0.044715 * (x * x * x))))
    return x * cdf


def _pack_kernel(t_ref, o_ref):
    o_ref[...] = pltpu.bitcast(t_ref[...].astype(bf16), jnp.uint32)


def _pack_table(t):
    e, d = t.shape
    rows = e * d // LANES
    blk = min(rows, 8192)
    assert rows % blk == 0 and d == 2 * ROWS_PER_EXPERT * LANES
    return pl.pallas_call(
        _pack_kernel,
        grid=(rows // blk,),
        in_specs=[pl.BlockSpec((blk, LANES), lambda i: (i, 0))],
        out_specs=pl.BlockSpec((blk // 2, LANES), lambda i: (i, 0)),
        out_shape=jax.ShapeDtypeStruct((rows // 2, LANES), jnp.uint32),
        compiler_params=_params(),
        name="pack_table",
    )(t.reshape(rows, LANES))


def _expert_row(tab_ref, row0):
    words = tab_ref[pl.ds(pl.multiple_of(row0, ROWS_PER_EXPERT), ROWS_PER_EXPERT), :]
    return pltpu.bitcast(words, bf16).astype(f32)


def _token_id(ids_ref, e, t):
    return ids_ref.at[pl.ds(e * LANES, LANES)][t]


_FOLD_ORDER = (0, 4, 2, 6, 1, 5, 3, 7)


def _fold8(z):
    sub = lax.broadcasted_iota(i32, z[0].shape, 0)
    lo4 = sub < 4
    lo2 = (sub % 4) < 2
    even = (sub % 2) == 0
    c = [jnp.where(lo4, z[2 * k], z[2 * k + 1]) + pltpu.roll(jnp.where(lo4, z[2 * k + 1], z[2 * k]), 4, 0)
         for k in range(4)]
    e = [jnp.where(lo2, c[2 * k] + pltpu.roll(c[2 * k], 6, 0), c[2 * k + 1] + pltpu.roll(c[2 * k + 1], 2, 0))
         for k in range(2)]
    return jnp.where(even, e[0] + pltpu.roll(e[0], 7, 0), e[1] + pltpu.roll(e[1], 1, 0))


def _with_ids_tile(ids_hbm, bufs, sems, body):
    i = pl.program_id(0)
    n_steps = pl.num_programs(0)

    def tile_copy(tile, slot):
        return pltpu.make_async_copy(ids_hbm.at[tile], bufs[slot], sems.at[slot])

    @pl.when(i == 0)
    def _():
        tile_copy(0, 0).start()

    for slot in range(2):
        @pl.when(i % 2 == slot)
        def _():
            @pl.when(i + 1 < n_steps)
            def _():
                tile_copy(i + 1, 1 - slot).start()
            tile_copy(i, slot).wait()
            body(bufs[slot])


def _ids_scratch():
    return [pltpu.SMEM((N_SEL * LANES,), i32), pltpu.SMEM((N_SEL * LANES,), i32), pltpu.SemaphoreType.DMA((2,))]


def _peer_u_kernel(ids_hbm, x_ref, gate_ref, u_ref, w_ref, a_sc, ids0, ids1, sems):
    tt = x_ref.shape[0]
    lane = lax.broadcasted_iota(i32, (N_SEL, LANES), 1)

    def finish(part, t):
        col = jnp.sum(part, axis=1, keepdims=True)
        a_sc[...] = jnp.where(lane == t, col, a_sc[...])

    def body(ids_ref):
        def token(t, part_prev):
            finish(part_prev, t - 1)
            xt = x_ref[t]
            groups = []
            for g in range(N_SEL // SUBLANES):
                z = [_expert_row(u_ref, _token_id(ids_ref, g * SUBLANES + _FOLD_ORDER[k], t)) * xt
                     for k in range(SUBLANES)]
                groups.append(_fold8(z))
            return jnp.concatenate(groups, axis=0)

        a_sc[...] = jnp.zeros_like(a_sc)
        part = lax.fori_loop(0, tt, token, jnp.zeros((N_SEL, LANES), f32))
        finish(part, tt - 1)
        act = jnp.transpose(a_sc[...])[0:tt, :]
        w_ref[...] = gate_ref[...] * _gelu_tanh(act)

    _with_ids_tile(ids_hbm, (ids0, ids1), sems, body)


def _peer_u(ids, a3, gate, u_tab, tt, n_tiles):
    n = ids.shape[0] * tt
    assert tt == LANES and ids.shape[1] == N_SEL * LANES
    return pl.pallas_call(
        _peer_u_kernel,
        grid=(n_tiles,),
        in_specs=[pl.BlockSpec(memory_space=pl.ANY),
                  pl.BlockSpec((tt, SUBLANES, LANES), lambda i: (i, 0, 0)),
                  pl.BlockSpec((tt, N_SEL), lambda i: (i, 0)),
                  pl.BlockSpec(u_tab.shape, lambda i: (0, 0), pipeline_mode=pl.Buffered(1))],
        out_specs=pl.BlockSpec((tt, N_SEL), lambda i: (i, 0)),
        out_shape=jax.ShapeDtypeStruct((n, N_SEL), f32),
        scratch_shapes=[pltpu.VMEM((N_SEL, LANES), f32)] + _ids_scratch(),
        compiler_params=_params(),
        name="peer_u",
    )(ids, a3, gate, u_tab)


def _peer_v_kernel(ids_hbm, w_ref, v_ref, h_ref, g_ref, o_ref, ids0, ids1, sems, p_sc):
    tt = o_ref.shape[0]
    n_acc = 4

    def lane_bcast_weights(t):
        return jnp.transpose(jnp.broadcast_to(w_ref[pl.ds(t, 1), :], (LANES, N_SEL)))

    def body(ids_ref):
        def token(t, wb):
            wb_next = lane_bcast_weights(jnp.minimum(t + 1, tt - 1))
            accs = [jnp.zeros((SUBLANES, LANES), f32) for _ in range(n_acc)]
            for e in range(N_SEL):
                wv = jnp.broadcast_to(wb[e:e + 1, :], (SUBLANES, LANES))
                accs[e % n_acc] = accs[e % n_acc] + _expert_row(v_ref, _token_id(ids_ref, e, t)) * wv
            p_sc[t] = (accs[0] + accs[1]) + (accs[2] + accs[3])
            return wb_next

        lax.fori_loop(0, tt, token, lane_bcast_weights(0))
        peer = jnp.concatenate([p_sc[:, c, :] for c in range(SUBLANES)], axis=1)
        o_ref[...] = _rms(h_ref[...] + peer, g_ref[...])

    _with_ids_tile(ids_hbm, (ids0, ids1), sems, body)


def _peer_v(ids, w, v_tab, h2, g, tt, n_tiles):
    n, d = h2.shape
    assert tt == LANES and ids.shape[1] == N_SEL * LANES and n == ids.shape[0] * tt and d == SUBLANES * LANES
    return pl.pallas_call(
        _peer_v_kernel,
        grid=(n_tiles,),
        in_specs=[pl.BlockSpec(memory_space=pl.ANY),
                  pl.BlockSpec((tt, N_SEL), lambda i: (i, 0)),
                  pl.BlockSpec(v_tab.shape, lambda i: (0, 0), pipeline_mode=pl.Buffered(1)),
                  pl.BlockSpec((tt, d), lambda i: (i, 0)),
                  pl.BlockSpec((1, d), lambda i: (0, 0))],
        out_specs=pl.BlockSpec((tt, d), lambda i: (i, 0)),
        out_shape=jax.ShapeDtypeStruct((n, d), f32),
        scratch_shapes=_ids_scratch() + [pltpu.VMEM((tt, SUBLANES, LANES), f32)],
        compiler_params=_params(),
        name="peer_v",
    )(ids, w, v_tab, h2, g)


SC_WORKERS = 32
SC_LANES = 16
SC_UNROLL = 2
SC_TILE_SHARE_64THS = 17


def _peer_sc(ids_tok, x3, gate, u_words, v_words, t0):
    from jax.experimental.pallas import tpu_sc as plsc
    n_sc = ids_tok.shape[0]
    words = v_words.shape[1]
    d = 2 * words
    tpw = n_sc // SC_WORKERS
    assert n_sc % SC_WORKERS == 0 and words == ROWS_PER_EXPERT * LANES
    half = words // 2
    nvec = half // SC_LANES
    mesh = plsc.VectorSubcoreMesh(core_axis_name="c", subcore_axis_name="s")

    def halves(wd):
        return (lax.bitcast_convert_type(wd << 16, f32), lax.bitcast_convert_type(wd & jnp.int32(-65536), f32))

    def body(u_hbm, v_hbm, ids_hbm, x_hbm, g_hbm, out_hbm, idx_v, g_v, w_v, x_v, rows_v, part_v, out_v, sem):
        wid = lax.axis_index("s") * 2 + lax.axis_index("c")
        lane = lax.broadcasted_iota(i32, (SC_LANES,), 0)

        def token(i, carry):
            t = wid * tpw + i
            pltpu.sync_copy(ids_hbm.at[t], idx_v)
            pltpu.sync_copy(g_hbm.at[t0 + t], g_v)
            pltpu.sync_copy(x_hbm.at[t0 + t], x_v)
            pltpu.async_copy(u_hbm.at[idx_v], rows_v, sem).wait()
            for p in range(2):
                xs = []
                for j in range(nvec):
                    o = p * half + j * SC_LANES
                    r, l0 = o // LANES, o % LANES
                    xs.append((x_v[2 * r, pl.ds(l0, SC_LANES)], x_v[2 * r + 1, pl.ds(l0, SC_LANES)]))

                @plsc.parallel_loop(0, N_SEL, unroll=SC_UNROLL)
                def _(e, p=p, xs=xs):
                    acc = None
                    for j in range(nvec):
                        lo, hi = halves(rows_v[e, pl.ds(p * half + j * SC_LANES, SC_LANES)])
                        term = lo * xs[j][0] + hi * xs[j][1]
                        acc = term if acc is None else acc + term
                    part_v[e, :] = acc if p == 0 else part_v[e, :] + acc
            for k in range(N_SEL // SC_LANES):
                act = jnp.zeros((SC_LANES,), f32)
                for q in range(SC_LANES):
                    act = jnp.where(lane == q, jnp.sum(part_v[k * SC_LANES + q, :]), act)
                inner = 0.7978845608028654 * (act + 0.044715 * (act * act * act))
                tanh = 1.0 - 2.0 / (jnp.exp(2.0 * inner) + 1.0)
                w_v[pl.ds(k * SC_LANES, SC_LANES)] = g_v[pl.ds(k * SC_LANES, SC_LANES)] * (act * (0.5 * (1.0 + tanh)))
            pltpu.async_copy(v_hbm.at[idx_v], rows_v, sem).wait()
            for p in range(2):
                def acc_pair(e, accs, p=p):
                    wvec = w_v[pl.ds(pl.multiple_of((e // SC_LANES) * SC_LANES, SC_LANES), SC_LANES)]
                    wr = jnp.sum(jnp.where(lane == e % SC_LANES, wvec, 0.0))
                    out = []
                    for j in range(nvec):
                        lo, hi = halves(rows_v[e, pl.ds(p * half + j * SC_LANES, SC_LANES)])
                        out.append(accs[2 * j] + lo * wr)
                        out.append(accs[2 * j + 1] + hi * wr)
                    return tuple(out)

                accs = plsc.parallel_loop(0, N_SEL, unroll=SC_UNROLL, carry=tuple(
                    jnp.zeros((SC_LANES,), f32) for _ in range(2 * nvec)))(acc_pair)
                for j in range(nvec):
                    o = p * half + j * SC_LANES
                    r, l0 = o // LANES, o % LANES
                    out_v[pl.ds(2 * LANES * r + l0, SC_LANES)] = accs[2 * j]
                    out_v[pl.ds(2 * LANES * r + LANES + l0, SC_LANES)] = accs[2 * j + 1]
            pltpu.sync_copy(out_v, out_hbm.at[t])
            return carry

        lax.fori_loop(0, tpw, token, 0)

    return pl.kernel(
        body,
        out_type=jax.ShapeDtypeStruct((n_sc, d), f32),
        mesh=mesh,
        scratch_types=[pltpu.VMEM((N_SEL,), i32), pltpu.VMEM((N_SEL,), f32), pltpu.VMEM((N_SEL,), f32),
                       pltpu.VMEM((SUBLANES, LANES), f32), pltpu.VMEM((N_SEL, words), i32),
                       pltpu.VMEM((N_SEL, SC_LANES), f32), pltpu.VMEM((d,), f32), pltpu.SemaphoreType.DMA],
        compiler_params=pltpu.CompilerParams(needs_layout_passes=False),
        name="peer_sc",
    )(u_words, v_words, ids_tok, x3, gate)


def _final_tail_kernel(out_in_ref, h_ref, p_ref, g_ref, o_ref):
    del out_in_ref
    o_ref[...] = _rms(h_ref[...] + p_ref[...], g_ref[...])


def _final_tail(out, h2, peer_tail, g, tm, first_tile):
    n, d = h2.shape
    tail_tiles = peer_tail.shape[0] // tm
    return pl.pallas_call(
        _final_tail_kernel,
        grid=(tail_tiles,),
        in_specs=[pl.BlockSpec(memory_space=pl.ANY),
                  pl.BlockSpec((tm, d), lambda i: (i + first_tile, 0)),
                  pl.BlockSpec((tm, d), lambda i: (i, 0)),
                  pl.BlockSpec((1, d), lambda i: (0, 0))],
        out_specs=pl.BlockSpec((tm, d), lambda i: (i + first_tile, 0)),
        out_shape=jax.ShapeDtypeStruct((n, d), f32),
        input_output_aliases={0: 0},
        compiler_params=_params(),
        name="final_tail",
    )(out, h2, peer_tail, g)


def _rope_tables(s):
    half = HEAD_DIM // 2
    inv = ROPE_THETA ** (-jnp.arange(half, dtype=f32) / half)
    ang = jnp.arange(s, dtype=f32)[:, None] * inv[None, :]
    cos, sin = jnp.cos(ang), jnp.sin(ang)
    reps = MOBA_GROUP_LANES // HEAD_DIM
    cos_t = jnp.tile(jnp.concatenate([cos, cos], axis=1), (1, reps))
    sin_t = jnp.tile(jnp.concatenate([-sin, sin], axis=1), (1, reps))
    return cos_t, sin_t


def _row_tile(s, want):
    t = min(want, s)
    while s % t:
        t //= 2
    assert t % SUBLANES == 0
    return t


def kernel(x, mem, g_mix, w_in, w_conv, w_conv_out, w_attn_out, w_merge, g_xattn, g_mem,
           w_xq, w_xkv, w_xo, g_ffn, w_pq, peer_sub_keys, peer_u, peer_v, g_final):
    b, s, d = x.shape
    n = b * s
    n_mem = mem.shape[1]
    assert w_in.shape[0] == 1, "single-layer trunk only"
    assert d == SUBLANES * LANES and s % MOBA_BLOCK == 0
    n_exp = peer_u.shape[1]
    cos_t, sin_t = _rope_tables(s)
    tm = _row_tile(s, 512)
    h = x.reshape(n, d)
    mem2 = mem.reshape(b * n_mem, d)
    l = 0
    proj = _inproj(h, g_mix[l][None, :], w_in[l].astype(bf16), tm)
    o = _moba(proj, cos_t, sin_t, b, s)
    h1 = _mix(proj, o, h, w_conv[l], w_conv_out[l].astype(bf16), w_attn_out[l].astype(bf16),
              w_merge[l].astype(bf16), s, tm)
    kv = _memkv(mem2, g_mem[l][None, :], w_xkv[l].astype(bf16), n_mem)
    h2 = _xattn(h1, g_xattn[l][None, :], kv, w_xq[l].astype(bf16), w_xo[l].astype(bf16), s, n_mem, tm)
    a3, ids, gate = _select(h2, g_ffn[l][None, :], w_pq[l].astype(bf16),
                            peer_sub_keys[l].astype(bf16), _row_tile(s, 256))
    u_tab = _pack_table(peer_u[l])
    v_tab = _pack_table(peer_v[l])
    tt = _row_tile(s, LANES)
    ids = ids.reshape(n // LANES, N_SEL * LANES)
    n_tiles = n // tt
    sc_tiles = (n_tiles * SC_TILE_SHARE_64THS) // 64
    tc_tiles = n_tiles - sc_tiles
    w = _peer_u(ids, a3, gate, u_tab, tt, tc_tiles)
    out = _peer_v(ids, w, v_tab, h2, g_final[None, :], tt, tc_tiles)
    if sc_tiles:
        ids_tok = (ids.reshape(n_tiles, N_SEL, tt)[tc_tiles:] // ROWS_PER_EXPERT).transpose(0, 2, 1)
        as_rows = lambda tab: lax.bitcast_convert_type(tab, i32).reshape(n_exp, ROWS_PER_EXPERT * LANES)
        peer_sc = _peer_sc(ids_tok.reshape(sc_tiles * tt, N_SEL), a3, gate, as_rows(u_tab), as_rows(v_tab),
                           tc_tiles * tt)
        out = _final_tail(out, h2, peer_sc, g_final[None, :], tt, tc_tiles)
    return out.reshape(b, s, d)
```

```python
import functools

import jax
import jax.numpy as jnp
from jax import lax
from jax.experimental import pallas as pl
from jax.experimental.pallas import tpu as pltpu

f32 = jnp.float32
bf16 = jnp.bfloat16
i32 = jnp.int32

EPS = 1e-6
MASK_VALUE = -1e30
ROPE_THETA = 10000.0

XATTN_HEADS = 4
ATTN_HEADS = 8
HEAD_DIM = 64
MOBA_BLOCK = 256
MOBA_TOPK = 3
MOBA_GROUP_LANES = 256
PEER_HEADS = 8
PEER_N_KEYS = 128
PEER_HALF = 128
PEER_TOPK = 16
N_SEL = PEER_HEADS * PEER_TOPK
ROWS_PER_EXPERT = 4

LANES = 128
SUBLANES = 8
VMEM_LIMIT = 56 * 1024 * 1024

_NT = (((1,), (1,)), ((), ()))


def _params(n_axes=1, vmem=VMEM_LIMIT):
    return pltpu.CompilerParams(dimension_semantics=("arbitrary",) * n_axes, vmem_limit_bytes=vmem)


def _rms(x, g):
    return x * lax.rsqrt(jnp.mean(x * x, axis=-1, keepdims=True) + EPS) * g


def _inproj_kernel(x_ref, g_ref, w_ref, o_ref, *, chunk):
    a = _rms(x_ref[...], g_ref[...]).astype(bf16)
    for c in range(o_ref.shape[1] // chunk):
        o_ref[:, c * chunk:(c + 1) * chunk] = jnp.dot(
            a, w_ref[:, c * chunk:(c + 1) * chunk], preferred_element_type=f32)


def _inproj(x2, g, w, tm):
    n, d = x2.shape
    width = w.shape[1]
    return pl.pallas_call(
        functools.partial(_inproj_kernel, chunk=width // 4),
        grid=(n // tm,),
        in_specs=[pl.BlockSpec((tm, d), lambda i: (i, 0)),
                  pl.BlockSpec((1, d), lambda i: (0, 0)),
                  pl.BlockSpec((d, width), lambda i: (0, 0), pipeline_mode=pl.Buffered(1))],
        out_specs=pl.BlockSpec((tm, width), lambda i: (i, 0)),
        out_shape=jax.ShapeDtypeStruct((n, width), f32),
        compiler_params=_params(),
        name="inproj",
    )(x2, g, w)


def _moba_kernel(q_ref, k_ref, v_ref, cos_ref, sin_ref, o_ref,
                 qb_sc, kb_sc, vt_sc, bias_sc, ot_sc, *, nb, topk):
    blk = MOBA_BLOCK
    s, gw = q_ref.shape
    lane = lax.broadcasted_iota(i32, (s, gw), 1)
    first_half = (lane % HEAD_DIM) < (HEAD_DIM // 2)
    cos = cos_ref[...]
    sin = sin_ref[...]

    def rope(t):
        partner = jnp.where(first_half, pltpu.roll(t, gw - HEAD_DIM // 2, 1),
                            pltpu.roll(t, HEAD_DIM // 2, 1))
        return t * cos + partner * sin

    q2 = rope(q_ref[...])
    k2 = rope(k_ref[...])
    v2 = v_ref[...]
    scale = HEAD_DIM ** -0.5
    blk_of_q = lax.broadcasted_iota(i32, (nb, s), 1) // blk
    m_iota = lax.broadcasted_iota(i32, (nb, s), 0)
    past = m_iota < blk_of_q
    kpos = lax.broadcasted_iota(i32, (blk, blk), 0)
    qpos = lax.broadcasted_iota(i32, (blk, blk), 1)
    causal = kpos <= qpos

    heads = range(gw // HEAD_DIM)
    for hh in heads:
        qh = q2[:, hh * HEAD_DIM:(hh + 1) * HEAD_DIM]
        kh = k2[:, hh * HEAD_DIM:(hh + 1) * HEAD_DIM]
        vh = v2[:, hh * HEAD_DIM:(hh + 1) * HEAD_DIM]
        kbar = jnp.mean(kh.reshape(nb, blk, HEAD_DIM), axis=1)
        gt = lax.dot_general(kbar, qh, _NT, precision=lax.Precision.HIGHEST,
                             preferred_element_type=f32)
        rows = []
        for n in range(nb):
            gn = gt[n:n + 1, :]
            beats = past & ((gt > gn) | ((gt == gn) & (m_iota < n)))
            cnt = jnp.sum(beats.astype(i32), axis=0, keepdims=True)
            sel = (blk_of_q[0:1, :] > n) & (cnt < topk)
            rows.append(jnp.where(sel, 0.0, MASK_VALUE).astype(f32))
        bias = jnp.concatenate(rows, axis=0)
        vt = jnp.transpose(vh).astype(bf16)
        for i in range(nb):
            bias_sc[hh, i] = bias[:, i * blk:(i + 1) * blk]
            vt_sc[hh, i] = vt[:, i * blk:(i + 1) * blk]
        qb_sc[hh] = qh.astype(bf16)
        kb_sc[hh] = kh.astype(bf16)

    def q_tile(i, carry):
        q0 = pl.multiple_of(i * blk, blk)
        diag = [lax.dot_general(kb_sc[hh, pl.ds(q0, blk), :], qb_sc[hh, pl.ds(q0, blk), :], _NT,
                                preferred_element_type=f32) for hh in heads]
        probs, stats = [], []
        for hh in heads:
            sd = jnp.where(causal, diag[hh] * scale, MASK_VALUE)
            m0 = jnp.max(sd, axis=0, keepdims=True)
            p0 = jnp.exp(sd - m0)
            probs.append(p0.astype(bf16))
            stats.append((m0, jnp.sum(p0, axis=0, keepdims=True)))
        init = [stats[hh] + (jnp.dot(vt_sc[hh, i], probs[hh], preferred_element_type=f32),)
                for hh in heads]

        def kv_step(j, st):
            k0 = pl.multiple_of(j * blk, blk)
            scores = [lax.dot_general(kb_sc[hh, pl.ds(k0, blk), :], qb_sc[hh, pl.ds(q0, blk), :], _NT,
                                      preferred_element_type=f32) for hh in heads]
            probs, stats = [], []
            for hh in heads:
                m, l, _ = st[hh]
                sj = scores[hh] * scale + bias_sc[hh, i, pl.ds(j, 1), :]
                m_new = jnp.maximum(m, jnp.max(sj, axis=0, keepdims=True))
                alpha = jnp.exp(m - m_new)
                p = jnp.exp(sj - m_new)
                probs.append(p.astype(bf16))
                stats.append((m_new, alpha * l + jnp.sum(p, axis=0, keepdims=True), alpha))
            out = []
            for hh in heads:
                m_new, l, alpha = stats[hh]
                acc = alpha * st[hh][2] + jnp.dot(vt_sc[hh, j], probs[hh], preferred_element_type=f32)
                out.append((m_new, l, acc))
            return tuple(out)

        final = lax.fori_loop(0, i, kv_step, tuple(init))
        for hh in heads:
            _, l, acc = final[hh]
            ot_sc[hh, i] = acc * (1.0 / l)
        return carry

    lax.fori_loop(0, nb, q_tile, 0)
    for hh in heads:
        for i in range(nb):
            o_ref[i * blk:(i + 1) * blk, hh * HEAD_DIM:(hh + 1) * HEAD_DIM] = jnp.transpose(ot_sc[hh, i])


def _moba(proj, cos_t, sin_t, b, s):
    n = b * s
    nb = s // MOBA_BLOCK
    topk = min(MOBA_TOPK, nb - 1)
    aw = ATTN_HEADS * HEAD_DIM
    gw = MOBA_GROUP_LANES
    hp = aw // gw
    nh = gw // HEAD_DIM
    qc, kc, vc = 3 * aw // gw, 4 * aw // gw, 5 * aw // gw
    return pl.pallas_call(
        functools.partial(_moba_kernel, nb=nb, topk=topk),
        grid=(b, hp),
        in_specs=[pl.BlockSpec((s, gw), lambda bi, h: (bi, qc + h)),
                  pl.BlockSpec((s, gw), lambda bi, h: (bi, kc + h)),
                  pl.BlockSpec((s, gw), lambda bi, h: (bi, vc + h)),
                  pl.BlockSpec((s, gw), lambda bi, h: (0, 0)),
                  pl.BlockSpec((s, gw), lambda bi, h: (0, 0))],
        out_specs=pl.BlockSpec((s, gw), lambda bi, h: (bi, h)),
        out_shape=jax.ShapeDtypeStruct((n, aw), f32),
        scratch_shapes=[pltpu.VMEM((nh, s, HEAD_DIM), bf16),
                        pltpu.VMEM((nh, s, HEAD_DIM), bf16),
                        pltpu.VMEM((nh, nb, HEAD_DIM, MOBA_BLOCK), bf16),
                        pltpu.VMEM((nh, nb, nb, MOBA_BLOCK), f32),
                        pltpu.VMEM((nh, nb, HEAD_DIM, MOBA_BLOCK), f32)],
        compiler_params=_params(2),
        name="moba",
    )(proj, proj, proj, cos_t, sin_t)


def _mix_kernel(xin_ref, bg_ref, cg_ref, pxin_ref, pcg_ref, gc_ref, ga_ref, o_ref, x_ref,
                wc_ref, wco_ref, wao_ref, wm_ref, h_ref, *, tiles_per_seq):
    tm = xin_ref.shape[0]
    i = pl.program_id(0)
    keep_prev = jnp.where(i % tiles_per_seq == 0, 0.0, 1.0).astype(f32)
    u = cg_ref[...] * xin_ref[...]
    pu = pcg_ref[...] * pxin_ref[...] * keep_prev
    rows = lax.broadcasted_iota(i32, u.shape, 0)
    u1 = jnp.where(rows == 0, pu[SUBLANES - 1:SUBLANES, :], pltpu.roll(u, 1, 0))
    u2 = jnp.where(rows == 0, pu[SUBLANES - 2:SUBLANES - 1, :],
                   jnp.where(rows == 1, pu[SUBLANES - 1:SUBLANES, :], pltpu.roll(u, 2, 0)))
    y = wc_ref[0:1, :] * u2 + wc_ref[1:2, :] * u1 + wc_ref[2:3, :] * u
    y_conv = jnp.dot((bg_ref[...] * y).astype(bf16), wco_ref[...], preferred_element_type=f32)
    y_attn = jnp.dot(o_ref[...].astype(bf16), wao_ref[...], preferred_element_type=f32)
    merged = jax.nn.sigmoid(gc_ref[...]) * y_conv + jax.nn.sigmoid(ga_ref[...]) * y_attn
    h_ref[...] = x_ref[...] + jnp.dot(merged.astype(bf16), wm_ref[...], preferred_element_type=f32)


def _mix(proj, o, x2, w_conv, wco, wao, wm, s, tm):
    n, d = x2.shape
    cw = wco.shape[0]
    aw = wao.shape[0]
    rb = tm // SUBLANES
    prev = lambda col: (lambda i: (jnp.maximum(i * rb - 1, 0), col))
    gcol = 3 * cw + 3 * aw
    return pl.pallas_call(
        functools.partial(_mix_kernel, tiles_per_seq=s // tm),
        grid=(n // tm,),
        in_specs=[pl.BlockSpec((tm, cw), lambda i: (i, 0)),
                  pl.BlockSpec((tm, cw), lambda i: (i, 1)),
                  pl.BlockSpec((tm, cw), lambda i: (i, 2)),
                  pl.BlockSpec((SUBLANES, cw), prev(0)),
                  pl.BlockSpec((SUBLANES, cw), prev(2)),
                  pl.BlockSpec((tm, d), lambda i: (i, gcol // d)),
                  pl.BlockSpec((tm, d), lambda i: (i, gcol // d + 1)),
                  pl.BlockSpec((tm, aw), lambda i: (i, 0)),
                  pl.BlockSpec((tm, d), lambda i: (i, 0)),
                  pl.BlockSpec(w_conv.shape, lambda i: (0, 0)),
                  pl.BlockSpec(wco.shape, lambda i: (0, 0)),
                  pl.BlockSpec(wao.shape, lambda i: (0, 0)),
                  pl.BlockSpec(wm.shape, lambda i: (0, 0))],
        out_specs=pl.BlockSpec((tm, d), lambda i: (i, 0)),
        out_shape=jax.ShapeDtypeStruct((n, d), f32),
        compiler_params=_params(),
        name="mix",
    )(proj, proj, proj, proj, proj, proj, proj, o, x2, w_conv, wco, wao, wm)


def _memkv_kernel(m_ref, g_ref, w_ref, o_ref):
    a = _rms(m_ref[...], g_ref[...]).astype(bf16)
    o_ref[...] = jnp.dot(a, w_ref[...], preferred_element_type=f32).astype(bf16)


def _memkv(mem2, g, w, n_mem):
    nm, d = mem2.shape
    return pl.pallas_call(
        _memkv_kernel,
        grid=(nm // n_mem,),
        in_specs=[pl.BlockSpec((n_mem, d), lambda i: (i, 0)),
                  pl.BlockSpec((1, d), lambda i: (0, 0)),
                  pl.BlockSpec(w.shape, lambda i: (0, 0))],
        out_specs=pl.BlockSpec((n_mem, w.shape[1]), lambda i: (i, 0)),
        out_shape=jax.ShapeDtypeStruct((nm, w.shape[1]), bf16),
        compiler_params=_params(),
        name="memkv",
    )(mem2, g, w)


def _xattn_kernel(h_ref, g_ref, kv_ref, wq_ref, wo_ref, o_ref):
    h = h_ref[...]
    d = h.shape[1]
    hd = d // XATTN_HEADS
    q = jnp.dot(_rms(h, g_ref[...]).astype(bf16), wq_ref[...], preferred_element_type=f32)
    outs = []
    for hh in range(XATTN_HEADS):
        qh = q[:, hh * hd:(hh + 1) * hd].astype(bf16)
        kh = kv_ref[:, hh * hd:(hh + 1) * hd]
        vh = kv_ref[:, d + hh * hd:d + (hh + 1) * hd]
        sc = lax.dot_general(qh, kh, _NT, preferred_element_type=f32) * (hd ** -0.5)
        sc = sc - jnp.max(sc, axis=-1, keepdims=True)
        p = jnp.exp(sc)
        p = p / jnp.sum(p, axis=-1, keepdims=True)
        outs.append(jnp.dot(p.astype(bf16), vh, preferred_element_type=f32).astype(bf16))
    o = jnp.concatenate(outs, axis=-1)
    o_ref[...] = h + jnp.dot(o, wo_ref[...], preferred_element_type=f32)


def _xattn(h1, g, kv, wq, wo, s, n_mem, tm):
    n, d = h1.shape
    tps = s // tm
    return pl.pallas_call(
        _xattn_kernel,
        grid=(n // tm,),
        in_specs=[pl.BlockSpec((tm, d), lambda i: (i, 0)),
                  pl.BlockSpec((1, d), lambda i: (0, 0)),
                  pl.BlockSpec((n_mem, 2 * d), lambda i: (i // tps, 0)),
                  pl.BlockSpec(wq.shape, lambda i: (0, 0)),
                  pl.BlockSpec(wo.shape, lambda i: (0, 0))],
        out_specs=pl.BlockSpec((tm, d), lambda i: (i, 0)),
        out_shape=jax.ShapeDtypeStruct((n, d), f32),
        compiler_params=_params(),
        name="xattn",
    )(h1, g, kv, wq, wo)


def _topk_cols(sc, k, payload=None):
    r = sc.shape[0]
    ridx = lax.broadcasted_iota(i32, sc.shape, 0).astype(f32)
    vals, picks = [], []
    for _ in range(k):
        m = jnp.max(sc, axis=0, keepdims=True)
        im = jnp.min(jnp.where(sc == m, ridx, float(r)), axis=0, keepdims=True)
        hit = ridx == im
        vals.append(m)
        picks.append(im if payload is None else jnp.max(jnp.where(hit, payload, -1.0), axis=0, keepdims=True))
        sc = jnp.where(hit, -jnp.inf, sc)
    return jnp.concatenate(vals, axis=0), jnp.concatenate(picks, axis=0)


def _pair_candidates(k):
    return [(a, b) for a in range(k) for b in range(k) if (a + 1) * (b + 1) <= k]


def _select_kernel(h_ref, g_ref, wq_ref, keys_ref, a_ref, ids_ref, gate_ref):
    k = PEER_TOPK
    a = _rms(h_ref[...], g_ref[...])
    for c in range(a_ref.shape[1]):
        a_ref[:, c, :] = a[:, c * LANES:(c + 1) * LANES]
    q = jnp.dot(a.astype(bf16), wq_ref[...], preferred_element_type=f32)
    tm = q.shape[0]
    pairs = _pair_candidates(k)
    n_pad = -len(pairs) % SUBLANES
    id_rows, gate_rows = [], []
    for h in range(PEER_HEADS):
        tops = []
        for p in range(2):
            c0 = (h * 2 + p) * PEER_HALF
            qhp = q[:, c0:c0 + PEER_HALF].astype(bf16)
            st = lax.dot_general(keys_ref[h, p], qhp, _NT, preferred_element_type=f32)
            tops.append(_topk_cols(st, k))
        (v0, i0), (v1, i1) = tops
        i0 = i0 * float(PEER_N_KEYS)
        cand, cand_id = [], []
        for a_ in range(k):
            nb = sum(1 for (pa, _) in pairs if pa == a_)
            cand.append(v0[a_:a_ + 1, :] + v1[0:nb, :])
            cand_id.append(i0[a_:a_ + 1, :] + i1[0:nb, :])
        if n_pad:
            cand.append(jnp.full((n_pad, tm), -jnp.inf, f32))
            cand_id.append(jnp.zeros((n_pad, tm), f32))
        top_s, top_id = _topk_cols(jnp.concatenate(cand, axis=0), k, jnp.concatenate(cand_id, axis=0))
        id_rows.append(top_id.astype(i32))
        ex = jnp.exp(top_s - top_s[0:1, :])
        gate_rows.append(ex / jnp.sum(ex, axis=0, keepdims=True))
    ids_t = jnp.concatenate(id_rows, axis=0) * ROWS_PER_EXPERT
    for j in range(ids_ref.shape[0]):
        ids_ref[j] = ids_t[:, j * LANES:(j + 1) * LANES]
    gate_ref[...] = jnp.transpose(jnp.concatenate(gate_rows, axis=0))


def _select(h2, g, wq, keys, tm, first_block, n_blocks):
    d = h2.shape[1]
    n = n_blocks * tm
    return pl.pallas_call(
        _select_kernel,
        grid=(n_blocks,),
        in_specs=[pl.BlockSpec((tm, d), lambda i: (i + first_block, 0)),
                  pl.BlockSpec((1, d), lambda i: (0, 0)),
                  pl.BlockSpec(wq.shape, lambda i: (0, 0)),
                  pl.BlockSpec(keys.shape, lambda i: (0, 0, 0, 0))],
        out_specs=[pl.BlockSpec((tm, d // LANES, LANES), lambda i: (i, 0, 0)),
                   pl.BlockSpec((tm // LANES, N_SEL, LANES), lambda i: (i, 0, 0)),
                   pl.BlockSpec((tm, N_SEL), lambda i: (i, 0))],
        out_shape=[jax.ShapeDtypeStruct((n, d // LANES, LANES), f32),
                   jax.ShapeDtypeStruct((n // LANES, N_SEL, LANES), i32),
                   jax.ShapeDtypeStruct((n, N_SEL), f32)],
        compiler_params=_params(),
        name="peer_select",
    )(h2, g, wq, keys)


def _gelu_tanh(x):
    cdf = 0.5 * (1.0 + jnp.tanh(0.7978845608028654 * (x + 0.044715 * (x * x * x))))
    return x * cdf


def _pack_kernel(t_ref, o_ref):
    o_ref[...] = pltpu.bitcast(t_ref[...].astype(bf16), jnp.uint32)


def _pack_table(t):
    e, d = t.shape
    rows = e * d // LANES
    blk = min(rows, 8192)
    assert rows % blk == 0 and d == 2 * ROWS_PER_EXPERT * LANES
    return pl.pallas_call(
        _pack_kernel,
        grid=(rows // blk,),
        in_specs=[pl.BlockSpec((blk, LANES), lambda i: (i, 0))],
        out_specs=pl.BlockSpec((blk // 2, LANES), lambda i: (i, 0)),
        out_shape=jax.ShapeDtypeStruct((rows // 2, LANES), jnp.uint32),
        compiler_params=_params(),
        name="pack_table",
    )(t.reshape(rows, LANES))


def _expert_row(tab_ref, row0):
    words = tab_ref[pl.ds(pl.multiple_of(row0, ROWS_PER_EXPERT), ROWS_PER_EXPERT), :]
    return pltpu.bitcast(words, bf16).astype(f32)


def _token_id(ids_ref, e, t):
    return ids_ref.at[pl.ds(e * LANES, LANES)][t]


_FOLD_ORDER = (0, 4, 2, 6, 1, 5, 3, 7)


def _fold8(z):
    sub = lax.broadcasted_iota(i32, z[0].shape, 0)
    lo4 = sub < 4
    lo2 = (sub % 4) < 2
    even = (sub % 2) == 0
    c = [jnp.where(lo4, z[2 * k], z[2 * k + 1]) + pltpu.roll(jnp.where(lo4, z[2 * k + 1], z[2 * k]), 4, 0)
         for k in range(4)]
    e = [jnp.where(lo2, c[2 * k] + pltpu.roll(c[2 * k], 6, 0), c[2 * k + 1] + pltpu.roll(c[2 * k + 1], 2, 0))
         for k in range(2)]
    return jnp.where(even, e[0] + pltpu.roll(e[0], 7, 0), e[1] + pltpu.roll(e[1], 1, 0))


def _with_ids_tile(ids_hbm, bufs, sems, body):
    i = pl.program_id(0)
    n_steps = pl.num_programs(0)

    def tile_copy(tile, slot):
        return pltpu.make_async_copy(ids_hbm.at[tile], bufs[slot], sems.at[slot])

    @pl.when(i == 0)
    def _():
        tile_copy(0, 0).start()

    for slot in range(2):
        @pl.when(i % 2 == slot)
        def _():
            @pl.when(i + 1 < n_steps)
            def _():
                tile_copy(i + 1, 1 - slot).start()
            tile_copy(i, slot).wait()
            body(bufs[slot])


def _ids_scratch():
    return [pltpu.SMEM((N_SEL * LANES,), i32), pltpu.SMEM((N_SEL * LANES,), i32), pltpu.SemaphoreType.DMA((2,))]


def _peer_u_kernel(ids_hbm, x_ref, gate_ref, u_ref, w_ref, a_sc, ids0, ids1, sems):
    tt = x_ref.shape[0]
    lane = lax.broadcasted_iota(i32, (N_SEL, LANES), 1)

    def finish(part, t):
        col = jnp.sum(part, axis=1, keepdims=True)
        a_sc[...] = jnp.where(lane == t, col, a_sc[...])

    def body(ids_ref):
        def token(t, part_prev):
            finish(part_prev, t - 1)
            xt = x_ref[t]
            groups = []
            for g in range(N_SEL // SUBLANES):
                z = [_expert_row(u_ref, _token_id(ids_ref, g * SUBLANES + _FOLD_ORDER[k], t)) * xt
                     for k in range(SUBLANES)]
                groups.append(_fold8(z))
            return jnp.concatenate(groups, axis=0)

        a_sc[...] = jnp.zeros_like(a_sc)
        part = lax.fori_loop(0, tt, token, jnp.zeros((N_SEL, LANES), f32))
        finish(part, tt - 1)
        act = jnp.transpose(a_sc[...])[0:tt, :]
        w_ref[...] = gate_ref[...] * _gelu_tanh(act)

    _with_ids_tile(ids_hbm, (ids0, ids1), sems, body)


def _peer_u(ids, a3, gate, u_tab, tt):
    n_tiles = ids.shape[0]
    n = n_tiles * tt
    assert tt == LANES and ids.shape[1] == N_SEL * LANES
    return pl.pallas_call(
        _peer_u_kernel,
        grid=(n_tiles,),
        in_specs=[pl.BlockSpec(memory_space=pl.ANY),
                  pl.BlockSpec((tt, SUBLANES, LANES), lambda i: (i, 0, 0)),
                  pl.BlockSpec((tt, N_SEL), lambda i: (i, 0)),
                  pl.BlockSpec(u_tab.shape, lambda i: (0, 0), pipeline_mode=pl.Buffered(1))],
        out_specs=pl.BlockSpec((tt, N_SEL), lambda i: (i, 0)),
        out_shape=jax.ShapeDtypeStruct((n, N_SEL), f32),
        scratch_shapes=[pltpu.VMEM((N_SEL, LANES), f32)] + _ids_scratch(),
        compiler_params=_params(),
        name="peer_u",
    )(ids, a3, gate, u_tab)


def _peer_v_kernel(ids_hbm, w_ref, v_ref, h_ref, g_ref, o_ref, ids0, ids1, sems, p_sc):
    tt = o_ref.shape[0]
    n_acc = 4

    def lane_bcast_weights(t):
        return jnp.transpose(jnp.broadcast_to(w_ref[pl.ds(t, 1), :], (LANES, N_SEL)))

    def body(ids_ref):
        def token(t, wb):
            wb_next = lane_bcast_weights(jnp.minimum(t + 1, tt - 1))
            accs = [jnp.zeros((SUBLANES, LANES), f32) for _ in range(n_acc)]
            for e in range(N_SEL):
                wv = jnp.broadcast_to(wb[e:e + 1, :], (SUBLANES, LANES))
                accs[e % n_acc] = accs[e % n_acc] + _expert_row(v_ref, _token_id(ids_ref, e, t)) * wv
            p_sc[t] = (accs[0] + accs[1]) + (accs[2] + accs[3])
            return wb_next

        lax.fori_loop(0, tt, token, lane_bcast_weights(0))
        peer = jnp.concatenate([p_sc[:, c, :] for c in range(SUBLANES)], axis=1)
        o_ref[...] = _rms(h_ref[...] + peer, g_ref[...])

    _with_ids_tile(ids_hbm, (ids0, ids1), sems, body)


def _peer_v(ids, w, v_tab, h2, g, tt):
    n_tiles = ids.shape[0]
    n, d = h2.shape
    assert tt == LANES and ids.shape[1] == N_SEL * LANES and n >= n_tiles * tt and d == SUBLANES * LANES
    return pl.pallas_call(
        _peer_v_kernel,
        grid=(n_tiles,),
        in_specs=[pl.BlockSpec(memory_space=pl.ANY),
                  pl.BlockSpec((tt, N_SEL), lambda i: (i, 0)),
                  pl.BlockSpec(v_tab.shape, lambda i: (0, 0), pipeline_mode=pl.Buffered(1)),
                  pl.BlockSpec((tt, d), lambda i: (i, 0)),
                  pl.BlockSpec((1, d), lambda i: (0, 0))],
        out_specs=pl.BlockSpec((tt, d), lambda i: (i, 0)),
        out_shape=jax.ShapeDtypeStruct((n, d), f32),
        scratch_shapes=_ids_scratch() + [pltpu.VMEM((tt, SUBLANES, LANES), f32)],
        compiler_params=_params(),
        name="peer_v",
    )(ids, w, v_tab, h2, g)


SC_WORKERS = 32
SC_LANES = 16
SC_TILE_SHARE_64THS = 20


def _peer_sc(ids_tok, x3, gate, u_words, v_words):
    from jax.experimental.pallas import tpu_sc as plsc
    n_sc = ids_tok.shape[0]
    words = v_words.shape[1]
    d = 2 * words
    tpw = n_sc // SC_WORKERS
    assert n_sc % SC_WORKERS == 0 and words == ROWS_PER_EXPERT * LANES
    half = words // 2
    nvec = half // SC_LANES
    mesh = plsc.VectorSubcoreMesh(core_axis_name="c", subcore_axis_name="s")

    def halves(wd):
        return (lax.bitcast_convert_type(wd << 16, f32), lax.bitcast_convert_type(wd & jnp.int32(-65536), f32))

    def body(u_hbm, v_hbm, ids_hbm, x_hbm, g_hbm, out_hbm, idx_v, g_v, w_v, x_v, rows_v, part_v, out_v, sem):
        wid = lax.axis_index("s") * 2 + lax.axis_index("c")
        lane = lax.broadcasted_iota(i32, (SC_LANES,), 0)

        def token(i, carry):
            t = wid * tpw + i
            pltpu.sync_copy(ids_hbm.at[t], idx_v)
            pltpu.sync_copy(g_hbm.at[t], g_v)
            pltpu.sync_copy(x_hbm.at[t], x_v)
            pltpu.async_copy(u_hbm.at[idx_v], rows_v, sem).wait()
            for p in range(2):
                xs = []
                for j in range(nvec):
                    o = p * half + j * SC_LANES
                    r, l0 = o // LANES, o % LANES
                    xs.append((x_v[2 * r, pl.ds(l0, SC_LANES)], x_v[2 * r + 1, pl.ds(l0, SC_LANES)]))

                def dot_pair(e, c, p=p, xs=xs):
                    acc = None
                    for j in range(nvec):
                        lo, hi = halves(rows_v[e, pl.ds(p * half + j * SC_LANES, SC_LANES)])
                        term = lo * xs[j][0] + hi * xs[j][1]
                        acc = term if acc is None else acc + term
                    part_v[e, :] = acc if p == 0 else part_v[e, :] + acc
                    return c

                lax.fori_loop(0, N_SEL, dot_pair, 0)
            for k in range(N_SEL // SC_LANES):
                act = jnp.zeros((SC_LANES,), f32)
                for q in range(SC_LANES):
                    act = jnp.where(lane == q, jnp.sum(part_v[k * SC_LANES + q, :]), act)
                inner = 0.7978845608028654 * (act + 0.044715 * (act * act * act))
                tanh = 1.0 - 2.0 / (jnp.exp(2.0 * inner) + 1.0)
                w_v[pl.ds(k * SC_LANES, SC_LANES)] = g_v[pl.ds(k * SC_LANES, SC_LANES)] * (act * (0.5 * (1.0 + tanh)))
            pltpu.async_copy(v_hbm.at[idx_v], rows_v, sem).wait()
            for p in range(2):
                def acc_pair(e, accs, p=p):
                    wvec = w_v[pl.ds(pl.multiple_of((e // SC_LANES) * SC_LANES, SC_LANES), SC_LANES)]
                    wr = jnp.sum(jnp.where(lane == e % SC_LANES, wvec, 0.0))
                    out = []
                    for j in range(nvec):
                        lo, hi = halves(rows_v[e, pl.ds(p * half + j * SC_LANES, SC_LANES)])
                        out.append(accs[2 * j] + lo * wr)
                        out.append(accs[2 * j + 1] + hi * wr)
                    return tuple(out)

                accs = lax.fori_loop(0, N_SEL, acc_pair,
                                     tuple(jnp.zeros((SC_LANES,), f32) for _ in range(2 * nvec)))
                for j in range(nvec):
                    o = p * half + j * SC_LANES
                    r, l0 = o // LANES, o % LANES
                    out_v[pl.ds(2 * LANES * r + l0, SC_LANES)] = accs[2 * j]
                    out_v[pl.ds(2 * LANES * r + LANES + l0, SC_LANES)] = accs[2 * j + 1]
            pltpu.sync_copy(out_v, out_hbm.at[t])
            return carry

        lax.fori_loop(0, tpw, token, 0)

    return pl.kernel(
        body,
        out_type=jax.ShapeDtypeStruct((n_sc, d), f32),
        mesh=mesh,
        scratch_types=[pltpu.VMEM((N_SEL,), i32), pltpu.VMEM((N_SEL,), f32), pltpu.VMEM((N_SEL,), f32),
                       pltpu.VMEM((SUBLANES, LANES), f32), pltpu.VMEM((N_SEL, words), i32),
                       pltpu.VMEM((N_SEL, SC_LANES), f32), pltpu.VMEM((d,), f32), pltpu.SemaphoreType.DMA],
        compiler_params=pltpu.CompilerParams(needs_layout_passes=False),
        name="peer_sc",
    )(u_words, v_words, ids_tok, x3, gate)


def _final_tail_kernel(out_in_ref, h_ref, p_ref, g_ref, o_ref):
    del out_in_ref
    o_ref[...] = _rms(h_ref[...] + p_ref[...], g_ref[...])


def _final_tail(out, h2, peer_tail, g, tm, first_tile):
    n, d = h2.shape
    tail_tiles = peer_tail.shape[0] // tm
    return pl.pallas_call(
        _final_tail_kernel,
        grid=(tail_tiles,),
        in_specs=[pl.BlockSpec(memory_space=pl.ANY),
                  pl.BlockSpec((tm, d), lambda i: (i + first_tile, 0)),
                  pl.BlockSpec((tm, d), lambda i: (i, 0)),
                  pl.BlockSpec((1, d), lambda i: (0, 0))],
        out_specs=pl.BlockSpec((tm, d), lambda i: (i + first_tile, 0)),
        out_shape=jax.ShapeDtypeStruct((n, d), f32),
        input_output_aliases={0: 0},
        compiler_params=_params(),
        name="final_tail",
    )(out, h2, peer_tail, g)


def _rope_tables(s):
    half = HEAD_DIM // 2
    inv = ROPE_THETA ** (-jnp.arange(half, dtype=f32) / half)
    ang = jnp.arange(s, dtype=f32)[:, None] * inv[None, :]
    cos, sin = jnp.cos(ang), jnp.sin(ang)
    reps = MOBA_GROUP_LANES // HEAD_DIM
    cos_t = jnp.tile(jnp.concatenate([cos, cos], axis=1), (1, reps))
    sin_t = jnp.tile(jnp.concatenate([-sin, sin], axis=1), (1, reps))
    return cos_t, sin_t


def _row_tile(s, want):
    t = min(want, s)
    while s % t:
        t //= 2
    assert t % SUBLANES == 0
    return t


def kernel(x, mem, g_mix, w_in, w_conv, w_conv_out, w_attn_out, w_merge, g_xattn, g_mem,
           w_xq, w_xkv, w_xo, g_ffn, w_pq, peer_sub_keys, peer_u, peer_v, g_final):
    b, s, d = x.shape
    n = b * s
    n_mem = mem.shape[1]
    assert w_in.shape[0] == 1, "single-layer trunk only"
    assert d == SUBLANES * LANES and s % MOBA_BLOCK == 0
    n_exp = peer_u.shape[1]
    cos_t, sin_t = _rope_tables(s)
    tm = _row_tile(s, 512)
    h = x.reshape(n, d)
    mem2 = mem.reshape(b * n_mem, d)
    l = 0
    proj = _inproj(h, g_mix[l][None, :], w_in[l].astype(bf16), tm)
    o = _moba(proj, cos_t, sin_t, b, s)
    h1 = _mix(proj, o, h, w_conv[l], w_conv_out[l].astype(bf16), w_attn_out[l].astype(bf16),
              w_merge[l].astype(bf16), s, tm)
    kv = _memkv(mem2, g_mem[l][None, :], w_xkv[l].astype(bf16), n_mem)
    h2 = _xattn(h1, g_xattn[l][None, :], kv, w_xq[l].astype(bf16), w_xo[l].astype(bf16), s, n_mem, tm)
    u_tab = _pack_table(peer_u[l])
    v_tab = _pack_table(peer_v[l])
    tt = _row_tile(s, LANES)
    tsel = _row_tile(s, 256)
    n_tiles = n // tt
    per_sel = tsel // tt
    sc_tiles = ((n_tiles * SC_TILE_SHARE_64THS) // 64) // per_sel * per_sel
    tc_tiles = n_tiles - sc_tiles
    sel_args = (h2, g_ffn[l][None, :], w_pq[l].astype(bf16), peer_sub_keys[l].astype(bf16), tsel)
    if sc_tiles:
        a3_s, ids_s, gate_s = _select(*sel_args, tc_tiles // per_sel, sc_tiles // per_sel)
        as_rows = lambda tab: lax.bitcast_convert_type(tab, i32).reshape(n_exp, ROWS_PER_EXPERT * LANES)
        ids_tok = (ids_s // ROWS_PER_EXPERT).transpose(0, 2, 1).reshape(sc_tiles * tt, N_SEL)
        peer_sc = _peer_sc(ids_tok, a3_s, gate_s, as_rows(u_tab), as_rows(v_tab))
    a3, ids, gate = _select(*sel_args, 0, tc_tiles // per_sel)
    ids = ids.reshape(tc_tiles, N_SEL * LANES)
    w = _peer_u(ids, a3, gate, u_tab, tt)
    out = _peer_v(ids, w, v_tab, h2, g_final[None, :], tt)
    if sc_tiles:
        out = _final_tail(out, h2, peer_sc, g_final[None, :], tt, tc_tiles)
    return out.reshape(b, s, d)
```

```python
import functools

import jax
import jax.numpy as jnp
from jax import lax
from jax.experimental import pallas as pl
from jax.experimental.pallas import tpu as pltpu

f32 = jnp.float32
bf16 = jnp.bfloat16
i32 = jnp.int32

EPS = 1e-6
MASK_VALUE = -1e30
ROPE_THETA = 10000.0

XATTN_HEADS = 4
ATTN_HEADS = 8
HEAD_DIM = 64
MOBA_BLOCK = 256
MOBA_TOPK = 3
MOBA_GROUP_LANES = 256
PEER_HEADS = 8
PEER_N_KEYS = 128
PEER_HALF = 128
PEER_TOPK = 16
N_SEL = PEER_HEADS * PEER_TOPK
ROWS_PER_EXPERT = 4

LANES = 128
SUBLANES = 8
VMEM_LIMIT = 56 * 1024 * 1024

_NT = (((1,), (1,)), ((), ()))


def _params(n_axes=1, vmem=VMEM_LIMIT):
    return pltpu.CompilerParams(dimension_semantics=("arbitrary",) * n_axes, vmem_limit_bytes=vmem)


def _rms(x, g):
    return x * lax.rsqrt(jnp.mean(x * x, axis=-1, keepdims=True) + EPS) * g


def _inproj_kernel(x_ref, g_ref, w_ref, o_ref, *, chunk):
    a = _rms(x_ref[...], g_ref[...]).astype(bf16)
    for c in range(o_ref.shape[1] // chunk):
        o_ref[:, c * chunk:(c + 1) * chunk] = jnp.dot(
            a, w_ref[:, c * chunk:(c + 1) * chunk], preferred_element_type=f32)


def _inproj(x2, g, w, tm):
    n, d = x2.shape
    width = w.shape[1]
    return pl.pallas_call(
        functools.partial(_inproj_kernel, chunk=width // 4),
        grid=(n // tm,),
        in_specs=[pl.BlockSpec((tm, d), lambda i: (i, 0)),
                  pl.BlockSpec((1, d), lambda i: (0, 0)),
                  pl.BlockSpec((d, width), lambda i: (0, 0), pipeline_mode=pl.Buffered(1))],
        out_specs=pl.BlockSpec((tm, width), lambda i: (i, 0)),
        out_shape=jax.ShapeDtypeStruct((n, width), f32),
        compiler_params=_params(),
        name="inproj",
    )(x2, g, w)


def _moba_kernel(q_ref, k_ref, v_ref, cos_ref, sin_ref, o_ref,
                 qb_sc, kb_sc, vt_sc, bias_sc, ot_sc, *, nb, topk):
    blk = MOBA_BLOCK
    s, gw = q_ref.shape
    lane = lax.broadcasted_iota(i32, (s, gw), 1)
    first_half = (lane % HEAD_DIM) < (HEAD_DIM // 2)
    cos = cos_ref[...]
    sin = sin_ref[...]

    def rope(t):
        partner = jnp.where(first_half, pltpu.roll(t, gw - HEAD_DIM // 2, 1),
                            pltpu.roll(t, HEAD_DIM // 2, 1))
        return t * cos + partner * sin

    q2 = rope(q_ref[...])
    k2 = rope(k_ref[...])
    v2 = v_ref[...]
    scale = HEAD_DIM ** -0.5
    blk_of_q = lax.broadcasted_iota(i32, (nb, s), 1) // blk
    m_iota = lax.broadcasted_iota(i32, (nb, s), 0)
    past = m_iota < blk_of_q
    kpos = lax.broadcasted_iota(i32, (blk, blk), 0)
    qpos = lax.broadcasted_iota(i32, (blk, blk), 1)
    causal = kpos <= qpos

    heads = range(gw // HEAD_DIM)
    for hh in heads:
        qh = q2[:, hh * HEAD_DIM:(hh + 1) * HEAD_DIM]
        kh = k2[:, hh * HEAD_DIM:(hh + 1) * HEAD_DIM]
        vh = v2[:, hh * HEAD_DIM:(hh + 1) * HEAD_DIM]
        kbar = jnp.mean(kh.reshape(nb, blk, HEAD_DIM), axis=1)
        gt = lax.dot_general(kbar, qh, _NT, precision=lax.Precision.HIGHEST,
                             preferred_element_type=f32)
        rows = []
        for n in range(nb):
            gn = gt[n:n + 1, :]
            beats = past & ((gt > gn) | ((gt == gn) & (m_iota < n)))
            cnt = jnp.sum(beats.astype(i32), axis=0, keepdims=True)
            sel = (blk_of_q[0:1, :] > n) & (cnt < topk)
            rows.append(jnp.where(sel, 0.0, MASK_VALUE).astype(f32))
        bias = jnp.concatenate(rows, axis=0)
        vt = jnp.transpose(vh).astype(bf16)
        for i in range(nb):
            bias_sc[hh, i] = bias[:, i * blk:(i + 1) * blk]
            vt_sc[hh, i] = vt[:, i * blk:(i + 1) * blk]
        qb_sc[hh] = qh.astype(bf16)
        kb_sc[hh] = kh.astype(bf16)

    def q_tile(i, carry):
        q0 = pl.multiple_of(i * blk, blk)
        diag = [lax.dot_general(kb_sc[hh, pl.ds(q0, blk), :], qb_sc[hh, pl.ds(q0, blk), :], _NT,
                                preferred_element_type=f32) for hh in heads]
        probs, stats = [], []
        for hh in heads:
            sd = jnp.where(causal, diag[hh] * scale, MASK_VALUE)
            m0 = jnp.max(sd, axis=0, keepdims=True)
            p0 = jnp.exp(sd - m0)
            probs.append(p0.astype(bf16))
            stats.append((m0, jnp.sum(p0, axis=0, keepdims=True)))
        init = [stats[hh] + (jnp.dot(vt_sc[hh, i], probs[hh], preferred_element_type=f32),)
                for hh in heads]

        def kv_step(j, st):
            k0 = pl.multiple_of(j * blk, blk)
            scores = [lax.dot_general(kb_sc[hh, pl.ds(k0, blk), :], qb_sc[hh, pl.ds(q0, blk), :], _NT,
                                      preferred_element_type=f32) for hh in heads]
            probs, stats = [], []
            for hh in heads:
                m, l, _ = st[hh]
                sj = scores[hh] * scale + bias_sc[hh, i, pl.ds(j, 1), :]
                m_new = jnp.maximum(m, jnp.max(sj, axis=0, keepdims=True))
                alpha = jnp.exp(m - m_new)
                p = jnp.exp(sj - m_new)
                probs.append(p.astype(bf16))
                stats.append((m_new, alpha * l + jnp.sum(p, axis=0, keepdims=True), alpha))
            out = []
            for hh in heads:
                m_new, l, alpha = stats[hh]
                acc = alpha * st[hh][2] + jnp.dot(vt_sc[hh, j], probs[hh], preferred_element_type=f32)
                out.append((m_new, l, acc))
            return tuple(out)

        final = lax.fori_loop(0, i, kv_step, tuple(init))
        for hh in heads:
            _, l, acc = final[hh]
            ot_sc[hh, i] = acc * (1.0 / l)
        return carry

    lax.fori_loop(0, nb, q_tile, 0)
    for hh in heads:
        for i in range(nb):
            o_ref[i * blk:(i + 1) * blk, hh * HEAD_DIM:(hh + 1) * HEAD_DIM] = jnp.transpose(ot_sc[hh, i])


def _moba(proj, cos_t, sin_t, b, s):
    n = b * s
    nb = s // MOBA_BLOCK
    topk = min(MOBA_TOPK, nb - 1)
    aw = ATTN_HEADS * HEAD_DIM
    gw = MOBA_GROUP_LANES
    hp = aw // gw
    nh = gw // HEAD_DIM
    qc, kc, vc = 3 * aw // gw, 4 * aw // gw, 5 * aw // gw
    return pl.pallas_call(
        functools.partial(_moba_kernel, nb=nb, topk=topk),
        grid=(b, hp),
        in_specs=[pl.BlockSpec((s, gw), lambda bi, h: (bi, qc + h)),
                  pl.BlockSpec((s, gw), lambda bi, h: (bi, kc + h)),
                  pl.BlockSpec((s, gw), lambda bi, h: (bi, vc + h)),
                  pl.BlockSpec((s, gw), lambda bi, h: (0, 0)),
                  pl.BlockSpec((s, gw), lambda bi, h: (0, 0))],
        out_specs=pl.BlockSpec((s, gw), lambda bi, h: (bi, h)),
        out_shape=jax.ShapeDtypeStruct((n, aw), f32),
        scratch_shapes=[pltpu.VMEM((nh, s, HEAD_DIM), bf16),
                        pltpu.VMEM((nh, s, HEAD_DIM), bf16),
                        pltpu.VMEM((nh, nb, HEAD_DIM, MOBA_BLOCK), bf16),
                        pltpu.VMEM((nh, nb, nb, MOBA_BLOCK), f32),
                        pltpu.VMEM((nh, nb, HEAD_DIM, MOBA_BLOCK), f32)],
        compiler_params=_params(2),
        name="moba",
    )(proj, proj, proj, cos_t, sin_t)


def _mix_kernel(xin_ref, bg_ref, cg_ref, pxin_ref, pcg_ref, gc_ref, ga_ref, o_ref, x_ref,
                wc_ref, wco_ref, wao_ref, wm_ref, h_ref, *, tiles_per_seq):
    tm = xin_ref.shape[0]
    i = pl.program_id(0)
    keep_prev = jnp.where(i % tiles_per_seq == 0, 0.0, 1.0).astype(f32)
    u = cg_ref[...] * xin_ref[...]
    pu = pcg_ref[...] * pxin_ref[...] * keep_prev
    rows = lax.broadcasted_iota(i32, u.shape, 0)
    u1 = jnp.where(rows == 0, pu[SUBLANES - 1:SUBLANES, :], pltpu.roll(u, 1, 0))
    u2 = jnp.where(rows == 0, pu[SUBLANES - 2:SUBLANES - 1, :],
                   jnp.where(rows == 1, pu[SUBLANES - 1:SUBLANES, :], pltpu.roll(u, 2, 0)))
    y = wc_ref[0:1, :] * u2 + wc_ref[1:2, :] * u1 + wc_ref[2:3, :] * u
    y_conv = jnp.dot((bg_ref[...] * y).astype(bf16), wco_ref[...], preferred_element_type=f32)
    y_attn = jnp.dot(o_ref[...].astype(bf16), wao_ref[...], preferred_element_type=f32)
    merged = jax.nn.sigmoid(gc_ref[...]) * y_conv + jax.nn.sigmoid(ga_ref[...]) * y_attn
    h_ref[...] = x_ref[...] + jnp.dot(merged.astype(bf16), wm_ref[...], preferred_element_type=f32)


def _mix(proj, o, x2, w_conv, wco, wao, wm, s, tm):
    n, d = x2.shape
    cw = wco.shape[0]
    aw = wao.shape[0]
    rb = tm // SUBLANES
    prev = lambda col: (lambda i: (jnp.maximum(i * rb - 1, 0), col))
    gcol = 3 * cw + 3 * aw
    return pl.pallas_call(
        functools.partial(_mix_kernel, tiles_per_seq=s // tm),
        grid=(n // tm,),
        in_specs=[pl.BlockSpec((tm, cw), lambda i: (i, 0)),
                  pl.BlockSpec((tm, cw), lambda i: (i, 1)),
                  pl.BlockSpec((tm, cw), lambda i: (i, 2)),
                  pl.BlockSpec((SUBLANES, cw), prev(0)),
                  pl.BlockSpec((SUBLANES, cw), prev(2)),
                  pl.BlockSpec((tm, d), lambda i: (i, gcol // d)),
                  pl.BlockSpec((tm, d), lambda i: (i, gcol // d + 1)),
                  pl.BlockSpec((tm, aw), lambda i: (i, 0)),
                  pl.BlockSpec((tm, d), lambda i: (i, 0)),
                  pl.BlockSpec(w_conv.shape, lambda i: (0, 0)),
                  pl.BlockSpec(wco.shape, lambda i: (0, 0)),
                  pl.BlockSpec(wao.shape, lambda i: (0, 0)),
                  pl.BlockSpec(wm.shape, lambda i: (0, 0))],
        out_specs=pl.BlockSpec((tm, d), lambda i: (i, 0)),
        out_shape=jax.ShapeDtypeStruct((n, d), f32),
        compiler_params=_params(),
        name="mix",
    )(proj, proj, proj, proj, proj, proj, proj, o, x2, w_conv, wco, wao, wm)


def _memkv_kernel(m_ref, g_ref, w_ref, o_ref):
    a = _rms(m_ref[...], g_ref[...]).astype(bf16)
    o_ref[...] = jnp.dot(a, w_ref[...], preferred_element_type=f32).astype(bf16)


def _memkv(mem2, g, w, n_mem):
    nm, d = mem2.shape
    return pl.pallas_call(
        _memkv_kernel,
        grid=(nm // n_mem,),
        in_specs=[pl.BlockSpec((n_mem, d), lambda i: (i, 0)),
                  pl.BlockSpec((1, d), lambda i: (0, 0)),
                  pl.BlockSpec(w.shape, lambda i: (0, 0))],
        out_specs=pl.BlockSpec((n_mem, w.shape[1]), lambda i: (i, 0)),
        out_shape=jax.ShapeDtypeStruct((nm, w.shape[1]), bf16),
        compiler_params=_params(),
        name="memkv",
    )(mem2, g, w)


def _xattn_kernel(h_ref, g_ref, kv_ref, wq_ref, wo_ref, o_ref):
    h = h_ref[...]
    d = h.shape[1]
    hd = d // XATTN_HEADS
    q = jnp.dot(_rms(h, g_ref[...]).astype(bf16), wq_ref[...], preferred_element_type=f32)
    outs = []
    for hh in range(XATTN_HEADS):
        qh = q[:, hh * hd:(hh + 1) * hd].astype(bf16)
        kh = kv_ref[:, hh * hd:(hh + 1) * hd]
        vh = kv_ref[:, d + hh * hd:d + (hh + 1) * hd]
        sc = lax.dot_general(qh, kh, _NT, preferred_element_type=f32) * (hd ** -0.5)
        sc = sc - jnp.max(sc, axis=-1, keepdims=True)
        p = jnp.exp(sc)
        p = p / jnp.sum(p, axis=-1, keepdims=True)
        outs.append(jnp.dot(p.astype(bf16), vh, preferred_element_type=f32).astype(bf16))
    o = jnp.concatenate(outs, axis=-1)
    o_ref[...] = h + jnp.dot(o, wo_ref[...], preferred_element_type=f32)


def _xattn(h1, g, kv, wq, wo, s, n_mem, tm):
    n, d = h1.shape
    tps = s // tm
    return pl.pallas_call(
        _xattn_kernel,
        grid=(n // tm,),
        in_specs=[pl.BlockSpec((tm, d), lambda i: (i, 0)),
                  pl.BlockSpec((1, d), lambda i: (0, 0)),
                  pl.BlockSpec((n_mem, 2 * d), lambda i: (i // tps, 0)),
                  pl.BlockSpec(wq.shape, lambda i: (0, 0)),
                  pl.BlockSpec(wo.shape, lambda i: (0, 0))],
        out_specs=pl.BlockSpec((tm, d), lambda i: (i, 0)),
        out_shape=jax.ShapeDtypeStruct((n, d), f32),
        compiler_params=_params(),
        name="xattn",
    )(h1, g, kv, wq, wo)


def _topk_cols(sc, k, payload=None):
    r = sc.shape[0]
    ridx = lax.broadcasted_iota(i32, sc.shape, 0).astype(f32)
    vals, picks = [], []
    for _ in range(k):
        m = jnp.max(sc, axis=0, keepdims=True)
        im = jnp.min(jnp.where(sc == m, ridx, float(r)), axis=0, keepdims=True)
        hit = ridx == im
        vals.append(m)
        picks.append(im if payload is None else jnp.max(jnp.where(hit, payload, -1.0), axis=0, keepdims=True))
        sc = jnp.where(hit, -jnp.inf, sc)
    return jnp.concatenate(vals, axis=0), jnp.concatenate(picks, axis=0)


def _pair_candidates(k):
    return [(a, b) for a in range(k) for b in range(k) if (a + 1) * (b + 1) <= k]


def _select_kernel(h_ref, g_ref, wq_ref, keys_ref, a_ref, ids_ref, gate_ref):
    k = PEER_TOPK
    a = _rms(h_ref[...], g_ref[...])
    for c in range(a_ref.shape[1]):
        a_ref[:, c, :] = a[:, c * LANES:(c + 1) * LANES]
    q = jnp.dot(a.astype(bf16), wq_ref[...], preferred_element_type=f32)
    tm = q.shape[0]
    pairs = _pair_candidates(k)
    n_pad = -len(pairs) % SUBLANES
    id_rows, gate_rows = [], []
    for h in range(PEER_HEADS):
        tops = []
        for p in range(2):
            c0 = (h * 2 + p) * PEER_HALF
            qhp = q[:, c0:c0 + PEER_HALF].astype(bf16)
            st = lax.dot_general(keys_ref[h, p], qhp, _NT, preferred_element_type=f32)
            tops.append(_topk_cols(st, k))
        (v0, i0), (v1, i1) = tops
        i0 = i0 * float(PEER_N_KEYS)
        cand, cand_id = [], []
        for a_ in range(k):
            nb = sum(1 for (pa, _) in pairs if pa == a_)
            cand.append(v0[a_:a_ + 1, :] + v1[0:nb, :])
            cand_id.append(i0[a_:a_ + 1, :] + i1[0:nb, :])
        if n_pad:
            cand.append(jnp.full((n_pad, tm), -jnp.inf, f32))
            cand_id.append(jnp.zeros((n_pad, tm), f32))
        top_s, top_id = _topk_cols(jnp.concatenate(cand, axis=0), k, jnp.concatenate(cand_id, axis=0))
        id_rows.append(top_id.astype(i32))
        ex = jnp.exp(top_s - top_s[0:1, :])
        gate_rows.append(ex / jnp.sum(ex, axis=0, keepdims=True))
    ids_t = jnp.concatenate(id_rows, axis=0) * ROWS_PER_EXPERT
    for j in range(ids_ref.shape[0]):
        ids_ref[j] = ids_t[:, j * LANES:(j + 1) * LANES]
    gate_ref[...] = jnp.transpose(jnp.concatenate(gate_rows, axis=0))


def _select(h2, g, wq, keys, tm, first_block, n_blocks):
    d = h2.shape[1]
    n = n_blocks * tm
    return pl.pallas_call(
        _select_kernel,
        grid=(n_blocks,),
        in_specs=[pl.BlockSpec((tm, d), lambda i: (i + first_block, 0)),
                  pl.BlockSpec((1, d), lambda i: (0, 0)),
                  pl.BlockSpec(wq.shape, lambda i: (0, 0)),
                  pl.BlockSpec(keys.shape, lambda i: (0, 0, 0, 0))],
        out_specs=[pl.BlockSpec((tm, d // LANES, LANES), lambda i: (i, 0, 0)),
                   pl.BlockSpec((tm // LANES, N_SEL, LANES), lambda i: (i, 0, 0)),
                   pl.BlockSpec((tm, N_SEL), lambda i: (i, 0))],
        out_shape=[jax.ShapeDtypeStruct((n, d // LANES, LANES), f32),
                   jax.ShapeDtypeStruct((n // LANES, N_SEL, LANES), i32),
                   jax.ShapeDtypeStruct((n, N_SEL), f32)],
        compiler_params=_params(),
        name="peer_select",
    )(h2, g, wq, keys)


def _gelu_tanh(x):
    cdf = 0.5 * (1.0 + jnp.tanh(0.7978845608028654 * (x + 0.044715 * (x * x * x))))
    return x * cdf


def _pack_kernel(t_ref, o_ref):
    o_ref[...] = pltpu.bitcast(t_ref[...].astype(bf16), jnp.uint32)


def _pack_table(t):
    e, d = t.shape
    rows = e * d // LANES
    blk = min(rows, 8192)
    assert rows % blk == 0 and d == 2 * ROWS_PER_EXPERT * LANES
    return pl.pallas_call(
        _pack_kernel,
        grid=(rows // blk,),
        in_specs=[pl.BlockSpec((blk, LANES), lambda i: (i, 0))],
        out_specs=pl.BlockSpec((blk // 2, LANES), lambda i: (i, 0)),
        out_shape=jax.ShapeDtypeStruct((rows // 2, LANES), jnp.uint32),
        compiler_params=_params(),
        name="pack_table",
    )(t.reshape(rows, LANES))


def _expert_row(tab_ref, row0):
    words = tab_ref[pl.ds(pl.multiple_of(row0, ROWS_PER_EXPERT), ROWS_PER_EXPERT), :]
    return pltpu.bitcast(words, bf16).astype(f32)


def _token_id(ids_ref, e, t):
    return ids_ref.at[pl.ds(e * LANES, LANES)][t]


_FOLD_ORDER = (0, 4, 2, 6, 1, 5, 3, 7)


def _fold8(z):
    sub = lax.broadcasted_iota(i32, z[0].shape, 0)
    lo4 = sub < 4
    lo2 = (sub % 4) < 2
    even = (sub % 2) == 0
    c = [jnp.where(lo4, z[2 * k], z[2 * k + 1]) + pltpu.roll(jnp.where(lo4, z[2 * k + 1], z[2 * k]), 4, 0)
         for k in range(4)]
    e = [jnp.where(lo2, c[2 * k] + pltpu.roll(c[2 * k], 6, 0), c[2 * k + 1] + pltpu.roll(c[2 * k + 1], 2, 0))
         for k in range(2)]
    return jnp.where(even, e[0] + pltpu.roll(e[0], 7, 0), e[1] + pltpu.roll(e[1], 1, 0))


def _with_ids_tile(ids_hbm, bufs, sems, body):
    i = pl.program_id(0)
    n_steps = pl.num_programs(0)

    def tile_copy(tile, slot):
        return pltpu.make_async_copy(ids_hbm.at[tile], bufs[slot], sems.at[slot])

    @pl.when(i == 0)
    def _():
        tile_copy(0, 0).start()

    for slot in range(2):
        @pl.when(i % 2 == slot)
        def _():
            @pl.when(i + 1 < n_steps)
            def _():
                tile_copy(i + 1, 1 - slot).start()
            tile_copy(i, slot).wait()
            body(bufs[slot])


def _ids_scratch():
    return [pltpu.SMEM((N_SEL * LANES,), i32), pltpu.SMEM((N_SEL * LANES,), i32), pltpu.SemaphoreType.DMA((2,))]


def _peer_u_kernel(ids_hbm, x_ref, gate_ref, u_ref, w_ref, a_sc, ids0, ids1, sems):
    tt = x_ref.shape[0]
    lane = lax.broadcasted_iota(i32, (N_SEL, LANES), 1)

    def finish(part, t):
        col = jnp.sum(part, axis=1, keepdims=True)
        a_sc[...] = jnp.where(lane == t, col, a_sc[...])

    def body(ids_ref):
        def token(t, part_prev):
            finish(part_prev, t - 1)
            xt = x_ref[t]
            groups = []
            for g in range(N_SEL // SUBLANES):
                z = [_expert_row(u_ref, _token_id(ids_ref, g * SUBLANES + _FOLD_ORDER[k], t)) * xt
                     for k in range(SUBLANES)]
                groups.append(_fold8(z))
            return jnp.concatenate(groups, axis=0)

        a_sc[...] = jnp.zeros_like(a_sc)
        part = lax.fori_loop(0, tt, token, jnp.zeros((N_SEL, LANES), f32))
        finish(part, tt - 1)
        act = jnp.transpose(a_sc[...])[0:tt, :]
        w_ref[...] = gate_ref[...] * _gelu_tanh(act)

    _with_ids_tile(ids_hbm, (ids0, ids1), sems, body)


def _peer_u(ids, a3, gate, u_tab, tt):
    n_tiles = ids.shape[0]
    n = n_tiles * tt
    assert tt == LANES and ids.shape[1] == N_SEL * LANES
    return pl.pallas_call(
        _peer_u_kernel,
        grid=(n_tiles,),
        in_specs=[pl.BlockSpec(memory_space=pl.ANY),
                  pl.BlockSpec((tt, SUBLANES, LANES), lambda i: (i, 0, 0)),
                  pl.BlockSpec((tt, N_SEL), lambda i: (i, 0)),
                  pl.BlockSpec(u_tab.shape, lambda i: (0, 0), pipeline_mode=pl.Buffered(1))],
        out_specs=pl.BlockSpec((tt, N_SEL), lambda i: (i, 0)),
        out_shape=jax.ShapeDtypeStruct((n, N_SEL), f32),
        scratch_shapes=[pltpu.VMEM((N_SEL, LANES), f32)] + _ids_scratch(),
        compiler_params=_params(),
        name="peer_u",
    )(ids, a3, gate, u_tab)


def _peer_v_kernel(ids_hbm, w_ref, v_ref, h_ref, g_ref, o_ref, ids0, ids1, sems, p_sc):
    tt = o_ref.shape[0]
    n_acc = 4

    def lane_bcast_weights(t):
        return jnp.transpose(jnp.broadcast_to(w_ref[pl.ds(t, 1), :], (LANES, N_SEL)))

    def body(ids_ref):
        def token(t, wb):
            wb_next = lane_bcast_weights(jnp.minimum(t + 1, tt - 1))
            accs = [jnp.zeros((SUBLANES, LANES), f32) for _ in range(n_acc)]
            for e in range(N_SEL):
                wv = jnp.broadcast_to(wb[e:e + 1, :], (SUBLANES, LANES))
                accs[e % n_acc] = accs[e % n_acc] + _expert_row(v_ref, _token_id(ids_ref, e, t)) * wv
            p_sc[t] = (accs[0] + accs[1]) + (accs[2] + accs[3])
            return wb_next

        lax.fori_loop(0, tt, token, lane_bcast_weights(0))
        peer = jnp.concatenate([p_sc[:, c, :] for c in range(SUBLANES)], axis=1)
        o_ref[...] = _rms(h_ref[...] + peer, g_ref[...])

    _with_ids_tile(ids_hbm, (ids0, ids1), sems, body)


def _peer_v(ids, w, v_tab, h2, g, tt, n_out):
    n_tiles = ids.shape[0]
    n, d = h2.shape
    assert tt == LANES and ids.shape[1] == N_SEL * LANES and n == n_tiles * tt <= n_out and d == SUBLANES * LANES
    return pl.pallas_call(
        _peer_v_kernel,
        grid=(n_tiles,),
        in_specs=[pl.BlockSpec(memory_space=pl.ANY),
                  pl.BlockSpec((tt, N_SEL), lambda i: (i, 0)),
                  pl.BlockSpec(v_tab.shape, lambda i: (0, 0), pipeline_mode=pl.Buffered(1)),
                  pl.BlockSpec((tt, d), lambda i: (i, 0)),
                  pl.BlockSpec((1, d), lambda i: (0, 0))],
        out_specs=pl.BlockSpec((tt, d), lambda i: (i, 0)),
        out_shape=jax.ShapeDtypeStruct((n_out, d), f32),
        scratch_shapes=_ids_scratch() + [pltpu.VMEM((tt, SUBLANES, LANES), f32)],
        compiler_params=_params(),
        name="peer_v",
    )(ids, w, v_tab, h2, g)


SC_WORKERS = 32
SC_LANES = 16
SC_BATCH_SHARE_16THS = 5


def _peer_sc(ids_tok, x3, gate, u_words, v_words):
    from jax.experimental.pallas import tpu_sc as plsc
    n_sc = ids_tok.shape[0]
    words = v_words.shape[1]
    d = 2 * words
    tpw = n_sc // SC_WORKERS
    assert n_sc % SC_WORKERS == 0 and words == ROWS_PER_EXPERT * LANES
    half = words // 2
    nvec = half // SC_LANES
    mesh = plsc.VectorSubcoreMesh(core_axis_name="c", subcore_axis_name="s")

    def halves(wd):
        return (lax.bitcast_convert_type(wd << 16, f32), lax.bitcast_convert_type(wd & jnp.int32(-65536), f32))

    def body(u_hbm, v_hbm, ids_hbm, x_hbm, g_hbm, out_hbm, idx_v, g_v, w_v, x_v, rows_v, part_v, out_v, sem):
        wid = lax.axis_index("s") * 2 + lax.axis_index("c")
        lane = lax.broadcasted_iota(i32, (SC_LANES,), 0)

        def token(i, carry):
            t = wid * tpw + i
            pltpu.sync_copy(ids_hbm.at[t], idx_v)
            pltpu.sync_copy(g_hbm.at[t], g_v)
            pltpu.sync_copy(x_hbm.at[t], x_v)
            pltpu.async_copy(u_hbm.at[idx_v], rows_v, sem).wait()
            for p in range(2):
                xs = []
                for j in range(nvec):
                    o = p * half + j * SC_LANES
                    r, l0 = o // LANES, o % LANES
                    xs.append((x_v[2 * r, pl.ds(l0, SC_LANES)], x_v[2 * r + 1, pl.ds(l0, SC_LANES)]))

                def dot_pair(e, c, p=p, xs=xs):
                    acc = None
                    for j in range(nvec):
                        lo, hi = halves(rows_v[e, pl.ds(p * half + j * SC_LANES, SC_LANES)])
                        term = lo * xs[j][0] + hi * xs[j][1]
                        acc = term if acc is None else acc + term
                    part_v[e, :] = acc if p == 0 else part_v[e, :] + acc
                    return c

                lax.fori_loop(0, N_SEL, dot_pair, 0)
            for k in range(N_SEL // SC_LANES):
                act = jnp.zeros((SC_LANES,), f32)
                for q in range(SC_LANES):
                    act = jnp.where(lane == q, jnp.sum(part_v[k * SC_LANES + q, :]), act)
                inner = 0.7978845608028654 * (act + 0.044715 * (act * act * act))
                tanh = 1.0 - 2.0 / (jnp.exp(2.0 * inner) + 1.0)
                w_v[pl.ds(k * SC_LANES, SC_LANES)] = g_v[pl.ds(k * SC_LANES, SC_LANES)] * (act * (0.5 * (1.0 + tanh)))
            pltpu.async_copy(v_hbm.at[idx_v], rows_v, sem).wait()
            for p in range(2):
                def acc_pair(e, accs, p=p):
                    wvec = w_v[pl.ds(pl.multiple_of((e // SC_LANES) * SC_LANES, SC_LANES), SC_LANES)]
                    wr = jnp.sum(jnp.where(lane == e % SC_LANES, wvec, 0.0))
                    out = []
                    for j in range(nvec):
                        lo, hi = halves(rows_v[e, pl.ds(p * half + j * SC_LANES, SC_LANES)])
                        out.append(accs[2 * j] + lo * wr)
                        out.append(accs[2 * j + 1] + hi * wr)
                    return tuple(out)

                accs = lax.fori_loop(0, N_SEL, acc_pair,
                                     tuple(jnp.zeros((SC_LANES,), f32) for _ in range(2 * nvec)))
                for j in range(nvec):
                    o = p * half + j * SC_LANES
                    r, l0 = o // LANES, o % LANES
                    out_v[pl.ds(2 * LANES * r + l0, SC_LANES)] = accs[2 * j]
                    out_v[pl.ds(2 * LANES * r + LANES + l0, SC_LANES)] = accs[2 * j + 1]
            pltpu.sync_copy(out_v, out_hbm.at[t])
            return carry

        lax.fori_loop(0, tpw, token, 0)

    return pl.kernel(
        body,
        out_type=jax.ShapeDtypeStruct((n_sc, d), f32),
        mesh=mesh,
        scratch_types=[pltpu.VMEM((N_SEL,), i32), pltpu.VMEM((N_SEL,), f32), pltpu.VMEM((N_SEL,), f32),
                       pltpu.VMEM((SUBLANES, LANES), f32), pltpu.VMEM((N_SEL, words), i32),
                       pltpu.VMEM((N_SEL, SC_LANES), f32), pltpu.VMEM((d,), f32), pltpu.SemaphoreType.DMA],
        compiler_params=pltpu.CompilerParams(needs_layout_passes=False),
        name="peer_sc",
    )(u_words, v_words, ids_tok, x3, gate)


def _final_tail_kernel(out_in_ref, h_ref, p_ref, g_ref, o_ref):
    del out_in_ref
    o_ref[...] = _rms(h_ref[...] + p_ref[...], g_ref[...])


def _final_tail(out, h2_tail, peer_tail, g, tm, first_tile):
    n, d = out.shape
    tail_tiles = peer_tail.shape[0] // tm
    return pl.pallas_call(
        _final_tail_kernel,
        grid=(tail_tiles,),
        in_specs=[pl.BlockSpec(memory_space=pl.ANY),
                  pl.BlockSpec((tm, d), lambda i: (i, 0)),
                  pl.BlockSpec((tm, d), lambda i: (i, 0)),
                  pl.BlockSpec((1, d), lambda i: (0, 0))],
        out_specs=pl.BlockSpec((tm, d), lambda i: (i + first_tile, 0)),
        out_shape=jax.ShapeDtypeStruct((n, d), f32),
        input_output_aliases={0: 0},
        compiler_params=_params(),
        name="final_tail",
    )(out, h2_tail, peer_tail, g)


def _rope_tables(s):
    half = HEAD_DIM // 2
    inv = ROPE_THETA ** (-jnp.arange(half, dtype=f32) / half)
    ang = jnp.arange(s, dtype=f32)[:, None] * inv[None, :]
    cos, sin = jnp.cos(ang), jnp.sin(ang)
    reps = MOBA_GROUP_LANES // HEAD_DIM
    cos_t = jnp.tile(jnp.concatenate([cos, cos], axis=1), (1, reps))
    sin_t = jnp.tile(jnp.concatenate([-sin, sin], axis=1), (1, reps))
    return cos_t, sin_t


def _row_tile(s, want):
    t = min(want, s)
    while s % t:
        t //= 2
    assert t % SUBLANES == 0
    return t


def kernel(x, mem, g_mix, w_in, w_conv, w_conv_out, w_attn_out, w_merge, g_xattn, g_mem,
           w_xq, w_xkv, w_xo, g_ffn, w_pq, peer_sub_keys, peer_u, peer_v, g_final):
    b, s, d = x.shape
    n_mem = mem.shape[1]
    assert w_in.shape[0] == 1, "single-layer trunk only"
    assert d == SUBLANES * LANES and s % MOBA_BLOCK == 0
    n_exp = peer_u.shape[1]
    l = 0
    cos_t, sin_t = _rope_tables(s)
    tm = _row_tile(s, 512)
    tt = _row_tile(s, LANES)
    tsel = _row_tile(s, 256)
    w_in_b, w_co_b, w_ao_b, w_m_b = (t[l].astype(bf16) for t in (w_in, w_conv_out, w_attn_out, w_merge))
    w_xq_b, w_xkv_b, w_xo_b, w_pq_b, keys_b = (t[l].astype(bf16) for t in (w_xq, w_xkv, w_xo, w_pq, peer_sub_keys))

    def trunk(xb, memb):
        nb = xb.shape[0]
        h = xb.reshape(nb * s, d)
        proj = _inproj(h, g_mix[l][None, :], w_in_b, tm)
        o = _moba(proj, cos_t, sin_t, nb, s)
        h1 = _mix(proj, o, h, w_conv[l], w_co_b, w_ao_b, w_m_b, s, tm)
        kv = _memkv(memb.reshape(nb * n_mem, d), g_mem[l][None, :], w_xkv_b, n_mem)
        return _xattn(h1, g_xattn[l][None, :], kv, w_xq_b, w_xo_b, s, n_mem, tm)

    def select(h2):
        a3, ids, gate = _select(h2, g_ffn[l][None, :], w_pq_b, keys_b, tsel, 0, h2.shape[0] // tsel)
        return a3, ids, gate

    u_tab = _pack_table(peer_u[l])
    v_tab = _pack_table(peer_v[l])
    b_sc = (b * SC_BATCH_SHARE_16THS) // 16
    b_tc = b - b_sc
    tc_tiles = b_tc * s // tt
    if b_sc:
        h2_s = trunk(x[b_tc:], mem[b_tc:])
        a3_s, ids_s, gate_s = select(h2_s)
        as_rows = lambda tab: lax.bitcast_convert_type(tab, i32).reshape(n_exp, ROWS_PER_EXPERT * LANES)
        ids_tok = (ids_s // ROWS_PER_EXPERT).transpose(0, 2, 1).reshape(b_sc * s, N_SEL)
        peer_sc = _peer_sc(ids_tok, a3_s, gate_s, as_rows(u_tab), as_rows(v_tab))
    h2 = trunk(x[:b_tc], mem[:b_tc])
    a3, ids, gate = select(h2)
    ids = ids.reshape(tc_tiles, N_SEL * LANES)
    w = _peer_u(ids, a3, gate, u_tab, tt)
    out = _peer_v(ids, w, v_tab, h2, g_final[None, :], tt, b * s)
    if b_sc:
        out = _final_tail(out, h2_s, peer_sc, g_final[None, :], tt, tc_tiles)
    return out.reshape(b, s, d)
```

```python
import functools

import jax
import jax.numpy as jnp
from jax import lax
from jax.experimental import pallas as pl
from jax.experimental.pallas import tpu as pltpu

f32 = jnp.float32
bf16 = jnp.bfloat16
i32 = jnp.int32

EPS = 1e-6
MASK_VALUE = -1e30
ROPE_THETA = 10000.0

XATTN_HEADS = 4
ATTN_HEADS = 8
HEAD_DIM = 64
MOBA_BLOCK = 256
MOBA_TOPK = 3
MOBA_GROUP_LANES = 256
PEER_HEADS = 8
PEER_N_KEYS = 128
PEER_HALF = 128
PEER_TOPK = 16
N_SEL = PEER_HEADS * PEER_TOPK
ROWS_PER_EXPERT = 4

LANES = 128
SUBLANES = 8
VMEM_LIMIT = 56 * 1024 * 1024

_NT = (((1,), (1,)), ((), ()))


def _params(n_axes=1, vmem=VMEM_LIMIT):
    return pltpu.CompilerParams(dimension_semantics=("arbitrary",) * n_axes, vmem_limit_bytes=vmem)


def _rms(x, g):
    return x * lax.rsqrt(jnp.mean(x * x, axis=-1, keepdims=True) + EPS) * g


def _inproj_kernel(x_ref, g_ref, w_ref, o_ref, *, chunk):
    a = _rms(x_ref[...], g_ref[...]).astype(bf16)
    for c in range(o_ref.shape[1] // chunk):
        o_ref[:, c * chunk:(c + 1) * chunk] = jnp.dot(
            a, w_ref[:, c * chunk:(c + 1) * chunk], preferred_element_type=f32)


def _inproj(x2, g, w, tm):
    n, d = x2.shape
    width = w.shape[1]
    return pl.pallas_call(
        functools.partial(_inproj_kernel, chunk=width // 4),
        grid=(n // tm,),
        in_specs=[pl.BlockSpec((tm, d), lambda i: (i, 0)),
                  pl.BlockSpec((1, d), lambda i: (0, 0)),
                  pl.BlockSpec((d, width), lambda i: (0, 0), pipeline_mode=pl.Buffered(1))],
        out_specs=pl.BlockSpec((tm, width), lambda i: (i, 0)),
        out_shape=jax.ShapeDtypeStruct((n, width), f32),
        compiler_params=_params(),
        name="inproj",
    )(x2, g, w)


def _moba_kernel(q_ref, k_ref, v_ref, cos_ref, sin_ref, o_ref,
                 qb_sc, kb_sc, vt_sc, bias_sc, ot_sc, *, nb, topk):
    blk = MOBA_BLOCK
    s, gw = q_ref.shape
    lane = lax.broadcasted_iota(i32, (s, gw), 1)
    first_half = (lane % HEAD_DIM) < (HEAD_DIM // 2)
    cos = cos_ref[...]
    sin = sin_ref[...]

    def rope(t):
        partner = jnp.where(first_half, pltpu.roll(t, gw - HEAD_DIM // 2, 1),
                            pltpu.roll(t, HEAD_DIM // 2, 1))
        return t * cos + partner * sin

    q2 = rope(q_ref[...])
    k2 = rope(k_ref[...])
    v2 = v_ref[...]
    scale = HEAD_DIM ** -0.5
    blk_of_q = lax.broadcasted_iota(i32, (nb, s), 1) // blk
    m_iota = lax.broadcasted_iota(i32, (nb, s), 0)
    past = m_iota < blk_of_q
    kpos = lax.broadcasted_iota(i32, (blk, blk), 0)
    qpos = lax.broadcasted_iota(i32, (blk, blk), 1)
    causal = kpos <= qpos

    heads = range(gw // HEAD_DIM)
    for hh in heads:
        qh = q2[:, hh * HEAD_DIM:(hh + 1) * HEAD_DIM]
        kh = k2[:, hh * HEAD_DIM:(hh + 1) * HEAD_DIM]
        vh = v2[:, hh * HEAD_DIM:(hh + 1) * HEAD_DIM]
        kbar = jnp.mean(kh.reshape(nb, blk, HEAD_DIM), axis=1)
        gt = lax.dot_general(kbar, qh, _NT, precision=lax.Precision.HIGHEST,
                             preferred_element_type=f32)
        rows = []
        for n in range(nb):
            gn = gt[n:n + 1, :]
            beats = past & ((gt > gn) | ((gt == gn) & (m_iota < n)))
            cnt = jnp.sum(beats.astype(i32), axis=0, keepdims=True)
            sel = (blk_of_q[0:1, :] > n) & (cnt < topk)
            rows.append(jnp.where(sel, 0.0, MASK_VALUE).astype(f32))
        bias = jnp.concatenate(rows, axis=0)
        vt = jnp.transpose(vh).astype(bf16)
        for i in range(nb):
            bias_sc[hh, i] = bias[:, i * blk:(i + 1) * blk]
            vt_sc[hh, i] = vt[:, i * blk:(i + 1) * blk]
        qb_sc[hh] = qh.astype(bf16)
        kb_sc[hh] = kh.astype(bf16)

    def q_tile(i, carry):
        q0 = pl.multiple_of(i * blk, blk)
        diag = [lax.dot_general(kb_sc[hh, pl.ds(q0, blk), :], qb_sc[hh, pl.ds(q0, blk), :], _NT,
                                preferred_element_type=f32) for hh in heads]
        probs, stats = [], []
        for hh in heads:
            sd = jnp.where(causal, diag[hh] * scale, MASK_VALUE)
            m0 = jnp.max(sd, axis=0, keepdims=True)
            p0 = jnp.exp(sd - m0)
            probs.append(p0.astype(bf16))
            stats.append((m0, jnp.sum(p0, axis=0, keepdims=True)))
        init = [stats[hh] + (jnp.dot(vt_sc[hh, i], probs[hh], preferred_element_type=f32),)
                for hh in heads]

        def kv_step(j, st):
            k0 = pl.multiple_of(j * blk, blk)
            scores = [lax.dot_general(kb_sc[hh, pl.ds(k0, blk), :], qb_sc[hh, pl.ds(q0, blk), :], _NT,
                                      preferred_element_type=f32) for hh in heads]
            probs, stats = [], []
            for hh in heads:
                m, l, _ = st[hh]
                sj = scores[hh] * scale + bias_sc[hh, i, pl.ds(j, 1), :]
                m_new = jnp.maximum(m, jnp.max(sj, axis=0, keepdims=True))
                alpha = jnp.exp(m - m_new)
                p = jnp.exp(sj - m_new)
                probs.append(p.astype(bf16))
                stats.append((m_new, alpha * l + jnp.sum(p, axis=0, keepdims=True), alpha))
            out = []
            for hh in heads:
                m_new, l, alpha = stats[hh]
                acc = alpha * st[hh][2] + jnp.dot(vt_sc[hh, j], probs[hh], preferred_element_type=f32)
                out.append((m_new, l, acc))
            return tuple(out)

        final = lax.fori_loop(0, i, kv_step, tuple(init))
        for hh in heads:
            _, l, acc = final[hh]
            ot_sc[hh, i] = acc * (1.0 / l)
        return carry

    lax.fori_loop(0, nb, q_tile, 0)
    for hh in heads:
        for i in range(nb):
            o_ref[i * blk:(i + 1) * blk, hh * HEAD_DIM:(hh + 1) * HEAD_DIM] = jnp.transpose(ot_sc[hh, i])


def _moba(proj, cos_t, sin_t, b, s):
    n = b * s
    nb = s // MOBA_BLOCK
    topk = min(MOBA_TOPK, nb - 1)
    aw = ATTN_HEADS * HEAD_DIM
    gw = MOBA_GROUP_LANES
    hp = aw // gw
    nh = gw // HEAD_DIM
    qc, kc, vc = 3 * aw // gw, 4 * aw // gw, 5 * aw // gw
    return pl.pallas_call(
        functools.partial(_moba_kernel, nb=nb, topk=topk),
        grid=(b, hp),
        in_specs=[pl.BlockSpec((s, gw), lambda bi, h: (bi, qc + h)),
                  pl.BlockSpec((s, gw), lambda bi, h: (bi, kc + h)),
                  pl.BlockSpec((s, gw), lambda bi, h: (bi, vc + h)),
                  pl.BlockSpec((s, gw), lambda bi, h: (0, 0)),
                  pl.BlockSpec((s, gw), lambda bi, h: (0, 0))],
        out_specs=pl.BlockSpec((s, gw), lambda bi, h: (bi, h)),
        out_shape=jax.ShapeDtypeStruct((n, aw), f32),
        scratch_shapes=[pltpu.VMEM((nh, s, HEAD_DIM), bf16),
                        pltpu.VMEM((nh, s, HEAD_DIM), bf16),
                        pltpu.VMEM((nh, nb, HEAD_DIM, MOBA_BLOCK), bf16),
                        pltpu.VMEM((nh, nb, nb, MOBA_BLOCK), f32),
                        pltpu.VMEM((nh, nb, HEAD_DIM, MOBA_BLOCK), f32)],
        compiler_params=_params(2),
        name="moba",
    )(proj, proj, proj, cos_t, sin_t)


def _mix_kernel(xin_ref, bg_ref, cg_ref, pxin_ref, pcg_ref, gc_ref, ga_ref, o_ref, x_ref,
                wc_ref, wco_ref, wao_ref, wm_ref, h_ref, *, tiles_per_seq):
    tm = xin_ref.shape[0]
    i = pl.program_id(0)
    keep_prev = jnp.where(i % tiles_per_seq == 0, 0.0, 1.0).astype(f32)
    u = cg_ref[...] * xin_ref[...]
    pu = pcg_ref[...] * pxin_ref[...] * keep_prev
    rows = lax.broadcasted_iota(i32, u.shape, 0)
    u1 = jnp.where(rows == 0, pu[SUBLANES - 1:SUBLANES, :], pltpu.roll(u, 1, 0))
    u2 = jnp.where(rows == 0, pu[SUBLANES - 2:SUBLANES - 1, :],
                   jnp.where(rows == 1, pu[SUBLANES - 1:SUBLANES, :], pltpu.roll(u, 2, 0)))
    y = wc_ref[0:1, :] * u2 + wc_ref[1:2, :] * u1 + wc_ref[2:3, :] * u
    y_conv = jnp.dot((bg_ref[...] * y).astype(bf16), wco_ref[...], preferred_element_type=f32)
    y_attn = jnp.dot(o_ref[...].astype(bf16), wao_ref[...], preferred_element_type=f32)
    merged = jax.nn.sigmoid(gc_ref[...]) * y_conv + jax.nn.sigmoid(ga_ref[...]) * y_attn
    h_ref[...] = x_ref[...] + jnp.dot(merged.astype(bf16), wm_ref[...], preferred_element_type=f32)


def _mix(proj, o, x2, w_conv, wco, wao, wm, s, tm):
    n, d = x2.shape
    cw = wco.shape[0]
    aw = wao.shape[0]
    rb = tm // SUBLANES
    prev = lambda col: (lambda i: (jnp.maximum(i * rb - 1, 0), col))
    gcol = 3 * cw + 3 * aw
    return pl.pallas_call(
        functools.partial(_mix_kernel, tiles_per_seq=s // tm),
        grid=(n // tm,),
        in_specs=[pl.BlockSpec((tm, cw), lambda i: (i, 0)),
                  pl.BlockSpec((tm, cw), lambda i: (i, 1)),
                  pl.BlockSpec((tm, cw), lambda i: (i, 2)),
                  pl.BlockSpec((SUBLANES, cw), prev(0)),
                  pl.BlockSpec((SUBLANES, cw), prev(2)),
                  pl.BlockSpec((tm, d), lambda i: (i, gcol // d)),
                  pl.BlockSpec((tm, d), lambda i: (i, gcol // d + 1)),
                  pl.BlockSpec((tm, aw), lambda i: (i, 0)),
                  pl.BlockSpec((tm, d), lambda i: (i, 0)),
                  pl.BlockSpec(w_conv.shape, lambda i: (0, 0)),
                  pl.BlockSpec(wco.shape, lambda i: (0, 0)),
                  pl.BlockSpec(wao.shape, lambda i: (0, 0)),
                  pl.BlockSpec(wm.shape, lambda i: (0, 0))],
        out_specs=pl.BlockSpec((tm, d), lambda i: (i, 0)),
        out_shape=jax.ShapeDtypeStruct((n, d), f32),
        compiler_params=_params(),
        name="mix",
    )(proj, proj, proj, proj, proj, proj, proj, o, x2, w_conv, wco, wao, wm)


def _memkv_kernel(m_ref, g_ref, w_ref, o_ref):
    a = _rms(m_ref[...], g_ref[...]).astype(bf16)
    o_ref[...] = jnp.dot(a, w_ref[...], preferred_element_type=f32).astype(bf16)


def _memkv(mem2, g, w, n_mem):
    nm, d = mem2.shape
    return pl.pallas_call(
        _memkv_kernel,
        grid=(nm // n_mem,),
        in_specs=[pl.BlockSpec((n_mem, d), lambda i: (i, 0)),
                  pl.BlockSpec((1, d), lambda i: (0, 0)),
                  pl.BlockSpec(w.shape, lambda i: (0, 0))],
        out_specs=pl.BlockSpec((n_mem, w.shape[1]), lambda i: (i, 0)),
        out_shape=jax.ShapeDtypeStruct((nm, w.shape[1]), bf16),
        compiler_params=_params(),
        name="memkv",
    )(mem2, g, w)


def _xattn_kernel(h_ref, g_ref, kv_ref, wq_ref, wo_ref, o_ref):
    h = h_ref[...]
    d = h.shape[1]
    hd = d // XATTN_HEADS
    q = jnp.dot(_rms(h, g_ref[...]).astype(bf16), wq_ref[...], preferred_element_type=f32)
    outs = []
    for hh in range(XATTN_HEADS):
        qh = q[:, hh * hd:(hh + 1) * hd].astype(bf16)
        kh = kv_ref[:, hh * hd:(hh + 1) * hd]
        vh = kv_ref[:, d + hh * hd:d + (hh + 1) * hd]
        sc = lax.dot_general(qh, kh, _NT, preferred_element_type=f32) * (hd ** -0.5)
        sc = sc - jnp.max(sc, axis=-1, keepdims=True)
        p = jnp.exp(sc)
        p = p / jnp.sum(p, axis=-1, keepdims=True)
        outs.append(jnp.dot(p.astype(bf16), vh, preferred_element_type=f32).astype(bf16))
    o = jnp.concatenate(outs, axis=-1)
    o_ref[...] = h + jnp.dot(o, wo_ref[...], preferred_element_type=f32)


def _xattn(h1, g, kv, wq, wo, s, n_mem, tm):
    n, d = h1.shape
    tps = s // tm
    return pl.pallas_call(
        _xattn_kernel,
        grid=(n // tm,),
        in_specs=[pl.BlockSpec((tm, d), lambda i: (i, 0)),
                  pl.BlockSpec((1, d), lambda i: (0, 0)),
                  pl.BlockSpec((n_mem, 2 * d), lambda i: (i // tps, 0)),
                  pl.BlockSpec(wq.shape, lambda i: (0, 0)),
                  pl.BlockSpec(wo.shape, lambda i: (0, 0))],
        out_specs=pl.BlockSpec((tm, d), lambda i: (i, 0)),
        out_shape=jax.ShapeDtypeStruct((n, d), f32),
        compiler_params=_params(),
        name="xattn",
    )(h1, g, kv, wq, wo)


def _topk_cols(sc, k, payload=None):
    r = sc.shape[0]
    ridx = lax.broadcasted_iota(i32, sc.shape, 0).astype(f32)
    vals, picks = [], []
    for _ in range(k):
        m = jnp.max(sc, axis=0, keepdims=True)
        im = jnp.min(jnp.where(sc == m, ridx, float(r)), axis=0, keepdims=True)
        hit = ridx == im
        vals.append(m)
        picks.append(im if payload is None else jnp.max(jnp.where(hit, payload, -1.0), axis=0, keepdims=True))
        sc = jnp.where(hit, -jnp.inf, sc)
    return jnp.concatenate(vals, axis=0), jnp.concatenate(picks, axis=0)


def _pair_candidates(k):
    return [(a, b) for a in range(k) for b in range(k) if (a + 1) * (b + 1) <= k]


def _select_kernel(h_ref, g_ref, wq_ref, keys_ref, a_ref, ids_ref, gate_ref):
    k = PEER_TOPK
    a = _rms(h_ref[...], g_ref[...])
    for c in range(a_ref.shape[1]):
        a_ref[:, c, :] = a[:, c * LANES:(c + 1) * LANES]
    q = jnp.dot(a.astype(bf16), wq_ref[...], preferred_element_type=f32)
    tm = q.shape[0]
    pairs = _pair_candidates(k)
    n_pad = -len(pairs) % SUBLANES
    id_rows, gate_rows = [], []
    for h in range(PEER_HEADS):
        tops = []
        for p in range(2):
            c0 = (h * 2 + p) * PEER_HALF
            qhp = q[:, c0:c0 + PEER_HALF].astype(bf16)
            st = lax.dot_general(keys_ref[h, p], qhp, _NT, preferred_element_type=f32)
            tops.append(_topk_cols(st, k))
        (v0, i0), (v1, i1) = tops
        i0 = i0 * float(PEER_N_KEYS)
        cand, cand_id = [], []
        for a_ in range(k):
            nb = sum(1 for (pa, _) in pairs if pa == a_)
            cand.append(v0[a_:a_ + 1, :] + v1[0:nb, :])
            cand_id.append(i0[a_:a_ + 1, :] + i1[0:nb, :])
        if n_pad:
            cand.append(jnp.full((n_pad, tm), -jnp.inf, f32))
            cand_id.append(jnp.zeros((n_pad, tm), f32))
        top_s, top_id = _topk_cols(jnp.concatenate(cand, axis=0), k, jnp.concatenate(cand_id, axis=0))
        id_rows.append(top_id.astype(i32))
        ex = jnp.exp(top_s - top_s[0:1, :])
        gate_rows.append(ex / jnp.sum(ex, axis=0, keepdims=True))
    ids_t = jnp.concatenate(id_rows, axis=0) * ROWS_PER_EXPERT
    for j in range(ids_ref.shape[0]):
        ids_ref[j] = ids_t[:, j * LANES:(j + 1) * LANES]
    gate_ref[...] = jnp.transpose(jnp.concatenate(gate_rows, axis=0))


def _select(h2, g, wq, keys, tm, first_block, n_blocks):
    d = h2.shape[1]
    n = n_blocks * tm
    return pl.pallas_call(
        _select_kernel,
        grid=(n_blocks,),
        in_specs=[pl.BlockSpec((tm, d), lambda i: (i + first_block, 0)),
                  pl.BlockSpec((1, d), lambda i: (0, 0)),
                  pl.BlockSpec(wq.shape, lambda i: (0, 0)),
                  pl.BlockSpec(keys.shape, lambda i: (0, 0, 0, 0))],
        out_specs=[pl.BlockSpec((tm, d // LANES, LANES), lambda i: (i, 0, 0)),
                   pl.BlockSpec((tm // LANES, N_SEL, LANES), lambda i: (i, 0, 0)),
                   pl.BlockSpec((tm, N_SEL), lambda i: (i, 0))],
        out_shape=[jax.ShapeDtypeStruct((n, d // LANES, LANES), f32),
                   jax.ShapeDtypeStruct((n // LANES, N_SEL, LANES), i32),
                   jax.ShapeDtypeStruct((n, N_SEL), f32)],
        compiler_params=_params(),
        name="peer_select",
    )(h2, g, wq, keys)


def _gelu_tanh(x):
    cdf = 0.5 * (1.0 + jnp.tanh(0.7978845608028654 * (x + 0.044715 * (x * x * x))))
    return x * cdf


def _pack_kernel(t_ref, o_ref):
    o_ref[...] = pltpu.bitcast(t_ref[...].astype(bf16), jnp.uint32)


def _pack_table(t):
    e, d = t.shape
    rows = e * d // LANES
    blk = min(rows, 8192)
    assert rows % blk == 0 and d == 2 * ROWS_PER_EXPERT * LANES
    return pl.pallas_call(
        _pack_kernel,
        grid=(rows // blk,),
        in_specs=[pl.BlockSpec((blk, LANES), lambda i: (i, 0))],
        out_specs=pl.BlockSpec((blk // 2, LANES), lambda i: (i, 0)),
        out_shape=jax.ShapeDtypeStruct((rows // 2, LANES), jnp.uint32),
        compiler_params=_params(),
        name="pack_table",
    )(t.reshape(rows, LANES))


def _expert_row(tab_ref, row0):
    words = tab_ref[pl.ds(pl.multiple_of(row0, ROWS_PER_EXPERT), ROWS_PER_EXPERT), :]
    return pltpu.bitcast(words, bf16).astype(f32)


def _token_id(ids_ref, e, t):
    return ids_ref.at[pl.ds(e * LANES, LANES)][t]


_FOLD_ORDER = (0, 4, 2, 6, 1, 5, 3, 7)


def _fold8(z):
    sub = lax.broadcasted_iota(i32, z[0].shape, 0)
    lo4 = sub < 4
    lo2 = (sub % 4) < 2
    even = (sub % 2) == 0
    c = [jnp.where(lo4, z[2 * k], z[2 * k + 1]) + pltpu.roll(jnp.where(lo4, z[2 * k + 1], z[2 * k]), 4, 0)
         for k in range(4)]
    e = [jnp.where(lo2, c[2 * k] + pltpu.roll(c[2 * k], 6, 0), c[2 * k + 1] + pltpu.roll(c[2 * k + 1], 2, 0))
         for k in range(2)]
    return jnp.where(even, e[0] + pltpu.roll(e[0], 7, 0), e[1] + pltpu.roll(e[1], 1, 0))


def _with_ids_tile(ids_hbm, bufs, sems, body):
    i = pl.program_id(0)
    n_steps = pl.num_programs(0)

    def tile_copy(tile, slot):
        return pltpu.make_async_copy(ids_hbm.at[tile], bufs[slot], sems.at[slot])

    @pl.when(i == 0)
    def _():
        tile_copy(0, 0).start()

    for slot in range(2):
        @pl.when(i % 2 == slot)
        def _():
            @pl.when(i + 1 < n_steps)
            def _():
                tile_copy(i + 1, 1 - slot).start()
            tile_copy(i, slot).wait()
            body(bufs[slot])


def _ids_scratch():
    return [pltpu.SMEM((N_SEL * LANES,), i32), pltpu.SMEM((N_SEL * LANES,), i32), pltpu.SemaphoreType.DMA((2,))]


def _peer_u_kernel(ids_hbm, x_ref, gate_ref, u_ref, w_ref, a_sc, ids0, ids1, sems):
    tt = x_ref.shape[0]
    lane = lax.broadcasted_iota(i32, (N_SEL, LANES), 1)

    def finish(part, t):
        col = jnp.sum(part, axis=1, keepdims=True)
        a_sc[...] = jnp.where(lane == t, col, a_sc[...])

    def body(ids_ref):
        def token(t, part_prev):
            finish(part_prev, t - 1)
            xt = x_ref[t]
            groups = []
            for g in range(N_SEL // SUBLANES):
                z = [_expert_row(u_ref, _token_id(ids_ref, g * SUBLANES + _FOLD_ORDER[k], t)) * xt
                     for k in range(SUBLANES)]
                groups.append(_fold8(z))
            return jnp.concatenate(groups, axis=0)

        a_sc[...] = jnp.zeros_like(a_sc)
        part = lax.fori_loop(0, tt, token, jnp.zeros((N_SEL, LANES), f32))
        finish(part, tt - 1)
        act = jnp.transpose(a_sc[...])[0:tt, :]
        w_ref[...] = gate_ref[...] * _gelu_tanh(act)

    _with_ids_tile(ids_hbm, (ids0, ids1), sems, body)


def _peer_u(ids, a3, gate, u_tab, tt):
    n_tiles = ids.shape[0]
    n = n_tiles * tt
    assert tt == LANES and ids.shape[1] == N_SEL * LANES
    return pl.pallas_call(
        _peer_u_kernel,
        grid=(n_tiles,),
        in_specs=[pl.BlockSpec(memory_space=pl.ANY),
                  pl.BlockSpec((tt, SUBLANES, LANES), lambda i: (i, 0, 0)),
                  pl.BlockSpec((tt, N_SEL), lambda i: (i, 0)),
                  pl.BlockSpec(u_tab.shape, lambda i: (0, 0), pipeline_mode=pl.Buffered(1))],
        out_specs=pl.BlockSpec((tt, N_SEL), lambda i: (i, 0)),
        out_shape=jax.ShapeDtypeStruct((n, N_SEL), f32),
        scratch_shapes=[pltpu.VMEM((N_SEL, LANES), f32)] + _ids_scratch(),
        compiler_params=_params(),
        name="peer_u",
    )(ids, a3, gate, u_tab)


def _peer_v_kernel(ids_hbm, w_ref, v_ref, h_ref, g_ref, o_ref, ids0, ids1, sems, p_sc):
    tt = o_ref.shape[0]
    n_acc = 4

    def lane_bcast_weights(t):
        return jnp.transpose(jnp.broadcast_to(w_ref[pl.ds(t, 1), :], (LANES, N_SEL)))

    def body(ids_ref):
        def token(t, wb):
            wb_next = lane_bcast_weights(jnp.minimum(t + 1, tt - 1))
            accs = [jnp.zeros((SUBLANES, LANES), f32) for _ in range(n_acc)]
            for e in range(N_SEL):
                wv = jnp.broadcast_to(wb[e:e + 1, :], (SUBLANES, LANES))
                accs[e % n_acc] = accs[e % n_acc] + _expert_row(v_ref, _token_id(ids_ref, e, t)) * wv
            p_sc[t] = (accs[0] + accs[1]) + (accs[2] + accs[3])
            return wb_next

        lax.fori_loop(0, tt, token, lane_bcast_weights(0))
        peer = jnp.concatenate([p_sc[:, c, :] for c in range(SUBLANES)], axis=1)
        o_ref[...] = _rms(h_ref[...] + peer, g_ref[...])

    _with_ids_tile(ids_hbm, (ids0, ids1), sems, body)


def _peer_v(ids, w, v_tab, h2, g, tt):
    n_tiles = ids.shape[0]
    n, d = h2.shape
    assert tt == LANES and ids.shape[1] == N_SEL * LANES and n >= n_tiles * tt and d == SUBLANES * LANES
    return pl.pallas_call(
        _peer_v_kernel,
        grid=(n_tiles,),
        in_specs=[pl.BlockSpec(memory_space=pl.ANY),
                  pl.BlockSpec((tt, N_SEL), lambda i: (i, 0)),
                  pl.BlockSpec(v_tab.shape, lambda i: (0, 0), pipeline_mode=pl.Buffered(1)),
                  pl.BlockSpec((tt, d), lambda i: (i, 0)),
                  pl.BlockSpec((1, d), lambda i: (0, 0))],
        out_specs=pl.BlockSpec((tt, d), lambda i: (i, 0)),
        out_shape=jax.ShapeDtypeStruct((n, d), f32),
        scratch_shapes=_ids_scratch() + [pltpu.VMEM((tt, SUBLANES, LANES), f32)],
        compiler_params=_params(),
        name="peer_v",
    )(ids, w, v_tab, h2, g)


SC_WORKERS = 32
SC_LANES = 16
SC_TILE_SHARE_64THS = 20


def _peer_sc(ids_tok, x3, gate, u_words, v_words):
    from jax.experimental.pallas import tpu_sc as plsc
    n_sc = ids_tok.shape[0]
    words = v_words.shape[1]
    d = 2 * words
    tpw = n_sc // SC_WORKERS
    hrows = N_SEL // 2
    assert n_sc % SC_WORKERS == 0 and tpw % 2 == 0 and words == ROWS_PER_EXPERT * LANES
    half = words // 2
    nvec = half // SC_LANES
    mesh = plsc.VectorSubcoreMesh(core_axis_name="c", subcore_axis_name="s")

    def halves(wd):
        return (lax.bitcast_convert_type(wd << 16, f32), lax.bitcast_convert_type(wd & jnp.int32(-65536), f32))

    def body(u_hbm, v_hbm, ids_hbm, x_hbm, g_hbm, out_hbm,
             idx_a0, idx_b0, g_0, x_0, idx_a1, idx_b1, g_1, x_1, w_v, rows_a, rows_b, part_v, out_v, sem_a, sem_b):
        wid = lax.axis_index("s") * 2 + lax.axis_index("c")
        lane = lax.broadcasted_iota(i32, (SC_LANES,), 0)
        first = wid * tpw
        last = first + tpw - 1
        inputs = ((idx_a0, idx_b0, g_0, x_0), (idx_a1, idx_b1, g_1, x_1))
        row_bufs = ((rows_a, sem_a), (rows_b, sem_b))

        def load_inputs(t, par):
            idx_a, idx_b, g_v, x_v = inputs[par]
            pltpu.sync_copy(ids_hbm.at[t, pl.ds(0, hrows)], idx_a)
            pltpu.sync_copy(ids_hbm.at[t, pl.ds(hrows, hrows)], idx_b)
            pltpu.sync_copy(g_hbm.at[t], g_v)
            pltpu.sync_copy(x_hbm.at[t], x_v)

        def gather(tab_hbm, par, h):
            rows, sem = row_bufs[h]
            return pltpu.make_async_copy(tab_hbm.at[inputs[par][h]], rows, sem)

        def dots(par, h):
            rows, x_v = row_bufs[h][0], inputs[par][3]
            for p in range(2):
                xs = []
                for j in range(nvec):
                    o = p * half + j * SC_LANES
                    r, l0 = o // LANES, o % LANES
                    xs.append((x_v[2 * r, pl.ds(l0, SC_LANES)], x_v[2 * r + 1, pl.ds(l0, SC_LANES)]))

                def dot_pair(e, c, p=p, xs=xs):
                    acc = None
                    for j in range(nvec):
                        lo, hi = halves(rows[e, pl.ds(p * half + j * SC_LANES, SC_LANES)])
                        term = lo * xs[j][0] + hi * xs[j][1]
                        acc = term if acc is None else acc + term
                    row = h * hrows + e
                    part_v[row, :] = acc if p == 0 else part_v[row, :] + acc
                    return c

                lax.fori_loop(0, hrows, dot_pair, 0)

        def weights(par):
            g_v = inputs[par][2]
            for k in range(N_SEL // SC_LANES):
                act = jnp.zeros((SC_LANES,), f32)
                for q in range(SC_LANES):
                    act = jnp.where(lane == q, jnp.sum(part_v[k * SC_LANES + q, :]), act)
                inner = 0.7978845608028654 * (act + 0.044715 * (act * act * act))
                tanh = 1.0 - 2.0 / (jnp.exp(2.0 * inner) + 1.0)
                w_v[pl.ds(k * SC_LANES, SC_LANES)] = g_v[pl.ds(k * SC_LANES, SC_LANES)] * (act * (0.5 * (1.0 + tanh)))

        def accumulate(h):
            rows = row_bufs[h][0]
            for p in range(2):
                slots = []
                for j in range(nvec):
                    o = p * half + j * SC_LANES
                    r, l0 = o // LANES, o % LANES
                    slots += [2 * LANES * r + l0, 2 * LANES * r + LANES + l0]

                def acc_pair(e, accs, p=p):
                    row = h * hrows + e
                    wvec = w_v[pl.ds(pl.multiple_of((row // SC_LANES) * SC_LANES, SC_LANES), SC_LANES)]
                    wr = jnp.sum(jnp.where(lane == row % SC_LANES, wvec, 0.0))
                    out = []
                    for j in range(nvec):
                        lo, hi = halves(rows[e, pl.ds(p * half + j * SC_LANES, SC_LANES)])
                        out.append(accs[2 * j] + lo * wr)
                        out.append(accs[2 * j + 1] + hi * wr)
                    return tuple(out)

                init = tuple(jnp.zeros((SC_LANES,), f32) if h == 0 else out_v[pl.ds(o, SC_LANES)] for o in slots)
                accs = lax.fori_loop(0, hrows, acc_pair, init)
                for o, acc in zip(slots, accs):
                    out_v[pl.ds(o, SC_LANES)] = acc

        def token(t, par):
            nxt = jnp.minimum(t + 1, last)
            gather(u_hbm, par, 0).wait()
            dots(par, 0)
            gather(v_hbm, par, 0).start()
            gather(u_hbm, par, 1).wait()
            dots(par, 1)
            gather(v_hbm, par, 1).start()
            weights(par)
            load_inputs(nxt, 1 - par)
            gather(v_hbm, par, 0).wait()
            accumulate(0)
            gather(u_hbm, 1 - par, 0).start()
            gather(v_hbm, par, 1).wait()
            accumulate(1)
            gather(u_hbm, 1 - par, 1).start()
            pltpu.sync_copy(out_v, out_hbm.at[t])

        load_inputs(first, 0)
        gather(u_hbm, 0, 0).start()
        gather(u_hbm, 0, 1).start()

        def token_pair(i, carry):
            token(first + 2 * i, 0)
            token(first + 2 * i + 1, 1)
            return carry

        lax.fori_loop(0, tpw // 2, token_pair, 0)
        gather(u_hbm, 0, 0).wait()
        gather(u_hbm, 0, 1).wait()

    small = [pltpu.VMEM((hrows,), i32), pltpu.VMEM((hrows,), i32), pltpu.VMEM((N_SEL,), f32),
             pltpu.VMEM((SUBLANES, LANES), f32)]
    return pl.kernel(
        body,
        out_type=jax.ShapeDtypeStruct((n_sc, d), f32),
        mesh=mesh,
        scratch_types=small + small + [pltpu.VMEM((N_SEL,), f32), pltpu.VMEM((hrows, words), i32),
                                       pltpu.VMEM((hrows, words), i32), pltpu.VMEM((N_SEL, SC_LANES), f32),
                                       pltpu.VMEM((d,), f32), pltpu.SemaphoreType.DMA, pltpu.SemaphoreType.DMA],
        compiler_params=pltpu.CompilerParams(needs_layout_passes=False),
        name="peer_sc",
    )(u_words, v_words, ids_tok, x3, gate)


def _final_tail_kernel(out_in_ref, h_ref, p_ref, g_ref, o_ref):
    del out_in_ref
    o_ref[...] = _rms(h_ref[...] + p_ref[...], g_ref[...])


def _final_tail(out, h2, peer_tail, g, tm, first_tile):
    n, d = h2.shape
    tail_tiles = peer_tail.shape[0] // tm
    return pl.pallas_call(
        _final_tail_kernel,
        grid=(tail_tiles,),
        in_specs=[pl.BlockSpec(memory_space=pl.ANY),
                  pl.BlockSpec((tm, d), lambda i: (i + first_tile, 0)),
                  pl.BlockSpec((tm, d), lambda i: (i, 0)),
                  pl.BlockSpec((1, d), lambda i: (0, 0))],
        out_specs=pl.BlockSpec((tm, d), lambda i: (i + first_tile, 0)),
        out_shape=jax.ShapeDtypeStruct((n, d), f32),
        input_output_aliases={0: 0},
        compiler_params=_params(),
        name="final_tail",
    )(out, h2, peer_tail, g)


def _rope_tables(s):
    half = HEAD_DIM // 2
    inv = ROPE_THETA ** (-jnp.arange(half, dtype=f32) / half)
    ang = jnp.arange(s, dtype=f32)[:, None] * inv[None, :]
    cos, sin = jnp.cos(ang), jnp.sin(ang)
    reps = MOBA_GROUP_LANES // HEAD_DIM
    cos_t = jnp.tile(jnp.concatenate([cos, cos], axis=1), (1, reps))
    sin_t = jnp.tile(jnp.concatenate([-sin, sin], axis=1), (1, reps))
    return cos_t, sin_t


def _row_tile(s, want):
    t = min(want, s)
    while s % t:
        t //= 2
    assert t % SUBLANES == 0
    return t


def kernel(x, mem, g_mix, w_in, w_conv, w_conv_out, w_attn_out, w_merge, g_xattn, g_mem,
           w_xq, w_xkv, w_xo, g_ffn, w_pq, peer_sub_keys, peer_u, peer_v, g_final):
    b, s, d = x.shape
    n = b * s
    n_mem = mem.shape[1]
    assert w_in.shape[0] == 1, "single-layer trunk only"
    assert d == SUBLANES * LANES and s % MOBA_BLOCK == 0
    n_exp = peer_u.shape[1]
    cos_t, sin_t = _rope_tables(s)
    tm = _row_tile(s, 512)
    h = x.reshape(n, d)
    mem2 = mem.reshape(b * n_mem, d)
    l = 0
    proj = _inproj(h, g_mix[l][None, :], w_in[l].astype(bf16), tm)
    o = _moba(proj, cos_t, sin_t, b, s)
    h1 = _mix(proj, o, h, w_conv[l], w_conv_out[l].astype(bf16), w_attn_out[l].astype(bf16),
              w_merge[l].astype(bf16), s, tm)
    kv = _memkv(mem2, g_mem[l][None, :], w_xkv[l].astype(bf16), n_mem)
    h2 = _xattn(h1, g_xattn[l][None, :], kv, w_xq[l].astype(bf16), w_xo[l].astype(bf16), s, n_mem, tm)
    u_tab = _pack_table(peer_u[l])
    v_tab = _pack_table(peer_v[l])
    tt = _row_tile(s, LANES)
    tsel = _row_tile(s, 256)
    n_tiles = n // tt
    per_sel = tsel // tt
    sc_tiles = ((n_tiles * SC_TILE_SHARE_64THS) // 64) // per_sel * per_sel
    tc_tiles = n_tiles - sc_tiles
    sel_args = (h2, g_ffn[l][None, :], w_pq[l].astype(bf16), peer_sub_keys[l].astype(bf16), tsel)
    if sc_tiles:
        a3_s, ids_s, gate_s = _select(*sel_args, tc_tiles // per_sel, sc_tiles // per_sel)
        as_rows = lambda tab: lax.bitcast_convert_type(tab, i32).reshape(n_exp, ROWS_PER_EXPERT * LANES)
        ids_tok = (ids_s // ROWS_PER_EXPERT).transpose(0, 2, 1).reshape(sc_tiles * tt, N_SEL)
        peer_sc = _peer_sc(ids_tok, a3_s, gate_s, as_rows(u_tab), as_rows(v_tab))
    a3, ids, gate = _select(*sel_args, 0, tc_tiles // per_sel)
    ids = ids.reshape(tc_tiles, N_SEL * LANES)
    w = _peer_u(ids, a3, gate, u_tab, tt)
    out = _peer_v(ids, w, v_tab, h2, g_final[None, :], tt)
    if sc_tiles:
        out = _final_tail(out, h2, peer_sc, g_final[None, :], tt, tc_tiles)
    return out.reshape(b, s, d)
```

```python
import functools

import jax
import jax.numpy as jnp
from jax import lax
from jax.experimental import pallas as pl
from jax.experimental.pallas import tpu as pltpu

f32 = jnp.float32
bf16 = jnp.bfloat16
i32 = jnp.int32

EPS = 1e-6
MASK_VALUE = -1e30
ROPE_THETA = 10000.0

XATTN_HEADS = 4
ATTN_HEADS = 8
HEAD_DIM = 64
MOBA_BLOCK = 256
MOBA_TOPK = 3
MOBA_GROUP_LANES = 256
PEER_HEADS = 8
PEER_N_KEYS = 128
PEER_HALF = 128
PEER_TOPK = 16
N_SEL = PEER_HEADS * PEER_TOPK
ROWS_PER_EXPERT = 4

LANES = 128
SUBLANES = 8
VMEM_LIMIT = 56 * 1024 * 1024

_NT = (((1,), (1,)), ((), ()))


def _params(n_axes=1, vmem=VMEM_LIMIT):
    return pltpu.CompilerParams(dimension_semantics=("arbitrary",) * n_axes, vmem_limit_bytes=vmem)


def _rms(x, g):
    return x * lax.rsqrt(jnp.mean(x * x, axis=-1, keepdims=True) + EPS) * g


def _inproj_kernel(x_ref, g_ref, w_ref, o_ref, *, chunk):
    a = _rms(x_ref[...], g_ref[...]).astype(bf16)
    for c in range(o_ref.shape[1] // chunk):
        o_ref[:, c * chunk:(c + 1) * chunk] = jnp.dot(
            a, w_ref[:, c * chunk:(c + 1) * chunk], preferred_element_type=f32)


def _inproj(x2, g, w, tm):
    n, d = x2.shape
    width = w.shape[1]
    return pl.pallas_call(
        functools.partial(_inproj_kernel, chunk=width // 4),
        grid=(n // tm,),
        in_specs=[pl.BlockSpec((tm, d), lambda i: (i, 0)),
                  pl.BlockSpec((1, d), lambda i: (0, 0)),
                  pl.BlockSpec((d, width), lambda i: (0, 0), pipeline_mode=pl.Buffered(1))],
        out_specs=pl.BlockSpec((tm, width), lambda i: (i, 0)),
        out_shape=jax.ShapeDtypeStruct((n, width), f32),
        compiler_params=_params(),
        name="inproj",
    )(x2, g, w)


def _moba_kernel(q_ref, k_ref, v_ref, cos_ref, sin_ref, o_ref,
                 qb_sc, kb_sc, vt_sc, bias_sc, ot_sc, *, nb, topk):
    blk = MOBA_BLOCK
    s, gw = q_ref.shape
    lane = lax.broadcasted_iota(i32, (s, gw), 1)
    first_half = (lane % HEAD_DIM) < (HEAD_DIM // 2)
    cos = cos_ref[...]
    sin = sin_ref[...]

    def rope(t):
        partner = jnp.where(first_half, pltpu.roll(t, gw - HEAD_DIM // 2, 1),
                            pltpu.roll(t, HEAD_DIM // 2, 1))
        return t * cos + partner * sin

    q2 = rope(q_ref[...])
    k2 = rope(k_ref[...])
    v2 = v_ref[...]
    scale = HEAD_DIM ** -0.5
    blk_of_q = lax.broadcasted_iota(i32, (nb, s), 1) // blk
    m_iota = lax.broadcasted_iota(i32, (nb, s), 0)
    past = m_iota < blk_of_q
    kpos = lax.broadcasted_iota(i32, (blk, blk), 0)
    qpos = lax.broadcasted_iota(i32, (blk, blk), 1)
    causal = kpos <= qpos

    heads = range(gw // HEAD_DIM)
    for hh in heads:
        qh = q2[:, hh * HEAD_DIM:(hh + 1) * HEAD_DIM]
        kh = k2[:, hh * HEAD_DIM:(hh + 1) * HEAD_DIM]
        vh = v2[:, hh * HEAD_DIM:(hh + 1) * HEAD_DIM]
        kbar = jnp.mean(kh.reshape(nb, blk, HEAD_DIM), axis=1)
        gt = lax.dot_general(kbar, qh, _NT, precision=lax.Precision.HIGHEST,
                             preferred_element_type=f32)
        rows = []
        for n in range(nb):
            gn = gt[n:n + 1, :]
            beats = past & ((gt > gn) | ((gt == gn) & (m_iota < n)))
            cnt = jnp.sum(beats.astype(i32), axis=0, keepdims=True)
            sel = (blk_of_q[0:1, :] > n) & (cnt < topk)
            rows.append(jnp.where(sel, 0.0, MASK_VALUE).astype(f32))
        bias = jnp.concatenate(rows, axis=0)
        vt = jnp.transpose(vh).astype(bf16)
        for i in range(nb):
            bias_sc[hh, i] = bias[:, i * blk:(i + 1) * blk]
            vt_sc[hh, i] = vt[:, i * blk:(i + 1) * blk]
        qb_sc[hh] = qh.astype(bf16)
        kb_sc[hh] = kh.astype(bf16)

    def q_tile(i, carry):
        q0 = pl.multiple_of(i * blk, blk)
        diag = [lax.dot_general(kb_sc[hh, pl.ds(q0, blk), :], qb_sc[hh, pl.ds(q0, blk), :], _NT,
                                preferred_element_type=f32) for hh in heads]
        probs, stats = [], []
        for hh in heads:
            sd = jnp.where(causal, diag[hh] * scale, MASK_VALUE)
            m0 = jnp.max(sd, axis=0, keepdims=True)
            p0 = jnp.exp(sd - m0)
            probs.append(p0.astype(bf16))
            stats.append((m0, jnp.sum(p0, axis=0, keepdims=True)))
        init = [stats[hh] + (jnp.dot(vt_sc[hh, i], probs[hh], preferred_element_type=f32),)
                for hh in heads]

        def kv_step(j, st):
            k0 = pl.multiple_of(j * blk, blk)
            scores = [lax.dot_general(kb_sc[hh, pl.ds(k0, blk), :], qb_sc[hh, pl.ds(q0, blk), :], _NT,
                                      preferred_element_type=f32) for hh in heads]
            probs, stats = [], []
            for hh in heads:
                m, l, _ = st[hh]
                sj = scores[hh] * scale + bias_sc[hh, i, pl.ds(j, 1), :]
                m_new = jnp.maximum(m, jnp.max(sj, axis=0, keepdims=True))
                alpha = jnp.exp(m - m_new)
                p = jnp.exp(sj - m_new)
                probs.append(p.astype(bf16))
                stats.append((m_new, alpha * l + jnp.sum(p, axis=0, keepdims=True), alpha))
            out = []
            for hh in heads:
                m_new, l, alpha = stats[hh]
                acc = alpha * st[hh][2] + jnp.dot(vt_sc[hh, j], probs[hh], preferred_element_type=f32)
                out.append((m_new, l, acc))
            return tuple(out)

        final = lax.fori_loop(0, i, kv_step, tuple(init))
        for hh in heads:
            _, l, acc = final[hh]
            ot_sc[hh, i] = acc * (1.0 / l)
        return carry

    lax.fori_loop(0, nb, q_tile, 0)
    for hh in heads:
        for i in range(nb):
            o_ref[i * blk:(i + 1) * blk, hh * HEAD_DIM:(hh + 1) * HEAD_DIM] = jnp.transpose(ot_sc[hh, i])


def _moba(proj, cos_t, sin_t, b, s):
    n = b * s
    nb = s // MOBA_BLOCK
    topk = min(MOBA_TOPK, nb - 1)
    aw = ATTN_HEADS * HEAD_DIM
    gw = MOBA_GROUP_LANES
    hp = aw // gw
    nh = gw // HEAD_DIM
    qc, kc, vc = 3 * aw // gw, 4 * aw // gw, 5 * aw // gw
    return pl.pallas_call(
        functools.partial(_moba_kernel, nb=nb, topk=topk),
        grid=(b, hp),
        in_specs=[pl.BlockSpec((s, gw), lambda bi, h: (bi, qc + h)),
                  pl.BlockSpec((s, gw), lambda bi, h: (bi, kc + h)),
                  pl.BlockSpec((s, gw), lambda bi, h: (bi, vc + h)),
                  pl.BlockSpec((s, gw), lambda bi, h: (0, 0)),
                  pl.BlockSpec((s, gw), lambda bi, h: (0, 0))],
        out_specs=pl.BlockSpec((s, gw), lambda bi, h: (bi, h)),
        out_shape=jax.ShapeDtypeStruct((n, aw), f32),
        scratch_shapes=[pltpu.VMEM((nh, s, HEAD_DIM), bf16),
                        pltpu.VMEM((nh, s, HEAD_DIM), bf16),
                        pltpu.VMEM((nh, nb, HEAD_DIM, MOBA_BLOCK), bf16),
                        pltpu.VMEM((nh, nb, nb, MOBA_BLOCK), f32),
                        pltpu.VMEM((nh, nb, HEAD_DIM, MOBA_BLOCK), f32)],
        compiler_params=_params(2),
        name="moba",
    )(proj, proj, proj, cos_t, sin_t)


def _mix_kernel(xin_ref, bg_ref, cg_ref, pxin_ref, pcg_ref, gc_ref, ga_ref, o_ref, x_ref,
                wc_ref, wco_ref, wao_ref, wm_ref, h_ref, *, tiles_per_seq):
    tm = xin_ref.shape[0]
    i = pl.program_id(0)
    keep_prev = jnp.where(i % tiles_per_seq == 0, 0.0, 1.0).astype(f32)
    u = cg_ref[...] * xin_ref[...]
    pu = pcg_ref[...] * pxin_ref[...] * keep_prev
    rows = lax.broadcasted_iota(i32, u.shape, 0)
    u1 = jnp.where(rows == 0, pu[SUBLANES - 1:SUBLANES, :], pltpu.roll(u, 1, 0))
    u2 = jnp.where(rows == 0, pu[SUBLANES - 2:SUBLANES - 1, :],
                   jnp.where(rows == 1, pu[SUBLANES - 1:SUBLANES, :], pltpu.roll(u, 2, 0)))
    y = wc_ref[0:1, :] * u2 + wc_ref[1:2, :] * u1 + wc_ref[2:3, :] * u
    y_conv = jnp.dot((bg_ref[...] * y).astype(bf16), wco_ref[...], preferred_element_type=f32)
    y_attn = jnp.dot(o_ref[...].astype(bf16), wao_ref[...], preferred_element_type=f32)
    merged = jax.nn.sigmoid(gc_ref[...]) * y_conv + jax.nn.sigmoid(ga_ref[...]) * y_attn
    h_ref[...] = x_ref[...] + jnp.dot(merged.astype(bf16), wm_ref[...], preferred_element_type=f32)


def _mix(proj, o, x2, w_conv, wco, wao, wm, s, tm):
    n, d = x2.shape
    cw = wco.shape[0]
    aw = wao.shape[0]
    rb = tm // SUBLANES
    prev = lambda col: (lambda i: (jnp.maximum(i * rb - 1, 0), col))
    gcol = 3 * cw + 3 * aw
    return pl.pallas_call(
        functools.partial(_mix_kernel, tiles_per_seq=s // tm),
        grid=(n // tm,),
        in_specs=[pl.BlockSpec((tm, cw), lambda i: (i, 0)),
                  pl.BlockSpec((tm, cw), lambda i: (i, 1)),
                  pl.BlockSpec((tm, cw), lambda i: (i, 2)),
                  pl.BlockSpec((SUBLANES, cw), prev(0)),
                  pl.BlockSpec((SUBLANES, cw), prev(2)),
                  pl.BlockSpec((tm, d), lambda i: (i, gcol // d)),
                  pl.BlockSpec((tm, d), lambda i: (i, gcol // d + 1)),
                  pl.BlockSpec((tm, aw), lambda i: (i, 0)),
                  pl.BlockSpec((tm, d), lambda i: (i, 0)),
                  pl.BlockSpec(w_conv.shape, lambda i: (0, 0)),
                  pl.BlockSpec(wco.shape, lambda i: (0, 0)),
                  pl.BlockSpec(wao.shape, lambda i: (0, 0)),
                  pl.BlockSpec(wm.shape, lambda i: (0, 0))],
        out_specs=pl.BlockSpec((tm, d), lambda i: (i, 0)),
        out_shape=jax.ShapeDtypeStruct((n, d), f32),
        compiler_params=_params(),
        name="mix",
    )(proj, proj, proj, proj, proj, proj, proj, o, x2, w_conv, wco, wao, wm)


def _memkv_kernel(m_ref, g_ref, w_ref, o_ref):
    a = _rms(m_ref[...], g_ref[...]).astype(bf16)
    o_ref[...] = jnp.dot(a, w_ref[...], preferred_element_type=f32).astype(bf16)


def _memkv(mem2, g, w, n_mem):
    nm, d = mem2.shape
    return pl.pallas_call(
        _memkv_kernel,
        grid=(nm // n_mem,),
        in_specs=[pl.BlockSpec((n_mem, d), lambda i: (i, 0)),
                  pl.BlockSpec((1, d), lambda i: (0, 0)),
                  pl.BlockSpec(w.shape, lambda i: (0, 0))],
        out_specs=pl.BlockSpec((n_mem, w.shape[1]), lambda i: (i, 0)),
        out_shape=jax.ShapeDtypeStruct((nm, w.shape[1]), bf16),
        compiler_params=_params(),
        name="memkv",
    )(mem2, g, w)


def _xattn_kernel(h_ref, g_ref, kv_ref, wq_ref, wo_ref, o_ref):
    h = h_ref[...]
    d = h.shape[1]
    hd = d // XATTN_HEADS
    q = jnp.dot(_rms(h, g_ref[...]).astype(bf16), wq_ref[...], preferred_element_type=f32)
    outs = []
    for hh in range(XATTN_HEADS):
        qh = q[:, hh * hd:(hh + 1) * hd].astype(bf16)
        kh = kv_ref[:, hh * hd:(hh + 1) * hd]
        vh = kv_ref[:, d + hh * hd:d + (hh + 1) * hd]
        sc = lax.dot_general(qh, kh, _NT, preferred_element_type=f32) * (hd ** -0.5)
        sc = sc - jnp.max(sc, axis=-1, keepdims=True)
        p = jnp.exp(sc)
        p = p / jnp.sum(p, axis=-1, keepdims=True)
        outs.append(jnp.dot(p.astype(bf16), vh, preferred_element_type=f32).astype(bf16))
    o = jnp.concatenate(outs, axis=-1)
    o_ref[...] = h + jnp.dot(o, wo_ref[...], preferred_element_type=f32)


def _xattn(h1, g, kv, wq, wo, s, n_mem, tm):
    n, d = h1.shape
    tps = s // tm
    return pl.pallas_call(
        _xattn_kernel,
        grid=(n // tm,),
        in_specs=[pl.BlockSpec((tm, d), lambda i: (i, 0)),
                  pl.BlockSpec((1, d), lambda i: (0, 0)),
                  pl.BlockSpec((n_mem, 2 * d), lambda i: (i // tps, 0)),
                  pl.BlockSpec(wq.shape, lambda i: (0, 0)),
                  pl.BlockSpec(wo.shape, lambda i: (0, 0))],
        out_specs=pl.BlockSpec((tm, d), lambda i: (i, 0)),
        out_shape=jax.ShapeDtypeStruct((n, d), f32),
        compiler_params=_params(),
        name="xattn",
    )(h1, g, kv, wq, wo)


def _topk_cols(sc, k, payload=None):
    r = sc.shape[0]
    ridx = lax.broadcasted_iota(i32, sc.shape, 0).astype(f32)
    vals, picks = [], []
    for _ in range(k):
        m = jnp.max(sc, axis=0, keepdims=True)
        im = jnp.min(jnp.where(sc == m, ridx, float(r)), axis=0, keepdims=True)
        hit = ridx == im
        vals.append(m)
        picks.append(im if payload is None else jnp.max(jnp.where(hit, payload, -1.0), axis=0, keepdims=True))
        sc = jnp.where(hit, -jnp.inf, sc)
    return jnp.concatenate(vals, axis=0), jnp.concatenate(picks, axis=0)


def _pair_candidates(k):
    return [(a, b) for a in range(k) for b in range(k) if (a + 1) * (b + 1) <= k]


def _select_kernel(h_ref, g_ref, wq_ref, keys_ref, a_ref, ids_ref, gate_ref):
    k = PEER_TOPK
    a = _rms(h_ref[...], g_ref[...])
    for c in range(a_ref.shape[1]):
        a_ref[:, c, :] = a[:, c * LANES:(c + 1) * LANES]
    q = jnp.dot(a.astype(bf16), wq_ref[...], preferred_element_type=f32)
    tm = q.shape[0]
    pairs = _pair_candidates(k)
    n_pad = -len(pairs) % SUBLANES
    id_rows, gate_rows = [], []
    for h in range(PEER_HEADS):
        tops = []
        for p in range(2):
            c0 = (h * 2 + p) * PEER_HALF
            qhp = q[:, c0:c0 + PEER_HALF].astype(bf16)
            st = lax.dot_general(keys_ref[h, p], qhp, _NT, preferred_element_type=f32)
            tops.append(_topk_cols(st, k))
        (v0, i0), (v1, i1) = tops
        i0 = i0 * float(PEER_N_KEYS)
        cand, cand_id = [], []
        for a_ in range(k):
            nb = sum(1 for (pa, _) in pairs if pa == a_)
            cand.append(v0[a_:a_ + 1, :] + v1[0:nb, :])
            cand_id.append(i0[a_:a_ + 1, :] + i1[0:nb, :])
        if n_pad:
            cand.append(jnp.full((n_pad, tm), -jnp.inf, f32))
            cand_id.append(jnp.zeros((n_pad, tm), f32))
        top_s, top_id = _topk_cols(jnp.concatenate(cand, axis=0), k, jnp.concatenate(cand_id, axis=0))
        id_rows.append(top_id.astype(i32))
        ex = jnp.exp(top_s - top_s[0:1, :])
        gate_rows.append(ex / jnp.sum(ex, axis=0, keepdims=True))
    ids_t = jnp.concatenate(id_rows, axis=0) * ROWS_PER_EXPERT
    for j in range(ids_ref.shape[0]):
        ids_ref[j] = ids_t[:, j * LANES:(j + 1) * LANES]
    gate_ref[...] = jnp.transpose(jnp.concatenate(gate_rows, axis=0))


def _select(h2, g, wq, keys, tm, first_block, n_blocks):
    d = h2.shape[1]
    n = n_blocks * tm
    return pl.pallas_call(
        _select_kernel,
        grid=(n_blocks,),
        in_specs=[pl.BlockSpec((tm, d), lambda i: (i + first_block, 0)),
                  pl.BlockSpec((1, d), lambda i: (0, 0)),
                  pl.BlockSpec(wq.shape, lambda i: (0, 0)),
                  pl.BlockSpec(keys.shape, lambda i: (0, 0, 0, 0))],
        out_specs=[pl.BlockSpec((tm, d // LANES, LANES), lambda i: (i, 0, 0)),
                   pl.BlockSpec((tm // LANES, N_SEL, LANES), lambda i: (i, 0, 0)),
                   pl.BlockSpec((tm, N_SEL), lambda i: (i, 0))],
        out_shape=[jax.ShapeDtypeStruct((n, d // LANES, LANES), f32),
                   jax.ShapeDtypeStruct((n // LANES, N_SEL, LANES), i32),
                   jax.ShapeDtypeStruct((n, N_SEL), f32)],
        compiler_params=_params(),
        name="peer_select",
    )(h2, g, wq, keys)


def _gelu_tanh(x):
    cdf = 0.5 * (1.0 + jnp.tanh(0.7978845608028654 * (x + 0.044715 * (x * x * x))))
    return x * cdf


def _pack_kernel(t_ref, o_ref):
    o_ref[...] = pltpu.bitcast(t_ref[...].astype(bf16), jnp.uint32)


def _pack_table(t):
    e, d = t.shape
    rows = e * d // LANES
    blk = min(rows, 8192)
    assert rows % blk == 0 and d == 2 * ROWS_PER_EXPERT * LANES
    return pl.pallas_call(
        _pack_kernel,
        grid=(rows // blk,),
        in_specs=[pl.BlockSpec((blk, LANES), lambda i: (i, 0))],
        out_specs=pl.BlockSpec((blk // 2, LANES), lambda i: (i, 0)),
        out_shape=jax.ShapeDtypeStruct((rows // 2, LANES), jnp.uint32),
        compiler_params=_params(),
        name="pack_table",
    )(t.reshape(rows, LANES))


def _expert_row(tab_ref, row0):
    words = tab_ref[pl.ds(pl.multiple_of(row0, ROWS_PER_EXPERT), ROWS_PER_EXPERT), :]
    return pltpu.bitcast(words, bf16).astype(f32)


def _token_id(ids_ref, e, t):
    return ids_ref.at[pl.ds(e * LANES, LANES)][t]


_FOLD_ORDER = (0, 4, 2, 6, 1, 5, 3, 7)


def _fold8(z):
    sub = lax.broadcasted_iota(i32, z[0].shape, 0)
    lo4 = sub < 4
    lo2 = (sub % 4) < 2
    even = (sub % 2) == 0
    c = [jnp.where(lo4, z[2 * k], z[2 * k + 1]) + pltpu.roll(jnp.where(lo4, z[2 * k + 1], z[2 * k]), 4, 0)
         for k in range(4)]
    e = [jnp.where(lo2, c[2 * k] + pltpu.roll(c[2 * k], 6, 0), c[2 * k + 1] + pltpu.roll(c[2 * k + 1], 2, 0))
         for k in range(2)]
    return jnp.where(even, e[0] + pltpu.roll(e[0], 7, 0), e[1] + pltpu.roll(e[1], 1, 0))


def _with_ids_tile(ids_hbm, bufs, sems, body):
    i = pl.program_id(0)
    n_steps = pl.num_programs(0)

    def tile_copy(tile, slot):
        return pltpu.make_async_copy(ids_hbm.at[tile], bufs[slot], sems.at[slot])

    @pl.when(i == 0)
    def _():
        tile_copy(0, 0).start()

    for slot in range(2):
        @pl.when(i % 2 == slot)
        def _():
            @pl.when(i + 1 < n_steps)
            def _():
                tile_copy(i + 1, 1 - slot).start()
            tile_copy(i, slot).wait()
            body(bufs[slot])


def _ids_scratch():
    return [pltpu.SMEM((N_SEL * LANES,), i32), pltpu.SMEM((N_SEL * LANES,), i32), pltpu.SemaphoreType.DMA((2,))]


def _peer_u_kernel(ids_hbm, x_ref, gate_ref, u_ref, w_ref, a_sc, ids0, ids1, sems):
    tt = x_ref.shape[0]
    lane = lax.broadcasted_iota(i32, (N_SEL, LANES), 1)

    def finish(part, t):
        col = jnp.sum(part, axis=1, keepdims=True)
        a_sc[...] = jnp.where(lane == t, col, a_sc[...])

    def body(ids_ref):
        def token(t, part_prev):
            finish(part_prev, t - 1)
            xt = x_ref[t]
            groups = []
            for g in range(N_SEL // SUBLANES):
                z = [_expert_row(u_ref, _token_id(ids_ref, g * SUBLANES + _FOLD_ORDER[k], t)) * xt
                     for k in range(SUBLANES)]
                groups.append(_fold8(z))
            return jnp.concatenate(groups, axis=0)

        a_sc[...] = jnp.zeros_like(a_sc)
        part = lax.fori_loop(0, tt, token, jnp.zeros((N_SEL, LANES), f32))
        finish(part, tt - 1)
        act = jnp.transpose(a_sc[...])[0:tt, :]
        w_ref[...] = gate_ref[...] * _gelu_tanh(act)

    _with_ids_tile(ids_hbm, (ids0, ids1), sems, body)


def _peer_u(ids, a3, gate, u_tab, tt):
    n_tiles = ids.shape[0]
    n = n_tiles * tt
    assert tt == LANES and ids.shape[1] == N_SEL * LANES
    return pl.pallas_call(
        _peer_u_kernel,
        grid=(n_tiles,),
        in_specs=[pl.BlockSpec(memory_space=pl.ANY),
                  pl.BlockSpec((tt, SUBLANES, LANES), lambda i: (i, 0, 0)),
                  pl.BlockSpec((tt, N_SEL), lambda i: (i, 0)),
                  pl.BlockSpec(u_tab.shape, lambda i: (0, 0), pipeline_mode=pl.Buffered(1))],
        out_specs=pl.BlockSpec((tt, N_SEL), lambda i: (i, 0)),
        out_shape=jax.ShapeDtypeStruct((n, N_SEL), f32),
        scratch_shapes=[pltpu.VMEM((N_SEL, LANES), f32)] + _ids_scratch(),
        compiler_params=_params(),
        name="peer_u",
    )(ids, a3, gate, u_tab)


def _peer_v_kernel(ids_hbm, w_ref, v_ref, h_ref, g_ref, o_ref, ids0, ids1, sems, p_sc):
    tt = o_ref.shape[0]
    n_acc = 4

    def lane_bcast_weights(t):
        return jnp.transpose(jnp.broadcast_to(w_ref[pl.ds(t, 1), :], (LANES, N_SEL)))

    def body(ids_ref):
        def token(t, wb):
            wb_next = lane_bcast_weights(jnp.minimum(t + 1, tt - 1))
            accs = [jnp.zeros((SUBLANES, LANES), f32) for _ in range(n_acc)]
            for e in range(N_SEL):
                wv = jnp.broadcast_to(wb[e:e + 1, :], (SUBLANES, LANES))
                accs[e % n_acc] = accs[e % n_acc] + _expert_row(v_ref, _token_id(ids_ref, e, t)) * wv
            p_sc[t] = (accs[0] + accs[1]) + (accs[2] + accs[3])
            return wb_next

        lax.fori_loop(0, tt, token, lane_bcast_weights(0))
        peer = jnp.concatenate([p_sc[:, c, :] for c in range(SUBLANES)], axis=1)
        o_ref[...] = _rms(h_ref[...] + peer, g_ref[...])

    _with_ids_tile(ids_hbm, (ids0, ids1), sems, body)


def _peer_v(ids, w, v_tab, h2, g, tt):
    n_tiles = ids.shape[0]
    n, d = h2.shape
    assert tt == LANES and ids.shape[1] == N_SEL * LANES and n >= n_tiles * tt and d == SUBLANES * LANES
    return pl.pallas_call(
        _peer_v_kernel,
        grid=(n_tiles,),
        in_specs=[pl.BlockSpec(memory_space=pl.ANY),
                  pl.BlockSpec((tt, N_SEL), lambda i: (i, 0)),
                  pl.BlockSpec(v_tab.shape, lambda i: (0, 0), pipeline_mode=pl.Buffered(1)),
                  pl.BlockSpec((tt, d), lambda i: (i, 0)),
                  pl.BlockSpec((1, d), lambda i: (0, 0))],
        out_specs=pl.BlockSpec((tt, d), lambda i: (i, 0)),
        out_shape=jax.ShapeDtypeStruct((n, d), f32),
        scratch_shapes=_ids_scratch() + [pltpu.VMEM((tt, SUBLANES, LANES), f32)],
        compiler_params=_params(),
        name="peer_v",
    )(ids, w, v_tab, h2, g)


SC_WORKERS = 32
SC_LANES = 16
SC_TILE_SHARE_64THS = 24


def _peer_sc(ids_tok, x3, gate, u_words, v_words):
    from jax.experimental.pallas import tpu_sc as plsc
    n_sc = ids_tok.shape[0]
    words = v_words.shape[1]
    d = 2 * words
    tpw = n_sc // SC_WORKERS
    hrows = N_SEL // 2
    assert n_sc % SC_WORKERS == 0 and tpw % 2 == 0 and words == ROWS_PER_EXPERT * LANES
    half = words // 2
    nvec = half // SC_LANES
    mesh = plsc.VectorSubcoreMesh(core_axis_name="c", subcore_axis_name="s")

    def halves(wd):
        return (lax.bitcast_convert_type(wd << 16, f32), lax.bitcast_convert_type(wd & jnp.int32(-65536), f32))

    def body(u_hbm, v_hbm, ids_hbm, x_hbm, g_hbm, out_hbm,
             idx_a0, idx_b0, g_0, x_0, idx_a1, idx_b1, g_1, x_1, w_v, rows_a, rows_b, part_v, out_v, sem_a, sem_b):
        wid = lax.axis_index("s") * 2 + lax.axis_index("c")
        lane = lax.broadcasted_iota(i32, (SC_LANES,), 0)
        first = wid * tpw
        last = first + tpw - 1
        inputs = ((idx_a0, idx_b0, g_0, x_0), (idx_a1, idx_b1, g_1, x_1))
        row_bufs = ((rows_a, sem_a), (rows_b, sem_b))

        def load_inputs(t, par):
            idx_a, idx_b, g_v, x_v = inputs[par]
            pltpu.sync_copy(ids_hbm.at[t, pl.ds(0, hrows)], idx_a)
            pltpu.sync_copy(ids_hbm.at[t, pl.ds(hrows, hrows)], idx_b)
            pltpu.sync_copy(g_hbm.at[t], g_v)
            pltpu.sync_copy(x_hbm.at[t], x_v)

        def gather(tab_hbm, par, h):
            rows, sem = row_bufs[h]
            return pltpu.make_async_copy(tab_hbm.at[inputs[par][h]], rows, sem)

        def dots(par, h):
            rows, x_v = row_bufs[h][0], inputs[par][3]
            for p in range(2):
                xs = []
                for j in range(nvec):
                    o = p * half + j * SC_LANES
                    r, l0 = o // LANES, o % LANES
                    xs.append((x_v[2 * r, pl.ds(l0, SC_LANES)], x_v[2 * r + 1, pl.ds(l0, SC_LANES)]))

                def dot_pair(e, c, p=p, xs=xs):
                    acc = None
                    for j in range(nvec):
                        lo, hi = halves(rows[e, pl.ds(p * half + j * SC_LANES, SC_LANES)])
                        term = lo * xs[j][0] + hi * xs[j][1]
                        acc = term if acc is None else acc + term
                    row = h * hrows + e
                    part_v[row, :] = acc if p == 0 else part_v[row, :] + acc
                    return c

                lax.fori_loop(0, hrows, dot_pair, 0)

        def weights(par):
            g_v = inputs[par][2]
            for k in range(N_SEL // SC_LANES):
                act = jnp.zeros((SC_LANES,), f32)
                for q in range(SC_LANES):
                    act = jnp.where(lane == q, jnp.sum(part_v[k * SC_LANES + q, :]), act)
                inner = 0.7978845608028654 * (act + 0.044715 * (act * act * act))
                tanh = 1.0 - 2.0 / (jnp.exp(2.0 * inner) + 1.0)
                w_v[pl.ds(k * SC_LANES, SC_LANES)] = g_v[pl.ds(k * SC_LANES, SC_LANES)] * (act * (0.5 * (1.0 + tanh)))

        def accumulate(h):
            rows = row_bufs[h][0]
            for p in range(2):
                slots = []
                for j in range(nvec):
                    o = p * half + j * SC_LANES
                    r, l0 = o // LANES, o % LANES
                    slots += [2 * LANES * r + l0, 2 * LANES * r + LANES + l0]

                def acc_pair(e, accs, p=p):
                    row = h * hrows + e
                    wvec = w_v[pl.ds(pl.multiple_of((row // SC_LANES) * SC_LANES, SC_LANES), SC_LANES)]
                    wr = jnp.sum(jnp.where(lane == row % SC_LANES, wvec, 0.0))
                    out = []
                    for j in range(nvec):
                        lo, hi = halves(rows[e, pl.ds(p * half + j * SC_LANES, SC_LANES)])
                        out.append(accs[2 * j] + lo * wr)
                        out.append(accs[2 * j + 1] + hi * wr)
                    return tuple(out)

                init = tuple(jnp.zeros((SC_LANES,), f32) if h == 0 else out_v[pl.ds(o, SC_LANES)] for o in slots)
                accs = lax.fori_loop(0, hrows, acc_pair, init)
                for o, acc in zip(slots, accs):
                    out_v[pl.ds(o, SC_LANES)] = acc

        def token(t, par):
            nxt = jnp.minimum(t + 1, last)
            gather(u_hbm, par, 0).wait()
            dots(par, 0)
            gather(v_hbm, par, 0).start()
            gather(u_hbm, par, 1).wait()
            dots(par, 1)
            gather(v_hbm, par, 1).start()
            weights(par)
            load_inputs(nxt, 1 - par)
            gather(v_hbm, par, 0).wait()
            accumulate(0)
            gather(u_hbm, 1 - par, 0).start()
            gather(v_hbm, par, 1).wait()
            accumulate(1)
            gather(u_hbm, 1 - par, 1).start()
            pltpu.sync_copy(out_v, out_hbm.at[t])

        load_inputs(first, 0)
        gather(u_hbm, 0, 0).start()
        gather(u_hbm, 0, 1).start()

        def token_pair(i, carry):
            token(first + 2 * i, 0)
            token(first + 2 * i + 1, 1)
            return carry

        lax.fori_loop(0, tpw // 2, token_pair, 0)
        gather(u_hbm, 0, 0).wait()
        gather(u_hbm, 0, 1).wait()

    small = [pltpu.VMEM((hrows,), i32), pltpu.VMEM((hrows,), i32), pltpu.VMEM((N_SEL,), f32),
             pltpu.VMEM((SUBLANES, LANES), f32)]
    return pl.kernel(
        body,
        out_type=jax.ShapeDtypeStruct((n_sc, d), f32),
        mesh=mesh,
        scratch_types=small + small + [pltpu.VMEM((N_SEL,), f32), pltpu.VMEM((hrows, words), i32),
                                       pltpu.VMEM((hrows, words), i32), pltpu.VMEM((N_SEL, SC_LANES), f32),
                                       pltpu.VMEM((d,), f32), pltpu.SemaphoreType.DMA, pltpu.SemaphoreType.DMA],
        compiler_params=pltpu.CompilerParams(needs_layout_passes=False),
        name="peer_sc",
    )(u_words, v_words, ids_tok, x3, gate)


def _final_tail_kernel(out_in_ref, h_ref, p_ref, g_ref, o_ref):
    del out_in_ref
    o_ref[...] = _rms(h_ref[...] + p_ref[...], g_ref[...])


def _final_tail(out, h2, peer_tail, g, tm, first_tile):
    n, d = h2.shape
    tail_tiles = peer_tail.shape[0] // tm
    return pl.pallas_call(
        _final_tail_kernel,
        grid=(tail_tiles,),
        in_specs=[pl.BlockSpec(memory_space=pl.ANY),
                  pl.BlockSpec((tm, d), lambda i: (i + first_tile, 0)),
                  pl.BlockSpec((tm, d), lambda i: (i, 0)),
                  pl.BlockSpec((1, d), lambda i: (0, 0))],
        out_specs=pl.BlockSpec((tm, d), lambda i: (i + first_tile, 0)),
        out_shape=jax.ShapeDtypeStruct((n, d), f32),
        input_output_aliases={0: 0},
        compiler_params=_params(),
        name="final_tail",
    )(out, h2, peer_tail, g)


def _rope_tables(s):
    half = HEAD_DIM // 2
    inv = ROPE_THETA ** (-jnp.arange(half, dtype=f32) / half)
    ang = jnp.arange(s, dtype=f32)[:, None] * inv[None, :]
    cos, sin = jnp.cos(ang), jnp.sin(ang)
    reps = MOBA_GROUP_LANES // HEAD_DIM
    cos_t = jnp.tile(jnp.concatenate([cos, cos], axis=1), (1, reps))
    sin_t = jnp.tile(jnp.concatenate([-sin, sin], axis=1), (1, reps))
    return cos_t, sin_t


def _row_tile(s, want):
    t = min(want, s)
    while s % t:
        t //= 2
    assert t % SUBLANES == 0
    return t


def kernel(x, mem, g_mix, w_in, w_conv, w_conv_out, w_attn_out, w_merge, g_xattn, g_mem,
           w_xq, w_xkv, w_xo, g_ffn, w_pq, peer_sub_keys, peer_u, peer_v, g_final):
    b, s, d = x.shape
    n = b * s
    n_mem = mem.shape[1]
    assert w_in.shape[0] == 1, "single-layer trunk only"
    assert d == SUBLANES * LANES and s % MOBA_BLOCK == 0
    n_exp = peer_u.shape[1]
    cos_t, sin_t = _rope_tables(s)
    tm = _row_tile(s, 512)
    h = x.reshape(n, d)
    mem2 = mem.reshape(b * n_mem, d)
    l = 0
    proj = _inproj(h, g_mix[l][None, :], w_in[l].astype(bf16), tm)
    o = _moba(proj, cos_t, sin_t, b, s)
    h1 = _mix(proj, o, h, w_conv[l], w_conv_out[l].astype(bf16), w_attn_out[l].astype(bf16),
              w_merge[l].astype(bf16), s, tm)
    kv = _memkv(mem2, g_mem[l][None, :], w_xkv[l].astype(bf16), n_mem)
    h2 = _xattn(h1, g_xattn[l][None, :], kv, w_xq[l].astype(bf16), w_xo[l].astype(bf16), s, n_mem, tm)
    u_tab = _pack_table(peer_u[l])
    v_tab = _pack_table(peer_v[l])
    tt = _row_tile(s, LANES)
    tsel = _row_tile(s, 256)
    n_tiles = n // tt
    per_sel = tsel // tt
    sc_tiles = ((n_tiles * SC_TILE_SHARE_64THS) // 64) // per_sel * per_sel
    tc_tiles = n_tiles - sc_tiles
    sel_args = (h2, g_ffn[l][None, :], w_pq[l].astype(bf16), peer_sub_keys[l].astype(bf16), tsel)
    if sc_tiles:
        a3_s, ids_s, gate_s = _select(*sel_args, tc_tiles // per_sel, sc_tiles // per_sel)
        as_rows = lambda tab: lax.bitcast_convert_type(tab, i32).reshape(n_exp, ROWS_PER_EXPERT * LANES)
        ids_tok = (ids_s // ROWS_PER_EXPERT).transpose(0, 2, 1).reshape(sc_tiles * tt, N_SEL)
        peer_sc = _peer_sc(ids_tok, a3_s, gate_s, as_rows(u_tab), as_rows(v_tab))
    a3, ids, gate = _select(*sel_args, 0, tc_tiles // per_sel)
    ids = ids.reshape(tc_tiles, N_SEL * LANES)
    w = _peer_u(ids, a3, gate, u_tab, tt)
    out = _peer_v(ids, w, v_tab, h2, g_final[None, :], tt)
    if sc_tiles:
        out = _final_tail(out, h2, peer_sc, g_final[None, :], tt, tc_tiles)
    return out.reshape(b, s, d)
```

```python
import functools

import jax
import jax.numpy as jnp
from jax import lax
from jax.experimental import pallas as pl
from jax.experimental.pallas import tpu as pltpu

f32 = jnp.float32
bf16 = jnp.bfloat16
i32 = jnp.int32

EPS = 1e-6
MASK_VALUE = -1e30
ROPE_THETA = 10000.0

XATTN_HEADS = 4
ATTN_HEADS = 8
HEAD_DIM = 64
MOBA_BLOCK = 256
MOBA_TOPK = 3
MOBA_GROUP_LANES = 256
PEER_HEADS = 8
PEER_N_KEYS = 128
PEER_HALF = 128
PEER_TOPK = 16
N_SEL = PEER_HEADS * PEER_TOPK
ROWS_PER_EXPERT = 4

LANES = 128
SUBLANES = 8
VMEM_LIMIT = 56 * 1024 * 1024

_NT = (((1,), (1,)), ((), ()))


def _params(n_axes=1, vmem=VMEM_LIMIT):
    return pltpu.CompilerParams(dimension_semantics=("arbitrary",) * n_axes, vmem_limit_bytes=vmem)


def _rms(x, g):
    return x * lax.rsqrt(jnp.mean(x * x, axis=-1, keepdims=True) + EPS) * g


def _inproj_kernel(x_ref, g_ref, w_ref, o_ref, *, chunk):
    a = _rms(x_ref[...], g_ref[...]).astype(bf16)
    for c in range(o_ref.shape[1] // chunk):
        o_ref[:, c * chunk:(c + 1) * chunk] = jnp.dot(
            a, w_ref[:, c * chunk:(c + 1) * chunk], preferred_element_type=f32)


def _inproj(x2, g, w, tm):
    n, d = x2.shape
    width = w.shape[1]
    return pl.pallas_call(
        functools.partial(_inproj_kernel, chunk=width // 4),
        grid=(n // tm,),
        in_specs=[pl.BlockSpec((tm, d), lambda i: (i, 0)),
                  pl.BlockSpec((1, d), lambda i: (0, 0)),
                  pl.BlockSpec((d, width), lambda i: (0, 0), pipeline_mode=pl.Buffered(1))],
        out_specs=pl.BlockSpec((tm, width), lambda i: (i, 0)),
        out_shape=jax.ShapeDtypeStruct((n, width), f32),
        compiler_params=_params(),
        name="inproj",
    )(x2, g, w)


def _moba_kernel(q_ref, k_ref, v_ref, cos_ref, sin_ref, o_ref,
                 qb_sc, kb_sc, vt_sc, bias_sc, ot_sc, *, nb, topk):
    blk = MOBA_BLOCK
    s, gw = q_ref.shape
    lane = lax.broadcasted_iota(i32, (s, gw), 1)
    first_half = (lane % HEAD_DIM) < (HEAD_DIM // 2)
    cos = cos_ref[...]
    sin = sin_ref[...]

    def rope(t):
        partner = jnp.where(first_half, pltpu.roll(t, gw - HEAD_DIM // 2, 1),
                            pltpu.roll(t, HEAD_DIM // 2, 1))
        return t * cos + partner * sin

    q2 = rope(q_ref[...])
    k2 = rope(k_ref[...])
    v2 = v_ref[...]
    scale = HEAD_DIM ** -0.5
    blk_of_q = lax.broadcasted_iota(i32, (nb, s), 1) // blk
    m_iota = lax.broadcasted_iota(i32, (nb, s), 0)
    past = m_iota < blk_of_q
    kpos = lax.broadcasted_iota(i32, (blk, blk), 0)
    qpos = lax.broadcasted_iota(i32, (blk, blk), 1)
    causal = kpos <= qpos

    heads = range(gw // HEAD_DIM)
    for hh in heads:
        qh = q2[:, hh * HEAD_DIM:(hh + 1) * HEAD_DIM]
        kh = k2[:, hh * HEAD_DIM:(hh + 1) * HEAD_DIM]
        vh = v2[:, hh * HEAD_DIM:(hh + 1) * HEAD_DIM]
        kbar = jnp.mean(kh.reshape(nb, blk, HEAD_DIM), axis=1)
        gt = lax.dot_general(kbar, qh, _NT, precision=lax.Precision.HIGHEST,
                             preferred_element_type=f32)
        rows = []
        for n in range(nb):
            gn = gt[n:n + 1, :]
            beats = past & ((gt > gn) | ((gt == gn) & (m_iota < n)))
            cnt = jnp.sum(beats.astype(i32), axis=0, keepdims=True)
            sel = (blk_of_q[0:1, :] > n) & (cnt < topk)
            rows.append(jnp.where(sel, 0.0, MASK_VALUE).astype(f32))
        bias = jnp.concatenate(rows, axis=0)
        vt = jnp.transpose(vh).astype(bf16)
        for i in range(nb):
            bias_sc[hh, i] = bias[:, i * blk:(i + 1) * blk]
            vt_sc[hh, i] = vt[:, i * blk:(i + 1) * blk]
        qb_sc[hh] = qh.astype(bf16)
        kb_sc[hh] = kh.astype(bf16)

    def q_tile(i, carry):
        q0 = pl.multiple_of(i * blk, blk)
        diag = [lax.dot_general(kb_sc[hh, pl.ds(q0, blk), :], qb_sc[hh, pl.ds(q0, blk), :], _NT,
                                preferred_element_type=f32) for hh in heads]
        probs, stats = [], []
        for hh in heads:
            sd = jnp.where(causal, diag[hh] * scale, MASK_VALUE)
            m0 = jnp.max(sd, axis=0, keepdims=True)
            p0 = jnp.exp(sd - m0)
            probs.append(p0.astype(bf16))
            stats.append((m0, jnp.sum(p0, axis=0, keepdims=True)))
        init = [stats[hh] + (jnp.dot(vt_sc[hh, i], probs[hh], preferred_element_type=f32),)
                for hh in heads]

        def kv_step(j, st):
            k0 = pl.multiple_of(j * blk, blk)
            scores = [lax.dot_general(kb_sc[hh, pl.ds(k0, blk), :], qb_sc[hh, pl.ds(q0, blk), :], _NT,
                                      preferred_element_type=f32) for hh in heads]
            probs, stats = [], []
            for hh in heads:
                m, l, _ = st[hh]
                sj = scores[hh] * scale + bias_sc[hh, i, pl.ds(j, 1), :]
                m_new = jnp.maximum(m, jnp.max(sj, axis=0, keepdims=True))
                alpha = jnp.exp(m - m_new)
                p = jnp.exp(sj - m_new)
                probs.append(p.astype(bf16))
                stats.append((m_new, alpha * l + jnp.sum(p, axis=0, keepdims=True), alpha))
            out = []
            for hh in heads:
                m_new, l, alpha = stats[hh]
                acc = alpha * st[hh][2] + jnp.dot(vt_sc[hh, j], probs[hh], preferred_element_type=f32)
                out.append((m_new, l, acc))
            return tuple(out)

        final = lax.fori_loop(0, i, kv_step, tuple(init))
        for hh in heads:
            _, l, acc = final[hh]
            ot_sc[hh, i] = acc * (1.0 / l)
        return carry

    lax.fori_loop(0, nb, q_tile, 0)
    for hh in heads:
        for i in range(nb):
            o_ref[i * blk:(i + 1) * blk, hh * HEAD_DIM:(hh + 1) * HEAD_DIM] = jnp.transpose(ot_sc[hh, i])


def _moba(proj, cos_t, sin_t, b, s):
    n = b * s
    nb = s // MOBA_BLOCK
    topk = min(MOBA_TOPK, nb - 1)
    aw = ATTN_HEADS * HEAD_DIM
    gw = MOBA_GROUP_LANES
    hp = aw // gw
    nh = gw // HEAD_DIM
    qc, kc, vc = 3 * aw // gw, 4 * aw // gw, 5 * aw // gw
    return pl.pallas_call(
        functools.partial(_moba_kernel, nb=nb, topk=topk),
        grid=(b, hp),
        in_specs=[pl.BlockSpec((s, gw), lambda bi, h: (bi, qc + h)),
                  pl.BlockSpec((s, gw), lambda bi, h: (bi, kc + h)),
                  pl.BlockSpec((s, gw), lambda bi, h: (bi, vc + h)),
                  pl.BlockSpec((s, gw), lambda bi, h: (0, 0)),
                  pl.BlockSpec((s, gw), lambda bi, h: (0, 0))],
        out_specs=pl.BlockSpec((s, gw), lambda bi, h: (bi, h)),
        out_shape=jax.ShapeDtypeStruct((n, aw), f32),
        scratch_shapes=[pltpu.VMEM((nh, s, HEAD_DIM), bf16),
                        pltpu.VMEM((nh, s, HEAD_DIM), bf16),
                        pltpu.VMEM((nh, nb, HEAD_DIM, MOBA_BLOCK), bf16),
                        pltpu.VMEM((nh, nb, nb, MOBA_BLOCK), f32),
                        pltpu.VMEM((nh, nb, HEAD_DIM, MOBA_BLOCK), f32)],
        compiler_params=_params(2),
        name="moba",
    )(proj, proj, proj, cos_t, sin_t)


def _mix_kernel(xin_ref, bg_ref, cg_ref, pxin_ref, pcg_ref, gc_ref, ga_ref, o_ref, x_ref,
                wc_ref, wco_ref, wao_ref, wm_ref, h_ref, *, tiles_per_seq):
    tm = xin_ref.shape[0]
    i = pl.program_id(0)
    keep_prev = jnp.where(i % tiles_per_seq == 0, 0.0, 1.0).astype(f32)
    u = cg_ref[...] * xin_ref[...]
    pu = pcg_ref[...] * pxin_ref[...] * keep_prev
    rows = lax.broadcasted_iota(i32, u.shape, 0)
    u1 = jnp.where(rows == 0, pu[SUBLANES - 1:SUBLANES, :], pltpu.roll(u, 1, 0))
    u2 = jnp.where(rows == 0, pu[SUBLANES - 2:SUBLANES - 1, :],
                   jnp.where(rows == 1, pu[SUBLANES - 1:SUBLANES, :], pltpu.roll(u, 2, 0)))
    y = wc_ref[0:1, :] * u2 + wc_ref[1:2, :] * u1 + wc_ref[2:3, :] * u
    y_conv = jnp.dot((bg_ref[...] * y).astype(bf16), wco_ref[...], preferred_element_type=f32)
    y_attn = jnp.dot(o_ref[...].astype(bf16), wao_ref[...], preferred_element_type=f32)
    merged = jax.nn.sigmoid(gc_ref[...]) * y_conv + jax.nn.sigmoid(ga_ref[...]) * y_attn
    h_ref[...] = x_ref[...] + jnp.dot(merged.astype(bf16), wm_ref[...], preferred_element_type=f32)


def _mix(proj, o, x2, w_conv, wco, wao, wm, s, tm):
    n, d = x2.shape
    cw = wco.shape[0]
    aw = wao.shape[0]
    rb = tm // SUBLANES
    prev = lambda col: (lambda i: (jnp.maximum(i * rb - 1, 0), col))
    gcol = 3 * cw + 3 * aw
    return pl.pallas_call(
        functools.partial(_mix_kernel, tiles_per_seq=s // tm),
        grid=(n // tm,),
        in_specs=[pl.BlockSpec((tm, cw), lambda i: (i, 0)),
                  pl.BlockSpec((tm, cw), lambda i: (i, 1)),
                  pl.BlockSpec((tm, cw), lambda i: (i, 2)),
                  pl.BlockSpec((SUBLANES, cw), prev(0)),
                  pl.BlockSpec((SUBLANES, cw), prev(2)),
                  pl.BlockSpec((tm, d), lambda i: (i, gcol // d)),
                  pl.BlockSpec((tm, d), lambda i: (i, gcol // d + 1)),
                  pl.BlockSpec((tm, aw), lambda i: (i, 0)),
                  pl.BlockSpec((tm, d), lambda i: (i, 0)),
                  pl.BlockSpec(w_conv.shape, lambda i: (0, 0)),
                  pl.BlockSpec(wco.shape, lambda i: (0, 0)),
                  pl.BlockSpec(wao.shape, lambda i: (0, 0)),
                  pl.BlockSpec(wm.shape, lambda i: (0, 0))],
        out_specs=pl.BlockSpec((tm, d), lambda i: (i, 0)),
        out_shape=jax.ShapeDtypeStruct((n, d), f32),
        compiler_params=_params(),
        name="mix",
    )(proj, proj, proj, proj, proj, proj, proj, o, x2, w_conv, wco, wao, wm)


def _memkv_kernel(m_ref, g_ref, w_ref, o_ref):
    a = _rms(m_ref[...], g_ref[...]).astype(bf16)
    o_ref[...] = jnp.dot(a, w_ref[...], preferred_element_type=f32).astype(bf16)


def _memkv(mem2, g, w, n_mem):
    nm, d = mem2.shape
    return pl.pallas_call(
        _memkv_kernel,
        grid=(nm // n_mem,),
        in_specs=[pl.BlockSpec((n_mem, d), lambda i: (i, 0)),
                  pl.BlockSpec((1, d), lambda i: (0, 0)),
                  pl.BlockSpec(w.shape, lambda i: (0, 0))],
        out_specs=pl.BlockSpec((n_mem, w.shape[1]), lambda i: (i, 0)),
        out_shape=jax.ShapeDtypeStruct((nm, w.shape[1]), bf16),
        compiler_params=_params(),
        name="memkv",
    )(mem2, g, w)


def _xattn_kernel(h_ref, g_ref, kv_ref, wq_ref, wo_ref, o_ref):
    h = h_ref[...]
    d = h.shape[1]
    hd = d // XATTN_HEADS
    q = jnp.dot(_rms(h, g_ref[...]).astype(bf16), wq_ref[...], preferred_element_type=f32)
    outs = []
    for hh in range(XATTN_HEADS):
        qh = q[:, hh * hd:(hh + 1) * hd].astype(bf16)
        kh = kv_ref[:, hh * hd:(hh + 1) * hd]
        vh = kv_ref[:, d + hh * hd:d + (hh + 1) * hd]
        sc = lax.dot_general(qh, kh, _NT, preferred_element_type=f32) * (hd ** -0.5)
        sc = sc - jnp.max(sc, axis=-1, keepdims=True)
        p = jnp.exp(sc)
        p = p / jnp.sum(p, axis=-1, keepdims=True)
        outs.append(jnp.dot(p.astype(bf16), vh, preferred_element_type=f32).astype(bf16))
    o = jnp.concatenate(outs, axis=-1)
    o_ref[...] = h + jnp.dot(o, wo_ref[...], preferred_element_type=f32)


def _xattn(h1, g, kv, wq, wo, s, n_mem, tm):
    n, d = h1.shape
    tps = s // tm
    return pl.pallas_call(
        _xattn_kernel,
        grid=(n // tm,),
        in_specs=[pl.BlockSpec((tm, d), lambda i: (i, 0)),
                  pl.BlockSpec((1, d), lambda i: (0, 0)),
                  pl.BlockSpec((n_mem, 2 * d), lambda i: (i // tps, 0)),
                  pl.BlockSpec(wq.shape, lambda i: (0, 0)),
                  pl.BlockSpec(wo.shape, lambda i: (0, 0))],
        out_specs=pl.BlockSpec((tm, d), lambda i: (i, 0)),
        out_shape=jax.ShapeDtypeStruct((n, d), f32),
        compiler_params=_params(),
        name="xattn",
    )(h1, g, kv, wq, wo)


def _topk_cols(sc, k, payload=None):
    r = sc.shape[0]
    ridx = lax.broadcasted_iota(i32, sc.shape, 0).astype(f32)
    vals, picks = [], []
    for _ in range(k):
        m = jnp.max(sc, axis=0, keepdims=True)
        im = jnp.min(jnp.where(sc == m, ridx, float(r)), axis=0, keepdims=True)
        hit = ridx == im
        vals.append(m)
        picks.append(im if payload is None else jnp.max(jnp.where(hit, payload, -1.0), axis=0, keepdims=True))
        sc = jnp.where(hit, -jnp.inf, sc)
    return jnp.concatenate(vals, axis=0), jnp.concatenate(picks, axis=0)


def _pair_candidates(k):
    return [(a, b) for a in range(k) for b in range(k) if (a + 1) * (b + 1) <= k]


def _select_kernel(h_ref, g_ref, wq_ref, keys_ref, a_ref, ids_ref, gate_ref):
    k = PEER_TOPK
    a = _rms(h_ref[...], g_ref[...])
    for c in range(a_ref.shape[1]):
        a_ref[:, c, :] = a[:, c * LANES:(c + 1) * LANES]
    q = jnp.dot(a.astype(bf16), wq_ref[...], preferred_element_type=f32)
    tm = q.shape[0]
    pairs = _pair_candidates(k)
    n_pad = -len(pairs) % SUBLANES
    id_rows, gate_rows = [], []
    for h in range(PEER_HEADS):
        tops = []
        for p in range(2):
            c0 = (h * 2 + p) * PEER_HALF
            qhp = q[:, c0:c0 + PEER_HALF].astype(bf16)
            st = lax.dot_general(keys_ref[h, p], qhp, _NT, preferred_element_type=f32)
            tops.append(_topk_cols(st, k))
        (v0, i0), (v1, i1) = tops
        i0 = i0 * float(PEER_N_KEYS)
        cand, cand_id = [], []
        for a_ in range(k):
            nb = sum(1 for (pa, _) in pairs if pa == a_)
            cand.append(v0[a_:a_ + 1, :] + v1[0:nb, :])
            cand_id.append(i0[a_:a_ + 1, :] + i1[0:nb, :])
        if n_pad:
            cand.append(jnp.full((n_pad, tm), -jnp.inf, f32))
            cand_id.append(jnp.zeros((n_pad, tm), f32))
        top_s, top_id = _topk_cols(jnp.concatenate(cand, axis=0), k, jnp.concatenate(cand_id, axis=0))
        id_rows.append(top_id.astype(i32))
        ex = jnp.exp(top_s - top_s[0:1, :])
        gate_rows.append(ex / jnp.sum(ex, axis=0, keepdims=True))
    ids_t = jnp.concatenate(id_rows, axis=0) * ROWS_PER_EXPERT
    for j in range(ids_ref.shape[0]):
        ids_ref[j] = ids_t[:, j * LANES:(j + 1) * LANES]
    gate_ref[...] = jnp.transpose(jnp.concatenate(gate_rows, axis=0))


def _select(h2, g, wq, keys, tm, first_block, n_blocks):
    d = h2.shape[1]
    n = n_blocks * tm
    return pl.pallas_call(
        _select_kernel,
        grid=(n_blocks,),
        in_specs=[pl.BlockSpec((tm, d), lambda i: (i + first_block, 0)),
                  pl.BlockSpec((1, d), lambda i: (0, 0)),
                  pl.BlockSpec(wq.shape, lambda i: (0, 0)),
                  pl.BlockSpec(keys.shape, lambda i: (0, 0, 0, 0))],
        out_specs=[pl.BlockSpec((tm, d // LANES, LANES), lambda i: (i, 0, 0)),
                   pl.BlockSpec((tm // LANES, N_SEL, LANES), lambda i: (i, 0, 0)),
                   pl.BlockSpec((tm, N_SEL), lambda i: (i, 0))],
        out_shape=[jax.ShapeDtypeStruct((n, d // LANES, LANES), f32),
                   jax.ShapeDtypeStruct((n // LANES, N_SEL, LANES), i32),
                   jax.ShapeDtypeStruct((n, N_SEL), f32)],
        compiler_params=_params(),
        name="peer_select",
    )(h2, g, wq, keys)


def _gelu_tanh(x):
    cdf = 0.5 * (1.0 + jnp.tanh(0.7978845608028654 * (x + 0.044715 * (x * x * x))))
    return x * cdf


def _pack_kernel(t_ref, o_ref):
    o_ref[...] = pltpu.bitcast(t_ref[...].astype(bf16), jnp.uint32)


def _pack_table(t):
    e, d = t.shape
    rows = e * d // LANES
    blk = min(rows, 8192)
    assert rows % blk == 0 and d == 2 * ROWS_PER_EXPERT * LANES
    return pl.pallas_call(
        _pack_kernel,
        grid=(rows // blk,),
        in_specs=[pl.BlockSpec((blk, LANES), lambda i: (i, 0))],
        out_specs=pl.BlockSpec((blk // 2, LANES), lambda i: (i, 0)),
        out_shape=jax.ShapeDtypeStruct((rows // 2, LANES), jnp.uint32),
        compiler_params=_params(),
        name="pack_table",
    )(t.reshape(rows, LANES))


def _expert_row(tab_ref, row0):
    words = tab_ref[pl.ds(pl.multiple_of(row0, ROWS_PER_EXPERT), ROWS_PER_EXPERT), :]
    return pltpu.bitcast(words, bf16).astype(f32)


def _token_id(ids_ref, e, t):
    return ids_ref.at[pl.ds(e * LANES, LANES)][t]


_FOLD_ORDER = (0, 4, 2, 6, 1, 5, 3, 7)


def _fold8(z):
    sub = lax.broadcasted_iota(i32, z[0].shape, 0)
    lo4 = sub < 4
    lo2 = (sub % 4) < 2
    even = (sub % 2) == 0
    c = [jnp.where(lo4, z[2 * k], z[2 * k + 1]) + pltpu.roll(jnp.where(lo4, z[2 * k + 1], z[2 * k]), 4, 0)
         for k in range(4)]
    e = [jnp.where(lo2, c[2 * k] + pltpu.roll(c[2 * k], 6, 0), c[2 * k + 1] + pltpu.roll(c[2 * k + 1], 2, 0))
         for k in range(2)]
    return jnp.where(even, e[0] + pltpu.roll(e[0], 7, 0), e[1] + pltpu.roll(e[1], 1, 0))


def _with_ids_tile(ids_hbm, bufs, sems, body):
    i = pl.program_id(0)
    n_steps = pl.num_programs(0)

    def tile_copy(tile, slot):
        return pltpu.make_async_copy(ids_hbm.at[tile], bufs[slot], sems.at[slot])

    @pl.when(i == 0)
    def _():
        tile_copy(0, 0).start()

    for slot in range(2):
        @pl.when(i % 2 == slot)
        def _():
            @pl.when(i + 1 < n_steps)
            def _():
                tile_copy(i + 1, 1 - slot).start()
            tile_copy(i, slot).wait()
            body(bufs[slot])


def _ids_scratch():
    return [pltpu.SMEM((N_SEL * LANES,), i32), pltpu.SMEM((N_SEL * LANES,), i32), pltpu.SemaphoreType.DMA((2,))]


def _peer_u_kernel(ids_hbm, x_ref, gate_ref, u_ref, w_ref, a_sc, ids0, ids1, sems):
    tt = x_ref.shape[0]
    lane = lax.broadcasted_iota(i32, (N_SEL, LANES), 1)

    def finish(part, t):
        col = jnp.sum(part, axis=1, keepdims=True)
        a_sc[...] = jnp.where(lane == t, col, a_sc[...])

    def body(ids_ref):
        def token(t, part_prev):
            finish(part_prev, t - 1)
            xt = x_ref[t]
            groups = []
            for g in range(N_SEL // SUBLANES):
                z = [_expert_row(u_ref, _token_id(ids_ref, g * SUBLANES + _FOLD_ORDER[k], t)) * xt
                     for k in range(SUBLANES)]
                groups.append(_fold8(z))
            return jnp.concatenate(groups, axis=0)

        a_sc[...] = jnp.zeros_like(a_sc)
        part = lax.fori_loop(0, tt, token, jnp.zeros((N_SEL, LANES), f32))
        finish(part, tt - 1)
        act = jnp.transpose(a_sc[...])[0:tt, :]
        w_ref[...] = gate_ref[...] * _gelu_tanh(act)

    _with_ids_tile(ids_hbm, (ids0, ids1), sems, body)


def _peer_u(ids, a3, gate, u_tab, tt):
    n_tiles = ids.shape[0]
    n = n_tiles * tt
    assert tt == LANES and ids.shape[1] == N_SEL * LANES
    return pl.pallas_call(
        _peer_u_kernel,
        grid=(n_tiles,),
        in_specs=[pl.BlockSpec(memory_space=pl.ANY),
                  pl.BlockSpec((tt, SUBLANES, LANES), lambda i: (i, 0, 0)),
                  pl.BlockSpec((tt, N_SEL), lambda i: (i, 0)),
                  pl.BlockSpec(u_tab.shape, lambda i: (0, 0), pipeline_mode=pl.Buffered(1))],
        out_specs=pl.BlockSpec((tt, N_SEL), lambda i: (i, 0)),
        out_shape=jax.ShapeDtypeStruct((n, N_SEL), f32),
        scratch_shapes=[pltpu.VMEM((N_SEL, LANES), f32)] + _ids_scratch(),
        compiler_params=_params(),
        name="peer_u",
    )(ids, a3, gate, u_tab)


def _peer_v_kernel(ids_hbm, w_ref, v_ref, h_ref, g_ref, o_ref, ids0, ids1, sems, p_sc):
    tt = o_ref.shape[0]
    n_acc = 4

    def lane_bcast_weights(t):
        return jnp.transpose(jnp.broadcast_to(w_ref[pl.ds(t, 1), :], (LANES, N_SEL)))

    def body(ids_ref):
        def token(t, wb):
            wb_next = lane_bcast_weights(jnp.minimum(t + 1, tt - 1))
            accs = [jnp.zeros((SUBLANES, LANES), f32) for _ in range(n_acc)]
            for e in range(N_SEL):
                wv = jnp.broadcast_to(wb[e:e + 1, :], (SUBLANES, LANES))
                accs[e % n_acc] = accs[e % n_acc] + _expert_row(v_ref, _token_id(ids_ref, e, t)) * wv
            p_sc[t] = (accs[0] + accs[1]) + (accs[2] + accs[3])
            return wb_next

        lax.fori_loop(0, tt, token, lane_bcast_weights(0))
        peer = jnp.concatenate([p_sc[:, c, :] for c in range(SUBLANES)], axis=1)
        o_ref[...] = _rms(h_ref[...] + peer, g_ref[...])

    _with_ids_tile(ids_hbm, (ids0, ids1), sems, body)


def _peer_v(ids, w, v_tab, h2, g, tt, n_out):
    n_tiles = ids.shape[0]
    n, d = h2.shape
    assert tt == LANES and ids.shape[1] == N_SEL * LANES and n == n_tiles * tt <= n_out and d == SUBLANES * LANES
    return pl.pallas_call(
        _peer_v_kernel,
        grid=(n_tiles,),
        in_specs=[pl.BlockSpec(memory_space=pl.ANY),
                  pl.BlockSpec((tt, N_SEL), lambda i: (i, 0)),
                  pl.BlockSpec(v_tab.shape, lambda i: (0, 0), pipeline_mode=pl.Buffered(1)),
                  pl.BlockSpec((tt, d), lambda i: (i, 0)),
                  pl.BlockSpec((1, d), lambda i: (0, 0))],
        out_specs=pl.BlockSpec((tt, d), lambda i: (i, 0)),
        out_shape=jax.ShapeDtypeStruct((n_out, d), f32),
        scratch_shapes=_ids_scratch() + [pltpu.VMEM((tt, SUBLANES, LANES), f32)],
        compiler_params=_params(),
        name="peer_v",
    )(ids, w, v_tab, h2, g)


SC_WORKERS = 32
SC_LANES = 16
SC_BATCH_SHARE_16THS = 7


def _peer_sc(ids_tok, x3, gate, u_words, v_words):
    from jax.experimental.pallas import tpu_sc as plsc
    n_sc = ids_tok.shape[0]
    words = v_words.shape[1]
    d = 2 * words
    tpw = n_sc // SC_WORKERS
    hrows = N_SEL // 2
    assert n_sc % SC_WORKERS == 0 and tpw % 2 == 0 and words == ROWS_PER_EXPERT * LANES
    half = words // 2
    nvec = half // SC_LANES
    mesh = plsc.VectorSubcoreMesh(core_axis_name="c", subcore_axis_name="s")

    def halves(wd):
        return (lax.bitcast_convert_type(wd << 16, f32), lax.bitcast_convert_type(wd & jnp.int32(-65536), f32))

    def body(u_hbm, v_hbm, ids_hbm, x_hbm, g_hbm, out_hbm,
             idx_a0, idx_b0, g_0, x_0, idx_a1, idx_b1, g_1, x_1, w_v, rows_a, rows_b, part_v, out_v, sem_a, sem_b):
        wid = lax.axis_index("s") * 2 + lax.axis_index("c")
        lane = lax.broadcasted_iota(i32, (SC_LANES,), 0)
        first = wid * tpw
        last = first + tpw - 1
        inputs = ((idx_a0, idx_b0, g_0, x_0), (idx_a1, idx_b1, g_1, x_1))
        row_bufs = ((rows_a, sem_a), (rows_b, sem_b))

        def load_inputs(t, par):
            idx_a, idx_b, g_v, x_v = inputs[par]
            pltpu.sync_copy(ids_hbm.at[t, pl.ds(0, hrows)], idx_a)
            pltpu.sync_copy(ids_hbm.at[t, pl.ds(hrows, hrows)], idx_b)
            pltpu.sync_copy(g_hbm.at[t], g_v)
            pltpu.sync_copy(x_hbm.at[t], x_v)

        def gather(tab_hbm, par, h):
            rows, sem = row_bufs[h]
            return pltpu.make_async_copy(tab_hbm.at[inputs[par][h]], rows, sem)

        def dots(par, h):
            rows, x_v = row_bufs[h][0], inputs[par][3]
            for p in range(2):
                xs = []
                for j in range(nvec):
                    o = p * half + j * SC_LANES
                    r, l0 = o // LANES, o % LANES
                    xs.append((x_v[2 * r, pl.ds(l0, SC_LANES)], x_v[2 * r + 1, pl.ds(l0, SC_LANES)]))

                def dot_pair(e, c, p=p, xs=xs):
                    acc = None
                    for j in range(nvec):
                        lo, hi = halves(rows[e, pl.ds(p * half + j * SC_LANES, SC_LANES)])
                        term = lo * xs[j][0] + hi * xs[j][1]
                        acc = term if acc is None else acc + term
                    row = h * hrows + e
                    part_v[row, :] = acc if p == 0 else part_v[row, :] + acc
                    return c

                lax.fori_loop(0, hrows, dot_pair, 0)

        def weights(par):
            g_v = inputs[par][2]
            for k in range(N_SEL // SC_LANES):
                act = jnp.zeros((SC_LANES,), f32)
                for q in range(SC_LANES):
                    act = jnp.where(lane == q, jnp.sum(part_v[k * SC_LANES + q, :]), act)
                inner = 0.7978845608028654 * (act + 0.044715 * (act * act * act))
                tanh = 1.0 - 2.0 / (jnp.exp(2.0 * inner) + 1.0)
                w_v[pl.ds(k * SC_LANES, SC_LANES)] = g_v[pl.ds(k * SC_LANES, SC_LANES)] * (act * (0.5 * (1.0 + tanh)))

        def accumulate(h):
            rows = row_bufs[h][0]
            for p in range(2):
                slots = []
                for j in range(nvec):
                    o = p * half + j * SC_LANES
                    r, l0 = o // LANES, o % LANES
                    slots += [2 * LANES * r + l0, 2 * LANES * r + LANES + l0]

                def acc_pair(e, accs, p=p):
                    row = h * hrows + e
                    wvec = w_v[pl.ds(pl.multiple_of((row // SC_LANES) * SC_LANES, SC_LANES), SC_LANES)]
                    wr = jnp.sum(jnp.where(lane == row % SC_LANES, wvec, 0.0))
                    out = []
                    for j in range(nvec):
                        lo, hi = halves(rows[e, pl.ds(p * half + j * SC_LANES, SC_LANES)])
                        out.append(accs[2 * j] + lo * wr)
                        out.append(accs[2 * j + 1] + hi * wr)
                    return tuple(out)

                init = tuple(jnp.zeros((SC_LANES,), f32) if h == 0 else out_v[pl.ds(o, SC_LANES)] for o in slots)
                accs = lax.fori_loop(0, hrows, acc_pair, init)
                for o, acc in zip(slots, accs):
                    out_v[pl.ds(o, SC_LANES)] = acc

        def token(t, par):
            nxt = jnp.minimum(t + 1, last)
            gather(u_hbm, par, 0).wait()
            dots(par, 0)
            gather(v_hbm, par, 0).start()
            gather(u_hbm, par, 1).wait()
            dots(par, 1)
            gather(v_hbm, par, 1).start()
            weights(par)
            load_inputs(nxt, 1 - par)
            gather(v_hbm, par, 0).wait()
            accumulate(0)
            gather(u_hbm, 1 - par, 0).start()
            gather(v_hbm, par, 1).wait()
            accumulate(1)
            gather(u_hbm, 1 - par, 1).start()
            pltpu.sync_copy(out_v, out_hbm.at[t])

        load_inputs(first, 0)
        gather(u_hbm, 0, 0).start()
        gather(u_hbm, 0, 1).start()

        def token_pair(i, carry):
            token(first + 2 * i, 0)
            token(first + 2 * i + 1, 1)
            return carry

        lax.fori_loop(0, tpw // 2, token_pair, 0)
        gather(u_hbm, 0, 0).wait()
        gather(u_hbm, 0, 1).wait()

    small = [pltpu.VMEM((hrows,), i32), pltpu.VMEM((hrows,), i32), pltpu.VMEM((N_SEL,), f32),
             pltpu.VMEM((SUBLANES, LANES), f32)]
    return pl.kernel(
        body,
        out_type=jax.ShapeDtypeStruct((n_sc, d), f32),
        mesh=mesh,
        scratch_types=small + small + [pltpu.VMEM((N_SEL,), f32), pltpu.VMEM((hrows, words), i32),
                                       pltpu.VMEM((hrows, words), i32), pltpu.VMEM((N_SEL, SC_LANES), f32),
                                       pltpu.VMEM((d,), f32), pltpu.SemaphoreType.DMA, pltpu.SemaphoreType.DMA],
        compiler_params=pltpu.CompilerParams(needs_layout_passes=False),
        name="peer_sc",
    )(u_words, v_words, ids_tok, x3, gate)


def _final_tail_kernel(out_in_ref, h_ref, p_ref, g_ref, o_ref):
    del out_in_ref
    o_ref[...] = _rms(h_ref[...] + p_ref[...], g_ref[...])


def _final_tail(out, h2_tail, peer_tail, g, tm, first_tile):
    n, d = out.shape
    tail_tiles = peer_tail.shape[0] // tm
    return pl.pallas_call(
        _final_tail_kernel,
        grid=(tail_tiles,),
        in_specs=[pl.BlockSpec(memory_space=pl.ANY),
                  pl.BlockSpec((tm, d), lambda i: (i, 0)),
                  pl.BlockSpec((tm, d), lambda i: (i, 0)),
                  pl.BlockSpec((1, d), lambda i: (0, 0))],
        out_specs=pl.BlockSpec((tm, d), lambda i: (i + first_tile, 0)),
        out_shape=jax.ShapeDtypeStruct((n, d), f32),
        input_output_aliases={0: 0},
        compiler_params=_params(),
        name="final_tail",
    )(out, h2_tail, peer_tail, g)


def _rope_tables(s):
    half = HEAD_DIM // 2
    inv = ROPE_THETA ** (-jnp.arange(half, dtype=f32) / half)
    ang = jnp.arange(s, dtype=f32)[:, None] * inv[None, :]
    cos, sin = jnp.cos(ang), jnp.sin(ang)
    reps = MOBA_GROUP_LANES // HEAD_DIM
    cos_t = jnp.tile(jnp.concatenate([cos, cos], axis=1), (1, reps))
    sin_t = jnp.tile(jnp.concatenate([-sin, sin], axis=1), (1, reps))
    return cos_t, sin_t


def _row_tile(s, want):
    t = min(want, s)
    while s % t:
        t //= 2
    assert t % SUBLANES == 0
    return t


def kernel(x, mem, g_mix, w_in, w_conv, w_conv_out, w_attn_out, w_merge, g_xattn, g_mem,
           w_xq, w_xkv, w_xo, g_ffn, w_pq, peer_sub_keys, peer_u, peer_v, g_final):
    b, s, d = x.shape
    n_mem = mem.shape[1]
    assert w_in.shape[0] == 1, "single-layer trunk only"
    assert d == SUBLANES * LANES and s % MOBA_BLOCK == 0
    n_exp = peer_u.shape[1]
    l = 0
    cos_t, sin_t = _rope_tables(s)
    tm = _row_tile(s, 512)
    tt = _row_tile(s, LANES)
    tsel = _row_tile(s, 256)
    w_in_b, w_co_b, w_ao_b, w_m_b = (t[l].astype(bf16) for t in (w_in, w_conv_out, w_attn_out, w_merge))
    w_xq_b, w_xkv_b, w_xo_b, w_pq_b, keys_b = (t[l].astype(bf16) for t in (w_xq, w_xkv, w_xo, w_pq, peer_sub_keys))

    def trunk(xb, memb):
        nb = xb.shape[0]
        h = xb.reshape(nb * s, d)
        proj = _inproj(h, g_mix[l][None, :], w_in_b, tm)
        o = _moba(proj, cos_t, sin_t, nb, s)
        h1 = _mix(proj, o, h, w_conv[l], w_co_b, w_ao_b, w_m_b, s, tm)
        kv = _memkv(memb.reshape(nb * n_mem, d), g_mem[l][None, :], w_xkv_b, n_mem)
        return _xattn(h1, g_xattn[l][None, :], kv, w_xq_b, w_xo_b, s, n_mem, tm)

    def select(h2):
        a3, ids, gate = _select(h2, g_ffn[l][None, :], w_pq_b, keys_b, tsel, 0, h2.shape[0] // tsel)
        return a3, ids, gate

    u_tab = _pack_table(peer_u[l])
    v_tab = _pack_table(peer_v[l])
    b_sc = (b * SC_BATCH_SHARE_16THS) // 16
    b_tc = b - b_sc
    tc_tiles = b_tc * s // tt
    if b_sc:
        h2_s = trunk(x[b_tc:], mem[b_tc:])
        a3_s, ids_s, gate_s = select(h2_s)
        as_rows = lambda tab: lax.bitcast_convert_type(tab, i32).reshape(n_exp, ROWS_PER_EXPERT * LANES)
        ids_tok = (ids_s // ROWS_PER_EXPERT).transpose(0, 2, 1).reshape(b_sc * s, N_SEL)
        peer_sc = _peer_sc(ids_tok, a3_s, gate_s, as_rows(u_tab), as_rows(v_tab))
    h2 = trunk(x[:b_tc], mem[:b_tc])
    a3, ids, gate = select(h2)
    ids = ids.reshape(tc_tiles, N_SEL * LANES)
    w = _peer_u(ids, a3, gate, u_tab, tt)
    out = _peer_v(ids, w, v_tab, h2, g_final[None, :], tt, b * s)
    if b_sc:
        out = _final_tail(out, h2_s, peer_sc, g_final[None, :], tt, tc_tiles)
    return out.reshape(b, s, d)
```

```python
import functools

import jax
import jax.numpy as jnp
from jax import lax
from jax.experimental import pallas as pl
from jax.experimental.pallas import tpu as pltpu

f32 = jnp.float32
bf16 = jnp.bfloat16
i32 = jnp.int32

EPS = 1e-6
MASK_VALUE = -1e30
ROPE_THETA = 10000.0

XATTN_HEADS = 4
ATTN_HEADS = 8
HEAD_DIM = 64
MOBA_BLOCK = 256
MOBA_TOPK = 3
MOBA_GROUP_LANES = 256
PEER_HEADS = 8
PEER_N_KEYS = 128
PEER_HALF = 128
PEER_TOPK = 16
N_SEL = PEER_HEADS * PEER_TOPK
ROWS_PER_EXPERT = 4

LANES = 128
SUBLANES = 8
VMEM_LIMIT = 56 * 1024 * 1024

_NT = (((1,), (1,)), ((), ()))


def _params(n_axes=1, vmem=VMEM_LIMIT):
    return pltpu.CompilerParams(dimension_semantics=("arbitrary",) * n_axes, vmem_limit_bytes=vmem)


def _rms(x, g):
    return x * lax.rsqrt(jnp.mean(x * x, axis=-1, keepdims=True) + EPS) * g


def _inproj_kernel(x_ref, g_ref, w_ref, o_ref, *, chunk):
    a = _rms(x_ref[...], g_ref[...]).astype(bf16)
    for c in range(o_ref.shape[1] // chunk):
        o_ref[:, c * chunk:(c + 1) * chunk] = jnp.dot(
            a, w_ref[:, c * chunk:(c + 1) * chunk], preferred_element_type=f32)


def _inproj(x2, g, w, tm):
    n, d = x2.shape
    width = w.shape[1]
    return pl.pallas_call(
        functools.partial(_inproj_kernel, chunk=width // 4),
        grid=(n // tm,),
        in_specs=[pl.BlockSpec((tm, d), lambda i: (i, 0)),
                  pl.BlockSpec((1, d), lambda i: (0, 0)),
                  pl.BlockSpec((d, width), lambda i: (0, 0), pipeline_mode=pl.Buffered(1))],
        out_specs=pl.BlockSpec((tm, width), lambda i: (i, 0)),
        out_shape=jax.ShapeDtypeStruct((n, width), f32),
        compiler_params=_params(),
        name="inproj",
    )(x2, g, w)


def _moba_kernel(q_ref, k_ref, v_ref, cos_ref, sin_ref, o_ref,
                 qb_sc, kb_sc, vt_sc, bias_sc, ot_sc, *, nb, topk):
    blk = MOBA_BLOCK
    s, gw = q_ref.shape
    lane = lax.broadcasted_iota(i32, (s, gw), 1)
    first_half = (lane % HEAD_DIM) < (HEAD_DIM // 2)
    cos = cos_ref[...]
    sin = sin_ref[...]

    def rope(t):
        partner = jnp.where(first_half, pltpu.roll(t, gw - HEAD_DIM // 2, 1),
                            pltpu.roll(t, HEAD_DIM // 2, 1))
        return t * cos + partner * sin

    q2 = rope(q_ref[...])
    k2 = rope(k_ref[...])
    v2 = v_ref[...]
    scale = HEAD_DIM ** -0.5
    blk_of_q = lax.broadcasted_iota(i32, (nb, s), 1) // blk
    m_iota = lax.broadcasted_iota(i32, (nb, s), 0)
    past = m_iota < blk_of_q
    kpos = lax.broadcasted_iota(i32, (blk, blk), 0)
    qpos = lax.broadcasted_iota(i32, (blk, blk), 1)
    causal = kpos <= qpos

    heads = range(gw // HEAD_DIM)
    for hh in heads:
        qh = q2[:, hh * HEAD_DIM:(hh + 1) * HEAD_DIM]
        kh = k2[:, hh * HEAD_DIM:(hh + 1) * HEAD_DIM]
        vh = v2[:, hh * HEAD_DIM:(hh + 1) * HEAD_DIM]
        kbar = jnp.mean(kh.reshape(nb, blk, HEAD_DIM), axis=1)
        gt = lax.dot_general(kbar, qh, _NT, precision=lax.Precision.HIGHEST,
                             preferred_element_type=f32)
        rows = []
        for n in range(nb):
            gn = gt[n:n + 1, :]
            beats = past & ((gt > gn) | ((gt == gn) & (m_iota < n)))
            cnt = jnp.sum(beats.astype(i32), axis=0, keepdims=True)
            sel = (blk_of_q[0:1, :] > n) & (cnt < topk)
            rows.append(jnp.where(sel, 0.0, MASK_VALUE).astype(f32))
        bias = jnp.concatenate(rows, axis=0)
        vt = jnp.transpose(vh).astype(bf16)
        for i in range(nb):
            bias_sc[hh, i] = bias[:, i * blk:(i + 1) * blk]
            vt_sc[hh, i] = vt[:, i * blk:(i + 1) * blk]
        qb_sc[hh] = qh.astype(bf16)
        kb_sc[hh] = kh.astype(bf16)

    def q_tile(i, carry):
        q0 = pl.multiple_of(i * blk, blk)
        diag = [lax.dot_general(kb_sc[hh, pl.ds(q0, blk), :], qb_sc[hh, pl.ds(q0, blk), :], _NT,
                                preferred_element_type=f32) for hh in heads]
        probs, stats = [], []
        for hh in heads:
            sd = jnp.where(causal, diag[hh] * scale, MASK_VALUE)
            m0 = jnp.max(sd, axis=0, keepdims=True)
            p0 = jnp.exp(sd - m0)
            probs.append(p0.astype(bf16))
            stats.append((m0, jnp.sum(p0, axis=0, keepdims=True)))
        init = [stats[hh] + (jnp.dot(vt_sc[hh, i], probs[hh], preferred_element_type=f32),)
                for hh in heads]

        def kv_step(j, st):
            k0 = pl.multiple_of(j * blk, blk)
            scores = [lax.dot_general(kb_sc[hh, pl.ds(k0, blk), :], qb_sc[hh, pl.ds(q0, blk), :], _NT,
                                      preferred_element_type=f32) for hh in heads]
            probs, stats = [], []
            for hh in heads:
                m, l, _ = st[hh]
                sj = scores[hh] * scale + bias_sc[hh, i, pl.ds(j, 1), :]
                m_new = jnp.maximum(m, jnp.max(sj, axis=0, keepdims=True))
                alpha = jnp.exp(m - m_new)
                p = jnp.exp(sj - m_new)
                probs.append(p.astype(bf16))
                stats.append((m_new, alpha * l + jnp.sum(p, axis=0, keepdims=True), alpha))
            out = []
            for hh in heads:
                m_new, l, alpha = stats[hh]
                acc = alpha * st[hh][2] + jnp.dot(vt_sc[hh, j], probs[hh], preferred_element_type=f32)
                out.append((m_new, l, acc))
            return tuple(out)

        final = lax.fori_loop(0, i, kv_step, tuple(init))
        for hh in heads:
            _, l, acc = final[hh]
            ot_sc[hh, i] = acc * (1.0 / l)
        return carry

    lax.fori_loop(0, nb, q_tile, 0)
    for hh in heads:
        for i in range(nb):
            o_ref[i * blk:(i + 1) * blk, hh * HEAD_DIM:(hh + 1) * HEAD_DIM] = jnp.transpose(ot_sc[hh, i])


def _moba(proj, cos_t, sin_t, b, s):
    n = b * s
    nb = s // MOBA_BLOCK
    topk = min(MOBA_TOPK, nb - 1)
    aw = ATTN_HEADS * HEAD_DIM
    gw = MOBA_GROUP_LANES
    hp = aw // gw
    nh = gw // HEAD_DIM
    qc, kc, vc = 3 * aw // gw, 4 * aw // gw, 5 * aw // gw
    return pl.pallas_call(
        functools.partial(_moba_kernel, nb=nb, topk=topk),
        grid=(b, hp),
        in_specs=[pl.BlockSpec((s, gw), lambda bi, h: (bi, qc + h)),
                  pl.BlockSpec((s, gw), lambda bi, h: (bi, kc + h)),
                  pl.BlockSpec((s, gw), lambda bi, h: (bi, vc + h)),
                  pl.BlockSpec((s, gw), lambda bi, h: (0, 0)),
                  pl.BlockSpec((s, gw), lambda bi, h: (0, 0))],
        out_specs=pl.BlockSpec((s, gw), lambda bi, h: (bi, h)),
        out_shape=jax.ShapeDtypeStruct((n, aw), f32),
        scratch_shapes=[pltpu.VMEM((nh, s, HEAD_DIM), bf16),
                        pltpu.VMEM((nh, s, HEAD_DIM), bf16),
                        pltpu.VMEM((nh, nb, HEAD_DIM, MOBA_BLOCK), bf16),
                        pltpu.VMEM((nh, nb, nb, MOBA_BLOCK), f32),
                        pltpu.VMEM((nh, nb, HEAD_DIM, MOBA_BLOCK), f32)],
        compiler_params=_params(2),
        name="moba",
    )(proj, proj, proj, cos_t, sin_t)


def _mix_kernel(xin_ref, bg_ref, cg_ref, pxin_ref, pcg_ref, gc_ref, ga_ref, o_ref, x_ref,
                wc_ref, wco_ref, wao_ref, wm_ref, h_ref, *, tiles_per_seq):
    tm = xin_ref.shape[0]
    i = pl.program_id(0)
    keep_prev = jnp.where(i % tiles_per_seq == 0, 0.0, 1.0).astype(f32)
    u = cg_ref[...] * xin_ref[...]
    pu = pcg_ref[...] * pxin_ref[...] * keep_prev
    rows = lax.broadcasted_iota(i32, u.shape, 0)
    u1 = jnp.where(rows == 0, pu[SUBLANES - 1:SUBLANES, :], pltpu.roll(u, 1, 0))
    u2 = jnp.where(rows == 0, pu[SUBLANES - 2:SUBLANES - 1, :],
                   jnp.where(rows == 1, pu[SUBLANES - 1:SUBLANES, :], pltpu.roll(u, 2, 0)))
    y = wc_ref[0:1, :] * u2 + wc_ref[1:2, :] * u1 + wc_ref[2:3, :] * u
    y_conv = jnp.dot((bg_ref[...] * y).astype(bf16), wco_ref[...], preferred_element_type=f32)
    y_attn = jnp.dot(o_ref[...].astype(bf16), wao_ref[...], preferred_element_type=f32)
    merged = jax.nn.sigmoid(gc_ref[...]) * y_conv + jax.nn.sigmoid(ga_ref[...]) * y_attn
    h_ref[...] = x_ref[...] + jnp.dot(merged.astype(bf16), wm_ref[...], preferred_element_type=f32)


def _mix(proj, o, x2, w_conv, wco, wao, wm, s, tm):
    n, d = x2.shape
    cw = wco.shape[0]
    aw = wao.shape[0]
    rb = tm // SUBLANES
    prev = lambda col: (lambda i: (jnp.maximum(i * rb - 1, 0), col))
    gcol = 3 * cw + 3 * aw
    return pl.pallas_call(
        functools.partial(_mix_kernel, tiles_per_seq=s // tm),
        grid=(n // tm,),
        in_specs=[pl.BlockSpec((tm, cw), lambda i: (i, 0)),
                  pl.BlockSpec((tm, cw), lambda i: (i, 1)),
                  pl.BlockSpec((tm, cw), lambda i: (i, 2)),
                  pl.BlockSpec((SUBLANES, cw), prev(0)),
                  pl.BlockSpec((SUBLANES, cw), prev(2)),
                  pl.BlockSpec((tm, d), lambda i: (i, gcol // d)),
                  pl.BlockSpec((tm, d), lambda i: (i, gcol // d + 1)),
                  pl.BlockSpec((tm, aw), lambda i: (i, 0)),
                  pl.BlockSpec((tm, d), lambda i: (i, 0)),
                  pl.BlockSpec(w_conv.shape, lambda i: (0, 0)),
                  pl.BlockSpec(wco.shape, lambda i: (0, 0)),
                  pl.BlockSpec(wao.shape, lambda i: (0, 0)),
                  pl.BlockSpec(wm.shape, lambda i: (0, 0))],
        out_specs=pl.BlockSpec((tm, d), lambda i: (i, 0)),
        out_shape=jax.ShapeDtypeStruct((n, d), f32),
        compiler_params=_params(),
        name="mix",
    )(proj, proj, proj, proj, proj, proj, proj, o, x2, w_conv, wco, wao, wm)


def _memkv_kernel(m_ref, g_ref, w_ref, o_ref):
    a = _rms(m_ref[...], g_ref[...]).astype(bf16)
    o_ref[...] = jnp.dot(a, w_ref[...], preferred_element_type=f32).astype(bf16)


def _memkv(mem2, g, w, n_mem):
    nm, d = mem2.shape
    return pl.pallas_call(
        _memkv_kernel,
        grid=(nm // n_mem,),
        in_specs=[pl.BlockSpec((n_mem, d), lambda i: (i, 0)),
                  pl.BlockSpec((1, d), lambda i: (0, 0)),
                  pl.BlockSpec(w.shape, lambda i: (0, 0))],
        out_specs=pl.BlockSpec((n_mem, w.shape[1]), lambda i: (i, 0)),
        out_shape=jax.ShapeDtypeStruct((nm, w.shape[1]), bf16),
        compiler_params=_params(),
        name="memkv",
    )(mem2, g, w)


def _xattn_kernel(h_ref, g_ref, kv_ref, wq_ref, wo_ref, o_ref):
    h = h_ref[...]
    d = h.shape[1]
    hd = d // XATTN_HEADS
    q = jnp.dot(_rms(h, g_ref[...]).astype(bf16), wq_ref[...], preferred_element_type=f32)
    outs = []
    for hh in range(XATTN_HEADS):
        qh = q[:, hh * hd:(hh + 1) * hd].astype(bf16)
        kh = kv_ref[:, hh * hd:(hh + 1) * hd]
        vh = kv_ref[:, d + hh * hd:d + (hh + 1) * hd]
        sc = lax.dot_general(qh, kh, _NT, preferred_element_type=f32) * (hd ** -0.5)
        sc = sc - jnp.max(sc, axis=-1, keepdims=True)
        p = jnp.exp(sc)
        p = p / jnp.sum(p, axis=-1, keepdims=True)
        outs.append(jnp.dot(p.astype(bf16), vh, preferred_element_type=f32).astype(bf16))
    o = jnp.concatenate(outs, axis=-1)
    o_ref[...] = h + jnp.dot(o, wo_ref[...], preferred_element_type=f32)


def _xattn(h1, g, kv, wq, wo, s, n_mem, tm):
    n, d = h1.shape
    tps = s // tm
    return pl.pallas_call(
        _xattn_kernel,
        grid=(n // tm,),
        in_specs=[pl.BlockSpec((tm, d), lambda i: (i, 0)),
                  pl.BlockSpec((1, d), lambda i: (0, 0)),
                  pl.BlockSpec((n_mem, 2 * d), lambda i: (i // tps, 0)),
                  pl.BlockSpec(wq.shape, lambda i: (0, 0)),
                  pl.BlockSpec(wo.shape, lambda i: (0, 0))],
        out_specs=pl.BlockSpec((tm, d), lambda i: (i, 0)),
        out_shape=jax.ShapeDtypeStruct((n, d), f32),
        compiler_params=_params(),
        name="xattn",
    )(h1, g, kv, wq, wo)


def _topk_cols(sc, k, payload=None):
    r = sc.shape[0]
    ridx = lax.broadcasted_iota(i32, sc.shape, 0).astype(f32)
    vals, picks = [], []
    for _ in range(k):
        m = jnp.max(sc, axis=0, keepdims=True)
        im = jnp.min(jnp.where(sc == m, ridx, float(r)), axis=0, keepdims=True)
        hit = ridx == im
        vals.append(m)
        picks.append(im if payload is None else jnp.max(jnp.where(hit, payload, -1.0), axis=0, keepdims=True))
        sc = jnp.where(hit, -jnp.inf, sc)
    return jnp.concatenate(vals, axis=0), jnp.concatenate(picks, axis=0)


def _pair_candidates(k):
    return [(a, b) for a in range(k) for b in range(k) if (a + 1) * (b + 1) <= k]


def _select_kernel(h_ref, g_ref, wq_ref, keys_ref, a_ref, ids_ref, gate_ref):
    k = PEER_TOPK
    a = _rms(h_ref[...], g_ref[...])
    for c in range(a_ref.shape[1]):
        a_ref[:, c, :] = a[:, c * LANES:(c + 1) * LANES]
    q = jnp.dot(a.astype(bf16), wq_ref[...], preferred_element_type=f32)
    tm = q.shape[0]
    pairs = _pair_candidates(k)
    n_pad = -len(pairs) % SUBLANES
    id_rows, gate_rows = [], []
    for h in range(PEER_HEADS):
        tops = []
        for p in range(2):
            c0 = (h * 2 + p) * PEER_HALF
            qhp = q[:, c0:c0 + PEER_HALF].astype(bf16)
            st = lax.dot_general(keys_ref[h, p], qhp, _NT, preferred_element_type=f32)
            tops.append(_topk_cols(st, k))
        (v0, i0), (v1, i1) = tops
        i0 = i0 * float(PEER_N_KEYS)
        cand, cand_id = [], []
        for a_ in range(k):
            nb = sum(1 for (pa, _) in pairs if pa == a_)
            cand.append(v0[a_:a_ + 1, :] + v1[0:nb, :])
            cand_id.append(i0[a_:a_ + 1, :] + i1[0:nb, :])
        if n_pad:
            cand.append(jnp.full((n_pad, tm), -jnp.inf, f32))
            cand_id.append(jnp.zeros((n_pad, tm), f32))
        top_s, top_id = _topk_cols(jnp.concatenate(cand, axis=0), k, jnp.concatenate(cand_id, axis=0))
        id_rows.append(top_id.astype(i32))
        ex = jnp.exp(top_s - top_s[0:1, :])
        gate_rows.append(ex / jnp.sum(ex, axis=0, keepdims=True))
    ids_t = jnp.concatenate(id_rows, axis=0) * ROWS_PER_EXPERT
    for j in range(ids_ref.shape[0]):
        ids_ref[j] = ids_t[:, j * LANES:(j + 1) * LANES]
    gate_ref[...] = jnp.transpose(jnp.concatenate(gate_rows, axis=0))


def _select(h2, g, wq, keys, tm, first_block, n_blocks):
    d = h2.shape[1]
    n = n_blocks * tm
    return pl.pallas_call(
        _select_kernel,
        grid=(n_blocks,),
        in_specs=[pl.BlockSpec((tm, d), lambda i: (i + first_block, 0)),
                  pl.BlockSpec((1, d), lambda i: (0, 0)),
                  pl.BlockSpec(wq.shape, lambda i: (0, 0)),
                  pl.BlockSpec(keys.shape, lambda i: (0, 0, 0, 0))],
        out_specs=[pl.BlockSpec((tm, d // LANES, LANES), lambda i: (i, 0, 0)),
                   pl.BlockSpec((tm // LANES, N_SEL, LANES), lambda i: (i, 0, 0)),
                   pl.BlockSpec((tm, N_SEL), lambda i: (i, 0))],
        out_shape=[jax.ShapeDtypeStruct((n, d // LANES, LANES), f32),
                   jax.ShapeDtypeStruct((n // LANES, N_SEL, LANES), i32),
                   jax.ShapeDtypeStruct((n, N_SEL), f32)],
        compiler_params=_params(),
        name="peer_select",
    )(h2, g, wq, keys)


def _gelu_tanh(x):
    cdf = 0.5 * (1.0 + jnp.tanh(0.7978845608028654 * (x + 0.044715 * (x * x * x))))
    return x * cdf


def _pack_kernel(t_ref, o_ref):
    o_ref[...] = pltpu.bitcast(t_ref[...].astype(bf16), jnp.uint32)


def _pack_table(t):
    e, d = t.shape
    rows = e * d // LANES
    blk = min(rows, 8192)
    assert rows % blk == 0 and d == 2 * ROWS_PER_EXPERT * LANES
    return pl.pallas_call(
        _pack_kernel,
        grid=(rows // blk,),
        in_specs=[pl.BlockSpec((blk, LANES), lambda i: (i, 0))],
        out_specs=pl.BlockSpec((blk // 2, LANES), lambda i: (i, 0)),
        out_shape=jax.ShapeDtypeStruct((rows // 2, LANES), jnp.uint32),
        compiler_params=_params(),
        name="pack_table",
    )(t.reshape(rows, LANES))


def _expert_row(tab_ref, row0):
    words = tab_ref[pl.ds(pl.multiple_of(row0, ROWS_PER_EXPERT), ROWS_PER_EXPERT), :]
    return pltpu.bitcast(words, bf16).astype(f32)


def _token_id(ids_ref, e, t):
    return ids_ref.at[pl.ds(e * LANES, LANES)][t]


_FOLD_ORDER = (0, 4, 2, 6, 1, 5, 3, 7)


def _fold8(z):
    sub = lax.broadcasted_iota(i32, z[0].shape, 0)
    lo4 = sub < 4
    lo2 = (sub % 4) < 2
    even = (sub % 2) == 0
    c = [jnp.where(lo4, z[2 * k], z[2 * k + 1]) + pltpu.roll(jnp.where(lo4, z[2 * k + 1], z[2 * k]), 4, 0)
         for k in range(4)]
    e = [jnp.where(lo2, c[2 * k] + pltpu.roll(c[2 * k], 6, 0), c[2 * k + 1] + pltpu.roll(c[2 * k + 1], 2, 0))
         for k in range(2)]
    return jnp.where(even, e[0] + pltpu.roll(e[0], 7, 0), e[1] + pltpu.roll(e[1], 1, 0))


def _with_ids_tile(ids_hbm, bufs, sems, body):
    i = pl.program_id(0)
    n_steps = pl.num_programs(0)

    def tile_copy(tile, slot):
        return pltpu.make_async_copy(ids_hbm.at[tile], bufs[slot], sems.at[slot])

    @pl.when(i == 0)
    def _():
        tile_copy(0, 0).start()

    for slot in range(2):
        @pl.when(i % 2 == slot)
        def _():
            @pl.when(i + 1 < n_steps)
            def _():
                tile_copy(i + 1, 1 - slot).start()
            tile_copy(i, slot).wait()
            body(bufs[slot])


def _ids_scratch():
    return [pltpu.SMEM((N_SEL * LANES,), i32), pltpu.SMEM((N_SEL * LANES,), i32), pltpu.SemaphoreType.DMA((2,))]


def _peer_u_kernel(ids_hbm, x_ref, gate_ref, u_ref, w_ref, a_sc, ids0, ids1, sems):
    tt = x_ref.shape[0]
    lane = lax.broadcasted_iota(i32, (N_SEL, LANES), 1)

    def finish(part, t):
        col = jnp.sum(part, axis=1, keepdims=True)
        a_sc[...] = jnp.where(lane == t, col, a_sc[...])

    def body(ids_ref):
        def token(t, part_prev):
            finish(part_prev, t - 1)
            xt = x_ref[t]
            groups = []
            for g in range(N_SEL // SUBLANES):
                z = [_expert_row(u_ref, _token_id(ids_ref, g * SUBLANES + _FOLD_ORDER[k], t)) * xt
                     for k in range(SUBLANES)]
                groups.append(_fold8(z))
            return jnp.concatenate(groups, axis=0)

        a_sc[...] = jnp.zeros_like(a_sc)
        part = lax.fori_loop(0, tt, token, jnp.zeros((N_SEL, LANES), f32))
        finish(part, tt - 1)
        act = jnp.transpose(a_sc[...])[0:tt, :]
        w_ref[...] = gate_ref[...] * _gelu_tanh(act)

    _with_ids_tile(ids_hbm, (ids0, ids1), sems, body)


def _peer_u(ids, a3, gate, u_tab, tt):
    n_tiles = ids.shape[0]
    n = n_tiles * tt
    assert tt == LANES and ids.shape[1] == N_SEL * LANES
    return pl.pallas_call(
        _peer_u_kernel,
        grid=(n_tiles,),
        in_specs=[pl.BlockSpec(memory_space=pl.ANY),
                  pl.BlockSpec((tt, SUBLANES, LANES), lambda i: (i, 0, 0)),
                  pl.BlockSpec((tt, N_SEL), lambda i: (i, 0)),
                  pl.BlockSpec(u_tab.shape, lambda i: (0, 0), pipeline_mode=pl.Buffered(1))],
        out_specs=pl.BlockSpec((tt, N_SEL), lambda i: (i, 0)),
        out_shape=jax.ShapeDtypeStruct((n, N_SEL), f32),
        scratch_shapes=[pltpu.VMEM((N_SEL, LANES), f32)] + _ids_scratch(),
        compiler_params=_params(),
        name="peer_u",
    )(ids, a3, gate, u_tab)


def _peer_v_kernel(ids_hbm, w_ref, v_ref, h_ref, g_ref, o_ref, ids0, ids1, sems, p_sc):
    tt = o_ref.shape[0]
    n_acc = 4

    def lane_bcast_weights(t):
        return jnp.transpose(jnp.broadcast_to(w_ref[pl.ds(t, 1), :], (LANES, N_SEL)))

    def body(ids_ref):
        def token(t, wb):
            wb_next = lane_bcast_weights(jnp.minimum(t + 1, tt - 1))
            accs = [jnp.zeros((SUBLANES, LANES), f32) for _ in range(n_acc)]
            for e in range(N_SEL):
                wv = jnp.broadcast_to(wb[e:e + 1, :], (SUBLANES, LANES))
                accs[e % n_acc] = accs[e % n_acc] + _expert_row(v_ref, _token_id(ids_ref, e, t)) * wv
            p_sc[t] = (accs[0] + accs[1]) + (accs[2] + accs[3])
            return wb_next

        lax.fori_loop(0, tt, token, lane_bcast_weights(0))
        peer = jnp.concatenate([p_sc[:, c, :] for c in range(SUBLANES)], axis=1)
        o_ref[...] = _rms(h_ref[...] + peer, g_ref[...])

    _with_ids_tile(ids_hbm, (ids0, ids1), sems, body)


def _peer_v(ids, w, v_tab, h2, g, tt):
    n_tiles = ids.shape[0]
    n, d = h2.shape
    assert tt == LANES and ids.shape[1] == N_SEL * LANES and n >= n_tiles * tt and d == SUBLANES * LANES
    return pl.pallas_call(
        _peer_v_kernel,
        grid=(n_tiles,),
        in_specs=[pl.BlockSpec(memory_space=pl.ANY),
                  pl.BlockSpec((tt, N_SEL), lambda i: (i, 0)),
                  pl.BlockSpec(v_tab.shape, lambda i: (0, 0), pipeline_mode=pl.Buffered(1)),
                  pl.BlockSpec((tt, d), lambda i: (i, 0)),
                  pl.BlockSpec((1, d), lambda i: (0, 0))],
        out_specs=pl.BlockSpec((tt, d), lambda i: (i, 0)),
        out_shape=jax.ShapeDtypeStruct((n, d), f32),
        scratch_shapes=_ids_scratch() + [pltpu.VMEM((tt, SUBLANES, LANES), f32)],
        compiler_params=_params(),
        name="peer_v",
    )(ids, w, v_tab, h2, g)


SC_WORKERS = 32
SC_LANES = 16
SC_TILE_SHARE_64THS = 26


def _peer_sc(ids_tok, x3, gate, u_words, v_words):
    from jax.experimental.pallas import tpu_sc as plsc
    n_sc = ids_tok.shape[0]
    words = v_words.shape[1]
    d = 2 * words
    tpw = n_sc // SC_WORKERS
    hrows = N_SEL // 2
    assert n_sc % SC_WORKERS == 0 and tpw % 2 == 0 and words == ROWS_PER_EXPERT * LANES
    half = words // 2
    nvec = half // SC_LANES
    mesh = plsc.VectorSubcoreMesh(core_axis_name="c", subcore_axis_name="s")

    def halves(wd):
        return (lax.bitcast_convert_type(wd << 16, f32), lax.bitcast_convert_type(wd & jnp.int32(-65536), f32))

    def body(u_hbm, v_hbm, ids_hbm, x_hbm, g_hbm, out_hbm,
             idx_a0, idx_b0, g_0, x_0, idx_a1, idx_b1, g_1, x_1, w_b, rows_a, rows_b, part_v, out_v, sem_a, sem_b):
        wid = lax.axis_index("s") * 2 + lax.axis_index("c")
        lane = lax.broadcasted_iota(i32, (SC_LANES,), 0)
        first = wid * tpw
        last = first + tpw - 1
        inputs = ((idx_a0, idx_b0, g_0, x_0), (idx_a1, idx_b1, g_1, x_1))
        row_bufs = ((rows_a, sem_a), (rows_b, sem_b))

        def load_inputs(t, par):
            idx_a, idx_b, g_v, x_v = inputs[par]
            pltpu.sync_copy(ids_hbm.at[t, pl.ds(0, hrows)], idx_a)
            pltpu.sync_copy(ids_hbm.at[t, pl.ds(hrows, hrows)], idx_b)
            pltpu.sync_copy(g_hbm.at[t], g_v)
            pltpu.sync_copy(x_hbm.at[t], x_v)

        def gather(tab_hbm, par, h):
            rows, sem = row_bufs[h]
            return pltpu.make_async_copy(tab_hbm.at[inputs[par][h]], rows, sem)

        def dots(par, h):
            rows, x_v = row_bufs[h][0], inputs[par][3]
            for p in range(2):
                xs = []
                for j in range(nvec):
                    o = p * half + j * SC_LANES
                    r, l0 = o // LANES, o % LANES
                    xs.append((x_v[2 * r, pl.ds(l0, SC_LANES)], x_v[2 * r + 1, pl.ds(l0, SC_LANES)]))

                def dot_pair(e, c, p=p, xs=xs):
                    acc = None
                    for j in range(nvec):
                        lo, hi = halves(rows[e, pl.ds(p * half + j * SC_LANES, SC_LANES)])
                        term = lo * xs[j][0] + hi * xs[j][1]
                        acc = term if acc is None else acc + term
                    row = h * hrows + e
                    part_v[row, :] = acc if p == 0 else part_v[row, :] + acc
                    return c

                lax.fori_loop(0, hrows, dot_pair, 0)

        def weights(par):
            g_v = inputs[par][2]
            for k in range(N_SEL // SC_LANES):
                act = jnp.zeros((SC_LANES,), f32)
                for q in range(SC_LANES):
                    act = jnp.where(lane == q, jnp.sum(part_v[k * SC_LANES + q, :]), act)
                inner = 0.7978845608028654 * (act + 0.044715 * (act * act * act))
                tanh = 1.0 - 2.0 / (jnp.exp(2.0 * inner) + 1.0)
                wk = g_v[pl.ds(k * SC_LANES, SC_LANES)] * (act * (0.5 * (1.0 + tanh)))
                for q in range(SC_LANES):
                    w_b[k * SC_LANES + q, :] = jnp.zeros((SC_LANES,), f32) + jnp.sum(jnp.where(lane == q, wk, 0.0))

        def accumulate(h):
            rows = row_bufs[h][0]
            for p in range(2):
                slots = []
                for j in range(nvec):
                    o = p * half + j * SC_LANES
                    r, l0 = o // LANES, o % LANES
                    slots += [2 * LANES * r + l0, 2 * LANES * r + LANES + l0]

                def acc_pair(e, accs, p=p):
                    wr = w_b[h * hrows + e, :]
                    out = []
                    for j in range(nvec):
                        lo, hi = halves(rows[e, pl.ds(p * half + j * SC_LANES, SC_LANES)])
                        out.append(accs[2 * j] + lo * wr)
                        out.append(accs[2 * j + 1] + hi * wr)
                    return tuple(out)

                init = tuple(jnp.zeros((SC_LANES,), f32) if h == 0 else out_v[pl.ds(o, SC_LANES)] for o in slots)
                accs = lax.fori_loop(0, hrows, acc_pair, init)
                for o, acc in zip(slots, accs):
                    out_v[pl.ds(o, SC_LANES)] = acc

        def token(t, par):
            nxt = jnp.minimum(t + 1, last)
            gather(u_hbm, par, 0).wait()
            dots(par, 0)
            gather(v_hbm, par, 0).start()
            gather(u_hbm, par, 1).wait()
            dots(par, 1)
            gather(v_hbm, par, 1).start()
            weights(par)
            load_inputs(nxt, 1 - par)
            gather(v_hbm, par, 0).wait()
            accumulate(0)
            gather(u_hbm, 1 - par, 0).start()
            gather(v_hbm, par, 1).wait()
            accumulate(1)
            gather(u_hbm, 1 - par, 1).start()
            pltpu.sync_copy(out_v, out_hbm.at[t])

        load_inputs(first, 0)
        gather(u_hbm, 0, 0).start()
        gather(u_hbm, 0, 1).start()

        def token_pair(i, carry):
            token(first + 2 * i, 0)
            token(first + 2 * i + 1, 1)
            return carry

        lax.fori_loop(0, tpw // 2, token_pair, 0)
        gather(u_hbm, 0, 0).wait()
        gather(u_hbm, 0, 1).wait()

    small = [pltpu.VMEM((hrows,), i32), pltpu.VMEM((hrows,), i32), pltpu.VMEM((N_SEL,), f32),
             pltpu.VMEM((SUBLANES, LANES), f32)]
    return pl.kernel(
        body,
        out_type=jax.ShapeDtypeStruct((n_sc, d), f32),
        mesh=mesh,
        scratch_types=small + small + [pltpu.VMEM((N_SEL, SC_LANES), f32), pltpu.VMEM((hrows, words), i32),
                                       pltpu.VMEM((hrows, words), i32), pltpu.VMEM((N_SEL, SC_LANES), f32),
                                       pltpu.VMEM((d,), f32), pltpu.SemaphoreType.DMA, pltpu.SemaphoreType.DMA],
        compiler_params=pltpu.CompilerParams(needs_layout_passes=False),
        name="peer_sc",
    )(u_words, v_words, ids_tok, x3, gate)


def _final_tail_kernel(out_in_ref, h_ref, p_ref, g_ref, o_ref):
    del out_in_ref
    o_ref[...] = _rms(h_ref[...] + p_ref[...], g_ref[...])


def _final_tail(out, h2, peer_tail, g, tm, first_tile):
    n, d = h2.shape
    tail_tiles = peer_tail.shape[0] // tm
    return pl.pallas_call(
        _final_tail_kernel,
        grid=(tail_tiles,),
        in_specs=[pl.BlockSpec(memory_space=pl.ANY),
                  pl.BlockSpec((tm, d), lambda i: (i + first_tile, 0)),
                  pl.BlockSpec((tm, d), lambda i: (i, 0)),
                  pl.BlockSpec((1, d), lambda i: (0, 0))],
        out_specs=pl.BlockSpec((tm, d), lambda i: (i + first_tile, 0)),
        out_shape=jax.ShapeDtypeStruct((n, d), f32),
        input_output_aliases={0: 0},
        compiler_params=_params(),
        name="final_tail",
    )(out, h2, peer_tail, g)


def _rope_tables(s):
    half = HEAD_DIM // 2
    inv = ROPE_THETA ** (-jnp.arange(half, dtype=f32) / half)
    ang = jnp.arange(s, dtype=f32)[:, None] * inv[None, :]
    cos, sin = jnp.cos(ang), jnp.sin(ang)
    reps = MOBA_GROUP_LANES // HEAD_DIM
    cos_t = jnp.tile(jnp.concatenate([cos, cos], axis=1), (1, reps))
    sin_t = jnp.tile(jnp.concatenate([-sin, sin], axis=1), (1, reps))
    return cos_t, sin_t


def _row_tile(s, want):
    t = min(want, s)
    while s % t:
        t //= 2
    assert t % SUBLANES == 0
    return t


def kernel(x, mem, g_mix, w_in, w_conv, w_conv_out, w_attn_out, w_merge, g_xattn, g_mem,
           w_xq, w_xkv, w_xo, g_ffn, w_pq, peer_sub_keys, peer_u, peer_v, g_final):
    b, s, d = x.shape
    n = b * s
    n_mem = mem.shape[1]
    assert w_in.shape[0] == 1, "single-layer trunk only"
    assert d == SUBLANES * LANES and s % MOBA_BLOCK == 0
    n_exp = peer_u.shape[1]
    cos_t, sin_t = _rope_tables(s)
    tm = _row_tile(s, 512)
    h = x.reshape(n, d)
    mem2 = mem.reshape(b * n_mem, d)
    l = 0
    proj = _inproj(h, g_mix[l][None, :], w_in[l].astype(bf16), tm)
    o = _moba(proj, cos_t, sin_t, b, s)
    h1 = _mix(proj, o, h, w_conv[l], w_conv_out[l].astype(bf16), w_attn_out[l].astype(bf16),
              w_merge[l].astype(bf16), s, tm)
    kv = _memkv(mem2, g_mem[l][None, :], w_xkv[l].astype(bf16), n_mem)
    h2 = _xattn(h1, g_xattn[l][None, :], kv, w_xq[l].astype(bf16), w_xo[l].astype(bf16), s, n_mem, tm)
    u_tab = _pack_table(peer_u[l])
    v_tab = _pack_table(peer_v[l])
    tt = _row_tile(s, LANES)
    tsel = _row_tile(s, 256)
    n_tiles = n // tt
    per_sel = tsel // tt
    sc_tiles = ((n_tiles * SC_TILE_SHARE_64THS) // 64) // per_sel * per_sel
    tc_tiles = n_tiles - sc_tiles
    sel_args = (h2, g_ffn[l][None, :], w_pq[l].astype(bf16), peer_sub_keys[l].astype(bf16), tsel)
    if sc_tiles:
        a3_s, ids_s, gate_s = _select(*sel_args, tc_tiles // per_sel, sc_tiles // per_sel)
        as_rows = lambda tab: lax.bitcast_convert_type(tab, i32).reshape(n_exp, ROWS_PER_EXPERT * LANES)
        ids_tok = (ids_s // ROWS_PER_EXPERT).transpose(0, 2, 1).reshape(sc_tiles * tt, N_SEL)
        peer_sc = _peer_sc(ids_tok, a3_s, gate_s, as_rows(u_tab), as_rows(v_tab))
    a3, ids, gate = _select(*sel_args, 0, tc_tiles // per_sel)
    ids = ids.reshape(tc_tiles, N_SEL * LANES)
    w = _peer_u(ids, a3, gate, u_tab, tt)
    out = _peer_v(ids, w, v_tab, h2, g_final[None, :], tt)
    if sc_tiles:
        out = _final_tail(out, h2, peer_sc, g_final[None, :], tt, tc_tiles)
    return out.reshape(b, s, d)
```

```python
import functools

import jax
import jax.numpy as jnp
from jax import lax
from jax.experimental import pallas as pl
from jax.experimental.pallas import tpu as pltpu

f32 = jnp.float32
bf16 = jnp.bfloat16
i32 = jnp.int32

EPS = 1e-6
MASK_VALUE = -1e30
ROPE_THETA = 10000.0

XATTN_HEADS = 4
ATTN_HEADS = 8
HEAD_DIM = 64
MOBA_BLOCK = 256
MOBA_TOPK = 3
MOBA_GROUP_LANES = 256
PEER_HEADS = 8
PEER_N_KEYS = 128
PEER_HALF = 128
PEER_TOPK = 16
N_SEL = PEER_HEADS * PEER_TOPK
ROWS_PER_EXPERT = 4

LANES = 128
SUBLANES = 8
VMEM_LIMIT = 56 * 1024 * 1024

_NT = (((1,), (1,)), ((), ()))


def _params(n_axes=1, vmem=VMEM_LIMIT):
    return pltpu.CompilerParams(dimension_semantics=("arbitrary",) * n_axes, vmem_limit_bytes=vmem)


def _rms(x, g):
    return x * lax.rsqrt(jnp.mean(x * x, axis=-1, keepdims=True) + EPS) * g


def _inproj_kernel(x_ref, g_ref, w_ref, o_ref, *, chunk):
    a = _rms(x_ref[...], g_ref[...]).astype(bf16)
    for c in range(o_ref.shape[1] // chunk):
        o_ref[:, c * chunk:(c + 1) * chunk] = jnp.dot(
            a, w_ref[:, c * chunk:(c + 1) * chunk], preferred_element_type=f32)


def _inproj(x2, g, w, tm):
    n, d = x2.shape
    width = w.shape[1]
    return pl.pallas_call(
        functools.partial(_inproj_kernel, chunk=width // 4),
        grid=(n // tm,),
        in_specs=[pl.BlockSpec((tm, d), lambda i: (i, 0)),
                  pl.BlockSpec((1, d), lambda i: (0, 0)),
                  pl.BlockSpec((d, width), lambda i: (0, 0), pipeline_mode=pl.Buffered(1))],
        out_specs=pl.BlockSpec((tm, width), lambda i: (i, 0)),
        out_shape=jax.ShapeDtypeStruct((n, width), f32),
        compiler_params=_params(),
        name="inproj",
    )(x2, g, w)


def _moba_kernel(q_ref, k_ref, v_ref, cos_ref, sin_ref, o_ref,
                 qb_sc, kb_sc, vt_sc, bias_sc, ot_sc, *, nb, topk):
    blk = MOBA_BLOCK
    s, gw = q_ref.shape
    lane = lax.broadcasted_iota(i32, (s, gw), 1)
    first_half = (lane % HEAD_DIM) < (HEAD_DIM // 2)
    cos = cos_ref[...]
    sin = sin_ref[...]

    def rope(t):
        partner = jnp.where(first_half, pltpu.roll(t, gw - HEAD_DIM // 2, 1),
                            pltpu.roll(t, HEAD_DIM // 2, 1))
        return t * cos + partner * sin

    q2 = rope(q_ref[...])
    k2 = rope(k_ref[...])
    v2 = v_ref[...]
    scale = HEAD_DIM ** -0.5
    blk_of_q = lax.broadcasted_iota(i32, (nb, s), 1) // blk
    m_iota = lax.broadcasted_iota(i32, (nb, s), 0)
    past = m_iota < blk_of_q
    kpos = lax.broadcasted_iota(i32, (blk, blk), 0)
    qpos = lax.broadcasted_iota(i32, (blk, blk), 1)
    causal = kpos <= qpos

    heads = range(gw // HEAD_DIM)
    for hh in heads:
        qh = q2[:, hh * HEAD_DIM:(hh + 1) * HEAD_DIM]
        kh = k2[:, hh * HEAD_DIM:(hh + 1) * HEAD_DIM]
        vh = v2[:, hh * HEAD_DIM:(hh + 1) * HEAD_DIM]
        kbar = jnp.mean(kh.reshape(nb, blk, HEAD_DIM), axis=1)
        gt = lax.dot_general(kbar, qh, _NT, precision=lax.Precision.HIGHEST,
                             preferred_element_type=f32)
        rows = []
        for n in range(nb):
            gn = gt[n:n + 1, :]
            beats = past & ((gt > gn) | ((gt == gn) & (m_iota < n)))
            cnt = jnp.sum(beats.astype(i32), axis=0, keepdims=True)
            sel = (blk_of_q[0:1, :] > n) & (cnt < topk)
            rows.append(jnp.where(sel, 0.0, MASK_VALUE).astype(f32))
        bias = jnp.concatenate(rows, axis=0)
        vt = jnp.transpose(vh).astype(bf16)
        for i in range(nb):
            bias_sc[hh, i] = bias[:, i * blk:(i + 1) * blk]
            vt_sc[hh, i] = vt[:, i * blk:(i + 1) * blk]
        qb_sc[hh] = qh.astype(bf16)
        kb_sc[hh] = kh.astype(bf16)

    def q_tile(i, carry):
        q0 = pl.multiple_of(i * blk, blk)
        diag = [lax.dot_general(kb_sc[hh, pl.ds(q0, blk), :], qb_sc[hh, pl.ds(q0, blk), :], _NT,
                                preferred_element_type=f32) for hh in heads]
        probs, stats = [], []
        for hh in heads:
            sd = jnp.where(causal, diag[hh] * scale, MASK_VALUE)
            m0 = jnp.max(sd, axis=0, keepdims=True)
            p0 = jnp.exp(sd - m0)
            probs.append(p0.astype(bf16))
            stats.append((m0, jnp.sum(p0, axis=0, keepdims=True)))
        init = [stats[hh] + (jnp.dot(vt_sc[hh, i], probs[hh], preferred_element_type=f32),)
                for hh in heads]

        def kv_step(j, st):
            k0 = pl.multiple_of(j * blk, blk)
            scores = [lax.dot_general(kb_sc[hh, pl.ds(k0, blk), :], qb_sc[hh, pl.ds(q0, blk), :], _NT,
                                      preferred_element_type=f32) for hh in heads]
            probs, stats = [], []
            for hh in heads:
                m, l, _ = st[hh]
                sj = scores[hh] * scale + bias_sc[hh, i, pl.ds(j, 1), :]
                m_new = jnp.maximum(m, jnp.max(sj, axis=0, keepdims=True))
                alpha = jnp.exp(m - m_new)
                p = jnp.exp(sj - m_new)
                probs.append(p.astype(bf16))
                stats.append((m_new, alpha * l + jnp.sum(p, axis=0, keepdims=True), alpha))
            out = []
            for hh in heads:
                m_new, l, alpha = stats[hh]
                acc = alpha * st[hh][2] + jnp.dot(vt_sc[hh, j], probs[hh], preferred_element_type=f32)
                out.append((m_new, l, acc))
            return tuple(out)

        final = lax.fori_loop(0, i, kv_step, tuple(init))
        for hh in heads:
            _, l, acc = final[hh]
            ot_sc[hh, i] = acc * (1.0 / l)
        return carry

    lax.fori_loop(0, nb, q_tile, 0)
    for hh in heads:
        for i in range(nb):
            o_ref[i * blk:(i + 1) * blk, hh * HEAD_DIM:(hh + 1) * HEAD_DIM] = jnp.transpose(ot_sc[hh, i])


def _moba(proj, cos_t, sin_t, b, s):
    n = b * s
    nb = s // MOBA_BLOCK
    topk = min(MOBA_TOPK, nb - 1)
    aw = ATTN_HEADS * HEAD_DIM
    gw = MOBA_GROUP_LANES
    hp = aw // gw
    nh = gw // HEAD_DIM
    qc, kc, vc = 3 * aw // gw, 4 * aw // gw, 5 * aw // gw
    return pl.pallas_call(
        functools.partial(_moba_kernel, nb=nb, topk=topk),
        grid=(b, hp),
        in_specs=[pl.BlockSpec((s, gw), lambda bi, h: (bi, qc + h)),
                  pl.BlockSpec((s, gw), lambda bi, h: (bi, kc + h)),
                  pl.BlockSpec((s, gw), lambda bi, h: (bi, vc + h)),
                  pl.BlockSpec((s, gw), lambda bi, h: (0, 0)),
                  pl.BlockSpec((s, gw), lambda bi, h: (0, 0))],
        out_specs=pl.BlockSpec((s, gw), lambda bi, h: (bi, h)),
        out_shape=jax.ShapeDtypeStruct((n, aw), f32),
        scratch_shapes=[pltpu.VMEM((nh, s, HEAD_DIM), bf16),
                        pltpu.VMEM((nh, s, HEAD_DIM), bf16),
                        pltpu.VMEM((nh, nb, HEAD_DIM, MOBA_BLOCK), bf16),
                        pltpu.VMEM((nh, nb, nb, MOBA_BLOCK), f32),
                        pltpu.VMEM((nh, nb, HEAD_DIM, MOBA_BLOCK), f32)],
        compiler_params=_params(2),
        name="moba",
    )(proj, proj, proj, cos_t, sin_t)


def _mix_kernel(xin_ref, bg_ref, cg_ref, pxin_ref, pcg_ref, gc_ref, ga_ref, o_ref, x_ref,
                wc_ref, wco_ref, wao_ref, wm_ref, h_ref, *, tiles_per_seq):
    tm = xin_ref.shape[0]
    i = pl.program_id(0)
    keep_prev = jnp.where(i % tiles_per_seq == 0, 0.0, 1.0).astype(f32)
    u = cg_ref[...] * xin_ref[...]
    pu = pcg_ref[...] * pxin_ref[...] * keep_prev
    rows = lax.broadcasted_iota(i32, u.shape, 0)
    u1 = jnp.where(rows == 0, pu[SUBLANES - 1:SUBLANES, :], pltpu.roll(u, 1, 0))
    u2 = jnp.where(rows == 0, pu[SUBLANES - 2:SUBLANES - 1, :],
                   jnp.where(rows == 1, pu[SUBLANES - 1:SUBLANES, :], pltpu.roll(u, 2, 0)))
    y = wc_ref[0:1, :] * u2 + wc_ref[1:2, :] * u1 + wc_ref[2:3, :] * u
    y_conv = jnp.dot((bg_ref[...] * y).astype(bf16), wco_ref[...], preferred_element_type=f32)
    y_attn = jnp.dot(o_ref[...].astype(bf16), wao_ref[...], preferred_element_type=f32)
    merged = jax.nn.sigmoid(gc_ref[...]) * y_conv + jax.nn.sigmoid(ga_ref[...]) * y_attn
    h_ref[...] = x_ref[...] + jnp.dot(merged.astype(bf16), wm_ref[...], preferred_element_type=f32)


def _mix(proj, o, x2, w_conv, wco, wao, wm, s, tm):
    n, d = x2.shape
    cw = wco.shape[0]
    aw = wao.shape[0]
    rb = tm // SUBLANES
    prev = lambda col: (lambda i: (jnp.maximum(i * rb - 1, 0), col))
    gcol = 3 * cw + 3 * aw
    return pl.pallas_call(
        functools.partial(_mix_kernel, tiles_per_seq=s // tm),
        grid=(n // tm,),
        in_specs=[pl.BlockSpec((tm, cw), lambda i: (i, 0)),
                  pl.BlockSpec((tm, cw), lambda i: (i, 1)),
                  pl.BlockSpec((tm, cw), lambda i: (i, 2)),
                  pl.BlockSpec((SUBLANES, cw), prev(0)),
                  pl.BlockSpec((SUBLANES, cw), prev(2)),
                  pl.BlockSpec((tm, d), lambda i: (i, gcol // d)),
                  pl.BlockSpec((tm, d), lambda i: (i, gcol // d + 1)),
                  pl.BlockSpec((tm, aw), lambda i: (i, 0)),
                  pl.BlockSpec((tm, d), lambda i: (i, 0)),
                  pl.BlockSpec(w_conv.shape, lambda i: (0, 0)),
                  pl.BlockSpec(wco.shape, lambda i: (0, 0)),
                  pl.BlockSpec(wao.shape, lambda i: (0, 0)),
                  pl.BlockSpec(wm.shape, lambda i: (0, 0))],
        out_specs=pl.BlockSpec((tm, d), lambda i: (i, 0)),
        out_shape=jax.ShapeDtypeStruct((n, d), f32),
        compiler_params=_params(),
        name="mix",
    )(proj, proj, proj, proj, proj, proj, proj, o, x2, w_conv, wco, wao, wm)


def _memkv_kernel(m_ref, g_ref, w_ref, o_ref):
    a = _rms(m_ref[...], g_ref[...]).astype(bf16)
    o_ref[...] = jnp.dot(a, w_ref[...], preferred_element_type=f32).astype(bf16)


def _memkv(mem2, g, w, n_mem):
    nm, d = mem2.shape
    return pl.pallas_call(
        _memkv_kernel,
        grid=(nm // n_mem,),
        in_specs=[pl.BlockSpec((n_mem, d), lambda i: (i, 0)),
                  pl.BlockSpec((1, d), lambda i: (0, 0)),
                  pl.BlockSpec(w.shape, lambda i: (0, 0))],
        out_specs=pl.BlockSpec((n_mem, w.shape[1]), lambda i: (i, 0)),
        out_shape=jax.ShapeDtypeStruct((nm, w.shape[1]), bf16),
        compiler_params=_params(),
        name="memkv",
    )(mem2, g, w)


def _xattn_kernel(h_ref, g_ref, kv_ref, wq_ref, wo_ref, o_ref):
    h = h_ref[...]
    d = h.shape[1]
    hd = d // XATTN_HEADS
    q = jnp.dot(_rms(h, g_ref[...]).astype(bf16), wq_ref[...], preferred_element_type=f32)
    outs = []
    for hh in range(XATTN_HEADS):
        qh = q[:, hh * hd:(hh + 1) * hd].astype(bf16)
        kh = kv_ref[:, hh * hd:(hh + 1) * hd]
        vh = kv_ref[:, d + hh * hd:d + (hh + 1) * hd]
        sc = lax.dot_general(qh, kh, _NT, preferred_element_type=f32) * (hd ** -0.5)
        sc = sc - jnp.max(sc, axis=-1, keepdims=True)
        p = jnp.exp(sc)
        p = p / jnp.sum(p, axis=-1, keepdims=True)
        outs.append(jnp.dot(p.astype(bf16), vh, preferred_element_type=f32).astype(bf16))
    o = jnp.concatenate(outs, axis=-1)
    o_ref[...] = h + jnp.dot(o, wo_ref[...], preferred_element_type=f32)


def _xattn(h1, g, kv, wq, wo, s, n_mem, tm):
    n, d = h1.shape
    tps = s // tm
    return pl.pallas_call(
        _xattn_kernel,
        grid=(n // tm,),
        in_specs=[pl.BlockSpec((tm, d), lambda i: (i, 0)),
                  pl.BlockSpec((1, d), lambda i: (0, 0)),
                  pl.BlockSpec((n_mem, 2 * d), lambda i: (i // tps, 0)),
                  pl.BlockSpec(wq.shape, lambda i: (0, 0)),
                  pl.BlockSpec(wo.shape, lambda i: (0, 0))],
        out_specs=pl.BlockSpec((tm, d), lambda i: (i, 0)),
        out_shape=jax.ShapeDtypeStruct((n, d), f32),
        compiler_params=_params(),
        name="xattn",
    )(h1, g, kv, wq, wo)


def _topk_cols(sc, k, payload=None):
    r = sc.shape[0]
    ridx = lax.broadcasted_iota(i32, sc.shape, 0).astype(f32)
    vals, picks = [], []
    for _ in range(k):
        m = jnp.max(sc, axis=0, keepdims=True)
        im = jnp.min(jnp.where(sc == m, ridx, float(r)), axis=0, keepdims=True)
        hit = ridx == im
        vals.append(m)
        picks.append(im if payload is None else jnp.max(jnp.where(hit, payload, -1.0), axis=0, keepdims=True))
        sc = jnp.where(hit, -jnp.inf, sc)
    return jnp.concatenate(vals, axis=0), jnp.concatenate(picks, axis=0)


def _pair_candidates(k):
    return [(a, b) for a in range(k) for b in range(k) if (a + 1) * (b + 1) <= k]


def _select_kernel(h_ref, g_ref, wq_ref, keys_ref, a_ref, ids_ref, gate_ref):
    k = PEER_TOPK
    a = _rms(h_ref[...], g_ref[...])
    for c in range(a_ref.shape[1]):
        a_ref[:, c, :] = a[:, c * LANES:(c + 1) * LANES]
    q = jnp.dot(a.astype(bf16), wq_ref[...], preferred_element_type=f32)
    tm = q.shape[0]
    pairs = _pair_candidates(k)
    n_pad = -len(pairs) % SUBLANES
    id_rows, gate_rows = [], []
    for h in range(PEER_HEADS):
        tops = []
        for p in range(2):
            c0 = (h * 2 + p) * PEER_HALF
            qhp = q[:, c0:c0 + PEER_HALF].astype(bf16)
            st = lax.dot_general(keys_ref[h, p], qhp, _NT, preferred_element_type=f32)
            tops.append(_topk_cols(st, k))
        (v0, i0), (v1, i1) = tops
        i0 = i0 * float(PEER_N_KEYS)
        cand, cand_id = [], []
        for a_ in range(k):
            nb = sum(1 for (pa, _) in pairs if pa == a_)
            cand.append(v0[a_:a_ + 1, :] + v1[0:nb, :])
            cand_id.append(i0[a_:a_ + 1, :] + i1[0:nb, :])
        if n_pad:
            cand.append(jnp.full((n_pad, tm), -jnp.inf, f32))
            cand_id.append(jnp.zeros((n_pad, tm), f32))
        top_s, top_id = _topk_cols(jnp.concatenate(cand, axis=0), k, jnp.concatenate(cand_id, axis=0))
        id_rows.append(top_id.astype(i32))
        ex = jnp.exp(top_s - top_s[0:1, :])
        gate_rows.append(ex / jnp.sum(ex, axis=0, keepdims=True))
    ids_t = jnp.concatenate(id_rows, axis=0) * ROWS_PER_EXPERT
    for j in range(ids_ref.shape[0]):
        ids_ref[j] = ids_t[:, j * LANES:(j + 1) * LANES]
    gate_ref[...] = jnp.transpose(jnp.concatenate(gate_rows, axis=0))


def _select(h2, g, wq, keys, tm, first_block, n_blocks):
    d = h2.shape[1]
    n = n_blocks * tm
    return pl.pallas_call(
        _select_kernel,
        grid=(n_blocks,),
        in_specs=[pl.BlockSpec((tm, d), lambda i: (i + first_block, 0)),
                  pl.BlockSpec((1, d), lambda i: (0, 0)),
                  pl.BlockSpec(wq.shape, lambda i: (0, 0)),
                  pl.BlockSpec(keys.shape, lambda i: (0, 0, 0, 0))],
        out_specs=[pl.BlockSpec((tm, d // LANES, LANES), lambda i: (i, 0, 0)),
                   pl.BlockSpec((tm // LANES, N_SEL, LANES), lambda i: (i, 0, 0)),
                   pl.BlockSpec((tm, N_SEL), lambda i: (i, 0))],
        out_shape=[jax.ShapeDtypeStruct((n, d // LANES, LANES), f32),
                   jax.ShapeDtypeStruct((n // LANES, N_SEL, LANES), i32),
                   jax.ShapeDtypeStruct((n, N_SEL), f32)],
        compiler_params=_params(),
        name="peer_select",
    )(h2, g, wq, keys)


def _gelu_tanh(x):
    cdf = 0.5 * (1.0 + jnp.tanh(0.7978845608028654 * (x + 0.044715 * (x * x * x))))
    return x * cdf


def _pack_kernel(t_ref, o_ref):
    o_ref[...] = pltpu.bitcast(t_ref[...].astype(bf16), jnp.uint32)


def _pack_table(t):
    e, d = t.shape
    rows = e * d // LANES
    blk = min(rows, 8192)
    assert rows % blk == 0 and d == 2 * ROWS_PER_EXPERT * LANES
    return pl.pallas_call(
        _pack_kernel,
        grid=(rows // blk,),
        in_specs=[pl.BlockSpec((blk, LANES), lambda i: (i, 0))],
        out_specs=pl.BlockSpec((blk // 2, LANES), lambda i: (i, 0)),
        out_shape=jax.ShapeDtypeStruct((rows // 2, LANES), jnp.uint32),
        compiler_params=_params(),
        name="pack_table",
    )(t.reshape(rows, LANES))


def _expert_row(tab_ref, row0):
    words = tab_ref[pl.ds(pl.multiple_of(row0, ROWS_PER_EXPERT), ROWS_PER_EXPERT), :]
    return pltpu.bitcast(words, bf16).astype(f32)


def _token_id(ids_ref, e, t):
    return ids_ref.at[pl.ds(e * LANES, LANES)][t]


_FOLD_ORDER = (0, 4, 2, 6, 1, 5, 3, 7)


def _fold8(z):
    sub = lax.broadcasted_iota(i32, z[0].shape, 0)
    lo4 = sub < 4
    lo2 = (sub % 4) < 2
    even = (sub % 2) == 0
    c = [jnp.where(lo4, z[2 * k], z[2 * k + 1]) + pltpu.roll(jnp.where(lo4, z[2 * k + 1], z[2 * k]), 4, 0)
         for k in range(4)]
    e = [jnp.where(lo2, c[2 * k] + pltpu.roll(c[2 * k], 6, 0), c[2 * k + 1] + pltpu.roll(c[2 * k + 1], 2, 0))
         for k in range(2)]
    return jnp.where(even, e[0] + pltpu.roll(e[0], 7, 0), e[1] + pltpu.roll(e[1], 1, 0))


def _with_ids_tile(ids_hbm, bufs, sems, body):
    i = pl.program_id(0)
    n_steps = pl.num_programs(0)

    def tile_copy(tile, slot):
        return pltpu.make_async_copy(ids_hbm.at[tile], bufs[slot], sems.at[slot])

    @pl.when(i == 0)
    def _():
        tile_copy(0, 0).start()

    for slot in range(2):
        @pl.when(i % 2 == slot)
        def _():
            @pl.when(i + 1 < n_steps)
            def _():
                tile_copy(i + 1, 1 - slot).start()
            tile_copy(i, slot).wait()
            body(bufs[slot])


def _ids_scratch():
    return [pltpu.SMEM((N_SEL * LANES,), i32), pltpu.SMEM((N_SEL * LANES,), i32), pltpu.SemaphoreType.DMA((2,))]


def _peer_u_kernel(ids_hbm, x_ref, gate_ref, u_ref, w_ref, a_sc, ids0, ids1, sems):
    tt = x_ref.shape[0]
    lane = lax.broadcasted_iota(i32, (N_SEL, LANES), 1)

    def finish(part, t):
        col = jnp.sum(part, axis=1, keepdims=True)
        a_sc[...] = jnp.where(lane == t, col, a_sc[...])

    def body(ids_ref):
        def token(t, part_prev):
            finish(part_prev, t - 1)
            xt = x_ref[t]
            groups = []
            for g in range(N_SEL // SUBLANES):
                z = [_expert_row(u_ref, _token_id(ids_ref, g * SUBLANES + _FOLD_ORDER[k], t)) * xt
                     for k in range(SUBLANES)]
                groups.append(_fold8(z))
            return jnp.concatenate(groups, axis=0)

        a_sc[...] = jnp.zeros_like(a_sc)
        part = lax.fori_loop(0, tt, token, jnp.zeros((N_SEL, LANES), f32))
        finish(part, tt - 1)
        act = jnp.transpose(a_sc[...])[0:tt, :]
        w_ref[...] = gate_ref[...] * _gelu_tanh(act)

    _with_ids_tile(ids_hbm, (ids0, ids1), sems, body)


def _peer_u(ids, a3, gate, u_tab, tt):
    n_tiles = ids.shape[0]
    n = n_tiles * tt
    assert tt == LANES and ids.shape[1] == N_SEL * LANES
    return pl.pallas_call(
        _peer_u_kernel,
        grid=(n_tiles,),
        in_specs=[pl.BlockSpec(memory_space=pl.ANY),
                  pl.BlockSpec((tt, SUBLANES, LANES), lambda i: (i, 0, 0)),
                  pl.BlockSpec((tt, N_SEL), lambda i: (i, 0)),
                  pl.BlockSpec(u_tab.shape, lambda i: (0, 0), pipeline_mode=pl.Buffered(1))],
        out_specs=pl.BlockSpec((tt, N_SEL), lambda i: (i, 0)),
        out_shape=jax.ShapeDtypeStruct((n, N_SEL), f32),
        scratch_shapes=[pltpu.VMEM((N_SEL, LANES), f32)] + _ids_scratch(),
        compiler_params=_params(),
        name="peer_u",
    )(ids, a3, gate, u_tab)


def _peer_v_kernel(ids_hbm, w_ref, v_ref, h_ref, g_ref, o_ref, ids0, ids1, sems, p_sc):
    tt = o_ref.shape[0]
    n_acc = 4

    def lane_bcast_weights(t):
        return jnp.transpose(jnp.broadcast_to(w_ref[pl.ds(t, 1), :], (LANES, N_SEL)))

    def body(ids_ref):
        def token(t, wb):
            wb_next = lane_bcast_weights(jnp.minimum(t + 1, tt - 1))
            accs = [jnp.zeros((SUBLANES, LANES), f32) for _ in range(n_acc)]
            for e in range(N_SEL):
                wv = jnp.broadcast_to(wb[e:e + 1, :], (SUBLANES, LANES))
                accs[e % n_acc] = accs[e % n_acc] + _expert_row(v_ref, _token_id(ids_ref, e, t)) * wv
            p_sc[t] = (accs[0] + accs[1]) + (accs[2] + accs[3])
            return wb_next

        lax.fori_loop(0, tt, token, lane_bcast_weights(0))
        peer = jnp.concatenate([p_sc[:, c, :] for c in range(SUBLANES)], axis=1)
        o_ref[...] = _rms(h_ref[...] + peer, g_ref[...])

    _with_ids_tile(ids_hbm, (ids0, ids1), sems, body)


def _peer_v(ids, w, v_tab, h2, g, tt, n_out):
    n_tiles = ids.shape[0]
    n, d = h2.shape
    assert tt == LANES and ids.shape[1] == N_SEL * LANES and n == n_tiles * tt <= n_out and d == SUBLANES * LANES
    return pl.pallas_call(
        _peer_v_kernel,
        grid=(n_tiles,),
        in_specs=[pl.BlockSpec(memory_space=pl.ANY),
                  pl.BlockSpec((tt, N_SEL), lambda i: (i, 0)),
                  pl.BlockSpec(v_tab.shape, lambda i: (0, 0), pipeline_mode=pl.Buffered(1)),
                  pl.BlockSpec((tt, d), lambda i: (i, 0)),
                  pl.BlockSpec((1, d), lambda i: (0, 0))],
        out_specs=pl.BlockSpec((tt, d), lambda i: (i, 0)),
        out_shape=jax.ShapeDtypeStruct((n_out, d), f32),
        scratch_shapes=_ids_scratch() + [pltpu.VMEM((tt, SUBLANES, LANES), f32)],
        compiler_params=_params(),
        name="peer_v",
    )(ids, w, v_tab, h2, g)


SC_WORKERS = 32
SC_LANES = 16
SC_BATCH_SHARE_16THS = 7


def _peer_sc(ids_tok, x3, gate, u_words, v_words):
    from jax.experimental.pallas import tpu_sc as plsc
    n_sc = ids_tok.shape[0]
    words = v_words.shape[1]
    d = 2 * words
    tpw = n_sc // SC_WORKERS
    hrows = N_SEL // 2
    assert n_sc % SC_WORKERS == 0 and tpw % 2 == 0 and words == ROWS_PER_EXPERT * LANES
    half = words // 2
    nvec = half // SC_LANES
    mesh = plsc.VectorSubcoreMesh(core_axis_name="c", subcore_axis_name="s")

    def halves(wd):
        return (lax.bitcast_convert_type(wd << 16, f32), lax.bitcast_convert_type(wd & jnp.int32(-65536), f32))

    def body(u_hbm, v_hbm, ids_hbm, x_hbm, g_hbm, out_hbm,
             idx_a0, idx_b0, g_0, x_0, idx_a1, idx_b1, g_1, x_1, w_v, rows_a, rows_b, part_v, out_v, sem_a, sem_b, sem_s):
        wid = lax.axis_index("s") * 2 + lax.axis_index("c")
        lane = lax.broadcasted_iota(i32, (SC_LANES,), 0)
        first = wid * tpw
        last = first + tpw - 1
        inputs = ((idx_a0, idx_b0, g_0, x_0), (idx_a1, idx_b1, g_1, x_1))
        row_bufs = ((rows_a, sem_a), (rows_b, sem_b))

        def input_copies(t, par):
            idx_a, idx_b, g_v, x_v = inputs[par]
            return (pltpu.make_async_copy(ids_hbm.at[t, pl.ds(0, hrows)], idx_a, sem_s),
                    pltpu.make_async_copy(ids_hbm.at[t, pl.ds(hrows, hrows)], idx_b, sem_s),
                    pltpu.make_async_copy(g_hbm.at[t], g_v, sem_s),
                    pltpu.make_async_copy(x_hbm.at[t], x_v, sem_s))

        def gather(tab_hbm, par, h):
            rows, sem = row_bufs[h]
            return pltpu.make_async_copy(tab_hbm.at[inputs[par][h]], rows, sem)

        def dots(par, h):
            rows, x_v = row_bufs[h][0], inputs[par][3]
            for p in range(2):
                xs = []
                for j in range(nvec):
                    o = p * half + j * SC_LANES
                    r, l0 = o // LANES, o % LANES
                    xs.append((x_v[2 * r, pl.ds(l0, SC_LANES)], x_v[2 * r + 1, pl.ds(l0, SC_LANES)]))

                def dot_pair(e, c, p=p, xs=xs):
                    acc = None
                    for j in range(nvec):
                        lo, hi = halves(rows[e, pl.ds(p * half + j * SC_LANES, SC_LANES)])
                        term = lo * xs[j][0] + hi * xs[j][1]
                        acc = term if acc is None else acc + term
                    row = h * hrows + e
                    part_v[row, :] = acc if p == 0 else part_v[row, :] + acc
                    return c

                lax.fori_loop(0, hrows, dot_pair, 0)

        def weights(par):
            g_v = inputs[par][2]
            for k in range(N_SEL // SC_LANES):
                act = jnp.zeros((SC_LANES,), f32)
                for q in range(SC_LANES):
                    act = jnp.where(lane == q, jnp.sum(part_v[k * SC_LANES + q, :]), act)
                inner = 0.7978845608028654 * (act + 0.044715 * (act * act * act))
                tanh = 1.0 - 2.0 / (jnp.exp(2.0 * inner) + 1.0)
                w_v[pl.ds(k * SC_LANES, SC_LANES)] = g_v[pl.ds(k * SC_LANES, SC_LANES)] * (act * (0.5 * (1.0 + tanh)))

        def accumulate(h):
            rows = row_bufs[h][0]
            for p in range(2):
                slots = []
                for j in range(nvec):
                    o = p * half + j * SC_LANES
                    r, l0 = o // LANES, o % LANES
                    slots += [2 * LANES * r + l0, 2 * LANES * r + LANES + l0]

                def acc_pair(e, accs, p=p):
                    row = h * hrows + e
                    wvec = w_v[pl.ds(pl.multiple_of((row // SC_LANES) * SC_LANES, SC_LANES), SC_LANES)]
                    wr = jnp.sum(jnp.where(lane == row % SC_LANES, wvec, 0.0))
                    out = []
                    for j in range(nvec):
                        lo, hi = halves(rows[e, pl.ds(p * half + j * SC_LANES, SC_LANES)])
                        out.append(accs[2 * j] + lo * wr)
                        out.append(accs[2 * j + 1] + hi * wr)
                    return tuple(out)

                init = tuple(jnp.zeros((SC_LANES,), f32) if h == 0 else out_v[pl.ds(o, SC_LANES)] for o in slots)
                accs = lax.fori_loop(0, hrows, acc_pair, init)
                for o, acc in zip(slots, accs):
                    out_v[pl.ds(o, SC_LANES)] = acc

        def token(t, par):
            nxt = jnp.minimum(t + 1, last)
            for c in input_copies(nxt, 1 - par):
                c.start()
            gather(u_hbm, par, 0).wait()
            dots(par, 0)
            gather(v_hbm, par, 0).start()
            gather(u_hbm, par, 1).wait()
            dots(par, 1)
            gather(v_hbm, par, 1).start()
            weights(par)
            for c in input_copies(nxt, 1 - par):
                c.wait()
            gather(v_hbm, par, 0).wait()
            accumulate(0)
            gather(u_hbm, 1 - par, 0).start()
            gather(v_hbm, par, 1).wait()
            accumulate(1)
            gather(u_hbm, 1 - par, 1).start()
            pltpu.sync_copy(out_v, out_hbm.at[t])

        for c in input_copies(first, 0):
            c.start()
        for c in input_copies(first, 0):
            c.wait()
        gather(u_hbm, 0, 0).start()
        gather(u_hbm, 0, 1).start()

        def token_pair(i, carry):
            token(first + 2 * i, 0)
            token(first + 2 * i + 1, 1)
            return carry

        lax.fori_loop(0, tpw // 2, token_pair, 0)
        gather(u_hbm, 0, 0).wait()
        gather(u_hbm, 0, 1).wait()

    small = [pltpu.VMEM((hrows,), i32), pltpu.VMEM((hrows,), i32), pltpu.VMEM((N_SEL,), f32),
             pltpu.VMEM((SUBLANES, LANES), f32)]
    return pl.kernel(
        body,
        out_type=jax.ShapeDtypeStruct((n_sc, d), f32),
        mesh=mesh,
        scratch_types=small + small + [pltpu.VMEM((N_SEL,), f32), pltpu.VMEM((hrows, words), i32),
                                       pltpu.VMEM((hrows, words), i32), pltpu.VMEM((N_SEL, SC_LANES), f32),
                                       pltpu.VMEM((d,), f32), pltpu.SemaphoreType.DMA, pltpu.SemaphoreType.DMA,
                                       pltpu.SemaphoreType.DMA],
        compiler_params=pltpu.CompilerParams(needs_layout_passes=False),
        name="peer_sc",
    )(u_words, v_words, ids_tok, x3, gate)


def _final_tail_kernel(out_in_ref, h_ref, p_ref, g_ref, o_ref):
    del out_in_ref
    o_ref[...] = _rms(h_ref[...] + p_ref[...], g_ref[...])


def _final_tail(out, h2_tail, peer_tail, g, tm, first_tile):
    n, d = out.shape
    tail_tiles = peer_tail.shape[0] // tm
    return pl.pallas_call(
        _final_tail_kernel,
        grid=(tail_tiles,),
        in_specs=[pl.BlockSpec(memory_space=pl.ANY),
                  pl.BlockSpec((tm, d), lambda i: (i, 0)),
                  pl.BlockSpec((tm, d), lambda i: (i, 0)),
                  pl.BlockSpec((1, d), lambda i: (0, 0))],
        out_specs=pl.BlockSpec((tm, d), lambda i: (i + first_tile, 0)),
        out_shape=jax.ShapeDtypeStruct((n, d), f32),
        input_output_aliases={0: 0},
        compiler_params=_params(),
        name="final_tail",
    )(out, h2_tail, peer_tail, g)


def _rope_tables(s):
    half = HEAD_DIM // 2
    inv = ROPE_THETA ** (-jnp.arange(half, dtype=f32) / half)
    ang = jnp.arange(s, dtype=f32)[:, None] * inv[None, :]
    cos, sin = jnp.cos(ang), jnp.sin(ang)
    reps = MOBA_GROUP_LANES // HEAD_DIM
    cos_t = jnp.tile(jnp.concatenate([cos, cos], axis=1), (1, reps))
    sin_t = jnp.tile(jnp.concatenate([-sin, sin], axis=1), (1, reps))
    return cos_t, sin_t


def _row_tile(s, want):
    t = min(want, s)
    while s % t:
        t //= 2
    assert t % SUBLANES == 0
    return t


def kernel(x, mem, g_mix, w_in, w_conv, w_conv_out, w_attn_out, w_merge, g_xattn, g_mem,
           w_xq, w_xkv, w_xo, g_ffn, w_pq, peer_sub_keys, peer_u, peer_v, g_final):
    b, s, d = x.shape
    n_mem = mem.shape[1]
    assert w_in.shape[0] == 1, "single-layer trunk only"
    assert d == SUBLANES * LANES and s % MOBA_BLOCK == 0
    n_exp = peer_u.shape[1]
    l = 0
    cos_t, sin_t = _rope_tables(s)
    tm = _row_tile(s, 512)
    tt = _row_tile(s, LANES)
    tsel = _row_tile(s, 256)
    w_in_b, w_co_b, w_ao_b, w_m_b = (t[l].astype(bf16) for t in (w_in, w_conv_out, w_attn_out, w_merge))
    w_xq_b, w_xkv_b, w_xo_b, w_pq_b, keys_b = (t[l].astype(bf16) for t in (w_xq, w_xkv, w_xo, w_pq, peer_sub_keys))

    def trunk(xb, memb):
        nb = xb.shape[0]
        h = xb.reshape(nb * s, d)
        proj = _inproj(h, g_mix[l][None, :], w_in_b, tm)
        o = _moba(proj, cos_t, sin_t, nb, s)
        h1 = _mix(proj, o, h, w_conv[l], w_co_b, w_ao_b, w_m_b, s, tm)
        kv = _memkv(memb.reshape(nb * n_mem, d), g_mem[l][None, :], w_xkv_b, n_mem)
        return _xattn(h1, g_xattn[l][None, :], kv, w_xq_b, w_xo_b, s, n_mem, tm)

    def select(h2):
        a3, ids, gate = _select(h2, g_ffn[l][None, :], w_pq_b, keys_b, tsel, 0, h2.shape[0] // tsel)
        return a3, ids, gate

    u_tab = _pack_table(peer_u[l])
    v_tab = _pack_table(peer_v[l])
    b_sc = (b * SC_BATCH_SHARE_16THS) // 16
    b_tc = b - b_sc
    tc_tiles = b_tc * s // tt
    if b_sc:
        h2_s = trunk(x[b_tc:], mem[b_tc:])
        a3_s, ids_s, gate_s = select(h2_s)
        as_rows = lambda tab: lax.bitcast_convert_type(tab, i32).reshape(n_exp, ROWS_PER_EXPERT * LANES)
        ids_tok = (ids_s // ROWS_PER_EXPERT).transpose(0, 2, 1).reshape(b_sc * s, N_SEL)
        peer_sc = _peer_sc(ids_tok, a3_s, gate_s, as_rows(u_tab), as_rows(v_tab))
    h2 = trunk(x[:b_tc], mem[:b_tc])
    a3, ids, gate = select(h2)
    ids = ids.reshape(tc_tiles, N_SEL * LANES)
    w = _peer_u(ids, a3, gate, u_tab, tt)
    out = _peer_v(ids, w, v_tab, h2, g_final[None, :], tt, b * s)
    if b_sc:
        out = _final_tail(out, h2_s, peer_sc, g_final[None, :], tt, tc_tiles)
    return out.reshape(b, s, d)
```

```python
import functools

import jax
import jax.numpy as jnp
from jax import lax
from jax.experimental import pallas as pl
from jax.experimental.pallas import tpu as pltpu

f32 = jnp.float32
bf16 = jnp.bfloat16
i32 = jnp.int32

EPS = 1e-6
MASK_VALUE = -1e30
ROPE_THETA = 10000.0

XATTN_HEADS = 4
ATTN_HEADS = 8
HEAD_DIM = 64
MOBA_BLOCK = 256
MOBA_TOPK = 3
MOBA_GROUP_LANES = 256
PEER_HEADS = 8
PEER_N_KEYS = 128
PEER_HALF = 128
PEER_TOPK = 16
N_SEL = PEER_HEADS * PEER_TOPK
ROWS_PER_EXPERT = 4

LANES = 128
SUBLANES = 8
VMEM_LIMIT = 56 * 1024 * 1024

_NT = (((1,), (1,)), ((), ()))


def _params(n_axes=1, vmem=VMEM_LIMIT):
    return pltpu.CompilerParams(dimension_semantics=("arbitrary",) * n_axes, vmem_limit_bytes=vmem)


def _rms(x, g):
    return x * lax.rsqrt(jnp.mean(x * x, axis=-1, keepdims=True) + EPS) * g


def _inproj_kernel(x_ref, g_ref, w_ref, o_ref, *, chunk):
    a = _rms(x_ref[...], g_ref[...]).astype(bf16)
    for c in range(o_ref.shape[1] // chunk):
        o_ref[:, c * chunk:(c + 1) * chunk] = jnp.dot(
            a, w_ref[:, c * chunk:(c + 1) * chunk], preferred_element_type=f32)


def _inproj(x2, g, w, tm):
    n, d = x2.shape
    width = w.shape[1]
    return pl.pallas_call(
        functools.partial(_inproj_kernel, chunk=width // 4),
        grid=(n // tm,),
        in_specs=[pl.BlockSpec((tm, d), lambda i: (i, 0)),
                  pl.BlockSpec((1, d), lambda i: (0, 0)),
                  pl.BlockSpec((d, width), lambda i: (0, 0), pipeline_mode=pl.Buffered(1))],
        out_specs=pl.BlockSpec((tm, width), lambda i: (i, 0)),
        out_shape=jax.ShapeDtypeStruct((n, width), f32),
        compiler_params=_params(),
        name="inproj",
    )(x2, g, w)


def _moba_kernel(q_ref, k_ref, v_ref, cos_ref, sin_ref, o_ref,
                 qb_sc, kb_sc, vt_sc, bias_sc, ot_sc, *, nb, topk):
    blk = MOBA_BLOCK
    s, gw = q_ref.shape
    lane = lax.broadcasted_iota(i32, (s, gw), 1)
    first_half = (lane % HEAD_DIM) < (HEAD_DIM // 2)
    cos = cos_ref[...]
    sin = sin_ref[...]

    def rope(t):
        partner = jnp.where(first_half, pltpu.roll(t, gw - HEAD_DIM // 2, 1),
                            pltpu.roll(t, HEAD_DIM // 2, 1))
        return t * cos + partner * sin

    q2 = rope(q_ref[...])
    k2 = rope(k_ref[...])
    v2 = v_ref[...]
    scale = HEAD_DIM ** -0.5
    blk_of_q = lax.broadcasted_iota(i32, (nb, s), 1) // blk
    m_iota = lax.broadcasted_iota(i32, (nb, s), 0)
    past = m_iota < blk_of_q
    kpos = lax.broadcasted_iota(i32, (blk, blk), 0)
    qpos = lax.broadcasted_iota(i32, (blk, blk), 1)
    causal = kpos <= qpos

    heads = range(gw // HEAD_DIM)
    for hh in heads:
        qh = q2[:, hh * HEAD_DIM:(hh + 1) * HEAD_DIM]
        kh = k2[:, hh * HEAD_DIM:(hh + 1) * HEAD_DIM]
        vh = v2[:, hh * HEAD_DIM:(hh + 1) * HEAD_DIM]
        kbar = jnp.mean(kh.reshape(nb, blk, HEAD_DIM), axis=1)
        gt = lax.dot_general(kbar, qh, _NT, precision=lax.Precision.HIGHEST,
                             preferred_element_type=f32)
        rows = []
        for n in range(nb):
            gn = gt[n:n + 1, :]
            beats = past & ((gt > gn) | ((gt == gn) & (m_iota < n)))
            cnt = jnp.sum(beats.astype(i32), axis=0, keepdims=True)
            sel = (blk_of_q[0:1, :] > n) & (cnt < topk)
            rows.append(jnp.where(sel, 0.0, MASK_VALUE).astype(f32))
        bias = jnp.concatenate(rows, axis=0)
        vt = jnp.transpose(vh).astype(bf16)
        for i in range(nb):
            bias_sc[hh, i] = bias[:, i * blk:(i + 1) * blk]
            vt_sc[hh, i] = vt[:, i * blk:(i + 1) * blk]
        qb_sc[hh] = qh.astype(bf16)
        kb_sc[hh] = kh.astype(bf16)

    def q_tile(i, carry):
        q0 = pl.multiple_of(i * blk, blk)
        diag = [lax.dot_general(kb_sc[hh, pl.ds(q0, blk), :], qb_sc[hh, pl.ds(q0, blk), :], _NT,
                                preferred_element_type=f32) for hh in heads]
        probs, stats = [], []
        for hh in heads:
            sd = jnp.where(causal, diag[hh] * scale, MASK_VALUE)
            m0 = jnp.max(sd, axis=0, keepdims=True)
            p0 = jnp.exp(sd - m0)
            probs.append(p0.astype(bf16))
            stats.append((m0, jnp.sum(p0, axis=0, keepdims=True)))
        init = [stats[hh] + (jnp.dot(vt_sc[hh, i], probs[hh], preferred_element_type=f32),)
                for hh in heads]

        def kv_step(j, st):
            k0 = pl.multiple_of(j * blk, blk)
            scores = [lax.dot_general(kb_sc[hh, pl.ds(k0, blk), :], qb_sc[hh, pl.ds(q0, blk), :], _NT,
                                      preferred_element_type=f32) for hh in heads]
            probs, stats = [], []
            for hh in heads:
                m, l, _ = st[hh]
                sj = scores[hh] * scale + bias_sc[hh, i, pl.ds(j, 1), :]
                m_new = jnp.maximum(m, jnp.max(sj, axis=0, keepdims=True))
                alpha = jnp.exp(m - m_new)
                p = jnp.exp(sj - m_new)
                probs.append(p.astype(bf16))
                stats.append((m_new, alpha * l + jnp.sum(p, axis=0, keepdims=True), alpha))
            out = []
            for hh in heads:
                m_new, l, alpha = stats[hh]
                acc = alpha * st[hh][2] + jnp.dot(vt_sc[hh, j], probs[hh], preferred_element_type=f32)
                out.append((m_new, l, acc))
            return tuple(out)

        final = lax.fori_loop(0, i, kv_step, tuple(init))
        for hh in heads:
            _, l, acc = final[hh]
            ot_sc[hh, i] = acc * (1.0 / l)
        return carry

    lax.fori_loop(0, nb, q_tile, 0)
    for hh in heads:
        for i in range(nb):
            o_ref[i * blk:(i + 1) * blk, hh * HEAD_DIM:(hh + 1) * HEAD_DIM] = jnp.transpose(ot_sc[hh, i])


def _moba(proj, cos_t, sin_t, b, s):
    n = b * s
    nb = s // MOBA_BLOCK
    topk = min(MOBA_TOPK, nb - 1)
    aw = ATTN_HEADS * HEAD_DIM
    gw = MOBA_GROUP_LANES
    hp = aw // gw
    nh = gw // HEAD_DIM
    qc, kc, vc = 3 * aw // gw, 4 * aw // gw, 5 * aw // gw
    return pl.pallas_call(
        functools.partial(_moba_kernel, nb=nb, topk=topk),
        grid=(b, hp),
        in_specs=[pl.BlockSpec((s, gw), lambda bi, h: (bi, qc + h)),
                  pl.BlockSpec((s, gw), lambda bi, h: (bi, kc + h)),
                  pl.BlockSpec((s, gw), lambda bi, h: (bi, vc + h)),
                  pl.BlockSpec((s, gw), lambda bi, h: (0, 0)),
                  pl.BlockSpec((s, gw), lambda bi, h: (0, 0))],
        out_specs=pl.BlockSpec((s, gw), lambda bi, h: (bi, h)),
        out_shape=jax.ShapeDtypeStruct((n, aw), f32),
        scratch_shapes=[pltpu.VMEM((nh, s, HEAD_DIM), bf16),
                        pltpu.VMEM((nh, s, HEAD_DIM), bf16),
                        pltpu.VMEM((nh, nb, HEAD_DIM, MOBA_BLOCK), bf16),
                        pltpu.VMEM((nh, nb, nb, MOBA_BLOCK), f32),
                        pltpu.VMEM((nh, nb, HEAD_DIM, MOBA_BLOCK), f32)],
        compiler_params=_params(2),
        name="moba",
    )(proj, proj, proj, cos_t, sin_t)


def _mix_kernel(xin_ref, bg_ref, cg_ref, pxin_ref, pcg_ref, gc_ref, ga_ref, o_ref, x_ref,
                wc_ref, wco_ref, wao_ref, wm_ref, h_ref, *, tiles_per_seq):
    tm = xin_ref.shape[0]
    i = pl.program_id(0)
    keep_prev = jnp.where(i % tiles_per_seq == 0, 0.0, 1.0).astype(f32)
    u = cg_ref[...] * xin_ref[...]
    pu = pcg_ref[...] * pxin_ref[...] * keep_prev
    rows = lax.broadcasted_iota(i32, u.shape, 0)
    u1 = jnp.where(rows == 0, pu[SUBLANES - 1:SUBLANES, :], pltpu.roll(u, 1, 0))
    u2 = jnp.where(rows == 0, pu[SUBLANES - 2:SUBLANES - 1, :],
                   jnp.where(rows == 1, pu[SUBLANES - 1:SUBLANES, :], pltpu.roll(u, 2, 0)))
    y = wc_ref[0:1, :] * u2 + wc_ref[1:2, :] * u1 + wc_ref[2:3, :] * u
    y_conv = jnp.dot((bg_ref[...] * y).astype(bf16), wco_ref[...], preferred_element_type=f32)
    y_attn = jnp.dot(o_ref[...].astype(bf16), wao_ref[...], preferred_element_type=f32)
    merged = jax.nn.sigmoid(gc_ref[...]) * y_conv + jax.nn.sigmoid(ga_ref[...]) * y_attn
    h_ref[...] = x_ref[...] + jnp.dot(merged.astype(bf16), wm_ref[...], preferred_element_type=f32)


def _mix(proj, o, x2, w_conv, wco, wao, wm, s, tm):
    n, d = x2.shape
    cw = wco.shape[0]
    aw = wao.shape[0]
    rb = tm // SUBLANES
    prev = lambda col: (lambda i: (jnp.maximum(i * rb - 1, 0), col))
    gcol = 3 * cw + 3 * aw
    return pl.pallas_call(
        functools.partial(_mix_kernel, tiles_per_seq=s // tm),
        grid=(n // tm,),
        in_specs=[pl.BlockSpec((tm, cw), lambda i: (i, 0)),
                  pl.BlockSpec((tm, cw), lambda i: (i, 1)),
                  pl.BlockSpec((tm, cw), lambda i: (i, 2)),
                  pl.BlockSpec((SUBLANES, cw), prev(0)),
                  pl.BlockSpec((SUBLANES, cw), prev(2)),
                  pl.BlockSpec((tm, d), lambda i: (i, gcol // d)),
                  pl.BlockSpec((tm, d), lambda i: (i, gcol // d + 1)),
                  pl.BlockSpec((tm, aw), lambda i: (i, 0)),
                  pl.BlockSpec((tm, d), lambda i: (i, 0)),
                  pl.BlockSpec(w_conv.shape, lambda i: (0, 0)),
                  pl.BlockSpec(wco.shape, lambda i: (0, 0)),
                  pl.BlockSpec(wao.shape, lambda i: (0, 0)),
                  pl.BlockSpec(wm.shape, lambda i: (0, 0))],
        out_specs=pl.BlockSpec((tm, d), lambda i: (i, 0)),
        out_shape=jax.ShapeDtypeStruct((n, d), f32),
        compiler_params=_params(),
        name="mix",
    )(proj, proj, proj, proj, proj, proj, proj, o, x2, w_conv, wco, wao, wm)


def _memkv_kernel(m_ref, g_ref, w_ref, o_ref):
    a = _rms(m_ref[...], g_ref[...]).astype(bf16)
    o_ref[...] = jnp.dot(a, w_ref[...], preferred_element_type=f32).astype(bf16)


def _memkv(mem2, g, w, n_mem):
    nm, d = mem2.shape
    return pl.pallas_call(
        _memkv_kernel,
        grid=(nm // n_mem,),
        in_specs=[pl.BlockSpec((n_mem, d), lambda i: (i, 0)),
                  pl.BlockSpec((1, d), lambda i: (0, 0)),
                  pl.BlockSpec(w.shape, lambda i: (0, 0))],
        out_specs=pl.BlockSpec((n_mem, w.shape[1]), lambda i: (i, 0)),
        out_shape=jax.ShapeDtypeStruct((nm, w.shape[1]), bf16),
        compiler_params=_params(),
        name="memkv",
    )(mem2, g, w)


def _xattn_kernel(h_ref, g_ref, kv_ref, wq_ref, wo_ref, o_ref):
    h = h_ref[...]
    d = h.shape[1]
    hd = d // XATTN_HEADS
    q = jnp.dot(_rms(h, g_ref[...]).astype(bf16), wq_ref[...], preferred_element_type=f32)
    outs = []
    for hh in range(XATTN_HEADS):
        qh = q[:, hh * hd:(hh + 1) * hd].astype(bf16)
        kh = kv_ref[:, hh * hd:(hh + 1) * hd]
        vh = kv_ref[:, d + hh * hd:d + (hh + 1) * hd]
        sc = lax.dot_general(qh, kh, _NT, preferred_element_type=f32) * (hd ** -0.5)
        sc = sc - jnp.max(sc, axis=-1, keepdims=True)
        p = jnp.exp(sc)
        p = p / jnp.sum(p, axis=-1, keepdims=True)
        outs.append(jnp.dot(p.astype(bf16), vh, preferred_element_type=f32).astype(bf16))
    o = jnp.concatenate(outs, axis=-1)
    o_ref[...] = h + jnp.dot(o, wo_ref[...], preferred_element_type=f32)


def _xattn(h1, g, kv, wq, wo, s, n_mem, tm):
    n, d = h1.shape
    tps = s // tm
    return pl.pallas_call(
        _xattn_kernel,
        grid=(n // tm,),
        in_specs=[pl.BlockSpec((tm, d), lambda i: (i, 0)),
                  pl.BlockSpec((1, d), lambda i: (0, 0)),
                  pl.BlockSpec((n_mem, 2 * d), lambda i: (i // tps, 0)),
                  pl.BlockSpec(wq.shape, lambda i: (0, 0)),
                  pl.BlockSpec(wo.shape, lambda i: (0, 0))],
        out_specs=pl.BlockSpec((tm, d), lambda i: (i, 0)),
        out_shape=jax.ShapeDtypeStruct((n, d), f32),
        compiler_params=_params(),
        name="xattn",
    )(h1, g, kv, wq, wo)


def _topk_cols(sc, k, payload=None):
    r = sc.shape[0]
    ridx = lax.broadcasted_iota(i32, sc.shape, 0).astype(f32)
    vals, picks = [], []
    for _ in range(k):
        m = jnp.max(sc, axis=0, keepdims=True)
        im = jnp.min(jnp.where(sc == m, ridx, float(r)), axis=0, keepdims=True)
        hit = ridx == im
        vals.append(m)
        picks.append(im if payload is None else jnp.max(jnp.where(hit, payload, -1.0), axis=0, keepdims=True))
        sc = jnp.where(hit, -jnp.inf, sc)
    return jnp.concatenate(vals, axis=0), jnp.concatenate(picks, axis=0)


def _pair_candidates(k):
    return [(a, b) for a in range(k) for b in range(k) if (a + 1) * (b + 1) <= k]


def _select_kernel(h_ref, g_ref, wq_ref, keys_ref, a_ref, ids_ref, gate_ref):
    k = PEER_TOPK
    a = _rms(h_ref[...], g_ref[...])
    for c in range(a_ref.shape[1]):
        a_ref[:, c, :] = a[:, c * LANES:(c + 1) * LANES]
    q = jnp.dot(a.astype(bf16), wq_ref[...], preferred_element_type=f32)
    tm = q.shape[0]
    pairs = _pair_candidates(k)
    n_pad = -len(pairs) % SUBLANES
    id_rows, gate_rows = [], []
    for h in range(PEER_HEADS):
        tops = []
        for p in range(2):
            c0 = (h * 2 + p) * PEER_HALF
            qhp = q[:, c0:c0 + PEER_HALF].astype(bf16)
            st = lax.dot_general(keys_ref[h, p], qhp, _NT, preferred_element_type=f32)
            tops.append(_topk_cols(st, k))
        (v0, i0), (v1, i1) = tops
        i0 = i0 * float(PEER_N_KEYS)
        cand, cand_id = [], []
        for a_ in range(k):
            nb = sum(1 for (pa, _) in pairs if pa == a_)
            cand.append(v0[a_:a_ + 1, :] + v1[0:nb, :])
            cand_id.append(i0[a_:a_ + 1, :] + i1[0:nb, :])
        if n_pad:
            cand.append(jnp.full((n_pad, tm), -jnp.inf, f32))
            cand_id.append(jnp.zeros((n_pad, tm), f32))
        top_s, top_id = _topk_cols(jnp.concatenate(cand, axis=0), k, jnp.concatenate(cand_id, axis=0))
        id_rows.append(top_id.astype(i32))
        ex = jnp.exp(top_s - top_s[0:1, :])
        gate_rows.append(ex / jnp.sum(ex, axis=0, keepdims=True))
    ids_t = jnp.concatenate(id_rows, axis=0) * ROWS_PER_EXPERT
    for j in range(ids_ref.shape[0]):
        ids_ref[j] = ids_t[:, j * LANES:(j + 1) * LANES]
    gate_ref[...] = jnp.transpose(jnp.concatenate(gate_rows, axis=0))


def _select(h2, g, wq, keys, tm, first_block, n_blocks):
    d = h2.shape[1]
    n = n_blocks * tm
    return pl.pallas_call(
        _select_kernel,
        grid=(n_blocks,),
        in_specs=[pl.BlockSpec((tm, d), lambda i: (i + first_block, 0)),
                  pl.BlockSpec((1, d), lambda i: (0, 0)),
                  pl.BlockSpec(wq.shape, lambda i: (0, 0)),
                  pl.BlockSpec(keys.shape, lambda i: (0, 0, 0, 0))],
        out_specs=[pl.BlockSpec((tm, d // LANES, LANES), lambda i: (i, 0, 0)),
                   pl.BlockSpec((tm // LANES, N_SEL, LANES), lambda i: (i, 0, 0)),
                   pl.BlockSpec((tm, N_SEL), lambda i: (i, 0))],
        out_shape=[jax.ShapeDtypeStruct((n, d // LANES, LANES), f32),
                   jax.ShapeDtypeStruct((n // LANES, N_SEL, LANES), i32),
                   jax.ShapeDtypeStruct((n, N_SEL), f32)],
        compiler_params=_params(),
        name="peer_select",
    )(h2, g, wq, keys)


def _gelu_tanh(x):
    cdf = 0.5 * (1.0 + jnp.tanh(0.7978845608028654 * (x + 0.044715 * (x * x * x))))
    return x * cdf


def _pack_kernel(t_ref, o_ref):
    o_ref[...] = pltpu.bitcast(t_ref[...].astype(bf16), jnp.uint32)


def _pack_table(t):
    e, d = t.shape
    rows = e * d // LANES
    blk = min(rows, 8192)
    assert rows % blk == 0 and d == 2 * ROWS_PER_EXPERT * LANES
    return pl.pallas_call(
        _pack_kernel,
        grid=(rows // blk,),
        in_specs=[pl.BlockSpec((blk, LANES), lambda i: (i, 0))],
        out_specs=pl.BlockSpec((blk // 2, LANES), lambda i: (i, 0)),
        out_shape=jax.ShapeDtypeStruct((rows // 2, LANES), jnp.uint32),
        compiler_params=_params(),
        name="pack_table",
    )(t.reshape(rows, LANES))


def _expert_row(tab_ref, row0):
    words = tab_ref[pl.ds(pl.multiple_of(row0, ROWS_PER_EXPERT), ROWS_PER_EXPERT), :]
    return pltpu.bitcast(words, bf16).astype(f32)


def _token_id(ids_ref, e, t):
    return ids_ref.at[pl.ds(e * LANES, LANES)][t]


_FOLD_ORDER = (0, 4, 2, 6, 1, 5, 3, 7)


def _fold8(z):
    sub = lax.broadcasted_iota(i32, z[0].shape, 0)
    lo4 = sub < 4
    lo2 = (sub % 4) < 2
    even = (sub % 2) == 0
    c = [jnp.where(lo4, z[2 * k], z[2 * k + 1]) + pltpu.roll(jnp.where(lo4, z[2 * k + 1], z[2 * k]), 4, 0)
         for k in range(4)]
    e = [jnp.where(lo2, c[2 * k] + pltpu.roll(c[2 * k], 6, 0), c[2 * k + 1] + pltpu.roll(c[2 * k + 1], 2, 0))
         for k in range(2)]
    return jnp.where(even, e[0] + pltpu.roll(e[0], 7, 0), e[1] + pltpu.roll(e[1], 1, 0))


def _with_ids_tile(ids_hbm, bufs, sems, body):
    i = pl.program_id(0)
    n_steps = pl.num_programs(0)

    def tile_copy(tile, slot):
        return pltpu.make_async_copy(ids_hbm.at[tile], bufs[slot], sems.at[slot])

    @pl.when(i == 0)
    def _():
        tile_copy(0, 0).start()

    for slot in range(2):
        @pl.when(i % 2 == slot)
        def _():
            @pl.when(i + 1 < n_steps)
            def _():
                tile_copy(i + 1, 1 - slot).start()
            tile_copy(i, slot).wait()
            body(bufs[slot])


def _ids_scratch():
    return [pltpu.SMEM((N_SEL * LANES,), i32), pltpu.SMEM((N_SEL * LANES,), i32), pltpu.SemaphoreType.DMA((2,))]


def _peer_u_kernel(ids_hbm, x_ref, gate_ref, u_ref, w_ref, a_sc, ids0, ids1, sems):
    tt = x_ref.shape[0]
    lane = lax.broadcasted_iota(i32, (N_SEL, LANES), 1)

    def finish(part, t):
        col = jnp.sum(part, axis=1, keepdims=True)
        a_sc[...] = jnp.where(lane == t, col, a_sc[...])

    def body(ids_ref):
        def token(t, part_prev):
            finish(part_prev, t - 1)
            xt = x_ref[t]
            groups = []
            for g in range(N_SEL // SUBLANES):
                z = [_expert_row(u_ref, _token_id(ids_ref, g * SUBLANES + _FOLD_ORDER[k], t)) * xt
                     for k in range(SUBLANES)]
                groups.append(_fold8(z))
            return jnp.concatenate(groups, axis=0)

        a_sc[...] = jnp.zeros_like(a_sc)
        part = lax.fori_loop(0, tt, token, jnp.zeros((N_SEL, LANES), f32))
        finish(part, tt - 1)
        act = jnp.transpose(a_sc[...])[0:tt, :]
        w_ref[...] = gate_ref[...] * _gelu_tanh(act)

    _with_ids_tile(ids_hbm, (ids0, ids1), sems, body)


def _peer_u(ids, a3, gate, u_tab, tt):
    n_tiles = ids.shape[0]
    n = n_tiles * tt
    assert tt == LANES and ids.shape[1] == N_SEL * LANES
    return pl.pallas_call(
        _peer_u_kernel,
        grid=(n_tiles,),
        in_specs=[pl.BlockSpec(memory_space=pl.ANY),
                  pl.BlockSpec((tt, SUBLANES, LANES), lambda i: (i, 0, 0)),
                  pl.BlockSpec((tt, N_SEL), lambda i: (i, 0)),
                  pl.BlockSpec(u_tab.shape, lambda i: (0, 0), pipeline_mode=pl.Buffered(1))],
        out_specs=pl.BlockSpec((tt, N_SEL), lambda i: (i, 0)),
        out_shape=jax.ShapeDtypeStruct((n, N_SEL), f32),
        scratch_shapes=[pltpu.VMEM((N_SEL, LANES), f32)] + _ids_scratch(),
        compiler_params=_params(),
        name="peer_u",
    )(ids, a3, gate, u_tab)


def _peer_v_kernel(ids_hbm, w_ref, v_ref, h_ref, g_ref, o_ref, ids0, ids1, sems, p_sc):
    tt = o_ref.shape[0]
    n_acc = 4

    def lane_bcast_weights(t):
        return jnp.transpose(jnp.broadcast_to(w_ref[pl.ds(t, 1), :], (LANES, N_SEL)))

    def body(ids_ref):
        def token(t, wb):
            wb_next = lane_bcast_weights(jnp.minimum(t + 1, tt - 1))
            accs = [jnp.zeros((SUBLANES, LANES), f32) for _ in range(n_acc)]
            for e in range(N_SEL):
                wv = jnp.broadcast_to(wb[e:e + 1, :], (SUBLANES, LANES))
                accs[e % n_acc] = accs[e % n_acc] + _expert_row(v_ref, _token_id(ids_ref, e, t)) * wv
            p_sc[t] = (accs[0] + accs[1]) + (accs[2] + accs[3])
            return wb_next

        lax.fori_loop(0, tt, token, lane_bcast_weights(0))
        peer = jnp.concatenate([p_sc[:, c, :] for c in range(SUBLANES)], axis=1)
        o_ref[...] = _rms(h_ref[...] + peer, g_ref[...])

    _with_ids_tile(ids_hbm, (ids0, ids1), sems, body)


def _peer_v(ids, w, v_tab, h2, g, tt):
    n_tiles = ids.shape[0]
    n, d = h2.shape
    assert tt == LANES and ids.shape[1] == N_SEL * LANES and n >= n_tiles * tt and d == SUBLANES * LANES
    return pl.pallas_call(
        _peer_v_kernel,
        grid=(n_tiles,),
        in_specs=[pl.BlockSpec(memory_space=pl.ANY),
                  pl.BlockSpec((tt, N_SEL), lambda i: (i, 0)),
                  pl.BlockSpec(v_tab.shape, lambda i: (0, 0), pipeline_mode=pl.Buffered(1)),
                  pl.BlockSpec((tt, d), lambda i: (i, 0)),
                  pl.BlockSpec((1, d), lambda i: (0, 0))],
        out_specs=pl.BlockSpec((tt, d), lambda i: (i, 0)),
        out_shape=jax.ShapeDtypeStruct((n, d), f32),
        scratch_shapes=_ids_scratch() + [pltpu.VMEM((tt, SUBLANES, LANES), f32)],
        compiler_params=_params(),
        name="peer_v",
    )(ids, w, v_tab, h2, g)


SC_WORKERS = 32
SC_LANES = 16
SC_TILE_SHARE_64THS = 26


def _peer_sc(ids_tok, x3, gate, u_words, v_words):
    from jax.experimental.pallas import tpu_sc as plsc
    n_sc = ids_tok.shape[0]
    words = v_words.shape[1]
    d = 2 * words
    tpw = n_sc // SC_WORKERS
    hrows = N_SEL // 2
    assert n_sc % SC_WORKERS == 0 and tpw % 2 == 0 and words == ROWS_PER_EXPERT * LANES
    half = words // 2
    nvec = half // SC_LANES
    mesh = plsc.VectorSubcoreMesh(core_axis_name="c", subcore_axis_name="s")

    def halves(wd):
        return (lax.bitcast_convert_type(wd << 16, f32), lax.bitcast_convert_type(wd & jnp.int32(-65536), f32))

    def body(u_hbm, v_hbm, ids_hbm, x_hbm, g_hbm, out_hbm,
             idx_a0, idx_b0, g_0, x_0, idx_a1, idx_b1, g_1, x_1, w_v, rows_a, rows_b, part_v, out_v, sem_a, sem_b, sem_s):
        wid = lax.axis_index("s") * 2 + lax.axis_index("c")
        lane = lax.broadcasted_iota(i32, (SC_LANES,), 0)
        first = wid * tpw
        last = first + tpw - 1
        inputs = ((idx_a0, idx_b0, g_0, x_0), (idx_a1, idx_b1, g_1, x_1))
        row_bufs = ((rows_a, sem_a), (rows_b, sem_b))

        def input_copies(t, par):
            idx_a, idx_b, g_v, x_v = inputs[par]
            return (pltpu.make_async_copy(ids_hbm.at[t, pl.ds(0, hrows)], idx_a, sem_s),
                    pltpu.make_async_copy(ids_hbm.at[t, pl.ds(hrows, hrows)], idx_b, sem_s),
                    pltpu.make_async_copy(g_hbm.at[t], g_v, sem_s),
                    pltpu.make_async_copy(x_hbm.at[t], x_v, sem_s))

        def gather(tab_hbm, par, h):
            rows, sem = row_bufs[h]
            return pltpu.make_async_copy(tab_hbm.at[inputs[par][h]], rows, sem)

        def dots(par, h):
            rows, x_v = row_bufs[h][0], inputs[par][3]
            for p in range(2):
                xs = []
                for j in range(nvec):
                    o = p * half + j * SC_LANES
                    r, l0 = o // LANES, o % LANES
                    xs.append((x_v[2 * r, pl.ds(l0, SC_LANES)], x_v[2 * r + 1, pl.ds(l0, SC_LANES)]))

                def dot_pair(e, c, p=p, xs=xs):
                    acc = None
                    for j in range(nvec):
                        lo, hi = halves(rows[e, pl.ds(p * half + j * SC_LANES, SC_LANES)])
                        term = lo * xs[j][0] + hi * xs[j][1]
                        acc = term if acc is None else acc + term
                    row = h * hrows + e
                    part_v[row, :] = acc if p == 0 else part_v[row, :] + acc
                    return c

                lax.fori_loop(0, hrows, dot_pair, 0)

        def weights(par):
            g_v = inputs[par][2]
            for k in range(N_SEL // SC_LANES):
                act = jnp.zeros((SC_LANES,), f32)
                for q in range(SC_LANES):
                    act = jnp.where(lane == q, jnp.sum(part_v[k * SC_LANES + q, :]), act)
                inner = 0.7978845608028654 * (act + 0.044715 * (act * act * act))
                tanh = 1.0 - 2.0 / (jnp.exp(2.0 * inner) + 1.0)
                w_v[pl.ds(k * SC_LANES, SC_LANES)] = g_v[pl.ds(k * SC_LANES, SC_LANES)] * (act * (0.5 * (1.0 + tanh)))

        def accumulate(h):
            rows = row_bufs[h][0]
            for p in range(2):
                slots = []
                for j in range(nvec):
                    o = p * half + j * SC_LANES
                    r, l0 = o // LANES, o % LANES
                    slots += [2 * LANES * r + l0, 2 * LANES * r + LANES + l0]

                def acc_pair(e, accs, p=p):
                    row = h * hrows + e
                    wvec = w_v[pl.ds(pl.multiple_of((row // SC_LANES) * SC_LANES, SC_LANES), SC_LANES)]
                    wr = jnp.sum(jnp.where(lane == row % SC_LANES, wvec, 0.0))
                    out = []
                    for j in range(nvec):
                        lo, hi = halves(rows[e, pl.ds(p * half + j * SC_LANES, SC_LANES)])
                        out.append(accs[2 * j] + lo * wr)
                        out.append(accs[2 * j + 1] + hi * wr)
                    return tuple(out)

                init = tuple(jnp.zeros((SC_LANES,), f32) if h == 0 else out_v[pl.ds(o, SC_LANES)] for o in slots)
                accs = lax.fori_loop(0, hrows, acc_pair, init)
                for o, acc in zip(slots, accs):
                    out_v[pl.ds(o, SC_LANES)] = acc

        def token(t, par):
            nxt = jnp.minimum(t + 1, last)
            for c in input_copies(nxt, 1 - par):
                c.start()
            gather(u_hbm, par, 0).wait()
            dots(par, 0)
            gather(v_hbm, par, 0).start()
            gather(u_hbm, par, 1).wait()
            dots(par, 1)
            gather(v_hbm, par, 1).start()
            weights(par)
            for c in input_copies(nxt, 1 - par):
                c.wait()
            gather(v_hbm, par, 0).wait()
            accumulate(0)
            gather(u_hbm, 1 - par, 0).start()
            gather(v_hbm, par, 1).wait()
            accumulate(1)
            gather(u_hbm, 1 - par, 1).start()
            pltpu.sync_copy(out_v, out_hbm.at[t])

        for c in input_copies(first, 0):
            c.start()
        for c in input_copies(first, 0):
            c.wait()
        gather(u_hbm, 0, 0).start()
        gather(u_hbm, 0, 1).start()

        def token_pair(i, carry):
            token(first + 2 * i, 0)
            token(first + 2 * i + 1, 1)
            return carry

        lax.fori_loop(0, tpw // 2, token_pair, 0)
        gather(u_hbm, 0, 0).wait()
        gather(u_hbm, 0, 1).wait()

    small = [pltpu.VMEM((hrows,), i32), pltpu.VMEM((hrows,), i32), pltpu.VMEM((N_SEL,), f32),
             pltpu.VMEM((SUBLANES, LANES), f32)]
    return pl.kernel(
        body,
        out_type=jax.ShapeDtypeStruct((n_sc, d), f32),
        mesh=mesh,
        scratch_types=small + small + [pltpu.VMEM((N_SEL,), f32), pltpu.VMEM((hrows, words), i32),
                                       pltpu.VMEM((hrows, words), i32), pltpu.VMEM((N_SEL, SC_LANES), f32),
                                       pltpu.VMEM((d,), f32), pltpu.SemaphoreType.DMA, pltpu.SemaphoreType.DMA,
                                       pltpu.SemaphoreType.DMA],
        compiler_params=pltpu.CompilerParams(needs_layout_passes=False),
        name="peer_sc",
    )(u_words, v_words, ids_tok, x3, gate)


def _final_tail_kernel(out_in_ref, h_ref, p_ref, g_ref, o_ref):
    del out_in_ref
    o_ref[...] = _rms(h_ref[...] + p_ref[...], g_ref[...])


def _final_tail(out, h2, peer_tail, g, tm, first_tile):
    n, d = h2.shape
    tail_tiles = peer_tail.shape[0] // tm
    return pl.pallas_call(
        _final_tail_kernel,
        grid=(tail_tiles,),
        in_specs=[pl.BlockSpec(memory_space=pl.ANY),
                  pl.BlockSpec((tm, d), lambda i: (i + first_tile, 0)),
                  pl.BlockSpec((tm, d), lambda i: (i, 0)),
                  pl.BlockSpec((1, d), lambda i: (0, 0))],
        out_specs=pl.BlockSpec((tm, d), lambda i: (i + first_tile, 0)),
        out_shape=jax.ShapeDtypeStruct((n, d), f32),
        input_output_aliases={0: 0},
        compiler_params=_params(),
        name="final_tail",
    )(out, h2, peer_tail, g)


def _rope_tables(s):
    half = HEAD_DIM // 2
    inv = ROPE_THETA ** (-jnp.arange(half, dtype=f32) / half)
    ang = jnp.arange(s, dtype=f32)[:, None] * inv[None, :]
    cos, sin = jnp.cos(ang), jnp.sin(ang)
    reps = MOBA_GROUP_LANES // HEAD_DIM
    cos_t = jnp.tile(jnp.concatenate([cos, cos], axis=1), (1, reps))
    sin_t = jnp.tile(jnp.concatenate([-sin, sin], axis=1), (1, reps))
    return cos_t, sin_t


def _row_tile(s, want):
    t = min(want, s)
    while s % t:
        t //= 2
    assert t % SUBLANES == 0
    return t


def kernel(x, mem, g_mix, w_in, w_conv, w_conv_out, w_attn_out, w_merge, g_xattn, g_mem,
           w_xq, w_xkv, w_xo, g_ffn, w_pq, peer_sub_keys, peer_u, peer_v, g_final):
    b, s, d = x.shape
    n = b * s
    n_mem = mem.shape[1]
    assert w_in.shape[0] == 1, "single-layer trunk only"
    assert d == SUBLANES * LANES and s % MOBA_BLOCK == 0
    n_exp = peer_u.shape[1]
    cos_t, sin_t = _rope_tables(s)
    tm = _row_tile(s, 512)
    h = x.reshape(n, d)
    mem2 = mem.reshape(b * n_mem, d)
    l = 0
    proj = _inproj(h, g_mix[l][None, :], w_in[l].astype(bf16), tm)
    o = _moba(proj, cos_t, sin_t, b, s)
    h1 = _mix(proj, o, h, w_conv[l], w_conv_out[l].astype(bf16), w_attn_out[l].astype(bf16),
              w_merge[l].astype(bf16), s, tm)
    kv = _memkv(mem2, g_mem[l][None, :], w_xkv[l].astype(bf16), n_mem)
    h2 = _xattn(h1, g_xattn[l][None, :], kv, w_xq[l].astype(bf16), w_xo[l].astype(bf16), s, n_mem, tm)
    u_tab = _pack_table(peer_u[l])
    v_tab = _pack_table(peer_v[l])
    tt = _row_tile(s, LANES)
    tsel = _row_tile(s, 256)
    n_tiles = n // tt
    per_sel = tsel // tt
    sc_tiles = ((n_tiles * SC_TILE_SHARE_64THS) // 64) // per_sel * per_sel
    tc_tiles = n_tiles - sc_tiles
    sel_args = (h2, g_ffn[l][None, :], w_pq[l].astype(bf16), peer_sub_keys[l].astype(bf16), tsel)
    if sc_tiles:
        a3_s, ids_s, gate_s = _select(*sel_args, tc_tiles // per_sel, sc_tiles // per_sel)
        as_rows = lambda tab: lax.bitcast_convert_type(tab, i32).reshape(n_exp, ROWS_PER_EXPERT * LANES)
        ids_tok = (ids_s // ROWS_PER_EXPERT).transpose(0, 2, 1).reshape(sc_tiles * tt, N_SEL)
        peer_sc = _peer_sc(ids_tok, a3_s, gate_s, as_rows(u_tab), as_rows(v_tab))
    a3, ids, gate = _select(*sel_args, 0, tc_tiles // per_sel)
    ids = ids.reshape(tc_tiles, N_SEL * LANES)
    w = _peer_u(ids, a3, gate, u_tab, tt)
    out = _peer_v(ids, w, v_tab, h2, g_final[None, :], tt)
    if sc_tiles:
        out = _final_tail(out, h2, peer_sc, g_final[None, :], tt, tc_tiles)
    return out.reshape(b, s, d)
```

```python
import functools

import jax
import jax.numpy as jnp
from jax import lax
from jax.experimental import pallas as pl
from jax.experimental.pallas import tpu as pltpu

f32 = jnp.float32
bf16 = jnp.bfloat16
i32 = jnp.int32

EPS = 1e-6
MASK_VALUE = -1e30
ROPE_THETA = 10000.0

XATTN_HEADS = 4
ATTN_HEADS = 8
HEAD_DIM = 64
MOBA_BLOCK = 256
MOBA_TOPK = 3
MOBA_GROUP_LANES = 256
PEER_HEADS = 8
PEER_N_KEYS = 128
PEER_HALF = 128
PEER_TOPK = 16
N_SEL = PEER_HEADS * PEER_TOPK
ROWS_PER_EXPERT = 4

LANES = 128
SUBLANES = 8
VMEM_LIMIT = 56 * 1024 * 1024

_NT = (((1,), (1,)), ((), ()))


def _params(n_axes=1, vmem=VMEM_LIMIT):
    return pltpu.CompilerParams(dimension_semantics=("arbitrary",) * n_axes, vmem_limit_bytes=vmem)


def _rms(x, g):
    return x * lax.rsqrt(jnp.mean(x * x, axis=-1, keepdims=True) + EPS) * g


def _inproj_kernel(x_ref, g_ref, w_ref, o_ref, *, chunk):
    a = _rms(x_ref[...], g_ref[...]).astype(bf16)
    for c in range(o_ref.shape[1] // chunk):
        o_ref[:, c * chunk:(c + 1) * chunk] = jnp.dot(
            a, w_ref[:, c * chunk:(c + 1) * chunk], preferred_element_type=f32)


def _inproj(x2, g, w, tm):
    n, d = x2.shape
    width = w.shape[1]
    return pl.pallas_call(
        functools.partial(_inproj_kernel, chunk=width // 4),
        grid=(n // tm,),
        in_specs=[pl.BlockSpec((tm, d), lambda i: (i, 0)),
                  pl.BlockSpec((1, d), lambda i: (0, 0)),
                  pl.BlockSpec((d, width), lambda i: (0, 0), pipeline_mode=pl.Buffered(1))],
        out_specs=pl.BlockSpec((tm, width), lambda i: (i, 0)),
        out_shape=jax.ShapeDtypeStruct((n, width), f32),
        compiler_params=_params(),
        name="inproj",
    )(x2, g, w)


def _moba_kernel(q_ref, k_ref, v_ref, cos_ref, sin_ref, o_ref,
                 qb_sc, kb_sc, vt_sc, bias_sc, ot_sc, *, nb, topk):
    blk = MOBA_BLOCK
    s, gw = q_ref.shape
    lane = lax.broadcasted_iota(i32, (s, gw), 1)
    first_half = (lane % HEAD_DIM) < (HEAD_DIM // 2)
    cos = cos_ref[...]
    sin = sin_ref[...]

    def rope(t):
        partner = jnp.where(first_half, pltpu.roll(t, gw - HEAD_DIM // 2, 1),
                            pltpu.roll(t, HEAD_DIM // 2, 1))
        return t * cos + partner * sin

    q2 = rope(q_ref[...])
    k2 = rope(k_ref[...])
    v2 = v_ref[...]
    scale = HEAD_DIM ** -0.5
    blk_of_q = lax.broadcasted_iota(i32, (nb, s), 1) // blk
    m_iota = lax.broadcasted_iota(i32, (nb, s), 0)
    past = m_iota < blk_of_q
    kpos = lax.broadcasted_iota(i32, (blk, blk), 0)
    qpos = lax.broadcasted_iota(i32, (blk, blk), 1)
    causal = kpos <= qpos

    heads = range(gw // HEAD_DIM)
    for hh in heads:
        qh = q2[:, hh * HEAD_DIM:(hh + 1) * HEAD_DIM]
        kh = k2[:, hh * HEAD_DIM:(hh + 1) * HEAD_DIM]
        vh = v2[:, hh * HEAD_DIM:(hh + 1) * HEAD_DIM]
        kbar = jnp.mean(kh.reshape(nb, blk, HEAD_DIM), axis=1)
        gt = lax.dot_general(kbar, qh, _NT, precision=lax.Precision.HIGHEST,
                             preferred_element_type=f32)
        rows = []
        for n in range(nb):
            gn = gt[n:n + 1, :]
            beats = past & ((gt > gn) | ((gt == gn) & (m_iota < n)))
            cnt = jnp.sum(beats.astype(i32), axis=0, keepdims=True)
            sel = (blk_of_q[0:1, :] > n) & (cnt < topk)
            rows.append(jnp.where(sel, 0.0, MASK_VALUE).astype(f32))
        bias = jnp.concatenate(rows, axis=0)
        vt = jnp.transpose(vh).astype(bf16)
        for i in range(nb):
            bias_sc[hh, i] = bias[:, i * blk:(i + 1) * blk]
            vt_sc[hh, i] = vt[:, i * blk:(i + 1) * blk]
        qb_sc[hh] = qh.astype(bf16)
        kb_sc[hh] = kh.astype(bf16)

    def q_tile(i, carry):
        q0 = pl.multiple_of(i * blk, blk)
        diag = [lax.dot_general(kb_sc[hh, pl.ds(q0, blk), :], qb_sc[hh, pl.ds(q0, blk), :], _NT,
                                preferred_element_type=f32) for hh in heads]
        probs, stats = [], []
        for hh in heads:
            sd = jnp.where(causal, diag[hh] * scale, MASK_VALUE)
            m0 = jnp.max(sd, axis=0, keepdims=True)
            p0 = jnp.exp(sd - m0)
            probs.append(p0.astype(bf16))
            stats.append((m0, jnp.sum(p0, axis=0, keepdims=True)))
        init = [stats[hh] + (jnp.dot(vt_sc[hh, i], probs[hh], preferred_element_type=f32),)
                for hh in heads]

        def kv_step(j, st):
            k0 = pl.multiple_of(j * blk, blk)
            scores = [lax.dot_general(kb_sc[hh, pl.ds(k0, blk), :], qb_sc[hh, pl.ds(q0, blk), :], _NT,
                                      preferred_element_type=f32) for hh in heads]
            probs, stats = [], []
            for hh in heads:
                m, l, _ = st[hh]
                sj = scores[hh] * scale + bias_sc[hh, i, pl.ds(j, 1), :]
                m_new = jnp.maximum(m, jnp.max(sj, axis=0, keepdims=True))
                alpha = jnp.exp(m - m_new)
                p = jnp.exp(sj - m_new)
                probs.append(p.astype(bf16))
                stats.append((m_new, alpha * l + jnp.sum(p, axis=0, keepdims=True), alpha))
            out = []
            for hh in heads:
                m_new, l, alpha = stats[hh]
                acc = alpha * st[hh][2] + jnp.dot(vt_sc[hh, j], probs[hh], preferred_element_type=f32)
                out.append((m_new, l, acc))
            return tuple(out)

        final = lax.fori_loop(0, i, kv_step, tuple(init))
        for hh in heads:
            _, l, acc = final[hh]
            ot_sc[hh, i] = acc * (1.0 / l)
        return carry

    lax.fori_loop(0, nb, q_tile, 0)
    for hh in heads:
        for i in range(nb):
            o_ref[i * blk:(i + 1) * blk, hh * HEAD_DIM:(hh + 1) * HEAD_DIM] = jnp.transpose(ot_sc[hh, i])


def _moba(proj, cos_t, sin_t, b, s):
    n = b * s
    nb = s // MOBA_BLOCK
    topk = min(MOBA_TOPK, nb - 1)
    aw = ATTN_HEADS * HEAD_DIM
    gw = MOBA_GROUP_LANES
    hp = aw // gw
    nh = gw // HEAD_DIM
    qc, kc, vc = 3 * aw // gw, 4 * aw // gw, 5 * aw // gw
    return pl.pallas_call(
        functools.partial(_moba_kernel, nb=nb, topk=topk),
        grid=(b, hp),
        in_specs=[pl.BlockSpec((s, gw), lambda bi, h: (bi, qc + h)),
                  pl.BlockSpec((s, gw), lambda bi, h: (bi, kc + h)),
                  pl.BlockSpec((s, gw), lambda bi, h: (bi, vc + h)),
                  pl.BlockSpec((s, gw), lambda bi, h: (0, 0)),
                  pl.BlockSpec((s, gw), lambda bi, h: (0, 0))],
        out_specs=pl.BlockSpec((s, gw), lambda bi, h: (bi, h)),
        out_shape=jax.ShapeDtypeStruct((n, aw), f32),
        scratch_shapes=[pltpu.VMEM((nh, s, HEAD_DIM), bf16),
                        pltpu.VMEM((nh, s, HEAD_DIM), bf16),
                        pltpu.VMEM((nh, nb, HEAD_DIM, MOBA_BLOCK), bf16),
                        pltpu.VMEM((nh, nb, nb, MOBA_BLOCK), f32),
                        pltpu.VMEM((nh, nb, HEAD_DIM, MOBA_BLOCK), f32)],
        compiler_params=_params(2),
        name="moba",
    )(proj, proj, proj, cos_t, sin_t)


def _mix_kernel(xin_ref, bg_ref, cg_ref, pxin_ref, pcg_ref, gc_ref, ga_ref, o_ref, x_ref,
                wc_ref, wco_ref, wao_ref, wm_ref, h_ref, *, tiles_per_seq):
    tm = xin_ref.shape[0]
    i = pl.program_id(0)
    keep_prev = jnp.where(i % tiles_per_seq == 0, 0.0, 1.0).astype(f32)
    u = cg_ref[...] * xin_ref[...]
    pu = pcg_ref[...] * pxin_ref[...] * keep_prev
    rows = lax.broadcasted_iota(i32, u.shape, 0)
    u1 = jnp.where(rows == 0, pu[SUBLANES - 1:SUBLANES, :], pltpu.roll(u, 1, 0))
    u2 = jnp.where(rows == 0, pu[SUBLANES - 2:SUBLANES - 1, :],
                   jnp.where(rows == 1, pu[SUBLANES - 1:SUBLANES, :], pltpu.roll(u, 2, 0)))
    y = wc_ref[0:1, :] * u2 + wc_ref[1:2, :] * u1 + wc_ref[2:3, :] * u
    y_conv = jnp.dot((bg_ref[...] * y).astype(bf16), wco_ref[...], preferred_element_type=f32)
    y_attn = jnp.dot(o_ref[...].astype(bf16), wao_ref[...], preferred_element_type=f32)
    merged = jax.nn.sigmoid(gc_ref[...]) * y_conv + jax.nn.sigmoid(ga_ref[...]) * y_attn
    h_ref[...] = x_ref[...] + jnp.dot(merged.astype(bf16), wm_ref[...], preferred_element_type=f32)


def _mix(proj, o, x2, w_conv, wco, wao, wm, s, tm):
    n, d = x2.shape
    cw = wco.shape[0]
    aw = wao.shape[0]
    rb = tm // SUBLANES
    prev = lambda col: (lambda i: (jnp.maximum(i * rb - 1, 0), col))
    gcol = 3 * cw + 3 * aw
    return pl.pallas_call(
        functools.partial(_mix_kernel, tiles_per_seq=s // tm),
        grid=(n // tm,),
        in_specs=[pl.BlockSpec((tm, cw), lambda i: (i, 0)),
                  pl.BlockSpec((tm, cw), lambda i: (i, 1)),
                  pl.BlockSpec((tm, cw), lambda i: (i, 2)),
                  pl.BlockSpec((SUBLANES, cw), prev(0)),
                  pl.BlockSpec((SUBLANES, cw), prev(2)),
                  pl.BlockSpec((tm, d), lambda i: (i, gcol // d)),
                  pl.BlockSpec((tm, d), lambda i: (i, gcol // d + 1)),
                  pl.BlockSpec((tm, aw), lambda i: (i, 0)),
                  pl.BlockSpec((tm, d), lambda i: (i, 0)),
                  pl.BlockSpec(w_conv.shape, lambda i: (0, 0)),
                  pl.BlockSpec(wco.shape, lambda i: (0, 0)),
                  pl.BlockSpec(wao.shape, lambda i: (0, 0)),
                  pl.BlockSpec(wm.shape, lambda i: (0, 0))],
        out_specs=pl.BlockSpec((tm, d), lambda i: (i, 0)),
        out_shape=jax.ShapeDtypeStruct((n, d), f32),
        compiler_params=_params(),
        name="mix",
    )(proj, proj, proj, proj, proj, proj, proj, o, x2, w_conv, wco, wao, wm)


def _memkv_kernel(m_ref, g_ref, w_ref, o_ref):
    a = _rms(m_ref[...], g_ref[...]).astype(bf16)
    o_ref[...] = jnp.dot(a, w_ref[...], preferred_element_type=f32).astype(bf16)


def _memkv(mem2, g, w, n_mem):
    nm, d = mem2.shape
    return pl.pallas_call(
        _memkv_kernel,
        grid=(nm // n_mem,),
        in_specs=[pl.BlockSpec((n_mem, d), lambda i: (i, 0)),
                  pl.BlockSpec((1, d), lambda i: (0, 0)),
                  pl.BlockSpec(w.shape, lambda i: (0, 0))],
        out_specs=pl.BlockSpec((n_mem, w.shape[1]), lambda i: (i, 0)),
        out_shape=jax.ShapeDtypeStruct((nm, w.shape[1]), bf16),
        compiler_params=_params(),
        name="memkv",
    )(mem2, g, w)


def _xattn_kernel(h_ref, g_ref, kv_ref, wq_ref, wo_ref, o_ref):
    h = h_ref[...]
    d = h.shape[1]
    hd = d // XATTN_HEADS
    q = jnp.dot(_rms(h, g_ref[...]).astype(bf16), wq_ref[...], preferred_element_type=f32)
    outs = []
    for hh in range(XATTN_HEADS):
        qh = q[:, hh * hd:(hh + 1) * hd].astype(bf16)
        kh = kv_ref[:, hh * hd:(hh + 1) * hd]
        vh = kv_ref[:, d + hh * hd:d + (hh + 1) * hd]
        sc = lax.dot_general(qh, kh, _NT, preferred_element_type=f32) * (hd ** -0.5)
        sc = sc - jnp.max(sc, axis=-1, keepdims=True)
        p = jnp.exp(sc)
        p = p / jnp.sum(p, axis=-1, keepdims=True)
        outs.append(jnp.dot(p.astype(bf16), vh, preferred_element_type=f32).astype(bf16))
    o = jnp.concatenate(outs, axis=-1)
    o_ref[...] = h + jnp.dot(o, wo_ref[...], preferred_element_type=f32)


def _xattn(h1, g, kv, wq, wo, s, n_mem, tm):
    n, d = h1.shape
    tps = s // tm
    return pl.pallas_call(
        _xattn_kernel,
        grid=(n // tm,),
        in_specs=[pl.BlockSpec((tm, d), lambda i: (i, 0)),
                  pl.BlockSpec((1, d), lambda i: (0, 0)),
                  pl.BlockSpec((n_mem, 2 * d), lambda i: (i // tps, 0)),
                  pl.BlockSpec(wq.shape, lambda i: (0, 0)),
                  pl.BlockSpec(wo.shape, lambda i: (0, 0))],
        out_specs=pl.BlockSpec((tm, d), lambda i: (i, 0)),
        out_shape=jax.ShapeDtypeStruct((n, d), f32),
        compiler_params=_params(),
        name="xattn",
    )(h1, g, kv, wq, wo)


def _topk_cols(sc, k, payload=None):
    r = sc.shape[0]
    ridx = lax.broadcasted_iota(i32, sc.shape, 0).astype(f32)
    vals, picks = [], []
    for _ in range(k):
        m = jnp.max(sc, axis=0, keepdims=True)
        im = jnp.min(jnp.where(sc == m, ridx, float(r)), axis=0, keepdims=True)
        hit = ridx == im
        vals.append(m)
        picks.append(im if payload is None else jnp.max(jnp.where(hit, payload, -1.0), axis=0, keepdims=True))
        sc = jnp.where(hit, -jnp.inf, sc)
    return jnp.concatenate(vals, axis=0), jnp.concatenate(picks, axis=0)


def _pair_candidates(k):
    return [(a, b) for a in range(k) for b in range(k) if (a + 1) * (b + 1) <= k]


def _select_kernel(h_ref, g_ref, wq_ref, keys_ref, a_ref, ids_ref, gate_ref):
    k = PEER_TOPK
    a = _rms(h_ref[...], g_ref[...])
    for c in range(a_ref.shape[1]):
        a_ref[:, c, :] = a[:, c * LANES:(c + 1) * LANES]
    q = jnp.dot(a.astype(bf16), wq_ref[...], preferred_element_type=f32)
    tm = q.shape[0]
    pairs = _pair_candidates(k)
    n_pad = -len(pairs) % SUBLANES
    id_rows, gate_rows = [], []
    for h in range(PEER_HEADS):
        tops = []
        for p in range(2):
            c0 = (h * 2 + p) * PEER_HALF
            qhp = q[:, c0:c0 + PEER_HALF].astype(bf16)
            st = lax.dot_general(keys_ref[h, p], qhp, _NT, preferred_element_type=f32)
            tops.append(_topk_cols(st, k))
        (v0, i0), (v1, i1) = tops
        i0 = i0 * float(PEER_N_KEYS)
        cand, cand_id = [], []
        for a_ in range(k):
            nb = sum(1 for (pa, _) in pairs if pa == a_)
            cand.append(v0[a_:a_ + 1, :] + v1[0:nb, :])
            cand_id.append(i0[a_:a_ + 1, :] + i1[0:nb, :])
        if n_pad:
            cand.append(jnp.full((n_pad, tm), -jnp.inf, f32))
            cand_id.append(jnp.zeros((n_pad, tm), f32))
        top_s, top_id = _topk_cols(jnp.concatenate(cand, axis=0), k, jnp.concatenate(cand_id, axis=0))
        id_rows.append(top_id.astype(i32))
        ex = jnp.exp(top_s - top_s[0:1, :])
        gate_rows.append(ex / jnp.sum(ex, axis=0, keepdims=True))
    ids_t = jnp.concatenate(id_rows, axis=0) * ROWS_PER_EXPERT
    for j in range(ids_ref.shape[0]):
        ids_ref[j] = ids_t[:, j * LANES:(j + 1) * LANES]
    gate_ref[...] = jnp.transpose(jnp.concatenate(gate_rows, axis=0))


def _select(h2, g, wq, keys, tm, first_block, n_blocks):
    d = h2.shape[1]
    n = n_blocks * tm
    return pl.pallas_call(
        _select_kernel,
        grid=(n_blocks,),
        in_specs=[pl.BlockSpec((tm, d), lambda i: (i + first_block, 0)),
                  pl.BlockSpec((1, d), lambda i: (0, 0)),
                  pl.BlockSpec(wq.shape, lambda i: (0, 0)),
                  pl.BlockSpec(keys.shape, lambda i: (0, 0, 0, 0))],
        out_specs=[pl.BlockSpec((tm, d // LANES, LANES), lambda i: (i, 0, 0)),
                   pl.BlockSpec((tm // LANES, N_SEL, LANES), lambda i: (i, 0, 0)),
                   pl.BlockSpec((tm, N_SEL), lambda i: (i, 0))],
        out_shape=[jax.ShapeDtypeStruct((n, d // LANES, LANES), f32),
                   jax.ShapeDtypeStruct((n // LANES, N_SEL, LANES), i32),
                   jax.ShapeDtypeStruct((n, N_SEL), f32)],
        compiler_params=_params(),
        name="peer_select",
    )(h2, g, wq, keys)


def _gelu_tanh(x):
    cdf = 0.5 * (1.0 + jnp.tanh(0.7978845608028654 * (x + 0.044715 * (x * x * x))))
    return x * cdf


def _pack_kernel(t_ref, o_ref):
    o_ref[...] = pltpu.bitcast(t_ref[...].astype(bf16), jnp.uint32)


def _pack_table(t):
    e, d = t.shape
    rows = e * d // LANES
    blk = min(rows, 8192)
    assert rows % blk == 0 and d == 2 * ROWS_PER_EXPERT * LANES
    return pl.pallas_call(
        _pack_kernel,
        grid=(rows // blk,),
        in_specs=[pl.BlockSpec((blk, LANES), lambda i: (i, 0))],
        out_specs=pl.BlockSpec((blk // 2, LANES), lambda i: (i, 0)),
        out_shape=jax.ShapeDtypeStruct((rows // 2, LANES), jnp.uint32),
        compiler_params=_params(),
        name="pack_table",
    )(t.reshape(rows, LANES))


def _expert_row(tab_ref, row0):
    words = tab_ref[pl.ds(pl.multiple_of(row0, ROWS_PER_EXPERT), ROWS_PER_EXPERT), :]
    return pltpu.bitcast(words, bf16).astype(f32)


def _token_id(ids_ref, e, t):
    return ids_ref.at[pl.ds(e * LANES, LANES)][t]


_FOLD_ORDER = (0, 4, 2, 6, 1, 5, 3, 7)


def _fold8(z):
    sub = lax.broadcasted_iota(i32, z[0].shape, 0)
    lo4 = sub < 4
    lo2 = (sub % 4) < 2
    even = (sub % 2) == 0
    c = [jnp.where(lo4, z[2 * k], z[2 * k + 1]) + pltpu.roll(jnp.where(lo4, z[2 * k + 1], z[2 * k]), 4, 0)
         for k in range(4)]
    e = [jnp.where(lo2, c[2 * k] + pltpu.roll(c[2 * k], 6, 0), c[2 * k + 1] + pltpu.roll(c[2 * k + 1], 2, 0))
         for k in range(2)]
    return jnp.where(even, e[0] + pltpu.roll(e[0], 7, 0), e[1] + pltpu.roll(e[1], 1, 0))


def _with_ids_tile(ids_hbm, bufs, sems, body):
    i = pl.program_id(0)
    n_steps = pl.num_programs(0)

    def tile_copy(tile, slot):
        return pltpu.make_async_copy(ids_hbm.at[tile], bufs[slot], sems.at[slot])

    @pl.when(i == 0)
    def _():
        tile_copy(0, 0).start()

    for slot in range(2):
        @pl.when(i % 2 == slot)
        def _():
            @pl.when(i + 1 < n_steps)
            def _():
                tile_copy(i + 1, 1 - slot).start()
            tile_copy(i, slot).wait()
            body(bufs[slot])


def _ids_scratch():
    return [pltpu.SMEM((N_SEL * LANES,), i32), pltpu.SMEM((N_SEL * LANES,), i32), pltpu.SemaphoreType.DMA((2,))]


def _peer_u_kernel(ids_hbm, x_ref, gate_ref, u_ref, w_ref, a_sc, ids0, ids1, sems):
    tt = x_ref.shape[0]
    lane = lax.broadcasted_iota(i32, (N_SEL, LANES), 1)

    def finish(part, t):
        col = jnp.sum(part, axis=1, keepdims=True)
        a_sc[...] = jnp.where(lane == t, col, a_sc[...])

    def body(ids_ref):
        def token(t, part_prev):
            finish(part_prev, t - 1)
            xt = x_ref[t]
            groups = []
            for g in range(N_SEL // SUBLANES):
                z = [_expert_row(u_ref, _token_id(ids_ref, g * SUBLANES + _FOLD_ORDER[k], t)) * xt
                     for k in range(SUBLANES)]
                groups.append(_fold8(z))
            return jnp.concatenate(groups, axis=0)

        a_sc[...] = jnp.zeros_like(a_sc)
        part = lax.fori_loop(0, tt, token, jnp.zeros((N_SEL, LANES), f32))
        finish(part, tt - 1)
        act = jnp.transpose(a_sc[...])[0:tt, :]
        w_ref[...] = gate_ref[...] * _gelu_tanh(act)

    _with_ids_tile(ids_hbm, (ids0, ids1), sems, body)


def _peer_u(ids, a3, gate, u_tab, tt):
    n_tiles = ids.shape[0]
    n = n_tiles * tt
    assert tt == LANES and ids.shape[1] == N_SEL * LANES
    return pl.pallas_call(
        _peer_u_kernel,
        grid=(n_tiles,),
        in_specs=[pl.BlockSpec(memory_space=pl.ANY),
                  pl.BlockSpec((tt, SUBLANES, LANES), lambda i: (i, 0, 0)),
                  pl.BlockSpec((tt, N_SEL), lambda i: (i, 0)),
                  pl.BlockSpec(u_tab.shape, lambda i: (0, 0), pipeline_mode=pl.Buffered(1))],
        out_specs=pl.BlockSpec((tt, N_SEL), lambda i: (i, 0)),
        out_shape=jax.ShapeDtypeStruct((n, N_SEL), f32),
        scratch_shapes=[pltpu.VMEM((N_SEL, LANES), f32)] + _ids_scratch(),
        compiler_params=_params(),
        name="peer_u",
    )(ids, a3, gate, u_tab)


def _peer_v_kernel(ids_hbm, w_ref, v_ref, h_ref, g_ref, o_ref, ids0, ids1, sems, p_sc):
    tt = o_ref.shape[0]
    n_acc = 4

    def lane_bcast_weights(t):
        return jnp.transpose(jnp.broadcast_to(w_ref[pl.ds(t, 1), :], (LANES, N_SEL)))

    def body(ids_ref):
        def token(t, wb):
            wb_next = lane_bcast_weights(jnp.minimum(t + 1, tt - 1))
            accs = [jnp.zeros((SUBLANES, LANES), f32) for _ in range(n_acc)]
            for e in range(N_SEL):
                wv = jnp.broadcast_to(wb[e:e + 1, :], (SUBLANES, LANES))
                accs[e % n_acc] = accs[e % n_acc] + _expert_row(v_ref, _token_id(ids_ref, e, t)) * wv
            p_sc[t] = (accs[0] + accs[1]) + (accs[2] + accs[3])
            return wb_next

        lax.fori_loop(0, tt, token, lane_bcast_weights(0))
        peer = jnp.concatenate([p_sc[:, c, :] for c in range(SUBLANES)], axis=1)
        o_ref[...] = _rms(h_ref[...] + peer, g_ref[...])

    _with_ids_tile(ids_hbm, (ids0, ids1), sems, body)


def _peer_v(ids, w, v_tab, h2, g, tt):
    n_tiles = ids.shape[0]
    n, d = h2.shape
    assert tt == LANES and ids.shape[1] == N_SEL * LANES and n >= n_tiles * tt and d == SUBLANES * LANES
    return pl.pallas_call(
        _peer_v_kernel,
        grid=(n_tiles,),
        in_specs=[pl.BlockSpec(memory_space=pl.ANY),
                  pl.BlockSpec((tt, N_SEL), lambda i: (i, 0)),
                  pl.BlockSpec(v_tab.shape, lambda i: (0, 0), pipeline_mode=pl.Buffered(1)),
                  pl.BlockSpec((tt, d), lambda i: (i, 0)),
                  pl.BlockSpec((1, d), lambda i: (0, 0))],
        out_specs=pl.BlockSpec((tt, d), lambda i: (i, 0)),
        out_shape=jax.ShapeDtypeStruct((n, d), f32),
        scratch_shapes=_ids_scratch() + [pltpu.VMEM((tt, SUBLANES, LANES), f32)],
        compiler_params=_params(),
        name="peer_v",
    )(ids, w, v_tab, h2, g)


SC_WORKERS = 32
SC_LANES = 16
SC_TILE_SHARE_64THS = 27


def _peer_sc(ids_tok, x3, gate, u_words, v_words):
    from jax.experimental.pallas import tpu_sc as plsc
    n_sc = ids_tok.shape[0]
    words = v_words.shape[1]
    d = 2 * words
    tpw = n_sc // SC_WORKERS
    hrows = N_SEL // 2
    assert n_sc % SC_WORKERS == 0 and tpw % 2 == 0 and words == ROWS_PER_EXPERT * LANES
    half = words // 2
    nvec = half // SC_LANES
    mesh = plsc.VectorSubcoreMesh(core_axis_name="c", subcore_axis_name="s")

    def halves(wd):
        return (lax.bitcast_convert_type(wd << 16, f32), lax.bitcast_convert_type(wd & jnp.int32(-65536), f32))

    def body(u_hbm, v_hbm, ids_hbm, x_hbm, g_hbm, out_hbm,
             idx_a0, idx_b0, g_0, x_0, idx_a1, idx_b1, g_1, x_1, w_v, rows_a, rows_b, part_v, out_v, sem_a, sem_b, sem_s):
        wid = lax.axis_index("s") * 2 + lax.axis_index("c")
        lane = lax.broadcasted_iota(i32, (SC_LANES,), 0)
        first = wid * tpw
        last = first + tpw - 1
        inputs = ((idx_a0, idx_b0, g_0, x_0), (idx_a1, idx_b1, g_1, x_1))
        row_bufs = ((rows_a, sem_a), (rows_b, sem_b))

        def input_copies(t, par):
            idx_a, idx_b, g_v, x_v = inputs[par]
            return (pltpu.make_async_copy(ids_hbm.at[t, pl.ds(0, hrows)], idx_a, sem_s),
                    pltpu.make_async_copy(ids_hbm.at[t, pl.ds(hrows, hrows)], idx_b, sem_s),
                    pltpu.make_async_copy(g_hbm.at[t], g_v, sem_s),
                    pltpu.make_async_copy(x_hbm.at[t], x_v, sem_s))

        def gather(tab_hbm, par, h):
            rows, sem = row_bufs[h]
            return pltpu.make_async_copy(tab_hbm.at[inputs[par][h]], rows, sem)

        def dots(par, h):
            rows, x_v = row_bufs[h][0], inputs[par][3]
            for p in range(2):
                xs = []
                for j in range(nvec):
                    o = p * half + j * SC_LANES
                    r, l0 = o // LANES, o % LANES
                    xs.append((x_v[2 * r, pl.ds(l0, SC_LANES)], x_v[2 * r + 1, pl.ds(l0, SC_LANES)]))

                def dot_pair(e, c, p=p, xs=xs):
                    acc = None
                    for j in range(nvec):
                        lo, hi = halves(rows[e, pl.ds(p * half + j * SC_LANES, SC_LANES)])
                        term = lo * xs[j][0] + hi * xs[j][1]
                        acc = term if acc is None else acc + term
                    row = h * hrows + e
                    part_v[row, :] = acc if p == 0 else part_v[row, :] + acc
                    return c

                lax.fori_loop(0, hrows, dot_pair, 0)

        def weights(par):
            g_v = inputs[par][2]
            for k in range(N_SEL // SC_LANES):
                act = jnp.zeros((SC_LANES,), f32)
                for q in range(SC_LANES):
                    act = jnp.where(lane == q, jnp.sum(part_v[k * SC_LANES + q, :]), act)
                inner = 0.7978845608028654 * (act + 0.044715 * (act * act * act))
                tanh = 1.0 - 2.0 / (jnp.exp(2.0 * inner) + 1.0)
                w_v[pl.ds(k * SC_LANES, SC_LANES)] = g_v[pl.ds(k * SC_LANES, SC_LANES)] * (act * (0.5 * (1.0 + tanh)))

        def accumulate(h):
            rows = row_bufs[h][0]
            for p in range(2):
                slots = []
                for j in range(nvec):
                    o = p * half + j * SC_LANES
                    r, l0 = o // LANES, o % LANES
                    slots += [2 * LANES * r + l0, 2 * LANES * r + LANES + l0]

                def acc_pair(e, accs, p=p):
                    row = h * hrows + e
                    wvec = w_v[pl.ds(pl.multiple_of((row // SC_LANES) * SC_LANES, SC_LANES), SC_LANES)]
                    wr = jnp.sum(jnp.where(lane == row % SC_LANES, wvec, 0.0))
                    out = []
                    for j in range(nvec):
                        lo, hi = halves(rows[e, pl.ds(p * half + j * SC_LANES, SC_LANES)])
                        out.append(accs[2 * j] + lo * wr)
                        out.append(accs[2 * j + 1] + hi * wr)
                    return tuple(out)

                init = tuple(jnp.zeros((SC_LANES,), f32) if h == 0 else out_v[pl.ds(o, SC_LANES)] for o in slots)
                accs = lax.fori_loop(0, hrows, acc_pair, init)
                for o, acc in zip(slots, accs):
                    out_v[pl.ds(o, SC_LANES)] = acc

        def token(t, par):
            nxt = jnp.minimum(t + 1, last)
            for c in input_copies(nxt, 1 - par):
                c.start()
            gather(u_hbm, par, 0).wait()
            dots(par, 0)
            gather(v_hbm, par, 0).start()
            gather(u_hbm, par, 1).wait()
            dots(par, 1)
            gather(v_hbm, par, 1).start()
            weights(par)
            for c in input_copies(nxt, 1 - par):
                c.wait()
            gather(v_hbm, par, 0).wait()
            accumulate(0)
            gather(u_hbm, 1 - par, 0).start()
            gather(v_hbm, par, 1).wait()
            accumulate(1)
            gather(u_hbm, 1 - par, 1).start()
            pltpu.sync_copy(out_v, out_hbm.at[t])

        for c in input_copies(first, 0):
            c.start()
        for c in input_copies(first, 0):
            c.wait()
        gather(u_hbm, 0, 0).start()
        gather(u_hbm, 0, 1).start()

        def token_pair(i, carry):
            token(first + 2 * i, 0)
            token(first + 2 * i + 1, 1)
            return carry

        lax.fori_loop(0, tpw // 2, token_pair, 0)
        gather(u_hbm, 0, 0).wait()
        gather(u_hbm, 0, 1).wait()

    small = [pltpu.VMEM((hrows,), i32), pltpu.VMEM((hrows,), i32), pltpu.VMEM((N_SEL,), f32),
             pltpu.VMEM((SUBLANES, LANES), f32)]
    return pl.kernel(
        body,
        out_type=jax.ShapeDtypeStruct((n_sc, d), f32),
        mesh=mesh,
        scratch_types=small + small + [pltpu.VMEM((N_SEL,), f32), pltpu.VMEM((hrows, words), i32),
                                       pltpu.VMEM((hrows, words), i32), pltpu.VMEM((N_SEL, SC_LANES), f32),
                                       pltpu.VMEM((d,), f32), pltpu.SemaphoreType.DMA, pltpu.SemaphoreType.DMA,
                                       pltpu.SemaphoreType.DMA],
        compiler_params=pltpu.CompilerParams(needs_layout_passes=False),
        name="peer_sc",
    )(u_words, v_words, ids_tok, x3, gate)


def _final_tail_kernel(out_in_ref, h_ref, p_ref, g_ref, o_ref):
    del out_in_ref
    o_ref[...] = _rms(h_ref[...] + p_ref[...], g_ref[...])


def _final_tail(out, h2, peer_tail, g, tm, first_tile):
    n, d = h2.shape
    tail_tiles = peer_tail.shape[0] // tm
    return pl.pallas_call(
        _final_tail_kernel,
        grid=(tail_tiles,),
        in_specs=[pl.BlockSpec(memory_space=pl.ANY),
                  pl.BlockSpec((tm, d), lambda i: (i + first_tile, 0)),
                  pl.BlockSpec((tm, d), lambda i: (i, 0)),
                  pl.BlockSpec((1, d), lambda i: (0, 0))],
        out_specs=pl.BlockSpec((tm, d), lambda i: (i + first_tile, 0)),
        out_shape=jax.ShapeDtypeStruct((n, d), f32),
        input_output_aliases={0: 0},
        compiler_params=_params(),
        name="final_tail",
    )(out, h2, peer_tail, g)


def _rope_tables(s):
    half = HEAD_DIM // 2
    inv = ROPE_THETA ** (-jnp.arange(half, dtype=f32) / half)
    ang = jnp.arange(s, dtype=f32)[:, None] * inv[None, :]
    cos, sin = jnp.cos(ang), jnp.sin(ang)
    reps = MOBA_GROUP_LANES // HEAD_DIM
    cos_t = jnp.tile(jnp.concatenate([cos, cos], axis=1), (1, reps))
    sin_t = jnp.tile(jnp.concatenate([-sin, sin], axis=1), (1, reps))
    return cos_t, sin_t


def _row_tile(s, want):
    t = min(want, s)
    while s % t:
        t //= 2
    assert t % SUBLANES == 0
    return t


def kernel(x, mem, g_mix, w_in, w_conv, w_conv_out, w_attn_out, w_merge, g_xattn, g_mem,
           w_xq, w_xkv, w_xo, g_ffn, w_pq, peer_sub_keys, peer_u, peer_v, g_final):
    b, s, d = x.shape
    n = b * s
    n_mem = mem.shape[1]
    assert w_in.shape[0] == 1, "single-layer trunk only"
    assert d == SUBLANES * LANES and s % MOBA_BLOCK == 0
    n_exp = peer_u.shape[1]
    cos_t, sin_t = _rope_tables(s)
    tm = _row_tile(s, 512)
    h = x.reshape(n, d)
    mem2 = mem.reshape(b * n_mem, d)
    l = 0
    proj = _inproj(h, g_mix[l][None, :], w_in[l].astype(bf16), tm)
    o = _moba(proj, cos_t, sin_t, b, s)
    h1 = _mix(proj, o, h, w_conv[l], w_conv_out[l].astype(bf16), w_attn_out[l].astype(bf16),
              w_merge[l].astype(bf16), s, tm)
    kv = _memkv(mem2, g_mem[l][None, :], w_xkv[l].astype(bf16), n_mem)
    h2 = _xattn(h1, g_xattn[l][None, :], kv, w_xq[l].astype(bf16), w_xo[l].astype(bf16), s, n_mem, tm)
    u_tab = _pack_table(peer_u[l])
    v_tab = _pack_table(peer_v[l])
    tt = _row_tile(s, LANES)
    tsel = _row_tile(s, 256)
    n_tiles = n // tt
    per_sel = tsel // tt
    sc_tiles = ((n_tiles * SC_TILE_SHARE_64THS) // 64) // per_sel * per_sel
    tc_tiles = n_tiles - sc_tiles
    sel_args = (h2, g_ffn[l][None, :], w_pq[l].astype(bf16), peer_sub_keys[l].astype(bf16), tsel)
    if sc_tiles:
        a3_s, ids_s, gate_s = _select(*sel_args, tc_tiles // per_sel, sc_tiles // per_sel)
        as_rows = lambda tab: lax.bitcast_convert_type(tab, i32).reshape(n_exp, ROWS_PER_EXPERT * LANES)
        ids_tok = (ids_s // ROWS_PER_EXPERT).transpose(0, 2, 1).reshape(sc_tiles * tt, N_SEL)
        peer_sc = _peer_sc(ids_tok, a3_s, gate_s, as_rows(u_tab), as_rows(v_tab))
    a3, ids, gate = _select(*sel_args, 0, tc_tiles // per_sel)
    ids = ids.reshape(tc_tiles, N_SEL * LANES)
    w = _peer_u(ids, a3, gate, u_tab, tt)
    out = _peer_v(ids, w, v_tab, h2, g_final[None, :], tt)
    if sc_tiles:
        out = _final_tail(out, h2, peer_sc, g_final[None, :], tt, tc_tiles)
    return out.reshape(b, s, d)
```

```python
import functools

import jax
import jax.numpy as jnp
from jax import lax
from jax.experimental import pallas as pl
from jax.experimental.pallas import tpu as pltpu

f32 = jnp.float32
bf16 = jnp.bfloat16
i32 = jnp.int32

EPS = 1e-6
MASK_VALUE = -1e30
ROPE_THETA = 10000.0

XATTN_HEADS = 4
ATTN_HEADS = 8
HEAD_DIM = 64
MOBA_BLOCK = 256
MOBA_TOPK = 3
MOBA_GROUP_LANES = 256
PEER_HEADS = 8
PEER_N_KEYS = 128
PEER_HALF = 128
PEER_TOPK = 16
N_SEL = PEER_HEADS * PEER_TOPK
ROWS_PER_EXPERT = 4

LANES = 128
SUBLANES = 8
VMEM_LIMIT = 56 * 1024 * 1024

_NT = (((1,), (1,)), ((), ()))


def _params(n_axes=1, vmem=VMEM_LIMIT):
    return pltpu.CompilerParams(dimension_semantics=("arbitrary",) * n_axes, vmem_limit_bytes=vmem)


def _rms(x, g):
    return x * lax.rsqrt(jnp.mean(x * x, axis=-1, keepdims=True) + EPS) * g


def _inproj_kernel(x_ref, g_ref, w_ref, o_ref, *, chunk):
    a = _rms(x_ref[...], g_ref[...]).astype(bf16)
    for c in range(o_ref.shape[1] // chunk):
        o_ref[:, c * chunk:(c + 1) * chunk] = jnp.dot(
            a, w_ref[:, c * chunk:(c + 1) * chunk], preferred_element_type=f32)


def _inproj(x2, g, w, tm):
    n, d = x2.shape
    width = w.shape[1]
    return pl.pallas_call(
        functools.partial(_inproj_kernel, chunk=width // 4),
        grid=(n // tm,),
        in_specs=[pl.BlockSpec((tm, d), lambda i: (i, 0)),
                  pl.BlockSpec((1, d), lambda i: (0, 0)),
                  pl.BlockSpec((d, width), lambda i: (0, 0), pipeline_mode=pl.Buffered(1))],
        out_specs=pl.BlockSpec((tm, width), lambda i: (i, 0)),
        out_shape=jax.ShapeDtypeStruct((n, width), f32),
        compiler_params=_params(),
        name="inproj",
    )(x2, g, w)


def _moba_kernel(q_ref, k_ref, v_ref, cos_ref, sin_ref, o_ref,
                 qb_sc, kb_sc, vt_sc, bias_sc, ot_sc, *, nb, topk):
    blk = MOBA_BLOCK
    s, gw = q_ref.shape
    lane = lax.broadcasted_iota(i32, (s, gw), 1)
    first_half = (lane % HEAD_DIM) < (HEAD_DIM // 2)
    cos = cos_ref[...]
    sin = sin_ref[...]

    def rope(t):
        partner = jnp.where(first_half, pltpu.roll(t, gw - HEAD_DIM // 2, 1),
                            pltpu.roll(t, HEAD_DIM // 2, 1))
        return t * cos + partner * sin

    q2 = rope(q_ref[...])
    k2 = rope(k_ref[...])
    v2 = v_ref[...]
    scale = HEAD_DIM ** -0.5
    blk_of_q = lax.broadcasted_iota(i32, (nb, s), 1) // blk
    m_iota = lax.broadcasted_iota(i32, (nb, s), 0)
    past = m_iota < blk_of_q
    kpos = lax.broadcasted_iota(i32, (blk, blk), 0)
    qpos = lax.broadcasted_iota(i32, (blk, blk), 1)
    causal = kpos <= qpos

    heads = range(gw // HEAD_DIM)
    for hh in heads:
        qh = q2[:, hh * HEAD_DIM:(hh + 1) * HEAD_DIM]
        kh = k2[:, hh * HEAD_DIM:(hh + 1) * HEAD_DIM]
        vh = v2[:, hh * HEAD_DIM:(hh + 1) * HEAD_DIM]
        kbar = jnp.mean(kh.reshape(nb, blk, HEAD_DIM), axis=1)
        gt = lax.dot_general(kbar, qh, _NT, precision=lax.Precision.HIGHEST,
                             preferred_element_type=f32)
        rows = []
        for n in range(nb):
            gn = gt[n:n + 1, :]
            beats = past & ((gt > gn) | ((gt == gn) & (m_iota < n)))
            cnt = jnp.sum(beats.astype(i32), axis=0, keepdims=True)
            sel = (blk_of_q[0:1, :] > n) & (cnt < topk)
            rows.append(jnp.where(sel, 0.0, MASK_VALUE).astype(f32))
        bias = jnp.concatenate(rows, axis=0)
        vt = jnp.transpose(vh).astype(bf16)
        for i in range(nb):
            bias_sc[hh, i] = bias[:, i * blk:(i + 1) * blk]
            vt_sc[hh, i] = vt[:, i * blk:(i + 1) * blk]
        qb_sc[hh] = qh.astype(bf16)
        kb_sc[hh] = kh.astype(bf16)

    def q_tile(i, carry):
        q0 = pl.multiple_of(i * blk, blk)
        diag = [lax.dot_general(kb_sc[hh, pl.ds(q0, blk), :], qb_sc[hh, pl.ds(q0, blk), :], _NT,
                                preferred_element_type=f32) for hh in heads]
        probs, stats = [], []
        for hh in heads:
            sd = jnp.where(causal, diag[hh] * scale, MASK_VALUE)
            m0 = jnp.max(sd, axis=0, keepdims=True)
            p0 = jnp.exp(sd - m0)
            probs.append(p0.astype(bf16))
            stats.append((m0, jnp.sum(p0, axis=0, keepdims=True)))
        init = [stats[hh] + (jnp.dot(vt_sc[hh, i], probs[hh], preferred_element_type=f32),)
                for hh in heads]

        def kv_step(j, st):
            k0 = pl.multiple_of(j * blk, blk)
            scores = [lax.dot_general(kb_sc[hh, pl.ds(k0, blk), :], qb_sc[hh, pl.ds(q0, blk), :], _NT,
                                      preferred_element_type=f32) for hh in heads]
            probs, stats = [], []
            for hh in heads:
                m, l, _ = st[hh]
                sj = scores[hh] * scale + bias_sc[hh, i, pl.ds(j, 1), :]
                m_new = jnp.maximum(m, jnp.max(sj, axis=0, keepdims=True))
                alpha = jnp.exp(m - m_new)
                p = jnp.exp(sj - m_new)
                probs.append(p.astype(bf16))
                stats.append((m_new, alpha * l + jnp.sum(p, axis=0, keepdims=True), alpha))
            out = []
            for hh in heads:
                m_new, l, alpha = stats[hh]
                acc = alpha * st[hh][2] + jnp.dot(vt_sc[hh, j], probs[hh], preferred_element_type=f32)
                out.append((m_new, l, acc))
            return tuple(out)

        final = lax.fori_loop(0, i, kv_step, tuple(init))
        for hh in heads:
            _, l, acc = final[hh]
            ot_sc[hh, i] = acc * (1.0 / l)
        return carry

    lax.fori_loop(0, nb, q_tile, 0)
    for hh in heads:
        for i in range(nb):
            o_ref[i * blk:(i + 1) * blk, hh * HEAD_DIM:(hh + 1) * HEAD_DIM] = jnp.transpose(ot_sc[hh, i])


def _moba(proj, cos_t, sin_t, b, s):
    n = b * s
    nb = s // MOBA_BLOCK
    topk = min(MOBA_TOPK, nb - 1)
    aw = ATTN_HEADS * HEAD_DIM
    gw = MOBA_GROUP_LANES
    hp = aw // gw
    nh = gw // HEAD_DIM
    qc, kc, vc = 3 * aw // gw, 4 * aw // gw, 5 * aw // gw
    return pl.pallas_call(
        functools.partial(_moba_kernel, nb=nb, topk=topk),
        grid=(b, hp),
        in_specs=[pl.BlockSpec((s, gw), lambda bi, h: (bi, qc + h)),
                  pl.BlockSpec((s, gw), lambda bi, h: (bi, kc + h)),
                  pl.BlockSpec((s, gw), lambda bi, h: (bi, vc + h)),
                  pl.BlockSpec((s, gw), lambda bi, h: (0, 0)),
                  pl.BlockSpec((s, gw), lambda bi, h: (0, 0))],
        out_specs=pl.BlockSpec((s, gw), lambda bi, h: (bi, h)),
        out_shape=jax.ShapeDtypeStruct((n, aw), f32),
        scratch_shapes=[pltpu.VMEM((nh, s, HEAD_DIM), bf16),
                        pltpu.VMEM((nh, s, HEAD_DIM), bf16),
                        pltpu.VMEM((nh, nb, HEAD_DIM, MOBA_BLOCK), bf16),
                        pltpu.VMEM((nh, nb, nb, MOBA_BLOCK), f32),
                        pltpu.VMEM((nh, nb, HEAD_DIM, MOBA_BLOCK), f32)],
        compiler_params=_params(2),
        name="moba",
    )(proj, proj, proj, cos_t, sin_t)


def _mix_kernel(xin_ref, bg_ref, cg_ref, pxin_ref, pcg_ref, gc_ref, ga_ref, o_ref, x_ref,
                wc_ref, wco_ref, wao_ref, wm_ref, h_ref, *, tiles_per_seq):
    tm = xin_ref.shape[0]
    i = pl.program_id(0)
    keep_prev = jnp.where(i % tiles_per_seq == 0, 0.0, 1.0).astype(f32)
    u = cg_ref[...] * xin_ref[...]
    pu = pcg_ref[...] * pxin_ref[...] * keep_prev
    rows = lax.broadcasted_iota(i32, u.shape, 0)
    u1 = jnp.where(rows == 0, pu[SUBLANES - 1:SUBLANES, :], pltpu.roll(u, 1, 0))
    u2 = jnp.where(rows == 0, pu[SUBLANES - 2:SUBLANES - 1, :],
                   jnp.where(rows == 1, pu[SUBLANES - 1:SUBLANES, :], pltpu.roll(u, 2, 0)))
    y = wc_ref[0:1, :] * u2 + wc_ref[1:2, :] * u1 + wc_ref[2:3, :] * u
    y_conv = jnp.dot((bg_ref[...] * y).astype(bf16), wco_ref[...], preferred_element_type=f32)
    y_attn = jnp.dot(o_ref[...].astype(bf16), wao_ref[...], preferred_element_type=f32)
    merged = jax.nn.sigmoid(gc_ref[...]) * y_conv + jax.nn.sigmoid(ga_ref[...]) * y_attn
    h_ref[...] = x_ref[...] + jnp.dot(merged.astype(bf16), wm_ref[...], preferred_element_type=f32)


def _mix(proj, o, x2, w_conv, wco, wao, wm, s, tm):
    n, d = x2.shape
    cw = wco.shape[0]
    aw = wao.shape[0]
    rb = tm // SUBLANES
    prev = lambda col: (lambda i: (jnp.maximum(i * rb - 1, 0), col))
    gcol = 3 * cw + 3 * aw
    return pl.pallas_call(
        functools.partial(_mix_kernel, tiles_per_seq=s // tm),
        grid=(n // tm,),
        in_specs=[pl.BlockSpec((tm, cw), lambda i: (i, 0)),
                  pl.BlockSpec((tm, cw), lambda i: (i, 1)),
                  pl.BlockSpec((tm, cw), lambda i: (i, 2)),
                  pl.BlockSpec((SUBLANES, cw), prev(0)),
                  pl.BlockSpec((SUBLANES, cw), prev(2)),
                  pl.BlockSpec((tm, d), lambda i: (i, gcol // d)),
                  pl.BlockSpec((tm, d), lambda i: (i, gcol // d + 1)),
                  pl.BlockSpec((tm, aw), lambda i: (i, 0)),
                  pl.BlockSpec((tm, d), lambda i: (i, 0)),
                  pl.BlockSpec(w_conv.shape, lambda i: (0, 0)),
                  pl.BlockSpec(wco.shape, lambda i: (0, 0)),
                  pl.BlockSpec(wao.shape, lambda i: (0, 0)),
                  pl.BlockSpec(wm.shape, lambda i: (0, 0))],
        out_specs=pl.BlockSpec((tm, d), lambda i: (i, 0)),
        out_shape=jax.ShapeDtypeStruct((n, d), f32),
        compiler_params=_params(),
        name="mix",
    )(proj, proj, proj, proj, proj, proj, proj, o, x2, w_conv, wco, wao, wm)


def _memkv_kernel(m_ref, g_ref, w_ref, o_ref):
    a = _rms(m_ref[...], g_ref[...]).astype(bf16)
    o_ref[...] = jnp.dot(a, w_ref[...], preferred_element_type=f32).astype(bf16)


def _memkv(mem2, g, w, n_mem):
    nm, d = mem2.shape
    return pl.pallas_call(
        _memkv_kernel,
        grid=(nm // n_mem,),
        in_specs=[pl.BlockSpec((n_mem, d), lambda i: (i, 0)),
                  pl.BlockSpec((1, d), lambda i: (0, 0)),
                  pl.BlockSpec(w.shape, lambda i: (0, 0))],
        out_specs=pl.BlockSpec((n_mem, w.shape[1]), lambda i: (i, 0)),
        out_shape=jax.ShapeDtypeStruct((nm, w.shape[1]), bf16),
        compiler_params=_params(),
        name="memkv",
    )(mem2, g, w)


def _xattn_kernel(h_ref, g_ref, kv_ref, wq_ref, wo_ref, o_ref):
    h = h_ref[...]
    d = h.shape[1]
    hd = d // XATTN_HEADS
    q = jnp.dot(_rms(h, g_ref[...]).astype(bf16), wq_ref[...], preferred_element_type=f32)
    outs = []
    for hh in range(XATTN_HEADS):
        qh = q[:, hh * hd:(hh + 1) * hd].astype(bf16)
        kh = kv_ref[:, hh * hd:(hh + 1) * hd]
        vh = kv_ref[:, d + hh * hd:d + (hh + 1) * hd]
        sc = lax.dot_general(qh, kh, _NT, preferred_element_type=f32) * (hd ** -0.5)
        sc = sc - jnp.max(sc, axis=-1, keepdims=True)
        p = jnp.exp(sc)
        p = p / jnp.sum(p, axis=-1, keepdims=True)
        outs.append(jnp.dot(p.astype(bf16), vh, preferred_element_type=f32).astype(bf16))
    o = jnp.concatenate(outs, axis=-1)
    o_ref[...] = h + jnp.dot(o, wo_ref[...], preferred_element_type=f32)


def _xattn(h1, g, kv, wq, wo, s, n_mem, tm):
    n, d = h1.shape
    tps = s // tm
    return pl.pallas_call(
        _xattn_kernel,
        grid=(n // tm,),
        in_specs=[pl.BlockSpec((tm, d), lambda i: (i, 0)),
                  pl.BlockSpec((1, d), lambda i: (0, 0)),
                  pl.BlockSpec((n_mem, 2 * d), lambda i: (i // tps, 0)),
                  pl.BlockSpec(wq.shape, lambda i: (0, 0)),
                  pl.BlockSpec(wo.shape, lambda i: (0, 0))],
        out_specs=pl.BlockSpec((tm, d), lambda i: (i, 0)),
        out_shape=jax.ShapeDtypeStruct((n, d), f32),
        compiler_params=_params(),
        name="xattn",
    )(h1, g, kv, wq, wo)


def _topk_cols(sc, k, payload=None):
    r = sc.shape[0]
    ridx = lax.broadcasted_iota(i32, sc.shape, 0).astype(f32)
    vals, picks = [], []
    for _ in range(k):
        m = jnp.max(sc, axis=0, keepdims=True)
        im = jnp.min(jnp.where(sc == m, ridx, float(r)), axis=0, keepdims=True)
        hit = ridx == im
        vals.append(m)
        picks.append(im if payload is None else jnp.max(jnp.where(hit, payload, -1.0), axis=0, keepdims=True))
        sc = jnp.where(hit, -jnp.inf, sc)
    return jnp.concatenate(vals, axis=0), jnp.concatenate(picks, axis=0)


def _pair_candidates(k):
    return [(a, b) for a in range(k) for b in range(k) if (a + 1) * (b + 1) <= k]


def _select_kernel(h_ref, g_ref, wq_ref, keys_ref, a_ref, ids_ref, gate_ref):
    k = PEER_TOPK
    a = _rms(h_ref[...], g_ref[...])
    for c in range(a_ref.shape[1]):
        a_ref[:, c, :] = a[:, c * LANES:(c + 1) * LANES]
    q = jnp.dot(a.astype(bf16), wq_ref[...], preferred_element_type=f32)
    tm = q.shape[0]
    pairs = _pair_candidates(k)
    n_pad = -len(pairs) % SUBLANES
    id_rows, gate_rows = [], []
    for h in range(PEER_HEADS):
        tops = []
        for p in range(2):
            c0 = (h * 2 + p) * PEER_HALF
            qhp = q[:, c0:c0 + PEER_HALF].astype(bf16)
            st = lax.dot_general(keys_ref[h, p], qhp, _NT, preferred_element_type=f32)
            tops.append(_topk_cols(st, k))
        (v0, i0), (v1, i1) = tops
        i0 = i0 * float(PEER_N_KEYS)
        cand, cand_id = [], []
        for a_ in range(k):
            nb = sum(1 for (pa, _) in pairs if pa == a_)
            cand.append(v0[a_:a_ + 1, :] + v1[0:nb, :])
            cand_id.append(i0[a_:a_ + 1, :] + i1[0:nb, :])
        if n_pad:
            cand.append(jnp.full((n_pad, tm), -jnp.inf, f32))
            cand_id.append(jnp.zeros((n_pad, tm), f32))
        top_s, top_id = _topk_cols(jnp.concatenate(cand, axis=0), k, jnp.concatenate(cand_id, axis=0))
        id_rows.append(top_id.astype(i32))
        ex = jnp.exp(top_s - top_s[0:1, :])
        gate_rows.append(ex / jnp.sum(ex, axis=0, keepdims=True))
    ids_t = jnp.concatenate(id_rows, axis=0) * ROWS_PER_EXPERT
    for j in range(ids_ref.shape[0]):
        ids_ref[j] = ids_t[:, j * LANES:(j + 1) * LANES]
    gate_ref[...] = jnp.transpose(jnp.concatenate(gate_rows, axis=0))


def _select(h2, g, wq, keys, tm, first_block, n_blocks):
    d = h2.shape[1]
    n = n_blocks * tm
    return pl.pallas_call(
        _select_kernel,
        grid=(n_blocks,),
        in_specs=[pl.BlockSpec((tm, d), lambda i: (i + first_block, 0)),
                  pl.BlockSpec((1, d), lambda i: (0, 0)),
                  pl.BlockSpec(wq.shape, lambda i: (0, 0)),
                  pl.BlockSpec(keys.shape, lambda i: (0, 0, 0, 0))],
        out_specs=[pl.BlockSpec((tm, d // LANES, LANES), lambda i: (i, 0, 0)),
                   pl.BlockSpec((tm // LANES, N_SEL, LANES), lambda i: (i, 0, 0)),
                   pl.BlockSpec((tm, N_SEL), lambda i: (i, 0))],
        out_shape=[jax.ShapeDtypeStruct((n, d // LANES, LANES), f32),
                   jax.ShapeDtypeStruct((n // LANES, N_SEL, LANES), i32),
                   jax.ShapeDtypeStruct((n, N_SEL), f32)],
        compiler_params=_params(),
        name="peer_select",
    )(h2, g, wq, keys)


def _gelu_tanh(x):
    cdf = 0.5 * (1.0 + jnp.tanh(0.7978845608028654 * (x + 0.044715 * (x * x * x))))
    return x * cdf


def _pack_kernel(t_ref, o_ref):
    o_ref[...] = pltpu.bitcast(t_ref[...].astype(bf16), jnp.uint32)


def _pack_table(t):
    e, d = t.shape
    rows = e * d // LANES
    blk = min(rows, 8192)
    assert rows % blk == 0 and d == 2 * ROWS_PER_EXPERT * LANES
    return pl.pallas_call(
        _pack_kernel,
        grid=(rows // blk,),
        in_specs=[pl.BlockSpec((blk, LANES), lambda i: (i, 0))],
        out_specs=pl.BlockSpec((blk // 2, LANES), lambda i: (i, 0)),
        out_shape=jax.ShapeDtypeStruct((rows // 2, LANES), jnp.uint32),
        compiler_params=_params(),
        name="pack_table",
    )(t.reshape(rows, LANES))


def _expert_row(tab_ref, row0):
    words = tab_ref[pl.ds(pl.multiple_of(row0, ROWS_PER_EXPERT), ROWS_PER_EXPERT), :]
    return pltpu.bitcast(words, bf16).astype(f32)


def _token_id(ids_ref, e, t):
    return ids_ref.at[pl.ds(e * LANES, LANES)][t]


_FOLD_ORDER = (0, 4, 2, 6, 1, 5, 3, 7)


def _fold8(z):
    sub = lax.broadcasted_iota(i32, z[0].shape, 0)
    lo4 = sub < 4
    lo2 = (sub % 4) < 2
    even = (sub % 2) == 0
    c = [jnp.where(lo4, z[2 * k], z[2 * k + 1]) + pltpu.roll(jnp.where(lo4, z[2 * k + 1], z[2 * k]), 4, 0)
         for k in range(4)]
    e = [jnp.where(lo2, c[2 * k] + pltpu.roll(c[2 * k], 6, 0), c[2 * k + 1] + pltpu.roll(c[2 * k + 1], 2, 0))
         for k in range(2)]
    return jnp.where(even, e[0] + pltpu.roll(e[0], 7, 0), e[1] + pltpu.roll(e[1], 1, 0))


def _with_ids_tile(ids_hbm, bufs, sems, body):
    i = pl.program_id(0)
    n_steps = pl.num_programs(0)

    def tile_copy(tile, slot):
        return pltpu.make_async_copy(ids_hbm.at[tile], bufs[slot], sems.at[slot])

    @pl.when(i == 0)
    def _():
        tile_copy(0, 0).start()

    for slot in range(2):
        @pl.when(i % 2 == slot)
        def _():
            @pl.when(i + 1 < n_steps)
            def _():
                tile_copy(i + 1, 1 - slot).start()
            tile_copy(i, slot).wait()
            body(bufs[slot])


def _ids_scratch():
    return [pltpu.SMEM((N_SEL * LANES,), i32), pltpu.SMEM((N_SEL * LANES,), i32), pltpu.SemaphoreType.DMA((2,))]


def _peer_u_kernel(ids_hbm, x_ref, gate_ref, u_ref, w_ref, a_sc, ids0, ids1, sems):
    tt = x_ref.shape[0]
    lane = lax.broadcasted_iota(i32, (N_SEL, LANES), 1)

    def finish(part, t):
        col = jnp.sum(part, axis=1, keepdims=True)
        a_sc[...] = jnp.where(lane == t, col, a_sc[...])

    def body(ids_ref):
        def token(t, part_prev):
            finish(part_prev, t - 1)
            xt = x_ref[t]
            groups = []
            for g in range(N_SEL // SUBLANES):
                z = [_expert_row(u_ref, _token_id(ids_ref, g * SUBLANES + _FOLD_ORDER[k], t)) * xt
                     for k in range(SUBLANES)]
                groups.append(_fold8(z))
            return jnp.concatenate(groups, axis=0)

        a_sc[...] = jnp.zeros_like(a_sc)
        part = lax.fori_loop(0, tt, token, jnp.zeros((N_SEL, LANES), f32))
        finish(part, tt - 1)
        act = jnp.transpose(a_sc[...])[0:tt, :]
        w_ref[...] = gate_ref[...] * _gelu_tanh(act)

    _with_ids_tile(ids_hbm, (ids0, ids1), sems, body)


def _peer_u(ids, a3, gate, u_tab, tt):
    n_tiles = ids.shape[0]
    n = n_tiles * tt
    assert tt == LANES and ids.shape[1] == N_SEL * LANES
    return pl.pallas_call(
        _peer_u_kernel,
        grid=(n_tiles,),
        in_specs=[pl.BlockSpec(memory_space=pl.ANY),
                  pl.BlockSpec((tt, SUBLANES, LANES), lambda i: (i, 0, 0)),
                  pl.BlockSpec((tt, N_SEL), lambda i: (i, 0)),
                  pl.BlockSpec(u_tab.shape, lambda i: (0, 0), pipeline_mode=pl.Buffered(1))],
        out_specs=pl.BlockSpec((tt, N_SEL), lambda i: (i, 0)),
        out_shape=jax.ShapeDtypeStruct((n, N_SEL), f32),
        scratch_shapes=[pltpu.VMEM((N_SEL, LANES), f32)] + _ids_scratch(),
        compiler_params=_params(),
        name="peer_u",
    )(ids, a3, gate, u_tab)


def _peer_v_kernel(ids_hbm, w_ref, v_ref, h_ref, g_ref, o_ref, ids0, ids1, sems, p_sc):
    tt = o_ref.shape[0]
    n_acc = 4

    def lane_bcast_weights(t):
        return jnp.transpose(jnp.broadcast_to(w_ref[pl.ds(t, 1), :], (LANES, N_SEL)))

    def body(ids_ref):
        def token(t, wb):
            wb_next = lane_bcast_weights(jnp.minimum(t + 1, tt - 1))
            accs = [jnp.zeros((SUBLANES, LANES), f32) for _ in range(n_acc)]
            for e in range(N_SEL):
                wv = jnp.broadcast_to(wb[e:e + 1, :], (SUBLANES, LANES))
                accs[e % n_acc] = accs[e % n_acc] + _expert_row(v_ref, _token_id(ids_ref, e, t)) * wv
            p_sc[t] = (accs[0] + accs[1]) + (accs[2] + accs[3])
            return wb_next

        lax.fori_loop(0, tt, token, lane_bcast_weights(0))
        peer = jnp.concatenate([p_sc[:, c, :] for c in range(SUBLANES)], axis=1)
        o_ref[...] = _rms(h_ref[...] + peer, g_ref[...])

    _with_ids_tile(ids_hbm, (ids0, ids1), sems, body)


def _peer_v(ids, w, v_tab, h2, g, tt):
    n_tiles = ids.shape[0]
    n, d = h2.shape
    assert tt == LANES and ids.shape[1] == N_SEL * LANES and n >= n_tiles * tt and d == SUBLANES * LANES
    return pl.pallas_call(
        _peer_v_kernel,
        grid=(n_tiles,),
        in_specs=[pl.BlockSpec(memory_space=pl.ANY),
                  pl.BlockSpec((tt, N_SEL), lambda i: (i, 0)),
                  pl.BlockSpec(v_tab.shape, lambda i: (0, 0), pipeline_mode=pl.Buffered(1)),
                  pl.BlockSpec((tt, d), lambda i: (i, 0)),
                  pl.BlockSpec((1, d), lambda i: (0, 0))],
        out_specs=pl.BlockSpec((tt, d), lambda i: (i, 0)),
        out_shape=jax.ShapeDtypeStruct((n, d), f32),
        scratch_shapes=_ids_scratch() + [pltpu.VMEM((tt, SUBLANES, LANES), f32)],
        compiler_params=_params(),
        name="peer_v",
    )(ids, w, v_tab, h2, g)


SC_WORKERS = 32
SC_LANES = 16
SC_TILE_SHARE_64THS = 26


def _peer_sc(ids_tok, x3, gate, u_words, v_words):
    from jax.experimental.pallas import tpu_sc as plsc
    n_sc = ids_tok.shape[0]
    words = v_words.shape[1]
    d = 2 * words
    tpw = n_sc // SC_WORKERS
    hrows = N_SEL // 2
    assert n_sc % SC_WORKERS == 0 and tpw % 2 == 0 and words == ROWS_PER_EXPERT * LANES
    half = words // 2
    nvec = half // SC_LANES
    mesh = plsc.VectorSubcoreMesh(core_axis_name="c", subcore_axis_name="s")

    def halves(wd):
        return (lax.bitcast_convert_type(wd << 16, f32), lax.bitcast_convert_type(wd & jnp.int32(-65536), f32))

    def body(u_hbm, v_hbm, ids_hbm, x_hbm, g_hbm, out_hbm,
             idx_a0, idx_b0, g_0, x_0, idx_a1, idx_b1, g_1, x_1, w_v, rows_a, rows_b, part_v, out_v, sem_a, sem_b, sem_s):
        wid = lax.axis_index("s") * 2 + lax.axis_index("c")
        lane = lax.broadcasted_iota(i32, (SC_LANES,), 0)
        first = wid * tpw
        last = first + tpw - 1
        inputs = ((idx_a0, idx_b0, g_0, x_0), (idx_a1, idx_b1, g_1, x_1))
        row_bufs = ((rows_a, sem_a), (rows_b, sem_b))

        def input_copies(t, par):
            idx_a, idx_b, g_v, x_v = inputs[par]
            return (pltpu.make_async_copy(ids_hbm.at[t, pl.ds(0, hrows)], idx_a, sem_s),
                    pltpu.make_async_copy(ids_hbm.at[t, pl.ds(hrows, hrows)], idx_b, sem_s),
                    pltpu.make_async_copy(g_hbm.at[t], g_v, sem_s),
                    pltpu.make_async_copy(x_hbm.at[t], x_v, sem_s))

        def gather(tab_hbm, par, h):
            rows, sem = row_bufs[h]
            return pltpu.make_async_copy(tab_hbm.at[inputs[par][h]], rows, sem)

        def dots(par, h):
            rows, x_v = row_bufs[h][0], inputs[par][3]
            for p in range(2):
                xs = []
                for j in range(nvec):
                    o = p * half + j * SC_LANES
                    r, l0 = o // LANES, o % LANES
                    xs.append((x_v[2 * r, pl.ds(l0, SC_LANES)], x_v[2 * r + 1, pl.ds(l0, SC_LANES)]))

                def dot_pair(e, c, p=p, xs=xs):
                    acc = None
                    for j in range(nvec):
                        lo, hi = halves(rows[e, pl.ds(p * half + j * SC_LANES, SC_LANES)])
                        term = lo * xs[j][0] + hi * xs[j][1]
                        acc = term if acc is None else acc + term
                    row = h * hrows + e
                    part_v[row, :] = acc if p == 0 else part_v[row, :] + acc
                    return c

                lax.fori_loop(0, hrows, dot_pair, 0)

        def weights(par):
            g_v = inputs[par][2]
            for k in range(N_SEL // SC_LANES):
                act = jnp.zeros((SC_LANES,), f32)
                for q in range(SC_LANES):
                    act = jnp.where(lane == q, jnp.sum(part_v[k * SC_LANES + q, :]), act)
                inner = 0.7978845608028654 * (act + 0.044715 * (act * act * act))
                tanh = 1.0 - 2.0 / (jnp.exp(2.0 * inner) + 1.0)
                w_v[pl.ds(k * SC_LANES, SC_LANES)] = g_v[pl.ds(k * SC_LANES, SC_LANES)] * (act * (0.5 * (1.0 + tanh)))

        def accumulate(h):
            rows = row_bufs[h][0]
            for p in range(2):
                slots = []
                for j in range(nvec):
                    o = p * half + j * SC_LANES
                    r, l0 = o // LANES, o % LANES
                    slots += [2 * LANES * r + l0, 2 * LANES * r + LANES + l0]

                def acc_pair(e, accs, p=p):
                    row = h * hrows + e
                    wvec = w_v[pl.ds(pl.multiple_of((row // SC_LANES) * SC_LANES, SC_LANES), SC_LANES)]
                    wr = jnp.sum(jnp.where(lane == row % SC_LANES, wvec, 0.0))
                    out = []
                    for j in range(nvec):
                        lo, hi = halves(rows[e, pl.ds(p * half + j * SC_LANES, SC_LANES)])
                        out.append(accs[2 * j] + lo * wr)
                        out.append(accs[2 * j + 1] + hi * wr)
                    return tuple(out)

                init = tuple(jnp.zeros((SC_LANES,), f32) if h == 0 else out_v[pl.ds(o, SC_LANES)] for o in slots)
                accs = lax.fori_loop(0, hrows, acc_pair, init)
                for o, acc in zip(slots, accs):
                    out_v[pl.ds(o, SC_LANES)] = acc

        def token(t, par):
            nxt = jnp.minimum(t + 1, last)
            for c in input_copies(nxt, 1 - par):
                c.start()
            gather(u_hbm, par, 0).wait()
            dots(par, 0)
            gather(v_hbm, par, 0).start()
            gather(u_hbm, par, 1).wait()
            dots(par, 1)
            gather(v_hbm, par, 1).start()
            weights(par)
            for c in input_copies(nxt, 1 - par):
                c.wait()
            gather(v_hbm, par, 0).wait()
            accumulate(0)
            gather(u_hbm, 1 - par, 0).start()
            gather(v_hbm, par, 1).wait()
            accumulate(1)
            gather(u_hbm, 1 - par, 1).start()
            pltpu.sync_copy(out_v, out_hbm.at[t])

        for c in input_copies(first, 0):
            c.start()
        for c in input_copies(first, 0):
            c.wait()
        gather(u_hbm, 0, 0).start()
        gather(u_hbm, 0, 1).start()

        def token_pair(i, carry):
            token(first + 2 * i, 0)
            token(first + 2 * i + 1, 1)
            return carry

        lax.fori_loop(0, tpw // 2, token_pair, 0)
        gather(u_hbm, 0, 0).wait()
        gather(u_hbm, 0, 1).wait()

    small = [pltpu.VMEM((hrows,), i32), pltpu.VMEM((hrows,), i32), pltpu.VMEM((N_SEL,), f32),
             pltpu.VMEM((SUBLANES, LANES), f32)]
    return pl.kernel(
        body,
        out_type=jax.ShapeDtypeStruct((n_sc, d), f32),
        mesh=mesh,
        scratch_types=small + small + [pltpu.VMEM((N_SEL,), f32), pltpu.VMEM((hrows, words), i32),
                                       pltpu.VMEM((hrows, words), i32), pltpu.VMEM((N_SEL, SC_LANES), f32),
                                       pltpu.VMEM((d,), f32), pltpu.SemaphoreType.DMA, pltpu.SemaphoreType.DMA,
                                       pltpu.SemaphoreType.DMA],
        compiler_params=pltpu.CompilerParams(needs_layout_passes=False),
        name="peer_sc",
    )(u_words, v_words, ids_tok, x3, gate)


def _final_tail_kernel(out_in_ref, h_ref, p_ref, g_ref, o_ref):
    del out_in_ref
    o_ref[...] = _rms(h_ref[...] + p_ref[...], g_ref[...])


def _final_tail(out, h2, peer_tail, g, tm, first_tile):
    n, d = h2.shape
    tail_tiles = peer_tail.shape[0] // tm
    return pl.pallas_call(
        _final_tail_kernel,
        grid=(tail_tiles,),
        in_specs=[pl.BlockSpec(memory_space=pl.ANY),
                  pl.BlockSpec((tm, d), lambda i: (i + first_tile, 0)),
                  pl.BlockSpec((tm, d), lambda i: (i, 0)),
                  pl.BlockSpec((1, d), lambda i: (0, 0))],
        out_specs=pl.BlockSpec((tm, d), lambda i: (i + first_tile, 0)),
        out_shape=jax.ShapeDtypeStruct((n, d), f32),
        input_output_aliases={0: 0},
        compiler_params=_params(),
        name="final_tail",
    )(out, h2, peer_tail, g)


def _rope_tables(s):
    half = HEAD_DIM // 2
    inv = ROPE_THETA ** (-jnp.arange(half, dtype=f32) / half)
    ang = jnp.arange(s, dtype=f32)[:, None] * inv[None, :]
    cos, sin = jnp.cos(ang), jnp.sin(ang)
    reps = MOBA_GROUP_LANES // HEAD_DIM
    cos_t = jnp.tile(jnp.concatenate([cos, cos], axis=1), (1, reps))
    sin_t = jnp.tile(jnp.concatenate([-sin, sin], axis=1), (1, reps))
    return cos_t, sin_t


def _row_tile(s, want):
    t = min(want, s)
    while s % t:
        t //= 2
    assert t % SUBLANES == 0
    return t


def kernel(x, mem, g_mix, w_in, w_conv, w_conv_out, w_attn_out, w_merge, g_xattn, g_mem,
           w_xq, w_xkv, w_xo, g_ffn, w_pq, peer_sub_keys, peer_u, peer_v, g_final):
    b, s, d = x.shape
    n = b * s
    n_mem = mem.shape[1]
    assert w_in.shape[0] == 1, "single-layer trunk only"
    assert d == SUBLANES * LANES and s % MOBA_BLOCK == 0
    n_exp = peer_u.shape[1]
    cos_t, sin_t = _rope_tables(s)
    tm = _row_tile(s, 512)
    h = x.reshape(n, d)
    mem2 = mem.reshape(b * n_mem, d)
    l = 0
    proj = _inproj(h, g_mix[l][None, :], w_in[l].astype(bf16), tm)
    o = _moba(proj, cos_t, sin_t, b, s)
    h1 = _mix(proj, o, h, w_conv[l], w_conv_out[l].astype(bf16), w_attn_out[l].astype(bf16),
              w_merge[l].astype(bf16), s, tm)
    kv = _memkv(mem2, g_mem[l][None, :], w_xkv[l].astype(bf16), n_mem)
    h2 = _xattn(h1, g_xattn[l][None, :], kv, w_xq[l].astype(bf16), w_xo[l].astype(bf16), s, n_mem, tm)
    u_tab = _pack_table(peer_u[l])
    v_tab = _pack_table(peer_v[l])
    tt = _row_tile(s, LANES)
    tsel = _row_tile(s, 256)
    n_tiles = n // tt
    per_sel = tsel // tt
    sc_tiles = ((n_tiles * SC_TILE_SHARE_64THS) // 64) // per_sel * per_sel
    tc_tiles = n_tiles - sc_tiles
    sel_args = (h2, g_ffn[l][None, :], w_pq[l].astype(bf16), peer_sub_keys[l].astype(bf16), tsel)
    if sc_tiles:
        a3_s, ids_s, gate_s = _select(*sel_args, tc_tiles // per_sel, sc_tiles // per_sel)
        as_rows = lambda tab: lax.bitcast_convert_type(tab, i32).reshape(n_exp, ROWS_PER_EXPERT * LANES)
        ids_tok = (ids_s // ROWS_PER_EXPERT).transpose(0, 2, 1).reshape(sc_tiles * tt, N_SEL)
        peer_sc = _peer_sc(ids_tok, a3_s, gate_s, as_rows(u_tab), as_rows(v_tab))
    a3, ids, gate = _select(*sel_args, 0, tc_tiles // per_sel)
    ids = ids.reshape(tc_tiles, N_SEL * LANES)
    w = _peer_u(ids, a3, gate, u_tab, tt)
    out = _peer_v(ids, w, v_tab, h2, g_final[None, :], tt)
    if sc_tiles:
        out = _final_tail(out, h2, peer_sc, g_final[None, :], tt, tc_tiles)
    return out.reshape(b, s, d)
```
